```python
import math
import jax, jax.numpy as jnp
from jax import lax
import numpy as np

D_MODEL = 2048
BATCH = 4
SEQ = 2048
DEPTH = 2

GLA_HEADS = 4
GLA_DK = 64
GLA_DV = 128
GLA_RANK = 16
GLA_TAU = 16.0
GLA_CHUNK = 64
GLA_QK = GLA_HEADS * GLA_DK
GLA_V = GLA_HEADS * GLA_DV

DIL_PAIRS = ((128, 1), (512, 4), (2048, 16))
DIL_HEADS_PER_GROUP = 4
DIL_HEAD_DIM = 64
DIL_HEADS = len(DIL_PAIRS) * DIL_HEADS_PER_GROUP
DIL_W = DIL_HEADS * DIL_HEAD_DIM
DIL_OUT = DIL_HEADS_PER_GROUP * DIL_HEAD_DIM
DIL_BLOCK = 128
ROPE_THETA = 10000.0

SGU_CHUNK = 128
SGU_GROUPS = 6
SGU_GROUP_CH = 128
SGU_W = SGU_GROUPS * SGU_GROUP_CH

MLSTM_HEADS = 4
MLSTM_HEAD_DIM = 128
MLSTM_CHUNK = 64
MLSTM_CONV = 4
ML_W = MLSTM_HEADS * MLSTM_HEAD_DIM

N_EXPERTS = 64
TOP_K = 8
N_GROUPS = 8
TOPK_GROUPS = 4
EXPERT_FF = 512
SHARED_FF = 512
ROUTED_SCALE = 2.5
EXPERT_BLOCK = 8

N_BRANCH = 4
DN_ALPHA = (2 * DEPTH) ** 0.25
DN_BETA = (8 * DEPTH) ** -0.25
LN_EPS = 1e-5

SPLIT_SIZES = (GLA_QK, GLA_QK, GLA_V, GLA_V, GLA_RANK,
               DIL_W, DIL_W, DIL_W,
               SGU_W, SGU_W,
               ML_W, ML_W, ML_W, ML_W, MLSTM_HEADS, MLSTM_HEADS,
               N_BRANCH * D_MODEL)
IN_COLS = sum(SPLIT_SIZES)

kernel_name = 'hybrid_gla_dilated_sgu_mlstm_moe'


def _layernorm(x, g, b):
    xf = x.astype(jnp.float32)
    mu = jnp.mean(xf, axis=-1, keepdims=True)
    var = jnp.mean(jnp.square(xf - mu), axis=-1, keepdims=True)
    y = (xf - mu) * lax.rsqrt(var + LN_EPS) * g.astype(jnp.float32) + b.astype(jnp.float32)
    return y.astype(x.dtype)


def _head_rmsnorm(x, g, n_heads):
    B, S, W = x.shape
    xf = x.astype(jnp.float32).reshape(B, S, n_heads, W // n_heads)
    xf = xf * lax.rsqrt(jnp.mean(jnp.square(xf), axis=-1, keepdims=True) + LN_EPS)
    return xf.reshape(B, S, W) * g.astype(jnp.float32)


def _rope(x, pos):
    dh = x.shape[-1]
    half = dh // 2
    inv = ROPE_THETA ** (-jnp.arange(half, dtype=jnp.float32) * 2.0 / dh)
    ang = pos.astype(jnp.float32)[..., None] * inv
    cos = jnp.cos(ang)[:, :, None, :]
    sin = jnp.sin(ang)[:, :, None, :]
    x1, x2 = x[..., :half], x[..., half:]
    return jnp.concatenate([x1 * cos - x2 * sin, x2 * cos + x1 * sin], axis=-1)


def _to_chunks(t, n_heads, chunk):
    B, S, W = t.shape
    return t.reshape(B, S // chunk, chunk, n_heads, W // n_heads).transpose(1, 0, 3, 2, 4)


def _from_chunks(t):
    N, B, H, L, d = t.shape
    return t.transpose(1, 0, 3, 2, 4).reshape(B, N * L, H * d)


def _gla(q, k, v, r, a_low, wa2, ba, norm_g):
    f32 = jnp.float32
    L = GLA_CHUNK
    g = jax.nn.log_sigmoid((a_low @ wa2 + ba).astype(f32)) / GLA_TAU
    qc = _to_chunks(q.astype(f32) * GLA_DK ** -0.5, GLA_HEADS, L)
    kc = _to_chunks(k.astype(f32), GLA_HEADS, L)
    vc = _to_chunks(v.astype(f32), GLA_HEADS, L)
    gc = _to_chunks(g, GLA_HEADS, L)
    causal = jnp.tril(jnp.ones((L, L), dtype=bool))

    def step(state, inp):
        qq, kk, vv, gg = inp
        b = jnp.cumsum(gg, axis=-2)
        qe = qq * jnp.exp(b)
        ke = kk * jnp.exp(-b)
        att = jnp.where(causal, jnp.einsum('bhid,bhjd->bhij', qe, ke), 0.0)
        o = jnp.einsum('bhij,bhje->bhie', att, vv) + jnp.einsum('bhid,bhde->bhie', qe, state)
        b_last = b[:, :, -1, :]
        kd = kk * jnp.exp(b_last[:, :, None, :] - b)
        state = jnp.exp(b_last)[..., None] * state + jnp.einsum('bhjd,bhje->bhde', kd, vv)
        return state, o

    B = q.shape[0]
    state0 = jnp.zeros((B, GLA_HEADS, GLA_DK, GLA_DV), f32)
    _, o = lax.scan(step, state0, (qc, kc, vc, gc))
    o = _head_rmsnorm(_from_chunks(o), norm_g, GLA_HEADS)
    return (o * jax.nn.silu(r.astype(f32))).astype(q.dtype)


def _dilated_group(q, k, v, window, dilation):
    B, S, H, dh = q.shape
    Lsub = S // dilation
    span = window // dilation
    blk = min(DIL_BLOCK, Lsub)
    nb = -(-Lsub // blk)
    Lp = nb * blk

    def strided(t):
        t = t.reshape(B, Lsub, dilation, H, dh).transpose(0, 2, 3, 1, 4)
        return jnp.pad(t, ((0, 0), (0, 0), (0, 0), (0, Lp - Lsub), (0, 0)))

    def with_prev(t):
        tb = t.reshape(B, dilation, H, nb, blk, dh)
        prev = jnp.pad(tb, ((0, 0), (0, 0), (0, 0), (1, 0), (0, 0), (0, 0)))[:, :, :, :-1]
        return jnp.concatenate([prev, tb], axis=4)

    qb = strided(q).reshape(B, dilation, H, nb, blk, dh)
    kb = with_prev(strided(k))
    vb = with_prev(strided(v))
    qpos = jnp.arange(nb)[:, None] * blk + jnp.arange(blk)[None, :]
    kpos = (jnp.arange(nb)[:, None] - 1) * blk + jnp.arange(2 * blk)[None, :]
    rel = qpos[:, :, None] - kpos[:, None, :]
    valid = (rel >= 0) & (rel <= span) & (kpos[:, None, :] >= 0)
    s = jnp.einsum('brhnqd,brhnkd->brhnqk', qb, kb) * dh ** -0.5
    s = jnp.where(valid, s, -jnp.inf)
    m = jnp.max(s, axis=-1, keepdims=True)
    p = jnp.exp(s - m)
    den = jnp.sum(p, axis=-1)
    o = jnp.einsum('brhnqk,brhnkd->brhnqd', p, vb) / den[..., None]
    lse = m[..., 0] + jnp.log(den)
    o = o.reshape(B, dilation, H, Lp, dh)[:, :, :, :Lsub].transpose(0, 3, 1, 2, 4).reshape(B, S, H, dh)
    lse = lse.reshape(B, dilation, H, Lp)[:, :, :, :Lsub].transpose(0, 3, 1, 2).reshape(B, S, H)
    return o, lse


def _dilated(q, k, v, pos):
    B, S, _ = q.shape
    f32 = jnp.float32
    qh = _rope(q.astype(f32).reshape(B, S, DIL_HEADS, DIL_HEAD_DIM), pos)
    kh = _rope(k.astype(f32).reshape(B, S, DIL_HEADS, DIL_HEAD_DIM), pos)
    vh = v.astype(f32).reshape(B, S, DIL_HEADS, DIL_HEAD_DIM)
    outs, lses = [], []
    for gi, (window, dilation) in enumerate(DIL_PAIRS):
        hs = slice(gi * DIL_HEADS_PER_GROUP, (gi + 1) * DIL_HEADS_PER_GROUP)
        o, lse = _dilated_group(qh[:, :, hs], kh[:, :, hs], vh[:, :, hs], window, dilation)
        outs.append(o)
        lses.append(lse)
    w = jax.nn.softmax(jnp.stack(lses, axis=0), axis=0)
    o = jnp.sum(w[..., None] * jnp.stack(outs, axis=0), axis=0)
    return o.reshape(B, S, DIL_OUT).astype(q.dtype)


def _sgu(u, v, ln_g, ln_b, ws, bs):
    f32 = jnp.float32
    B, S, _ = u.shape
    N = S // SGU_CHUNK
    zu = jax.nn.gelu(u.astype(f32), approximate=False)
    zv = jax.nn.gelu(v.astype(f32), approximate=False)
    vn = _layernorm(zv, ln_g, ln_b).reshape(B, N, SGU_CHUNK, SGU_GROUPS, SGU_GROUP_CH)
    w_causal = ws.astype(f32) * jnp.tril(jnp.ones((SGU_CHUNK, SGU_CHUNK), f32))
    s = jnp.einsum('gij,bnjgc->bnigc', w_causal, vn) + bs.astype(f32).T[None, None, :, :, None]
    return (zu * s.reshape(B, S, SGU_W)).astype(u.dtype)


def _mlstm(q, k, v, o_pre, i_pre, f_pre, conv_w, conv_b, b_i, b_f, norm_g):
    f32 = jnp.float32
    B, S, _ = q.shape
    L = MLSTM_CHUNK
    qk = jnp.concatenate([q, k], axis=-1)
    qk = lax.conv_general_dilated(qk, conv_w[:, None, :].astype(qk.dtype), window_strides=(1,),
                                  padding=((MLSTM_CONV - 1, 0),), dimension_numbers=('NWC', 'WIO', 'NWC'),
                                  feature_group_count=qk.shape[-1]) + conv_b
    qk = jax.nn.silu(qk.astype(f32))
    qq_all, kk_all = jnp.split(qk, 2, axis=-1)
    qc = _to_chunks(qq_all, MLSTM_HEADS, L)
    kc = _to_chunks(kk_all * MLSTM_HEAD_DIM ** -0.5, MLSTM_HEADS, L)
    vc = _to_chunks(v.astype(f32), MLSTM_HEADS, L)
    logf = jax.nn.log_sigmoid((f_pre + b_f).astype(f32))
    logi = (i_pre + b_i).astype(f32)
    lfc = logf.reshape(B, S // L, L, MLSTM_HEADS).transpose(1, 0, 3, 2)
    lic = logi.reshape(B, S // L, L, MLSTM_HEADS).transpose(1, 0, 3, 2)
    causal = jnp.tril(jnp.ones((L, L), dtype=bool))

    def step(carry, inp):
        C, n, m = carry
        qq, kk, vv, lf, li = inp
        b = jnp.cumsum(lf, axis=-1)
        dmat = jnp.where(causal, b[..., :, None] - b[..., None, :] + li[..., None, :], -jnp.inf)
        inter = b + m[..., None]
        m_t = jnp.maximum(inter, jnp.max(dmat, axis=-1))
        w_in = jnp.exp(dmat - m_t[..., None])
        w_st = jnp.exp(inter - m_t)
        sc = jnp.einsum('bhid,bhjd->bhij', qq, kk) * w_in
        num = jnp.einsum('bhij,bhje->bhie', sc, vv) + w_st[..., None] * jnp.einsum('bhid,bhde->bhie', qq, C)
        den = jnp.sum(sc, axis=-1) + w_st * jnp.einsum('bhid,bhd->bhi', qq, n)
        h = num / jnp.maximum(jnp.abs(den), jnp.exp(-m_t))[..., None]
        b_last = b[..., -1]
        dec = b_last[..., None] - b + li
        m_new = jnp.maximum(b_last + m, jnp.max(dec, axis=-1))
        wk = jnp.exp(dec - m_new[..., None])
        keep = jnp.exp(b_last + m - m_new)
        C = keep[..., None, None] * C + jnp.einsum('bhj,bhjd,bhje->bhde', wk, kk, vv)
        n = keep[..., None] * n + jnp.einsum('bhj,bhjd->bhd', wk, kk)
        return (C, n, m_new), h

    carry0 = (jnp.zeros((B, MLSTM_HEADS, MLSTM_HEAD_DIM, MLSTM_HEAD_DIM), f32),
              jnp.zeros((B, MLSTM_HEADS, MLSTM_HEAD_DIM), f32),
              jnp.zeros((B, MLSTM_HEADS), f32))
    _, h = lax.scan(step, carry0, (qc, kc, vc, lfc, lic))
    h = jax.nn.sigmoid(o_pre.astype(f32)) * _from_chunks(h)
    return _head_rmsnorm(h, norm_g, MLSTM_HEADS).astype(q.dtype)


def _moe(x, w_router, router_bias, w1, w3, w2, ws1, ws3, ws2):
    f32 = jnp.float32
    B, S, D = x.shape
    t = x.reshape(B * S, D)
    scores = jax.nn.sigmoid(t.astype(f32) @ w_router.astype(f32))
    choice = scores + router_bias.astype(f32)
    grp = choice.reshape(-1, N_GROUPS, N_EXPERTS // N_GROUPS)
    grp_score = jnp.sum(lax.top_k(grp, 2)[0], axis=-1)
    _, grp_idx = lax.top_k(grp_score, TOPK_GROUPS)
    grp_mask = jnp.sum(jax.nn.one_hot(grp_idx, N_GROUPS, dtype=f32), axis=-2) > 0
    exp_mask = jnp.repeat(grp_mask, N_EXPERTS // N_GROUPS, axis=-1)
    _, idx = lax.top_k(jnp.where(exp_mask, choice, -jnp.inf), TOP_K)
    wts = jnp.take_along_axis(scores, idx, axis=-1)
    wts = wts / jnp.sum(wts, axis=-1, keepdims=True) * ROUTED_SCALE
    combine = jnp.sum(jax.nn.one_hot(idx, N_EXPERTS, dtype=f32) * wts[..., None], axis=-2)
    out = (jax.nn.silu(t @ ws1) * (t @ ws3)) @ ws2
    for e0 in range(0, N_EXPERTS, EXPERT_BLOCK):
        sl = slice(e0, e0 + EXPERT_BLOCK)
        h = jax.nn.silu(jnp.einsum('nd,edf->nef', t, w1[sl])) * jnp.einsum('nd,edf->nef', t, w3[sl])
        h = h * combine[:, sl, None].astype(h.dtype)
        out = out + jnp.einsum('nef,efd->nd', h, w2[sl])
    return out.reshape(B, S, D).astype(x.dtype)


def setup_inputs(seed: int = 0) -> dict:
    key = jax.random.key(seed)
    ks = iter(jax.random.split(key, 48))
    f32 = jnp.float32
    L, D, E, F, SF = DEPTH, D_MODEL, N_EXPERTS, EXPERT_FF, SHARED_FF

    def nrm(shape, scale):
        return jax.random.normal(next(ks), shape, f32) * scale

    x = jax.random.normal(next(ks), (BATCH, SEQ, D), f32)
    offs = jax.random.randint(next(ks), (BATCH, 1), 0, 4096, dtype=jnp.int32)
    positions = offs + jnp.arange(SEQ, dtype=jnp.int32)[None, :]
    return {
        'x': x,
        'positions': positions,
        'w_in': nrm((L, D, IN_COLS), D ** -0.5),
        'gla_wa2': nrm((L, GLA_RANK, GLA_QK), GLA_RANK ** -0.5),
        'gla_ba': nrm((L, GLA_QK), 0.1),
        'gla_norm': 1.0 + nrm((L, GLA_V), 0.02),
        'mlstm_conv_w': nrm((L, MLSTM_CONV, 2 * ML_W), MLSTM_CONV ** -0.5),
        'mlstm_conv_b': nrm((L, 2 * ML_W), 0.02),
        'mlstm_bi': nrm((L, MLSTM_HEADS), 0.1),
        'mlstm_bf': 3.0 + nrm((L, MLSTM_HEADS), 0.5),
        'mlstm_norm': 1.0 + nrm((L, ML_W), 0.02),
        'sgu_ln_g': 1.0 + nrm((L, SGU_W), 0.02),
        'sgu_ln_b': nrm((L, SGU_W), 0.02),
        'sgu_ws': nrm((L, SGU_GROUPS, SGU_CHUNK, SGU_CHUNK), 0.5 * SGU_CHUNK ** -0.5),
        'sgu_bs': 1.0 + nrm((L, SGU_GROUPS, SGU_CHUNK), 0.1),
        'w_pa': nrm((L, GLA_V, D), DN_BETA * GLA_V ** -0.5),
        'w_pb': nrm((L, DIL_OUT, D), DN_BETA * DIL_OUT ** -0.5),
        'w_pc': nrm((L, SGU_W, D), DN_BETA * SGU_W ** -0.5),
        'w_pd': nrm((L, ML_W, D), DN_BETA * ML_W ** -0.5),
        'w_out': nrm((L, D, D), DN_BETA * D ** -0.5),
        'ln1_g': 1.0 + nrm((L, D), 0.02),
        'ln1_b': nrm((L, D), 0.02),
        'w_router': nrm((L, D, E), D ** -0.5),
        'router_bias': nrm((L, E), 0.01),
        'w1': nrm((L, E, D, F), D ** -0.5),
        'w3': nrm((L, E, D, F), D ** -0.5),
        'w2': nrm((L, E, F, D), DN_BETA * F ** -0.5),
        'ws1': nrm((L, D, SF), D ** -0.5),
        'ws3': nrm((L, D, SF), D ** -0.5),
        'ws2': nrm((L, SF, D), DN_BETA * SF ** -0.5),
        'ln2_g': 1.0 + nrm((L, D), 0.02),
        'ln2_b': nrm((L, D), 0.02),
    }


def reference(x, positions, w_in, gla_wa2, gla_ba, gla_norm, mlstm_conv_w, mlstm_conv_b, mlstm_bi, mlstm_bf,
              mlstm_norm, sgu_ln_g, sgu_ln_b, sgu_ws, sgu_bs, w_pa, w_pb, w_pc, w_pd, w_out, ln1_g, ln1_b,
              w_router, router_bias, w1, w3, w2, ws1, ws3, ws2, ln2_g, ln2_b):
    B, S, D = x.shape
    points = np.cumsum(SPLIT_SIZES)[:-1].tolist()
    for l in range(DEPTH):
        y = x @ w_in[l]
        (a_q, a_k, a_v, a_r, a_low, b_q, b_k, b_v, c_u, c_v,
         d_q, d_k, d_v, d_o, d_i, d_f, g_pre) = jnp.split(y, points, axis=-1)
        ya = _gla(a_q, a_k, a_v, a_r, a_low, gla_wa2[l], gla_ba[l], gla_norm[l])
        yb = _dilated(b_q, b_k, b_v, positions)
        yc = _sgu(c_u, c_v, sgu_ln_g[l], sgu_ln_b[l], sgu_ws[l], sgu_bs[l])
        yd = _mlstm(d_q, d_k, d_v, d_o, d_i, d_f, mlstm_conv_w[l], mlstm_conv_b[l],
                    mlstm_bi[l], mlstm_bf[l], mlstm_norm[l])
        gates = jax.nn.sigmoid(g_pre.astype(jnp.float32)).reshape(B, S, N_BRANCH, D)
        merged = (gates[:, :, 0] * (ya @ w_pa[l]) + gates[:, :, 1] * (yb @ w_pb[l])
                  + gates[:, :, 2] * (yc @ w_pc[l]) + gates[:, :, 3] * (yd @ w_pd[l]))
        h = merged.astype(x.dtype) @ w_out[l]
        x = _layernorm(DN_ALPHA * x + h, ln1_g[l], ln1_b[l])
        h = _moe(x, w_router[l], router_bias[l], w1[l], w3[l], w2[l], ws1[l], ws3[l], ws2[l])
        x = _layernorm(DN_ALPHA * x + h, ln2_g[l], ln2_b[l])
    return x
```

```python
import functools
import math

import jax
import jax.numpy as jnp
from jax import lax
from jax.experimental import pallas as pl
from jax.experimental.pallas import tpu as pltpu

D_MODEL = 2048
DEPTH = 2

GLA_HEADS = 4
GLA_DK = 64
GLA_DV = 128
GLA_RANK = 16
GLA_TAU = 16.0
GLA_CHUNK = 64
GLA_QK = GLA_HEADS * GLA_DK
GLA_V = GLA_HEADS * GLA_DV

DIL_PAIRS = ((128, 1), (512, 4), (2048, 16))
DIL_HEADS_PER_GROUP = 4
DIL_HEAD_DIM = 64
DIL_HEADS = len(DIL_PAIRS) * DIL_HEADS_PER_GROUP
DIL_W = DIL_HEADS * DIL_HEAD_DIM
DIL_OUT = DIL_HEADS_PER_GROUP * DIL_HEAD_DIM
DIL_BLOCK = 128
ROPE_THETA = 10000.0

SGU_CHUNK = 128
SGU_GROUPS = 6
SGU_GROUP_CH = 128
SGU_W = SGU_GROUPS * SGU_GROUP_CH

MLSTM_HEADS = 4
MLSTM_HEAD_DIM = 128
MLSTM_CHUNK = 64
MLSTM_CONV = 4
ML_W = MLSTM_HEADS * MLSTM_HEAD_DIM

N_EXPERTS = 64
TOP_K = 8
N_GROUPS = 8
TOPK_GROUPS = 4
EXPERT_FF = 512
SHARED_FF = 512
ROUTED_SCALE = 2.5

N_BRANCH = 4
DN_ALPHA = (2 * DEPTH) ** 0.25
LN_EPS = 1e-5

_OFF_A = 0
_W_A = 2 * GLA_QK + 2 * GLA_V
_OFF_LOW = _OFF_A + _W_A
_OFF_B = _OFF_LOW + GLA_RANK
_W_B = 3 * DIL_W
_OFF_C = _OFF_B + _W_B
_W_C = 2 * SGU_W
_OFF_D = _OFF_C + _W_C
_W_D = 4 * ML_W
_OFF_IF = _OFF_D + _W_D
_OFF_G = _OFF_IF + 2 * MLSTM_HEADS
_W_G = N_BRANCH * D_MODEL

LANES = 128
VMEM_LIMIT = 56 * 1024 * 1024

_SM_LOW = 0
_SM_I = GLA_RANK
_SM_F = GLA_RANK + MLSTM_HEADS

HI = lax.Precision.HIGHEST
F32 = jnp.float32
BF16 = jnp.bfloat16
NEG_INF = float("-inf")


def _cparams(sem):
    return pltpu.CompilerParams(dimension_semantics=sem, vmem_limit_bytes=VMEM_LIMIT)


def _log_sigmoid(x):
    return jnp.minimum(x, 0.0) - jnp.log1p(jnp.exp(-jnp.abs(x)))


def _sigmoid(x):
    return 1.0 / (1.0 + jnp.exp(-x))


def _silu(x):
    return x * _sigmoid(x)


def _iota2(shape, dim):
    return lax.broadcasted_iota(jnp.int32, shape, dim)


def _col_to_row(col, n):
    eye = _iota2((n, n), 0) == _iota2((n, n), 1)
    return jnp.sum(jnp.where(eye, col, 0.0), axis=0, keepdims=True)


def _row_to_col(row, n):
    eye = _iota2((n, n), 0) == _iota2((n, n), 1)
    return jnp.sum(jnp.where(eye, row, 0.0), axis=1, keepdims=True)


def _mm_kernel(x_ref, w_ref, o_ref):
    o_ref[...] = jnp.dot(x_ref[...], w_ref[...], preferred_element_type=F32).astype(o_ref.dtype)


def _matmul(x, w, tm, tn, out_dtype=F32):
    n, k = x.shape
    m = w.shape[1]
    return pl.pallas_call(
        _mm_kernel,
        grid=(n // tm, m // tn),
        in_specs=[pl.BlockSpec((tm, k), lambda i, j: (i, 0)),
                  pl.BlockSpec((k, tn), lambda i, j: (0, j))],
        out_specs=pl.BlockSpec((tm, tn), lambda i, j: (i, j)),
        out_shape=jax.ShapeDtypeStruct((n, m), out_dtype),
        compiler_params=_cparams(("parallel", "arbitrary")),
        name="in_proj",
    )(x, w)


def _gla_kernel(y_ref, sm_ref, wa2_ref, ba_ref, g_ref, o_ref, state_ref, *, nb):
    L, H, DK, DV = GLA_CHUNK, GLA_HEADS, GLA_DK, GLA_DV

    @pl.when(pl.program_id(0) == 0)
    def _():
        state_ref[...] = jnp.zeros_like(state_ref)

    tril = (_iota2((L, L), 0) >= _iota2((L, L), 1))
    tril_f = tril.astype(F32)
    for b in range(nb):
        y = y_ref[b]
        a_low = sm_ref[b][:, _SM_LOW:_SM_LOW + GLA_RANK]
        glog = jnp.dot(a_low, wa2_ref[...], preferred_element_type=F32) + ba_ref[...]
        g = _log_sigmoid(glog) * (1.0 / GLA_TAU)
        bc = jnp.dot(tril_f, g, precision=HI, preferred_element_type=F32)
        outs = []
        for h in range(H):
            q = y[:, h * DK:(h + 1) * DK] * (DK ** -0.5)
            k = y[:, GLA_QK + h * DK:GLA_QK + (h + 1) * DK]
            v = y[:, 2 * GLA_QK + h * DV:2 * GLA_QK + (h + 1) * DV]
            bh = bc[:, h * DK:(h + 1) * DK]
            qe = q * jnp.exp(bh)
            ke = k * jnp.exp(-bh)
            att = lax.dot_general(qe, ke, (((1,), (1,)), ((), ())), preferred_element_type=F32)
            att = jnp.where(tril, att, 0.0)
            st = state_ref[b * H + h]
            o = (jnp.dot(att, v, preferred_element_type=F32)
                 + jnp.dot(qe, st, preferred_element_type=F32))
            b_last = bh[L - 1:L, :]
            kd = k * jnp.exp(b_last - bh)
            decay = _row_to_col(jnp.exp(b_last), DK)
            state_ref[b * H + h] = decay * st + lax.dot_general(
                kd, v, (((0,), (0,)), ((), ())), preferred_element_type=F32)
            o = o * lax.rsqrt(jnp.mean(o * o, axis=-1, keepdims=True) + LN_EPS)
            outs.append(o)
        o_all = jnp.concatenate(outs, axis=-1) * g_ref[...]
        r = y[:, 2 * GLA_QK + GLA_V:2 * GLA_QK + 2 * GLA_V]
        o_ref[b] = (o_all * _silu(r)).astype(o_ref.dtype)


def _gla(ya3, sm3, wa2, ba, norm_g):
    nb, s, _ = ya3.shape
    L = GLA_CHUNK
    return pl.pallas_call(
        functools.partial(_gla_kernel, nb=nb),
        grid=(s // L,),
        in_specs=[pl.BlockSpec((nb, L, _W_A), lambda n: (0, n, 0)),
                  pl.BlockSpec((nb, L, LANES), lambda n: (0, n, 0)),
                  pl.BlockSpec((GLA_RANK, GLA_QK), lambda n: (0, 0)),
                  pl.BlockSpec((1, GLA_QK), lambda n: (0, 0)),
                  pl.BlockSpec((1, GLA_V), lambda n: (0, 0))],
        out_specs=pl.BlockSpec((nb, L, GLA_V), lambda n: (0, n, 0)),
        out_shape=jax.ShapeDtypeStruct((nb, s, GLA_V), BF16),
        scratch_shapes=[pltpu.VMEM((nb * GLA_HEADS, GLA_DK, GLA_DV), F32)],
        compiler_params=_cparams(("arbitrary",)),
        name="gla",
    )(ya3, sm3, wa2, ba.reshape(1, GLA_QK), norm_g.reshape(1, GLA_V))


def _mlstm_kernel(y_ref, sm_ref, cw_ref, cb_ref, gb_ref, g_ref, o_ref, c_ref, n_ref, m_ref, tail_ref, *, nb):
    L, H, DH = MLSTM_CHUNK, MLSTM_HEADS, MLSTM_HEAD_DIM
    W2 = 2 * ML_W
    HALO = 8

    @pl.when(pl.program_id(0) == 0)
    def _():
        c_ref[...] = jnp.zeros_like(c_ref)
        n_ref[...] = jnp.zeros_like(n_ref)
        m_ref[...] = jnp.zeros_like(m_ref)
        tail_ref[...] = jnp.zeros_like(tail_ref)

    tril = (_iota2((L, L), 0) >= _iota2((L, L), 1))
    tril_f = tril.astype(F32)
    for b in range(nb):
        y = y_ref[b]
        qk_raw = y[:, :W2]
        ext = jnp.concatenate([tail_ref[b], qk_raw], axis=0)
        tail_ref[b] = qk_raw[L - HALO:, :]
        conv = cb_ref[...]
        for j in range(MLSTM_CONV):
            s0 = HALO - (MLSTM_CONV - 1) + j
            conv = conv + cw_ref[j:j + 1, :] * ext[s0:s0 + L, :]
        qk = _silu(conv)
        gates = sm_ref[b] + gb_ref[...]
        bcum = jnp.dot(tril_f, _log_sigmoid(gates), precision=HI, preferred_element_type=F32)
        outs = []
        for h in range(H):
            q = qk[:, h * DH:(h + 1) * DH]
            k = qk[:, ML_W + h * DH:ML_W + (h + 1) * DH] * (DH ** -0.5)
            v = y[:, W2 + h * DH:W2 + (h + 1) * DH]
            b_col = bcum[:, _SM_F + h:_SM_F + h + 1]
            li_col = gates[:, _SM_I + h:_SM_I + h + 1]
            b_row = _col_to_row(b_col, L)
            li_row = _col_to_row(li_col, L)
            m_prev = m_ref[b * H + h][:, 0:1]
            dmat = jnp.where(tril, b_col - b_row + li_row, NEG_INF)
            inter = b_col + m_prev
            m_t = jnp.maximum(inter, jnp.max(dmat, axis=-1, keepdims=True))
            w_in = jnp.exp(dmat - m_t)
            w_st = jnp.exp(inter - m_t)
            sc = lax.dot_general(q, k, (((1,), (1,)), ((), ())), preferred_element_type=F32) * w_in
            cst = c_ref[b * H + h]
            nst = n_ref[b * H + h]
            num = (jnp.dot(sc, v, preferred_element_type=F32)
                   + w_st * jnp.dot(q, cst, preferred_element_type=F32))
            den = jnp.sum(sc, axis=-1, keepdims=True) + w_st * jnp.sum(q * nst, axis=-1, keepdims=True)
            hh = num / jnp.maximum(jnp.abs(den), jnp.exp(-m_t))
            b_last = b_col[L - 1:L, :]
            dec = b_last - b_col + li_col
            m_new = jnp.maximum(b_last + m_prev, jnp.max(dec, axis=0, keepdims=True))
            wk = jnp.exp(dec - m_new)
            keep = jnp.exp(b_last + m_prev - m_new)
            wkk = wk * k
            c_ref[b * H + h] = keep * cst + lax.dot_general(
                wkk, v, (((0,), (0,)), ((), ())), preferred_element_type=F32)
            n_ref[b * H + h] = keep * nst + jnp.sum(wkk, axis=0, keepdims=True)
            m_ref[b * H + h] = jnp.broadcast_to(m_new, (1, LANES))
            o_pre = y[:, W2 + ML_W + h * DH:W2 + ML_W + (h + 1) * DH]
            hh = _sigmoid(o_pre) * hh
            hh = hh * lax.rsqrt(jnp.mean(hh * hh, axis=-1, keepdims=True) + LN_EPS)
            outs.append(hh)
        o_ref[b] = (jnp.concatenate(outs, axis=-1) * g_ref[...]).astype(o_ref.dtype)


def _mlstm(yd3, sm3, conv_w, conv_b, b_i, b_f, norm_g):
    nb, s, _ = yd3.shape
    L = MLSTM_CHUNK
    gate_bias = jnp.zeros((1, LANES), F32)
    gate_bias = gate_bias.at[0, _SM_I:_SM_I + MLSTM_HEADS].set(b_i).at[0, _SM_F:_SM_F + MLSTM_HEADS].set(b_f)
    return pl.pallas_call(
        functools.partial(_mlstm_kernel, nb=nb),
        grid=(s // L,),
        in_specs=[pl.BlockSpec((nb, L, _W_D), lambda n: (0, n, 0)),
                  pl.BlockSpec((nb, L, LANES), lambda n: (0, n, 0)),
                  pl.BlockSpec((MLSTM_CONV, 2 * ML_W), lambda n: (0, 0)),
                  pl.BlockSpec((1, 2 * ML_W), lambda n: (0, 0)),
                  pl.BlockSpec((1, LANES), lambda n: (0, 0)),
                  pl.BlockSpec((1, ML_W), lambda n: (0, 0))],
        out_specs=pl.BlockSpec((nb, L, ML_W), lambda n: (0, n, 0)),
        out_shape=jax.ShapeDtypeStruct((nb, s, ML_W), BF16),
        scratch_shapes=[pltpu.VMEM((nb * MLSTM_HEADS, MLSTM_HEAD_DIM, MLSTM_HEAD_DIM), F32),
                        pltpu.VMEM((nb * MLSTM_HEADS, 1, MLSTM_HEAD_DIM), F32),
                        pltpu.VMEM((nb * MLSTM_HEADS, 1, LANES), F32),
                        pltpu.VMEM((nb, 8, 2 * ML_W), F32)],
        compiler_params=_cparams(("arbitrary",)),
        name="mlstm",
    )(yd3, sm3, conv_w, conv_b.reshape(1, 2 * ML_W), gate_bias, norm_g.reshape(1, ML_W))


def _gelu(x):
    return 0.5 * x * (1.0 + lax.erf(x * (0.5 ** 0.5)))


def _sgu_kernel(y_ref, lg_ref, lb_ref, ws_ref, bst_ref, o_ref, *, nchunk):
    C, G, GC = SGU_CHUNK, SGU_GROUPS, SGU_GROUP_CH
    y = y_ref[...]
    zu = _gelu(y[:, :SGU_W])
    zv = _gelu(y[:, SGU_W:])
    mu = jnp.mean(zv, axis=-1, keepdims=True)
    var = jnp.mean(jnp.square(zv - mu), axis=-1, keepdims=True)
    vn = (zv - mu) * lax.rsqrt(var + LN_EPS) * lg_ref[...] + lb_ref[...]
    tril = _iota2((C, C), 0) >= _iota2((C, C), 1)
    for g in range(G):
        wc = jnp.where(tril, ws_ref[g], 0.0)
        bias = bst_ref[:, g:g + 1]
        for c in range(nchunk):
            rows = slice(c * C, (c + 1) * C)
            cols = slice(g * GC, (g + 1) * GC)
            s = jnp.dot(wc, vn[rows, cols], preferred_element_type=F32) + bias
            o_ref[rows, cols] = (zu[rows, cols] * s).astype(o_ref.dtype)


def _sgu(yc, ln_g, ln_b, ws, bs, nchunk=2):
    n = yc.shape[0]
    t = nchunk * SGU_CHUNK
    bst = jnp.zeros((SGU_CHUNK, LANES), F32).at[:, :SGU_GROUPS].set(bs.T)
    return pl.pallas_call(
        functools.partial(_sgu_kernel, nchunk=nchunk),
        grid=(n // t,),
        in_specs=[pl.BlockSpec((t, _W_C), lambda i: (i, 0)),
                  pl.BlockSpec((1, SGU_W), lambda i: (0, 0)),
                  pl.BlockSpec((1, SGU_W), lambda i: (0, 0)),
                  pl.BlockSpec((SGU_GROUPS, SGU_CHUNK, SGU_CHUNK), lambda i: (0, 0, 0)),
                  pl.BlockSpec((SGU_CHUNK, LANES), lambda i: (0, 0))],
        out_specs=pl.BlockSpec((t, SGU_W), lambda i: (i, 0)),
        out_shape=jax.ShapeDtypeStruct((n, SGU_W), BF16),
        compiler_params=_cparams(("parallel",)),
        name="sgu",
    )(yc, ln_g.reshape(1, SGU_W), ln_b.reshape(1, SGU_W), ws, bst)


def _rope_table_kernel(pos_ref, inv_ref, cos_ref, sin_ref):
    ang = pos_ref[...].astype(F32) * inv_ref[...]
    half = DIL_HEAD_DIM // 2
    sign = jnp.where((_iota2(ang.shape, 1) % DIL_HEAD_DIM) < half, -1.0, 1.0)
    cos_ref[...] = jnp.cos(ang)
    sin_ref[...] = jnp.sin(ang) * sign


def _rope_tables(positions):
    n = positions.size
    half = DIL_HEAD_DIM // 2
    inv = ROPE_THETA ** (-jnp.arange(half, dtype=F32) * 2.0 / DIL_HEAD_DIM)
    inv = jnp.tile(inv, LANES // half).reshape(1, LANES)
    t = 1024
    return pl.pallas_call(
        _rope_table_kernel,
        grid=(n // t,),
        in_specs=[pl.BlockSpec((t, 1), lambda i: (i, 0)),
                  pl.BlockSpec((1, LANES), lambda i: (0, 0))],
        out_specs=[pl.BlockSpec((t, LANES), lambda i: (i, 0)),
                   pl.BlockSpec((t, LANES), lambda i: (i, 0))],
        out_shape=[jax.ShapeDtypeStruct((n, LANES), F32)] * 2,
        compiler_params=_cparams(("parallel",)),
        name="rope_tables",
    )(positions.reshape(n, 1), inv)


def _dil_kernel(q0, q1, q2, k0, k1, k2, v0, v1, v2, cos_ref, sin_ref, o_ref,
                qs_ref, ks_ref, num_ref, m_ref, den_ref, *, seq):
    DH, BLK = DIL_HEAD_DIM, DIL_BLOCK
    half = DH // 2
    q_refs, k_refs, v_refs = (q0, q1, q2), (k0, k1, k2), (v0, v1, v2)
    cos = cos_ref[...]
    sin = sin_ref[...]
    first_half = (_iota2((seq, LANES), 1) % DH) < half

    def rope(x):
        swapped = jnp.where(first_half, pltpu.roll(x, LANES - half, 1), pltpu.roll(x, half, 1))
        return x * cos + swapped * sin

    for g in range(len(DIL_PAIRS)):
        qs_ref[g] = rope(q_refs[g][...]) * (DH ** -0.5)
        ks_ref[g] = rope(k_refs[g][...])

    ii = _iota2((BLK, BLK), 0)
    jj = _iota2((BLK, BLK), 1)
    mask_cur = jj <= ii
    mask_prev = jj >= ii

    for g, (window, dil) in enumerate(DIL_PAIRS):
        lsub = seq // dil
        nblk = lsub // BLK
        assert window // dil == BLK and lsub % BLK == 0
        v_ref = v_refs[g]

        def unit(u, carry, g=g, dil=dil, nblk=nblk, v_ref=v_ref):
            r = u % dil
            n = u // dil
            rows = pl.ds(n * (BLK * dil) + r, BLK, stride=dil)
            qb = qs_ref[g, rows, :]
            kc = ks_ref[g, rows, :]
            vc = v_ref[rows, :]
            if nblk > 1:
                prow = pl.ds(jnp.maximum(n - 1, 0) * (BLK * dil) + r, BLK, stride=dil)
                kp = ks_ref[g, prow, :]
                vp = v_ref[prow, :]
                has_prev = n > 0
            nums, ms, dens = [], [], []
            for h in range(LANES // DH):
                ls = slice(h * DH, (h + 1) * DH)
                s_c = lax.dot_general(qb[:, ls], kc[:, ls], (((1,), (1,)), ((), ())), preferred_element_type=F32)
                s_c = jnp.where(mask_cur, s_c, NEG_INF)
                mx = jnp.max(s_c, axis=-1, keepdims=True)
                if nblk > 1:
                    s_p = lax.dot_general(qb[:, ls], kp[:, ls], (((1,), (1,)), ((), ())), preferred_element_type=F32)
                    s_p = jnp.where(jnp.logical_and(mask_prev, has_prev), s_p, NEG_INF)
                    mx = jnp.maximum(mx, jnp.max(s_p, axis=-1, keepdims=True))
                p_c = jnp.exp(s_c - mx)
                den = jnp.sum(p_c, axis=-1, keepdims=True)
                num = jnp.dot(p_c, vc[:, ls], preferred_element_type=F32)
                if nblk > 1:
                    p_p = jnp.exp(s_p - mx)
                    den = den + jnp.sum(p_p, axis=-1, keepdims=True)
                    num = num + jnp.dot(p_p, vp[:, ls], preferred_element_type=F32)
                nums.append(num)
                ms.append(jnp.broadcast_to(mx, (BLK, DH)))
                dens.append(jnp.broadcast_to(den, (BLK, DH)))
            num_ref[g, rows, :] = jnp.concatenate(nums, axis=-1)
            m_ref[g, rows, :] = jnp.concatenate(ms, axis=-1)
            den_ref[g, rows, :] = jnp.concatenate(dens, axis=-1)
            return carry

        lax.fori_loop(0, dil * nblk, unit, 0)

    m_all = jnp.maximum(jnp.maximum(m_ref[0], m_ref[1]), m_ref[2])
    num = jnp.zeros((seq, LANES), F32)
    den = jnp.zeros((seq, LANES), F32)
    for g in range(len(DIL_PAIRS)):
        e = jnp.exp(m_ref[g] - m_all)
        num = num + e * num_ref[g]
        den = den + e * den_ref[g]
    o_ref[...] = (num / den).astype(o_ref.dtype)


def _dilated(yb, cos_t, sin_t, nbatch):
    n = yb.shape[0]
    seq = n // nbatch
    npair = DIL_HEADS_PER_GROUP * DIL_HEAD_DIM // LANES
    nblk_cols = DIL_W // LANES

    def spec(section, g):
        return pl.BlockSpec((seq, LANES), lambda b, p: (b, section * nblk_cols + g * npair + p))

    in_specs = ([spec(0, g) for g in range(3)] + [spec(1, g) for g in range(3)] + [spec(2, g) for g in range(3)]
                + [pl.BlockSpec((seq, LANES), lambda b, p: (b, 0))] * 2)
    return pl.pallas_call(
        functools.partial(_dil_kernel, seq=seq),
        grid=(nbatch, npair),
        in_specs=in_specs,
        out_specs=pl.BlockSpec((seq, LANES), lambda b, p: (b, p)),
        out_shape=jax.ShapeDtypeStruct((n, DIL_OUT), BF16),
        scratch_shapes=[pltpu.VMEM((3, seq, LANES), F32)] * 5,
        compiler_params=_cparams(("parallel", "parallel")),
        name="dilated_attn",
    )(*([yb] * 9), cos_t, sin_t)


def _merge_kernel(x_ref, g0, g1, g2, g3, ya, yb, yc, yd, pa, pb, pc, pd, o_ref):
    x = x_ref[...]
    acc = None
    for wg, y, p in ((g0, ya, pa), (g1, yb, pb), (g2, yc, pc), (g3, yd, pd)):
        gate = _sigmoid(jnp.dot(x, wg[...], preferred_element_type=F32))
        term = gate * jnp.dot(y[...], p[...], preferred_element_type=F32)
        acc = term if acc is None else acc + term
    o_ref[...] = acc.astype(o_ref.dtype)


def _merge(xb, wg, ys, ps, tm=1024, tn=512):
    n, d = xb.shape
    ncol = d // tn

    def gate_spec(br):
        return pl.BlockSpec((d, tn), lambda i, j: (0, br * ncol + j))

    in_specs = ([pl.BlockSpec((tm, d), lambda i, j: (i, 0))]
                + [gate_spec(br) for br in range(N_BRANCH)]
                + [pl.BlockSpec((tm, y.shape[1]), lambda i, j: (i, 0)) for y in ys]
                + [pl.BlockSpec((p.shape[0], tn), lambda i, j: (0, j)) for p in ps])
    return pl.pallas_call(
        _merge_kernel,
        grid=(n // tm, ncol),
        in_specs=in_specs,
        out_specs=pl.BlockSpec((tm, tn), lambda i, j: (i, j)),
        out_shape=jax.ShapeDtypeStruct((n, d), BF16),
        compiler_params=_cparams(("parallel", "arbitrary")),
        name="gated_merge",
    )(xb, wg, wg, wg, wg, *ys, *ps)


def _layernorm_rows(z, g, b):
    mu = jnp.mean(z, axis=-1, keepdims=True)
    var = jnp.mean(jnp.square(z - mu), axis=-1, keepdims=True)
    return (z - mu) * lax.rsqrt(var + LN_EPS) * g + b


def _pack_bf16_pairs(y):
    half = y.shape[1] // 2
    lo = lax.bitcast_convert_type(y[:, :half].astype(BF16).astype(F32), jnp.uint32)
    hi = lax.bitcast_convert_type(y[:, half:].astype(BF16).astype(F32), jnp.uint32)
    return (lo >> 16) | (hi & jnp.uint32(0xFFFF0000))


def _unpack_bf16_pairs(w):
    lo = lax.bitcast_convert_type(w << 16, F32).astype(BF16)
    hi = lax.bitcast_convert_type(w & jnp.uint32(0xFFFF0000), F32).astype(BF16)
    return lo, hi


def _outproj_ln_kernel(m_ref, w_ref, x_ref, g_ref, b_ref, o_ref, p_ref):
    h = jnp.dot(m_ref[...], w_ref[...], preferred_element_type=F32)
    y = _layernorm_rows(DN_ALPHA * x_ref[...] + h, g_ref[...], b_ref[...])
    o_ref[...] = y
    p_ref[...] = _pack_bf16_pairs(y)


def _outproj_ln(merged, w_out, x, g, b, tm=512):
    n, d = x.shape
    return pl.pallas_call(
        _outproj_ln_kernel,
        grid=(n // tm,),
        in_specs=[pl.BlockSpec((tm, d), lambda i: (i, 0)),
                  pl.BlockSpec((d, d), lambda i: (0, 0)),
                  pl.BlockSpec((tm, d), lambda i: (i, 0)),
                  pl.BlockSpec((1, d), lambda i: (0, 0)),
                  pl.BlockSpec((1, d), lambda i: (0, 0))],
        out_specs=[pl.BlockSpec((tm, d), lambda i: (i, 0)),
                   pl.BlockSpec((tm, d // 2), lambda i: (i, 0))],
        out_shape=[jax.ShapeDtypeStruct((n, d), F32), jax.ShapeDtypeStruct((n, d // 2), jnp.uint32)],
        compiler_params=_cparams(("parallel",)),
        name="outproj_ln",
    )(merged, w_out, x, g.reshape(1, d), b.reshape(1, d))


def _router_kernel(x_ref, wr_ref, rb_ref, idx_ref, wts_ref, rank_ref, cnt_ref, carry_ref, *, tr):
    gsz = N_EXPERTS // N_GROUPS

    @pl.when(pl.program_id(0) == 0)
    def _():
        carry_ref[...] = jnp.zeros_like(carry_ref)

    lane = _iota2((tr, LANES), 1)
    real = lane < N_EXPERTS
    logits = jnp.dot(x_ref[...], wr_ref[...], precision=HI, preferred_element_type=F32)
    scores = _sigmoid(logits)
    choice = jnp.where(real, scores + rb_ref[...], NEG_INF)

    def group_allreduce(v, op):
        s = 1
        while s < gsz:
            partner = jnp.where((lane & s) == 0, pltpu.roll(v, LANES - s, 1), pltpu.roll(v, s, 1))
            v = op(v, partner)
            s *= 2
        return v

    max1 = group_allreduce(choice, jnp.maximum)
    first = group_allreduce(jnp.where(choice == max1, lane, LANES), jnp.minimum)
    max2 = group_allreduce(jnp.where(lane == first, NEG_INF, choice), jnp.maximum)
    gscore = max1 + max2
    gid = lane // gsz
    gsel = jnp.zeros((tr, LANES), jnp.bool_)
    for _ in range(TOPK_GROUPS):
        gmax = jnp.max(gscore, axis=-1, keepdims=True)
        pick = jnp.min(jnp.where(gscore == gmax, gid, LANES), axis=-1, keepdims=True)
        hit = gid == pick
        gsel = jnp.logical_or(gsel, hit)
        gscore = jnp.where(hit, NEG_INF, gscore)

    cand = jnp.where(gsel, choice, NEG_INF)
    idx_out = jnp.zeros((tr, LANES), jnp.int32)
    w_out = jnp.zeros((tr, LANES), F32)
    sel = jnp.zeros((tr, LANES), F32)
    hits = []
    for k in range(TOP_K):
        vmax = jnp.max(cand, axis=-1, keepdims=True)
        pick = jnp.min(jnp.where(cand == vmax, lane, LANES), axis=-1, keepdims=True)
        hit = lane == pick
        hits.append(hit)
        wk = jnp.sum(jnp.where(hit, scores, 0.0), axis=-1, keepdims=True)
        idx_out = jnp.where(lane == k, pick, idx_out)
        w_out = jnp.where(lane == k, wk, w_out)
        sel = jnp.where(hit, 1.0, sel)
        cand = jnp.where(hit, NEG_INF, cand)
    w_out = w_out / jnp.sum(w_out, axis=-1, keepdims=True) * ROUTED_SCALE

    strict = (_iota2((tr, tr), 0) > _iota2((tr, tr), 1)).astype(BF16)
    rank = jnp.dot(strict, sel.astype(BF16), preferred_element_type=F32) + carry_ref[0:1, :]
    rank_out = jnp.zeros((tr, LANES), F32)
    for k in range(TOP_K):
        rk = jnp.sum(jnp.where(hits[k], rank, 0.0), axis=-1, keepdims=True)
        rank_out = jnp.where(lane == k, rk, rank_out)
    carry_ref[...] = carry_ref[...] + jnp.sum(sel, axis=0, keepdims=True)
    idx_ref[...] = idx_out
    wts_ref[...] = w_out
    rank_ref[...] = rank_out.astype(jnp.int32)
    cnt_ref[...] = carry_ref[...]


def _router(x1, w_router, router_bias, tr=256):
    n, d = x1.shape
    wr = jnp.zeros((d, LANES), F32).at[:, :N_EXPERTS].set(w_router)
    rb = jnp.zeros((1, LANES), F32).at[0, :N_EXPERTS].set(router_bias)
    row = pl.BlockSpec((tr, LANES), lambda i: (i, 0))
    return pl.pallas_call(
        functools.partial(_router_kernel, tr=tr),
        grid=(n // tr,),
        in_specs=[pl.BlockSpec((tr, d), lambda i: (i, 0)),
                  pl.BlockSpec((d, LANES), lambda i: (0, 0)),
                  pl.BlockSpec((1, LANES), lambda i: (0, 0))],
        out_specs=[row, row, row, pl.BlockSpec((8, LANES), lambda i: (0, 0))],
        out_shape=[jax.ShapeDtypeStruct((n, LANES), jnp.int32), jax.ShapeDtypeStruct((n, LANES), F32),
                   jax.ShapeDtypeStruct((n, LANES), jnp.int32), jax.ShapeDtypeStruct((8, LANES), F32)],
        scratch_shapes=[pltpu.VMEM((8, LANES), F32)],
        compiler_params=_cparams(("arbitrary",)),
        name="router",
    )(x1, wr, rb)


def _group_offsets(cnt, tm):
    padded = jnp.floor((cnt + (tm - 1)) * (1.0 / tm)) * tm
    upper = (_iota2((LANES, LANES), 0) < _iota2((LANES, LANES), 1)).astype(F32)
    offs = jnp.dot(padded, upper, precision=HI, preferred_element_type=F32)
    return padded, offs


def _pos_kernel(idx_ref, rank_ref, cnt_ref, pos_ref, *, tm):
    _, offs = _group_offsets(cnt_ref[...], tm)
    offs = offs[0:1, :]
    idx = idx_ref[...]
    lane = _iota2(idx.shape, 1)
    out = jnp.zeros(idx.shape, jnp.int32)
    for k in range(TOP_K):
        ok = jnp.sum(jnp.where(lane == idx[:, k:k + 1], offs, 0.0), axis=-1, keepdims=True)
        out = jnp.where(lane == k, ok.astype(jnp.int32), out)
    pos_ref[...] = out + rank_ref[...]


def _positions(idx, rank, cnt, tm, tp=1024):
    n = idx.shape[0]
    row = pl.BlockSpec((tp, LANES), lambda i: (i, 0))
    return pl.pallas_call(
        functools.partial(_pos_kernel, tm=tm),
        grid=(n // tp,),
        in_specs=[row, row, pl.BlockSpec((8, LANES), lambda i: (0, 0))],
        out_specs=row,
        out_shape=jax.ShapeDtypeStruct((n, LANES), jnp.int32),
        compiler_params=_cparams(("parallel",)),
        name="dispatch_positions",
    )(idx, rank, cnt)


def _tile_map_kernel(cnt_ref, te_ref, *, tm, width):
    padded, offs = _group_offsets(cnt_ref[...], tm)
    ends = _row_to_col((offs + padded)[0:1, :], LANES)
    expert = _iota2((LANES, width), 0)
    start = (_iota2((LANES, width), 1) * tm).astype(F32)
    done = jnp.logical_and(ends <= start, expert < N_EXPERTS)
    te = jnp.sum(jnp.where(done, 1, 0), axis=0, keepdims=True)
    total = jnp.max(jnp.where(expert < N_EXPERTS, ends, 0.0), axis=0, keepdims=True)
    ntile = (total * (1.0 / tm)).astype(jnp.int32)
    row = _iota2((8, width), 0)
    te_ref[...] = jnp.where(row == 0, te, jnp.where(row == 1, ntile, 0))


def _tile_map(cnt, tm, ntile_max):
    width = -(-ntile_max // LANES) * LANES
    out = pl.pallas_call(
        functools.partial(_tile_map_kernel, tm=tm, width=width),
        in_specs=[pl.BlockSpec((8, LANES), lambda: (0, 0))],
        out_specs=pl.BlockSpec((8, width), lambda: (0, 0)),
        out_shape=jax.ShapeDtypeStruct((8, width), jnp.int32),
        name="tile_map",
    )(cnt)
    return out[0, :ntile_max], out[1, :1]


def _dispatch_kernel(pos_ref, x_hbm, xs_in, xs_hbm, sem, *, td):
    del xs_in
    base = pl.program_id(0) * td
    per_row = LANES // TOP_K

    def row_copy(tok, dst):
        return pltpu.make_async_copy(x_hbm.at[pl.ds(tok, 1)], xs_hbm.at[pl.ds(dst, 1)], sem)

    def issue(i, c):
        for k in range(TOP_K):
            row_copy(base + i, pos_ref[i // per_row, (i % per_row) * TOP_K + k]).start()
        return c

    lax.fori_loop(0, td, issue, 0)

    def drain(i, c):
        row_copy(0, 0).wait()
        return c

    lax.fori_loop(0, td * TOP_K, drain, 0)


def _dispatch(pos2d, x1p, nrows, td=256):
    n, w = x1p.shape
    zeros = jnp.zeros((nrows, w), x1p.dtype)
    rows = td * TOP_K // LANES
    return pl.pallas_call(
        functools.partial(_dispatch_kernel, td=td),
        grid=(n // td,),
        in_specs=[pl.BlockSpec((rows, LANES), lambda i: (i, 0), memory_space=pltpu.SMEM),
                  pl.BlockSpec(memory_space=pl.ANY),
                  pl.BlockSpec(memory_space=pl.ANY)],
        out_specs=pl.BlockSpec(memory_space=pl.ANY),
        out_shape=jax.ShapeDtypeStruct((nrows, w), x1p.dtype),
        scratch_shapes=[pltpu.SemaphoreType.DMA(())],
        input_output_aliases={2: 0},
        compiler_params=_cparams(("arbitrary",)),
        name="moe_dispatch",
    )(pos2d, x1p, zeros)


def _ffn_kernel(te_ref, nt_ref, xs_ref, w1_ref, w3_ref, w2_ref, o_ref, w1b, w3b, w2b):
    i = pl.program_id(0)

    @pl.when(i < nt_ref[0])
    def _():
        e = te_ref[i]
        prev = te_ref[jnp.maximum(i - 1, 0)]

        @pl.when(jnp.logical_or(i == 0, e != prev))
        def _():
            w1b[...] = w1_ref[0].astype(BF16)
            w3b[...] = w3_ref[0].astype(BF16)
            w2b[...] = w2_ref[0].astype(BF16)

        lo, hi = _unpack_bf16_pairs(xs_ref[...])
        half = lo.shape[1]
        h1 = (jnp.dot(lo, w1b[:half, :], preferred_element_type=F32)
              + jnp.dot(hi, w1b[half:, :], preferred_element_type=F32))
        h3 = (jnp.dot(lo, w3b[:half, :], preferred_element_type=F32)
              + jnp.dot(hi, w3b[half:, :], preferred_element_type=F32))
        h = (_silu(h1) * h3).astype(BF16)
        o_ref[...] = jnp.dot(h, w2b[...], preferred_element_type=F32)

    @pl.when(i >= nt_ref[0])
    def _():
        o_ref[...] = jnp.zeros_like(o_ref)


def _expert_ffn(te, nt, xs, w1, w3, w2, tm):
    nrows, half = xs.shape
    d = 2 * half
    f = w1.shape[2]

    def tile(i, te_r, nt_r):
        return (jnp.minimum(i, nt_r[0] - 1), 0)

    def expert(i, te_r, nt_r):
        return (te_r[jnp.minimum(i, nt_r[0] - 1)], 0, 0)

    grid_spec = pltpu.PrefetchScalarGridSpec(
        num_scalar_prefetch=2,
        grid=(nrows // tm,),
        in_specs=[pl.BlockSpec((tm, half), tile),
                  pl.BlockSpec((1, d, f), expert),
                  pl.BlockSpec((1, d, f), expert),
                  pl.BlockSpec((1, f, d), expert)],
        out_specs=pl.BlockSpec((tm, d), lambda i, te_r, nt_r: (i, 0)),
        scratch_shapes=[pltpu.VMEM((d, f), BF16), pltpu.VMEM((d, f), BF16), pltpu.VMEM((f, d), BF16)],
    )
    return pl.pallas_call(
        _ffn_kernel,
        grid_spec=grid_spec,
        out_shape=jax.ShapeDtypeStruct((nrows, d), F32),
        compiler_params=_cparams(("arbitrary",)),
        name="expert_ffn",
    )(te, nt, xs, w1, w3, w2)


def _combine_kernel(pos_ref, ys_hbm, wts_ref, x_ref, ws1_ref, ws3_ref, ws2_ref, g_ref, b_ref,
                    o_ref, ob_ref, buf, sem, *, tc):
    per_row = LANES // TOP_K

    def row_copy(src, k, i):
        return pltpu.make_async_copy(ys_hbm.at[pl.ds(src, 1)], buf.at[k, pl.ds(i, 1)], sem)

    def issue(i, c):
        for k in range(TOP_K):
            row_copy(pos_ref[i // per_row, (i % per_row) * TOP_K + k], k, i).start()
        return c

    lax.fori_loop(0, tc, issue, 0)

    x = x_ref[...]
    xb = x.astype(BF16)
    s1 = jnp.dot(xb, ws1_ref[...], preferred_element_type=F32)
    s3 = jnp.dot(xb, ws3_ref[...], preferred_element_type=F32)
    acc = jnp.dot((_silu(s1) * s3).astype(BF16), ws2_ref[...], preferred_element_type=F32)

    def drain(i, c):
        row_copy(0, 0, 0).wait()
        return c

    lax.fori_loop(0, tc * TOP_K, drain, 0)

    wts = wts_ref[...]
    for k in range(TOP_K):
        acc = acc + wts[:, k:k + 1] * buf[k]
    y = _layernorm_rows(DN_ALPHA * x + acc, g_ref[...], b_ref[...])
    o_ref[...] = y
    ob_ref[...] = y.astype(BF16)


def _combine(pos2d, ys, wts, x1, ws1, ws3, ws2, g, b, tc=256):
    n, d = x1.shape
    sf = ws1.shape[1]
    rows = tc * TOP_K // LANES
    return pl.pallas_call(
        functools.partial(_combine_kernel, tc=tc),
        grid=(n // tc,),
        in_specs=[pl.BlockSpec((rows, LANES), lambda i: (i, 0), memory_space=pltpu.SMEM),
                  pl.BlockSpec(memory_space=pl.ANY),
                  pl.BlockSpec((tc, LANES), lambda i: (i, 0)),
                  pl.BlockSpec((tc, d), lambda i: (i, 0)),
                  pl.BlockSpec((d, sf), lambda i: (0, 0)),
                  pl.BlockSpec((d, sf), lambda i: (0, 0)),
                  pl.BlockSpec((sf, d), lambda i: (0, 0)),
                  pl.BlockSpec((1, d), lambda i: (0, 0)),
                  pl.BlockSpec((1, d), lambda i: (0, 0))],
        out_specs=[pl.BlockSpec((tc, d), lambda i: (i, 0)), pl.BlockSpec((tc, d), lambda i: (i, 0))],
        out_shape=[jax.ShapeDtypeStruct((n, d), F32), jax.ShapeDtypeStruct((n, d), BF16)],
        scratch_shapes=[pltpu.VMEM((TOP_K, tc, d), F32), pltpu.SemaphoreType.DMA(())],
        compiler_params=_cparams(("arbitrary",)),
        name="moe_combine",
    )(pos2d, ys, wts, x1, ws1, ws3, ws2, g.reshape(1, d), b.reshape(1, d))


FFN_TILE = 256


def _moe(x1, x1p, w_router, router_bias, w1, w3, w2, ws1, ws3, ws2, g, b):
    n = x1.shape[0]
    tm = FFN_TILE
    nrows = n * TOP_K + N_EXPERTS * tm
    idx, wts, rank, cnt = _router(x1, w_router, router_bias)
    pos = _positions(idx, rank, cnt, tm)
    te, nt = _tile_map(cnt, tm, nrows // tm)
    pos2d = pos[:, :TOP_K].reshape(n * TOP_K // LANES, LANES)
    xs = _dispatch(pos2d, x1p, nrows)
    ys = _expert_ffn(te, nt, xs, w1, w3, w2, tm)
    return _combine(pos2d, ys, wts, x1, ws1.astype(BF16), ws3.astype(BF16), ws2.astype(BF16), g, b)


def kernel(x, positions, w_in, gla_wa2, gla_ba, gla_norm, mlstm_conv_w, mlstm_conv_b, mlstm_bi, mlstm_bf,
           mlstm_norm, sgu_ln_g, sgu_ln_b, sgu_ws, sgu_bs, w_pa, w_pb, w_pc, w_pd, w_out, ln1_g, ln1_b,
           w_router, router_bias, w1, w3, w2, ws1, ws3, ws2, ln2_g, ln2_b):
    nbatch, seq, d = x.shape
    n = nbatch * seq
    xf = x.reshape(n, d)
    xb = xf.astype(BF16)
    cos_t, sin_t = _rope_tables(positions)
    for l in range(DEPTH):
        wl = w_in[l]
        w_a = wl[:, _OFF_A:_OFF_A + _W_A].astype(BF16)
        w_b = wl[:, _OFF_B:_OFF_B + _W_B].astype(BF16)
        w_c = wl[:, _OFF_C:_OFF_C + _W_C].astype(BF16)
        w_d = wl[:, _OFF_D:_OFF_D + _W_D].astype(BF16)
        w_g = wl[:, _OFF_G:_OFF_G + _W_G].astype(BF16)
        w_s = jnp.concatenate(
            [wl[:, _OFF_LOW:_OFF_LOW + GLA_RANK], wl[:, _OFF_IF:_OFF_IF + 2 * MLSTM_HEADS],
             jnp.zeros((d, LANES - GLA_RANK - 2 * MLSTM_HEADS), F32)], axis=1).astype(BF16)
        y_a = _matmul(xb, w_a, 1024, 512)
        y_b = _matmul(xb, w_b, 1024, 768)
        y_c = _matmul(xb, w_c, 1024, 512)
        y_d = _matmul(xb, w_d, 1024, 512)
        y_s = _matmul(xb, w_s, 1024, LANES)
        sm3 = y_s.reshape(nbatch, seq, LANES)
        o_a = _gla(y_a.reshape(nbatch, seq, _W_A), sm3, gla_wa2[l], gla_ba[l], gla_norm[l]).reshape(n, GLA_V)
        o_b = _dilated(y_b, cos_t, sin_t, nbatch)
        o_c = _sgu(y_c, sgu_ln_g[l], sgu_ln_b[l], sgu_ws[l], sgu_bs[l])
        o_d = _mlstm(y_d.reshape(nbatch, seq, _W_D), sm3, mlstm_conv_w[l], mlstm_conv_b[l], mlstm_bi[l],
                     mlstm_bf[l], mlstm_norm[l]).reshape(n, ML_W)
        merged = _merge(xb, w_g, (o_a, o_b, o_c, o_d),
                        (w_pa[l].astype(BF16), w_pb[l].astype(BF16), w_pc[l].astype(BF16), w_pd[l].astype(BF16)))
        x1, x1p = _outproj_ln(merged, w_out[l].astype(BF16), xf, ln1_g[l], ln1_b[l])
        xf, xb = _moe(x1, x1p, w_router[l], router_bias[l], w1[l], w3[l], w2[l], ws1[l], ws3[l], ws2[l],
                      ln2_g[l], ln2_b[l])
    return xf.reshape(nbatch, seq, d)
```

```python
import functools
import math

import jax
import jax.numpy as jnp
from jax import lax
from jax.experimental import pallas as pl
from jax.experimental.pallas import tpu as pltpu

D_MODEL = 2048
DEPTH = 2

GLA_HEADS = 4
GLA_DK = 64
GLA_DV = 128
GLA_RANK = 16
GLA_TAU = 16.0
GLA_CHUNK = 64
GLA_QK = GLA_HEADS * GLA_DK
GLA_V = GLA_HEADS * GLA_DV

DIL_PAIRS = ((128, 1), (512, 4), (2048, 16))
DIL_HEADS_PER_GROUP = 4
DIL_HEAD_DIM = 64
DIL_HEADS = len(DIL_PAIRS) * DIL_HEADS_PER_GROUP
DIL_W = DIL_HEADS * DIL_HEAD_DIM
DIL_OUT = DIL_HEADS_PER_GROUP * DIL_HEAD_DIM
DIL_BLOCK = 128
ROPE_THETA = 10000.0

SGU_CHUNK = 128
SGU_GROUPS = 6
SGU_GROUP_CH = 128
SGU_W = SGU_GROUPS * SGU_GROUP_CH

MLSTM_HEADS = 4
MLSTM_HEAD_DIM = 128
MLSTM_CHUNK = 64
MLSTM_CONV = 4
ML_W = MLSTM_HEADS * MLSTM_HEAD_DIM

N_EXPERTS = 64
TOP_K = 8
N_GROUPS = 8
TOPK_GROUPS = 4
EXPERT_FF = 512
SHARED_FF = 512
ROUTED_SCALE = 2.5

N_BRANCH = 4
DN_ALPHA = (2 * DEPTH) ** 0.25
LN_EPS = 1e-5

_OFF_A = 0
_W_A = 2 * GLA_QK + 2 * GLA_V
_OFF_LOW = _OFF_A + _W_A
_OFF_B = _OFF_LOW + GLA_RANK
_W_B = 3 * DIL_W
_OFF_C = _OFF_B + _W_B
_W_C = 2 * SGU_W
_OFF_D = _OFF_C + _W_C
_W_D = 4 * ML_W
_OFF_IF = _OFF_D + _W_D
_OFF_G = _OFF_IF + 2 * MLSTM_HEADS
_W_G = N_BRANCH * D_MODEL

LANES = 128
VMEM_LIMIT = 56 * 1024 * 1024

_SM_LOW = 0
_SM_I = GLA_RANK
_SM_F = GLA_RANK + MLSTM_HEADS

HI = lax.Precision.HIGHEST
F32 = jnp.float32
BF16 = jnp.bfloat16
NEG_INF = float("-inf")


def _cparams(sem):
    return pltpu.CompilerParams(dimension_semantics=sem, vmem_limit_bytes=VMEM_LIMIT)


def _log_sigmoid(x):
    return jnp.minimum(x, 0.0) - jnp.log1p(jnp.exp(-jnp.abs(x)))


def _sigmoid(x):
    return 1.0 / (1.0 + jnp.exp(-x))


def _silu(x):
    return x * _sigmoid(x)


def _iota2(shape, dim):
    return lax.broadcasted_iota(jnp.int32, shape, dim)


def _col_to_row(col, n):
    eye = _iota2((n, n), 0) == _iota2((n, n), 1)
    return jnp.sum(jnp.where(eye, col, 0.0), axis=0, keepdims=True)


def _row_to_col(row, n):
    eye = _iota2((n, n), 0) == _iota2((n, n), 1)
    return jnp.sum(jnp.where(eye, row, 0.0), axis=1, keepdims=True)


def _mm_kernel(x_ref, w_ref, o_ref):
    o_ref[...] = jnp.dot(x_ref[...], w_ref[...], preferred_element_type=F32).astype(o_ref.dtype)


def _matmul(x, w, tm, tn, out_dtype=F32):
    n, k = x.shape
    m = w.shape[1]
    return pl.pallas_call(
        _mm_kernel,
        grid=(n // tm, m // tn),
        in_specs=[pl.BlockSpec((tm, k), lambda i, j: (i, 0)),
                  pl.BlockSpec((k, tn), lambda i, j: (0, j))],
        out_specs=pl.BlockSpec((tm, tn), lambda i, j: (i, j)),
        out_shape=jax.ShapeDtypeStruct((n, m), out_dtype),
        compiler_params=_cparams(("parallel", "arbitrary")),
        name="in_proj",
    )(x, w)


def _gla_kernel(y_ref, sm_ref, wa2_ref, ba_ref, g_ref, o_ref, state_ref, *, nb):
    L, H, DK, DV = GLA_CHUNK, GLA_HEADS, GLA_DK, GLA_DV

    @pl.when(pl.program_id(0) == 0)
    def _():
        state_ref[...] = jnp.zeros_like(state_ref)

    tril = (_iota2((L, L), 0) >= _iota2((L, L), 1))
    tril_f = tril.astype(F32)
    for b in range(nb):
        y = y_ref[b]
        a_low = sm_ref[b][:, _SM_LOW:_SM_LOW + GLA_RANK]
        glog = jnp.dot(a_low, wa2_ref[...], preferred_element_type=F32) + ba_ref[...]
        g = _log_sigmoid(glog) * (1.0 / GLA_TAU)
        bc = jnp.dot(tril_f, g, precision=HI, preferred_element_type=F32)
        outs = []
        for h in range(H):
            q = y[:, h * DK:(h + 1) * DK] * (DK ** -0.5)
            k = y[:, GLA_QK + h * DK:GLA_QK + (h + 1) * DK]
            v = y[:, 2 * GLA_QK + h * DV:2 * GLA_QK + (h + 1) * DV]
            bh = bc[:, h * DK:(h + 1) * DK]
            qe = q * jnp.exp(bh)
            ke = k * jnp.exp(-bh)
            att = lax.dot_general(qe, ke, (((1,), (1,)), ((), ())), preferred_element_type=F32)
            att = jnp.where(tril, att, 0.0)
            st = state_ref[b * H + h]
            o = (jnp.dot(att, v, preferred_element_type=F32)
                 + jnp.dot(qe, st, preferred_element_type=F32))
            b_last = bh[L - 1:L, :]
            kd = k * jnp.exp(b_last - bh)
            decay = _row_to_col(jnp.exp(b_last), DK)
            state_ref[b * H + h] = decay * st + lax.dot_general(
                kd, v, (((0,), (0,)), ((), ())), preferred_element_type=F32)
            o = o * lax.rsqrt(jnp.mean(o * o, axis=-1, keepdims=True) + LN_EPS)
            outs.append(o)
        o_all = jnp.concatenate(outs, axis=-1) * g_ref[...]
        r = y[:, 2 * GLA_QK + GLA_V:2 * GLA_QK + 2 * GLA_V]
        o_ref[b] = (o_all * _silu(r)).astype(o_ref.dtype)


def _gla(ya3, sm3, wa2, ba, norm_g):
    nb, s, _ = ya3.shape
    L = GLA_CHUNK
    return pl.pallas_call(
        functools.partial(_gla_kernel, nb=nb),
        grid=(s // L,),
        in_specs=[pl.BlockSpec((nb, L, _W_A), lambda n: (0, n, 0)),
                  pl.BlockSpec((nb, L, LANES), lambda n: (0, n, 0)),
                  pl.BlockSpec((GLA_RANK, GLA_QK), lambda n: (0, 0)),
                  pl.BlockSpec((1, GLA_QK), lambda n: (0, 0)),
                  pl.BlockSpec((1, GLA_V), lambda n: (0, 0))],
        out_specs=pl.BlockSpec((nb, L, GLA_V), lambda n: (0, n, 0)),
        out_shape=jax.ShapeDtypeStruct((nb, s, GLA_V), BF16),
        scratch_shapes=[pltpu.VMEM((nb * GLA_HEADS, GLA_DK, GLA_DV), F32)],
        compiler_params=_cparams(("arbitrary",)),
        name="gla",
    )(ya3, sm3, wa2, ba.reshape(1, GLA_QK), norm_g.reshape(1, GLA_V))


def _mlstm_kernel(y_ref, sm_ref, cw_ref, cb_ref, gb_ref, g_ref, o_ref, c_ref, n_ref, m_ref, tail_ref, *, nb):
    L, H, DH = MLSTM_CHUNK, MLSTM_HEADS, MLSTM_HEAD_DIM
    W2 = 2 * ML_W
    HALO = 8

    @pl.when(pl.program_id(0) == 0)
    def _():
        c_ref[...] = jnp.zeros_like(c_ref)
        n_ref[...] = jnp.zeros_like(n_ref)
        m_ref[...] = jnp.zeros_like(m_ref)
        tail_ref[...] = jnp.zeros_like(tail_ref)

    tril = (_iota2((L, L), 0) >= _iota2((L, L), 1))
    tril_f = tril.astype(F32)
    for b in range(nb):
        y = y_ref[b]
        qk_raw = y[:, :W2]
        ext = jnp.concatenate([tail_ref[b], qk_raw], axis=0)
        tail_ref[b] = qk_raw[L - HALO:, :]
        conv = cb_ref[...]
        for j in range(MLSTM_CONV):
            s0 = HALO - (MLSTM_CONV - 1) + j
            conv = conv + cw_ref[j:j + 1, :] * ext[s0:s0 + L, :]
        qk = _silu(conv)
        gates = sm_ref[b] + gb_ref[...]
        bcum = jnp.dot(tril_f, _log_sigmoid(gates), precision=HI, preferred_element_type=F32)
        outs = []
        for h in range(H):
            q = qk[:, h * DH:(h + 1) * DH]
            k = qk[:, ML_W + h * DH:ML_W + (h + 1) * DH] * (DH ** -0.5)
            v = y[:, W2 + h * DH:W2 + (h + 1) * DH]
            b_col = bcum[:, _SM_F + h:_SM_F + h + 1]
            li_col = gates[:, _SM_I + h:_SM_I + h + 1]
            b_row = _col_to_row(b_col, L)
            li_row = _col_to_row(li_col, L)
            m_prev = m_ref[b * H + h][:, 0:1]
            dmat = jnp.where(tril, b_col - b_row + li_row, NEG_INF)
            inter = b_col + m_prev
            m_t = jnp.maximum(inter, jnp.max(dmat, axis=-1, keepdims=True))
            w_in = jnp.exp(dmat - m_t)
            w_st = jnp.exp(inter - m_t)
            sc = lax.dot_general(q, k, (((1,), (1,)), ((), ())), preferred_element_type=F32) * w_in
            cst = c_ref[b * H + h]
            nst = n_ref[b * H + h]
            num = (jnp.dot(sc, v, preferred_element_type=F32)
                   + w_st * jnp.dot(q, cst, preferred_element_type=F32))
            den = jnp.sum(sc, axis=-1, keepdims=True) + w_st * jnp.sum(q * nst, axis=-1, keepdims=True)
            hh = num / jnp.maximum(jnp.abs(den), jnp.exp(-m_t))
            b_last = b_col[L - 1:L, :]
            dec = b_last - b_col + li_col
            m_new = jnp.maximum(b_last + m_prev, jnp.max(dec, axis=0, keepdims=True))
            wk = jnp.exp(dec - m_new)
            keep = jnp.exp(b_last + m_prev - m_new)
            wkk = wk * k
            c_ref[b * H + h] = keep * cst + lax.dot_general(
                wkk, v, (((0,), (0,)), ((), ())), preferred_element_type=F32)
            n_ref[b * H + h] = keep * nst + jnp.sum(wkk, axis=0, keepdims=True)
            m_ref[b * H + h] = jnp.broadcast_to(m_new, (1, LANES))
            o_pre = y[:, W2 + ML_W + h * DH:W2 + ML_W + (h + 1) * DH]
            hh = _sigmoid(o_pre) * hh
            hh = hh * lax.rsqrt(jnp.mean(hh * hh, axis=-1, keepdims=True) + LN_EPS)
            outs.append(hh)
        o_ref[b] = (jnp.concatenate(outs, axis=-1) * g_ref[...]).astype(o_ref.dtype)


def _mlstm(yd3, sm3, conv_w, conv_b, b_i, b_f, norm_g):
    nb, s, _ = yd3.shape
    L = MLSTM_CHUNK
    gate_bias = jnp.zeros((1, LANES), F32)
    gate_bias = gate_bias.at[0, _SM_I:_SM_I + MLSTM_HEADS].set(b_i).at[0, _SM_F:_SM_F + MLSTM_HEADS].set(b_f)
    return pl.pallas_call(
        functools.partial(_mlstm_kernel, nb=nb),
        grid=(s // L,),
        in_specs=[pl.BlockSpec((nb, L, _W_D), lambda n: (0, n, 0)),
                  pl.BlockSpec((nb, L, LANES), lambda n: (0, n, 0)),
                  pl.BlockSpec((MLSTM_CONV, 2 * ML_W), lambda n: (0, 0)),
                  pl.BlockSpec((1, 2 * ML_W), lambda n: (0, 0)),
                  pl.BlockSpec((1, LANES), lambda n: (0, 0)),
                  pl.BlockSpec((1, ML_W), lambda n: (0, 0))],
        out_specs=pl.BlockSpec((nb, L, ML_W), lambda n: (0, n, 0)),
        out_shape=jax.ShapeDtypeStruct((nb, s, ML_W), BF16),
        scratch_shapes=[pltpu.VMEM((nb * MLSTM_HEADS, MLSTM_HEAD_DIM, MLSTM_HEAD_DIM), F32),
                        pltpu.VMEM((nb * MLSTM_HEADS, 1, MLSTM_HEAD_DIM), F32),
                        pltpu.VMEM((nb * MLSTM_HEADS, 1, LANES), F32),
                        pltpu.VMEM((nb, 8, 2 * ML_W), F32)],
        compiler_params=_cparams(("arbitrary",)),
        name="mlstm",
    )(yd3, sm3, conv_w, conv_b.reshape(1, 2 * ML_W), gate_bias, norm_g.reshape(1, ML_W))


def _gelu(x):
    return 0.5 * x * (1.0 + lax.erf(x * (0.5 ** 0.5)))


def _sgu_kernel(y_ref, lg_ref, lb_ref, ws_ref, bst_ref, o_ref, *, nchunk):
    C, G, GC = SGU_CHUNK, SGU_GROUPS, SGU_GROUP_CH
    y = y_ref[...]
    zu = _gelu(y[:, :SGU_W])
    zv = _gelu(y[:, SGU_W:])
    mu = jnp.mean(zv, axis=-1, keepdims=True)
    var = jnp.mean(jnp.square(zv - mu), axis=-1, keepdims=True)
    vn = (zv - mu) * lax.rsqrt(var + LN_EPS) * lg_ref[...] + lb_ref[...]
    tril = _iota2((C, C), 0) >= _iota2((C, C), 1)
    for g in range(G):
        wc = jnp.where(tril, ws_ref[g], 0.0)
        bias = bst_ref[:, g:g + 1]
        for c in range(nchunk):
            rows = slice(c * C, (c + 1) * C)
            cols = slice(g * GC, (g + 1) * GC)
            s = jnp.dot(wc, vn[rows, cols], preferred_element_type=F32) + bias
            o_ref[rows, cols] = (zu[rows, cols] * s).astype(o_ref.dtype)


def _sgu(yc, ln_g, ln_b, ws, bs, nchunk=2):
    n = yc.shape[0]
    t = nchunk * SGU_CHUNK
    bst = jnp.zeros((SGU_CHUNK, LANES), F32).at[:, :SGU_GROUPS].set(bs.T)
    return pl.pallas_call(
        functools.partial(_sgu_kernel, nchunk=nchunk),
        grid=(n // t,),
        in_specs=[pl.BlockSpec((t, _W_C), lambda i: (i, 0)),
                  pl.BlockSpec((1, SGU_W), lambda i: (0, 0)),
                  pl.BlockSpec((1, SGU_W), lambda i: (0, 0)),
                  pl.BlockSpec((SGU_GROUPS, SGU_CHUNK, SGU_CHUNK), lambda i: (0, 0, 0)),
                  pl.BlockSpec((SGU_CHUNK, LANES), lambda i: (0, 0))],
        out_specs=pl.BlockSpec((t, SGU_W), lambda i: (i, 0)),
        out_shape=jax.ShapeDtypeStruct((n, SGU_W), BF16),
        compiler_params=_cparams(("parallel",)),
        name="sgu",
    )(yc, ln_g.reshape(1, SGU_W), ln_b.reshape(1, SGU_W), ws, bst)


def _rope_table_kernel(pos_ref, inv_ref, cos_ref, sin_ref):
    ang = pos_ref[...].astype(F32) * inv_ref[...]
    half = DIL_HEAD_DIM // 2
    sign = jnp.where((_iota2(ang.shape, 1) % DIL_HEAD_DIM) < half, -1.0, 1.0)
    cos_ref[...] = jnp.cos(ang)
    sin_ref[...] = jnp.sin(ang) * sign


def _rope_tables(positions):
    n = positions.size
    half = DIL_HEAD_DIM // 2
    inv = ROPE_THETA ** (-jnp.arange(half, dtype=F32) * 2.0 / DIL_HEAD_DIM)
    inv = jnp.tile(inv, LANES // half).reshape(1, LANES)
    t = 1024
    return pl.pallas_call(
        _rope_table_kernel,
        grid=(n // t,),
        in_specs=[pl.BlockSpec((t, 1), lambda i: (i, 0)),
                  pl.BlockSpec((1, LANES), lambda i: (0, 0))],
        out_specs=[pl.BlockSpec((t, LANES), lambda i: (i, 0)),
                   pl.BlockSpec((t, LANES), lambda i: (i, 0))],
        out_shape=[jax.ShapeDtypeStruct((n, LANES), F32)] * 2,
        compiler_params=_cparams(("parallel",)),
        name="rope_tables",
    )(positions.reshape(n, 1), inv)


def _dil_kernel(q0, q1, q2, k0, k1, k2, v0, v1, v2, cos_ref, sin_ref, o_ref,
                qs_ref, ks_ref, num_ref, m_ref, den_ref, *, seq):
    DH, BLK = DIL_HEAD_DIM, DIL_BLOCK
    half = DH // 2
    q_refs, k_refs, v_refs = (q0, q1, q2), (k0, k1, k2), (v0, v1, v2)
    cos = cos_ref[...]
    sin = sin_ref[...]
    first_half = (_iota2((seq, LANES), 1) % DH) < half

    def rope(x):
        swapped = jnp.where(first_half, pltpu.roll(x, LANES - half, 1), pltpu.roll(x, half, 1))
        return x * cos + swapped * sin

    for g in range(len(DIL_PAIRS)):
        qs_ref[g] = rope(q_refs[g][...]) * (DH ** -0.5)
        ks_ref[g] = rope(k_refs[g][...])

    ii = _iota2((BLK, BLK), 0)
    jj = _iota2((BLK, BLK), 1)
    mask_cur = jj <= ii
    mask_prev = jj >= ii

    for g, (window, dil) in enumerate(DIL_PAIRS):
        lsub = seq // dil
        nblk = lsub // BLK
        assert window // dil == BLK and lsub % BLK == 0
        v_ref = v_refs[g]

        def unit(u, carry, g=g, dil=dil, nblk=nblk, v_ref=v_ref):
            r = u % dil
            n = u // dil
            rows = pl.ds(n * (BLK * dil) + r, BLK, stride=dil)
            qb = qs_ref[g, rows, :]
            kc = ks_ref[g, rows, :]
            vc = v_ref[rows, :]
            if nblk > 1:
                prow = pl.ds(jnp.maximum(n - 1, 0) * (BLK * dil) + r, BLK, stride=dil)
                kp = ks_ref[g, prow, :]
                vp = v_ref[prow, :]
                has_prev = n > 0
            nums, ms, dens = [], [], []
            for h in range(LANES // DH):
                ls = slice(h * DH, (h + 1) * DH)
                s_c = lax.dot_general(qb[:, ls], kc[:, ls], (((1,), (1,)), ((), ())), preferred_element_type=F32)
                s_c = jnp.where(mask_cur, s_c, NEG_INF)
                mx = jnp.max(s_c, axis=-1, keepdims=True)
                if nblk > 1:
                    s_p = lax.dot_general(qb[:, ls], kp[:, ls], (((1,), (1,)), ((), ())), preferred_element_type=F32)
                    s_p = jnp.where(jnp.logical_and(mask_prev, has_prev), s_p, NEG_INF)
                    mx = jnp.maximum(mx, jnp.max(s_p, axis=-1, keepdims=True))
                p_c = jnp.exp(s_c - mx)
                den = jnp.sum(p_c, axis=-1, keepdims=True)
                num = jnp.dot(p_c, vc[:, ls], preferred_element_type=F32)
                if nblk > 1:
                    p_p = jnp.exp(s_p - mx)
                    den = den + jnp.sum(p_p, axis=-1, keepdims=True)
                    num = num + jnp.dot(p_p, vp[:, ls], preferred_element_type=F32)
                nums.append(num)
                ms.append(jnp.broadcast_to(mx, (BLK, DH)))
                dens.append(jnp.broadcast_to(den, (BLK, DH)))
            num_ref[g, rows, :] = jnp.concatenate(nums, axis=-1)
            m_ref[g, rows, :] = jnp.concatenate(ms, axis=-1)
            den_ref[g, rows, :] = jnp.concatenate(dens, axis=-1)
            return carry

        lax.fori_loop(0, dil * nblk, unit, 0, unroll=4)

    m_all = jnp.maximum(jnp.maximum(m_ref[0], m_ref[1]), m_ref[2])
    num = jnp.zeros((seq, LANES), F32)
    den = jnp.zeros((seq, LANES), F32)
    for g in range(len(DIL_PAIRS)):
        e = jnp.exp(m_ref[g] - m_all)
        num = num + e * num_ref[g]
        den = den + e * den_ref[g]
    o_ref[...] = (num / den).astype(o_ref.dtype)


def _dilated(yb, cos_t, sin_t, nbatch):
    n = yb.shape[0]
    seq = n // nbatch
    npair = DIL_HEADS_PER_GROUP * DIL_HEAD_DIM // LANES
    nblk_cols = DIL_W // LANES

    def spec(section, g):
        return pl.BlockSpec((seq, LANES), lambda b, p: (b, section * nblk_cols + g * npair + p))

    in_specs = ([spec(0, g) for g in range(3)] + [spec(1, g) for g in range(3)] + [spec(2, g) for g in range(3)]
                + [pl.BlockSpec((seq, LANES), lambda b, p: (b, 0))] * 2)
    return pl.pallas_call(
        functools.partial(_dil_kernel, seq=seq),
        grid=(nbatch, npair),
        in_specs=in_specs,
        out_specs=pl.BlockSpec((seq, LANES), lambda b, p: (b, p)),
        out_shape=jax.ShapeDtypeStruct((n, DIL_OUT), BF16),
        scratch_shapes=[pltpu.VMEM((3, seq, LANES), F32)] * 5,
        compiler_params=_cparams(("parallel", "parallel")),
        name="dilated_attn",
    )(*([yb] * 9), cos_t, sin_t)


def _merge_kernel(x_ref, g0, g1, g2, g3, ya, yb, yc, yd, pa, pb, pc, pd, o_ref):
    x = x_ref[...]
    acc = None
    for wg, y, p in ((g0, ya, pa), (g1, yb, pb), (g2, yc, pc), (g3, yd, pd)):
        gate = _sigmoid(jnp.dot(x, wg[...], preferred_element_type=F32))
        term = gate * jnp.dot(y[...], p[...], preferred_element_type=F32)
        acc = term if acc is None else acc + term
    o_ref[...] = acc.astype(o_ref.dtype)


def _merge(xb, wg, ys, ps, tm=1024, tn=512):
    n, d = xb.shape
    ncol = d // tn

    def gate_spec(br):
        return pl.BlockSpec((d, tn), lambda i, j: (0, br * ncol + j))

    in_specs = ([pl.BlockSpec((tm, d), lambda i, j: (i, 0))]
                + [gate_spec(br) for br in range(N_BRANCH)]
                + [pl.BlockSpec((tm, y.shape[1]), lambda i, j: (i, 0)) for y in ys]
                + [pl.BlockSpec((p.shape[0], tn), lambda i, j: (0, j)) for p in ps])
    return pl.pallas_call(
        _merge_kernel,
        grid=(n // tm, ncol),
        in_specs=in_specs,
        out_specs=pl.BlockSpec((tm, tn), lambda i, j: (i, j)),
        out_shape=jax.ShapeDtypeStruct((n, d), BF16),
        compiler_params=_cparams(("parallel", "arbitrary")),
        name="gated_merge",
    )(xb, wg, wg, wg, wg, *ys, *ps)


def _layernorm_rows(z, g, b):
    mu = jnp.mean(z, axis=-1, keepdims=True)
    var = jnp.mean(jnp.square(z - mu), axis=-1, keepdims=True)
    return (z - mu) * lax.rsqrt(var + LN_EPS) * g + b


def _pack_bf16_pairs(y):
    half = y.shape[1] // 2
    lo = lax.bitcast_convert_type(y[:, :half].astype(BF16).astype(F32), jnp.uint32)
    hi = lax.bitcast_convert_type(y[:, half:].astype(BF16).astype(F32), jnp.uint32)
    return (lo >> 16) | (hi & jnp.uint32(0xFFFF0000))


def _unpack_bf16_pairs(w):
    lo = lax.bitcast_convert_type(w << 16, F32).astype(BF16)
    hi = lax.bitcast_convert_type(w & jnp.uint32(0xFFFF0000), F32).astype(BF16)
    return lo, hi


def _store_token_major(ref, val):
    t, w = val.shape
    c = w // LANES
    for s in range(c):
        ref[pl.ds(s, t, stride=c), :] = val[:, s * LANES:(s + 1) * LANES]


def _load_token_major(ref, start, t, c):
    return [ref[pl.ds(start + s, t, stride=c), :] for s in range(c)]


def _outproj_ln_kernel(m_ref, w_ref, x_ref, g_ref, b_ref, o_ref, p_ref):
    h = jnp.dot(m_ref[...], w_ref[...], preferred_element_type=F32)
    y = _layernorm_rows(DN_ALPHA * x_ref[...] + h, g_ref[...], b_ref[...])
    o_ref[...] = y
    _store_token_major(p_ref, _pack_bf16_pairs(y))


def _outproj_ln(merged, w_out, x, g, b, tm=512):
    n, d = x.shape
    c = d // 2 // LANES
    return pl.pallas_call(
        _outproj_ln_kernel,
        grid=(n // tm,),
        in_specs=[pl.BlockSpec((tm, d), lambda i: (i, 0)),
                  pl.BlockSpec((d, d), lambda i: (0, 0)),
                  pl.BlockSpec((tm, d), lambda i: (i, 0)),
                  pl.BlockSpec((1, d), lambda i: (0, 0)),
                  pl.BlockSpec((1, d), lambda i: (0, 0))],
        out_specs=[pl.BlockSpec((tm, d), lambda i: (i, 0)),
                   pl.BlockSpec((tm * c, LANES), lambda i: (i, 0))],
        out_shape=[jax.ShapeDtypeStruct((n, d), F32), jax.ShapeDtypeStruct((n * c, LANES), jnp.uint32)],
        compiler_params=_cparams(("parallel",)),
        name="outproj_ln",
    )(merged, w_out, x, g.reshape(1, d), b.reshape(1, d))


def _router_kernel(x_ref, wr_ref, rb_ref, idx_ref, wts_ref, rank_ref, cnt_ref, carry_ref, *, tr):
    gsz = N_EXPERTS // N_GROUPS

    @pl.when(pl.program_id(0) == 0)
    def _():
        carry_ref[...] = jnp.zeros_like(carry_ref)

    lane = _iota2((tr, LANES), 1)
    real = lane < N_EXPERTS
    logits = jnp.dot(x_ref[...], wr_ref[...], precision=HI, preferred_element_type=F32)
    scores = _sigmoid(logits)
    choice = jnp.where(real, scores + rb_ref[...], NEG_INF)

    def group_allreduce(v, op):
        s = 1
        while s < gsz:
            partner = jnp.where((lane & s) == 0, pltpu.roll(v, LANES - s, 1), pltpu.roll(v, s, 1))
            v = op(v, partner)
            s *= 2
        return v

    max1 = group_allreduce(choice, jnp.maximum)
    first = group_allreduce(jnp.where(choice == max1, lane, LANES), jnp.minimum)
    max2 = group_allreduce(jnp.where(lane == first, NEG_INF, choice), jnp.maximum)
    gscore = max1 + max2
    gid = lane // gsz
    gsel = jnp.zeros((tr, LANES), jnp.bool_)
    for _ in range(TOPK_GROUPS):
        gmax = jnp.max(gscore, axis=-1, keepdims=True)
        pick = jnp.min(jnp.where(gscore == gmax, gid, LANES), axis=-1, keepdims=True)
        hit = gid == pick
        gsel = jnp.logical_or(gsel, hit)
        gscore = jnp.where(hit, NEG_INF, gscore)

    cand = jnp.where(gsel, choice, NEG_INF)
    idx_out = jnp.zeros((tr, LANES), jnp.int32)
    w_out = jnp.zeros((tr, LANES), F32)
    sel = jnp.zeros((tr, LANES), F32)
    hits = []
    for k in range(TOP_K):
        vmax = jnp.max(cand, axis=-1, keepdims=True)
        pick = jnp.min(jnp.where(cand == vmax, lane, LANES), axis=-1, keepdims=True)
        hit = lane == pick
        hits.append(hit)
        wk = jnp.sum(jnp.where(hit, scores, 0.0), axis=-1, keepdims=True)
        idx_out = jnp.where(lane == k, pick, idx_out)
        w_out = jnp.where(lane == k, wk, w_out)
        sel = jnp.where(hit, 1.0, sel)
        cand = jnp.where(hit, NEG_INF, cand)
    w_out = w_out / jnp.sum(w_out, axis=-1, keepdims=True) * ROUTED_SCALE

    strict = (_iota2((tr, tr), 0) > _iota2((tr, tr), 1)).astype(BF16)
    rank = jnp.dot(strict, sel.astype(BF16), preferred_element_type=F32) + carry_ref[0:1, :]
    rank_out = jnp.zeros((tr, LANES), F32)
    for k in range(TOP_K):
        rk = jnp.sum(jnp.where(hits[k], rank, 0.0), axis=-1, keepdims=True)
        rank_out = jnp.where(lane == k, rk, rank_out)
    carry_ref[...] = carry_ref[...] + jnp.sum(sel, axis=0, keepdims=True)
    idx_ref[...] = idx_out
    wts_ref[...] = w_out
    rank_ref[...] = rank_out.astype(jnp.int32)
    cnt_ref[...] = carry_ref[...]


def _router(x1, w_router, router_bias, tr=512):
    n, d = x1.shape
    wr = jnp.zeros((d, LANES), F32).at[:, :N_EXPERTS].set(w_router)
    rb = jnp.zeros((1, LANES), F32).at[0, :N_EXPERTS].set(router_bias)
    row = pl.BlockSpec((tr, LANES), lambda i: (i, 0))
    return pl.pallas_call(
        functools.partial(_router_kernel, tr=tr),
        grid=(n // tr,),
        in_specs=[pl.BlockSpec((tr, d), lambda i: (i, 0)),
                  pl.BlockSpec((d, LANES), lambda i: (0, 0)),
                  pl.BlockSpec((1, LANES), lambda i: (0, 0))],
        out_specs=[row, row, row, pl.BlockSpec((8, LANES), lambda i: (0, 0))],
        out_shape=[jax.ShapeDtypeStruct((n, LANES), jnp.int32), jax.ShapeDtypeStruct((n, LANES), F32),
                   jax.ShapeDtypeStruct((n, LANES), jnp.int32), jax.ShapeDtypeStruct((8, LANES), F32)],
        scratch_shapes=[pltpu.VMEM((8, LANES), F32)],
        compiler_params=_cparams(("arbitrary",)),
        name="router",
    )(x1, wr, rb)


def _group_offsets(cnt, tm):
    padded = jnp.floor((cnt + (tm - 1)) * (1.0 / tm)) * tm
    upper = (_iota2((LANES, LANES), 0) < _iota2((LANES, LANES), 1)).astype(F32)
    offs = jnp.dot(padded, upper, precision=HI, preferred_element_type=F32)
    return padded, offs


def _pos_kernel(idx_ref, rank_ref, cnt_ref, pos_ref, *, tm):
    _, offs = _group_offsets(cnt_ref[...], tm)
    offs = offs[0:1, :]
    idx = idx_ref[...]
    lane = _iota2(idx.shape, 1)
    out = jnp.zeros(idx.shape, jnp.int32)
    for k in range(TOP_K):
        ok = jnp.sum(jnp.where(lane == idx[:, k:k + 1], offs, 0.0), axis=-1, keepdims=True)
        out = jnp.where(lane == k, ok.astype(jnp.int32), out)
    pos_ref[...] = out + rank_ref[...]


def _positions(idx, rank, cnt, tm, tp=1024):
    n = idx.shape[0]
    row = pl.BlockSpec((tp, LANES), lambda i: (i, 0))
    return pl.pallas_call(
        functools.partial(_pos_kernel, tm=tm),
        grid=(n // tp,),
        in_specs=[row, row, pl.BlockSpec((8, LANES), lambda i: (0, 0))],
        out_specs=row,
        out_shape=jax.ShapeDtypeStruct((n, LANES), jnp.int32),
        compiler_params=_cparams(("parallel",)),
        name="dispatch_positions",
    )(idx, rank, cnt)


def _tile_map_kernel(cnt_ref, te_ref, *, tm, width):
    padded, offs = _group_offsets(cnt_ref[...], tm)
    ends = _row_to_col((offs + padded)[0:1, :], LANES)
    expert = _iota2((LANES, width), 0)
    start = (_iota2((LANES, width), 1) * tm).astype(F32)
    done = jnp.logical_and(ends <= start, expert < N_EXPERTS)
    te = jnp.sum(jnp.where(done, 1, 0), axis=0, keepdims=True)
    total = jnp.max(jnp.where(expert < N_EXPERTS, ends, 0.0), axis=0, keepdims=True)
    ntile = (total * (1.0 / tm)).astype(jnp.int32)
    row = _iota2((8, width), 0)
    te_ref[...] = jnp.where(row == 0, te, jnp.where(row == 1, ntile, 0))


def _tile_map(cnt, tm, ntile_max):
    width = -(-ntile_max // LANES) * LANES
    out = pl.pallas_call(
        functools.partial(_tile_map_kernel, tm=tm, width=width),
        in_specs=[pl.BlockSpec((8, LANES), lambda: (0, 0))],
        out_specs=pl.BlockSpec((8, width), lambda: (0, 0)),
        out_shape=jax.ShapeDtypeStruct((8, width), jnp.int32),
        name="tile_map",
    )(cnt)
    return out[0, :ntile_max], out[1, :1]


def _dispatch_kernel(pos_ref, x_ref, xs_in, xs_hbm, sem, *, td, c):
    del xs_in
    per_row = LANES // TOP_K

    def row_copy(i, dst):
        return pltpu.make_async_copy(x_ref.at[pl.ds(pl.multiple_of(i * c, c), c)],
                                     xs_hbm.at[pl.ds(pl.multiple_of(dst * c, c), c)], sem)

    def issue(i, carry):
        for k in range(TOP_K):
            row_copy(i, pos_ref[i // per_row, (i % per_row) * TOP_K + k]).start()
        return carry

    lax.fori_loop(0, td, issue, 0)

    def drain(i, carry):
        row_copy(0, 0).wait()
        return carry

    lax.fori_loop(0, td * TOP_K, drain, 0)


def _dispatch(pos2d, x1p, n, nrows, td=256):
    c = x1p.shape[0] // n
    zeros = jnp.zeros((nrows * c, LANES), x1p.dtype)
    rows = td * TOP_K // LANES
    return pl.pallas_call(
        functools.partial(_dispatch_kernel, td=td, c=c),
        grid=(n // td,),
        in_specs=[pl.BlockSpec((rows, LANES), lambda i: (i, 0), memory_space=pltpu.SMEM),
                  pl.BlockSpec((td * c, LANES), lambda i: (i, 0)),
                  pl.BlockSpec(memory_space=pl.ANY)],
        out_specs=pl.BlockSpec(memory_space=pl.ANY),
        out_shape=jax.ShapeDtypeStruct((nrows * c, LANES), x1p.dtype),
        scratch_shapes=[pltpu.SemaphoreType.DMA(())],
        input_output_aliases={2: 0},
        compiler_params=_cparams(("arbitrary",)),
        name="moe_dispatch",
    )(pos2d, x1p, zeros)


def _ffn_kernel(te_ref, nt_ref, xs_ref, w1_ref, w3_ref, w2_ref, o_ref, w1b, w3b, w2b, *, tm):
    i = pl.program_id(0)

    @pl.when(i < nt_ref[0])
    def _():
        e = te_ref[i]
        prev = te_ref[jnp.maximum(i - 1, 0)]

        @pl.when(jnp.logical_or(i == 0, e != prev))
        def _():
            w1b[...] = w1_ref[0, 0].astype(BF16)
            w3b[...] = w3_ref[0, 0].astype(BF16)
            w2b[...] = w2_ref[0, 0].astype(BF16)

        c = xs_ref.shape[0] // tm
        parts = [_unpack_bf16_pairs(p) for p in _load_token_major(xs_ref, 0, tm, c)]
        lo = jnp.concatenate([p[0] for p in parts], axis=1)
        hi = jnp.concatenate([p[1] for p in parts], axis=1)
        half = lo.shape[1]
        h1 = (jnp.dot(lo, w1b[:half, :], preferred_element_type=F32)
              + jnp.dot(hi, w1b[half:, :], preferred_element_type=F32))
        h3 = (jnp.dot(lo, w3b[:half, :], preferred_element_type=F32)
              + jnp.dot(hi, w3b[half:, :], preferred_element_type=F32))
        h = (_silu(h1) * h3).astype(BF16)
        _store_token_major(o_ref, jnp.dot(h, w2b[...], preferred_element_type=F32))

    @pl.when(i >= nt_ref[0])
    def _():
        o_ref[...] = jnp.zeros_like(o_ref)


def _expert_ffn(te, nt, xs, w1, w3, w2, layer, nrows, tm):
    d, f = w1.shape[2], w1.shape[3]
    c = xs.shape[0] // nrows
    c2 = d // LANES

    def tile(i, te_r, nt_r):
        return (jnp.minimum(i, nt_r[0] - 1), 0)

    def expert(i, te_r, nt_r):
        return (layer, te_r[jnp.minimum(i, nt_r[0] - 1)], 0, 0)

    grid_spec = pltpu.PrefetchScalarGridSpec(
        num_scalar_prefetch=2,
        grid=(nrows // tm,),
        in_specs=[pl.BlockSpec((tm * c, LANES), tile),
                  pl.BlockSpec((1, 1, d, f), expert),
                  pl.BlockSpec((1, 1, d, f), expert),
                  pl.BlockSpec((1, 1, f, d), expert)],
        out_specs=pl.BlockSpec((tm * c2, LANES), lambda i, te_r, nt_r: (i, 0)),
        scratch_shapes=[pltpu.VMEM((d, f), BF16), pltpu.VMEM((d, f), BF16), pltpu.VMEM((f, d), BF16)],
    )
    return pl.pallas_call(
        functools.partial(_ffn_kernel, tm=tm),
        grid_spec=grid_spec,
        out_shape=jax.ShapeDtypeStruct((nrows * c2, LANES), F32),
        compiler_params=_cparams(("arbitrary",)),
        name="expert_ffn",
    )(te, nt, xs, w1, w3, w2)


def _combine_kernel(pos_ref, ys_hbm, wts_ref, x_ref, ws1_ref, ws3_ref, ws2_ref, g_ref, b_ref,
                    o_ref, ob_ref, buf, sem, *, tc):
    per_row = LANES // TOP_K
    c2 = x_ref.shape[1] // LANES

    def row_copy(src, k, i):
        return pltpu.make_async_copy(ys_hbm.at[pl.ds(pl.multiple_of(src * c2, c2), c2)],
                                     buf.at[pl.ds(pl.multiple_of((k * tc + i) * c2, c2), c2)], sem)

    def issue(i, carry):
        for k in range(TOP_K):
            row_copy(pos_ref[i // per_row, (i % per_row) * TOP_K + k], k, i).start()
        return carry

    lax.fori_loop(0, tc, issue, 0)

    x = x_ref[...]
    xb = x.astype(BF16)
    s1 = jnp.dot(xb, ws1_ref[...], preferred_element_type=F32)
    s3 = jnp.dot(xb, ws3_ref[...], preferred_element_type=F32)
    acc = jnp.dot((_silu(s1) * s3).astype(BF16), ws2_ref[...], preferred_element_type=F32)

    def drain(i, carry):
        row_copy(0, 0, 0).wait()
        return carry

    lax.fori_loop(0, tc * TOP_K, drain, 0)

    wts = wts_ref[...]
    for k in range(TOP_K):
        yk = jnp.concatenate(_load_token_major(buf, k * tc * c2, tc, c2), axis=1)
        acc = acc + wts[:, k:k + 1] * yk
    y = _layernorm_rows(DN_ALPHA * x + acc, g_ref[...], b_ref[...])
    o_ref[...] = y
    ob_ref[...] = y.astype(BF16)


def _combine(pos2d, ys, wts, x1, ws1, ws3, ws2, g, b, tc=256):
    n, d = x1.shape
    sf = ws1.shape[1]
    rows = tc * TOP_K // LANES
    return pl.pallas_call(
        functools.partial(_combine_kernel, tc=tc),
        grid=(n // tc,),
        in_specs=[pl.BlockSpec((rows, LANES), lambda i: (i, 0), memory_space=pltpu.SMEM),
                  pl.BlockSpec(memory_space=pl.ANY),
                  pl.BlockSpec((tc, LANES), lambda i: (i, 0)),
                  pl.BlockSpec((tc, d), lambda i: (i, 0)),
                  pl.BlockSpec((d, sf), lambda i: (0, 0)),
                  pl.BlockSpec((d, sf), lambda i: (0, 0)),
                  pl.BlockSpec((sf, d), lambda i: (0, 0)),
                  pl.BlockSpec((1, d), lambda i: (0, 0)),
                  pl.BlockSpec((1, d), lambda i: (0, 0))],
        out_specs=[pl.BlockSpec((tc, d), lambda i: (i, 0)), pl.BlockSpec((tc, d), lambda i: (i, 0))],
        out_shape=[jax.ShapeDtypeStruct((n, d), F32), jax.ShapeDtypeStruct((n, d), BF16)],
        scratch_shapes=[pltpu.VMEM((TOP_K * tc * (d // LANES), LANES), F32), pltpu.SemaphoreType.DMA(())],
        compiler_params=_cparams(("arbitrary",)),
        name="moe_combine",
    )(pos2d, ys, wts, x1, ws1, ws3, ws2, g.reshape(1, d), b.reshape(1, d))


FFN_TILE = 256


def _moe(x1, x1p, w_router, router_bias, w1, w3, w2, layer, ws1, ws3, ws2, g, b):
    n = x1.shape[0]
    tm = FFN_TILE
    nrows = n * TOP_K + N_EXPERTS * tm
    idx, wts, rank, cnt = _router(x1, w_router, router_bias)
    pos = _positions(idx, rank, cnt, tm)
    te, nt = _tile_map(cnt, tm, nrows // tm)
    pos2d = pos[:, :TOP_K].reshape(n * TOP_K // LANES, LANES)
    xs = _dispatch(pos2d, x1p, n, nrows)
    ys = _expert_ffn(te, nt, xs, w1, w3, w2, layer, nrows, tm)
    return _combine(pos2d, ys, wts, x1, ws1.astype(BF16), ws3.astype(BF16), ws2.astype(BF16), g, b)


def kernel(x, positions, w_in, gla_wa2, gla_ba, gla_norm, mlstm_conv_w, mlstm_conv_b, mlstm_bi, mlstm_bf,
           mlstm_norm, sgu_ln_g, sgu_ln_b, sgu_ws, sgu_bs, w_pa, w_pb, w_pc, w_pd, w_out, ln1_g, ln1_b,
           w_router, router_bias, w1, w3, w2, ws1, ws3, ws2, ln2_g, ln2_b):
    nbatch, seq, d = x.shape
    n = nbatch * seq
    xf = x.reshape(n, d)
    xb = xf.astype(BF16)
    cos_t, sin_t = _rope_tables(positions)
    for l in range(DEPTH):
        wl = w_in[l]
        w_a = wl[:, _OFF_A:_OFF_A + _W_A].astype(BF16)
        w_b = wl[:, _OFF_B:_OFF_B + _W_B].astype(BF16)
        w_c = wl[:, _OFF_C:_OFF_C + _W_C].astype(BF16)
        w_d = wl[:, _OFF_D:_OFF_D + _W_D].astype(BF16)
        w_g = wl[:, _OFF_G:_OFF_G + _W_G].astype(BF16)
        w_s = jnp.concatenate(
            [wl[:, _OFF_LOW:_OFF_LOW + GLA_RANK], wl[:, _OFF_IF:_OFF_IF + 2 * MLSTM_HEADS],
             jnp.zeros((d, LANES - GLA_RANK - 2 * MLSTM_HEADS), F32)], axis=1).astype(BF16)
        y_a = _matmul(xb, w_a, 1024, 512)
        y_b = _matmul(xb, w_b, 1024, 768)
        y_c = _matmul(xb, w_c, 1024, 512)
        y_d = _matmul(xb, w_d, 1024, 512)
        y_s = _matmul(xb, w_s, 1024, LANES)
        sm3 = y_s.reshape(nbatch, seq, LANES)
        o_a = _gla(y_a.reshape(nbatch, seq, _W_A), sm3, gla_wa2[l], gla_ba[l], gla_norm[l]).reshape(n, GLA_V)
        o_b = _dilated(y_b, cos_t, sin_t, nbatch)
        o_c = _sgu(y_c, sgu_ln_g[l], sgu_ln_b[l], sgu_ws[l], sgu_bs[l])
        o_d = _mlstm(y_d.reshape(nbatch, seq, _W_D), sm3, mlstm_conv_w[l], mlstm_conv_b[l], mlstm_bi[l],
                     mlstm_bf[l], mlstm_norm[l]).reshape(n, ML_W)
        merged = _merge(xb, w_g, (o_a, o_b, o_c, o_d),
                        (w_pa[l].astype(BF16), w_pb[l].astype(BF16), w_pc[l].astype(BF16), w_pd[l].astype(BF16)))
        x1, x1p = _outproj_ln(merged, w_out[l].astype(BF16), xf, ln1_g[l], ln1_b[l])
        xf, xb = _moe(x1, x1p, w_router[l], router_bias[l], w1, w3, w2, l, ws1[l], ws3[l], ws2[l],
                      ln2_g[l], ln2_b[l])
    return xf.reshape(nbatch, seq, d)
```

```python
import functools
import math

import jax
import jax.numpy as jnp
from jax import lax
from jax.experimental import pallas as pl
from jax.experimental.pallas import tpu as pltpu
from jax.experimental.pallas import tpu_sc as plsc

D_MODEL = 2048
DEPTH = 2

GLA_HEADS = 4
GLA_DK = 64
GLA_DV = 128
GLA_RANK = 16
GLA_TAU = 16.0
GLA_CHUNK = 64
GLA_QK = GLA_HEADS * GLA_DK
GLA_V = GLA_HEADS * GLA_DV

DIL_PAIRS = ((128, 1), (512, 4), (2048, 16))
DIL_HEADS_PER_GROUP = 4
DIL_HEAD_DIM = 64
DIL_HEADS = len(DIL_PAIRS) * DIL_HEADS_PER_GROUP
DIL_W = DIL_HEADS * DIL_HEAD_DIM
DIL_OUT = DIL_HEADS_PER_GROUP * DIL_HEAD_DIM
DIL_BLOCK = 128
ROPE_THETA = 10000.0

SGU_CHUNK = 128
SGU_GROUPS = 6
SGU_GROUP_CH = 128
SGU_W = SGU_GROUPS * SGU_GROUP_CH

MLSTM_HEADS = 4
MLSTM_HEAD_DIM = 128
MLSTM_CHUNK = 64
MLSTM_CONV = 4
ML_W = MLSTM_HEADS * MLSTM_HEAD_DIM

N_EXPERTS = 64
TOP_K = 8
N_GROUPS = 8
TOPK_GROUPS = 4
EXPERT_FF = 512
SHARED_FF = 512
ROUTED_SCALE = 2.5

N_BRANCH = 4
DN_ALPHA = (2 * DEPTH) ** 0.25
LN_EPS = 1e-5

_OFF_A = 0
_W_A = 2 * GLA_QK + 2 * GLA_V
_OFF_LOW = _OFF_A + _W_A
_OFF_B = _OFF_LOW + GLA_RANK
_W_B = 3 * DIL_W
_OFF_C = _OFF_B + _W_B
_W_C = 2 * SGU_W
_OFF_D = _OFF_C + _W_C
_W_D = 4 * ML_W
_OFF_IF = _OFF_D + _W_D
_OFF_G = _OFF_IF + 2 * MLSTM_HEADS
_W_G = N_BRANCH * D_MODEL

LANES = 128
VMEM_LIMIT = 56 * 1024 * 1024

_SM_LOW = 0
_SM_I = GLA_RANK
_SM_F = GLA_RANK + MLSTM_HEADS

HI = lax.Precision.HIGHEST
F32 = jnp.float32
BF16 = jnp.bfloat16
NEG_INF = float("-inf")


def _cparams(sem):
    return pltpu.CompilerParams(dimension_semantics=sem, vmem_limit_bytes=VMEM_LIMIT)


def _log_sigmoid(x):
    return jnp.minimum(x, 0.0) - jnp.log1p(jnp.exp(-jnp.abs(x)))


def _sigmoid(x):
    return 1.0 / (1.0 + jnp.exp(-x))


def _silu(x):
    return x * _sigmoid(x)


def _iota2(shape, dim):
    return lax.broadcasted_iota(jnp.int32, shape, dim)


def _col_to_row(col, n):
    eye = _iota2((n, n), 0) == _iota2((n, n), 1)
    return jnp.sum(jnp.where(eye, col, 0.0), axis=0, keepdims=True)


def _row_to_col(row, n):
    eye = _iota2((n, n), 0) == _iota2((n, n), 1)
    return jnp.sum(jnp.where(eye, row, 0.0), axis=1, keepdims=True)


def _mm_kernel(x_ref, w_ref, o_ref):
    o_ref[...] = jnp.dot(x_ref[...], w_ref[...], preferred_element_type=F32).astype(o_ref.dtype)


def _matmul(x, w, tm, tn, out_dtype=F32):
    n, k = x.shape
    m = w.shape[1]
    return pl.pallas_call(
        _mm_kernel,
        grid=(n // tm, m // tn),
        in_specs=[pl.BlockSpec((tm, k), lambda i, j: (i, 0)),
                  pl.BlockSpec((k, tn), lambda i, j: (0, j))],
        out_specs=pl.BlockSpec((tm, tn), lambda i, j: (i, j)),
        out_shape=jax.ShapeDtypeStruct((n, m), out_dtype),
        compiler_params=_cparams(("parallel", "arbitrary")),
        name="in_proj",
    )(x, w)


def _gla_kernel(y_ref, sm_ref, wa2_ref, ba_ref, g_ref, o_ref, state_ref, *, nb):
    L, H, DK, DV = GLA_CHUNK, GLA_HEADS, GLA_DK, GLA_DV

    @pl.when(pl.program_id(0) == 0)
    def _():
        state_ref[...] = jnp.zeros_like(state_ref)

    tril = (_iota2((L, L), 0) >= _iota2((L, L), 1))
    tril_f = tril.astype(F32)
    for b in range(nb):
        y = y_ref[b]
        a_low = sm_ref[b][:, _SM_LOW:_SM_LOW + GLA_RANK]
        glog = jnp.dot(a_low, wa2_ref[...], preferred_element_type=F32) + ba_ref[...]
        g = _log_sigmoid(glog) * (1.0 / GLA_TAU)
        bc = jnp.dot(tril_f, g, precision=HI, preferred_element_type=F32)
        outs = []
        for h in range(H):
            q = y[:, h * DK:(h + 1) * DK] * (DK ** -0.5)
            k = y[:, GLA_QK + h * DK:GLA_QK + (h + 1) * DK]
            v = y[:, 2 * GLA_QK + h * DV:2 * GLA_QK + (h + 1) * DV]
            bh = bc[:, h * DK:(h + 1) * DK]
            qe = q * jnp.exp(bh)
            ke = k * jnp.exp(-bh)
            att = lax.dot_general(qe, ke, (((1,), (1,)), ((), ())), preferred_element_type=F32)
            att = jnp.where(tril, att, 0.0)
            st = state_ref[b * H + h]
            o = (jnp.dot(att, v, preferred_element_type=F32)
                 + jnp.dot(qe, st, preferred_element_type=F32))
            b_last = bh[L - 1:L, :]
            kd = k * jnp.exp(b_last - bh)
            decay = _row_to_col(jnp.exp(b_last), DK)
            state_ref[b * H + h] = decay * st + lax.dot_general(
                kd, v, (((0,), (0,)), ((), ())), preferred_element_type=F32)
            o = o * lax.rsqrt(jnp.mean(o * o, axis=-1, keepdims=True) + LN_EPS)
            outs.append(o)
        o_all = jnp.concatenate(outs, axis=-1) * g_ref[...]
        r = y[:, 2 * GLA_QK + GLA_V:2 * GLA_QK + 2 * GLA_V]
        o_ref[b] = (o_all * _silu(r)).astype(o_ref.dtype)


def _gla(ya3, sm3, wa2, ba, norm_g):
    nb, s, _ = ya3.shape
    L = GLA_CHUNK
    return pl.pallas_call(
        functools.partial(_gla_kernel, nb=nb),
        grid=(s // L,),
        in_specs=[pl.BlockSpec((nb, L, _W_A), lambda n: (0, n, 0)),
                  pl.BlockSpec((nb, L, LANES), lambda n: (0, n, 0)),
                  pl.BlockSpec((GLA_RANK, GLA_QK), lambda n: (0, 0)),
                  pl.BlockSpec((1, GLA_QK), lambda n: (0, 0)),
                  pl.BlockSpec((1, GLA_V), lambda n: (0, 0))],
        out_specs=pl.BlockSpec((nb, L, GLA_V), lambda n: (0, n, 0)),
        out_shape=jax.ShapeDtypeStruct((nb, s, GLA_V), BF16),
        scratch_shapes=[pltpu.VMEM((nb * GLA_HEADS, GLA_DK, GLA_DV), F32)],
        compiler_params=_cparams(("arbitrary",)),
        name="gla",
    )(ya3, sm3, wa2, ba.reshape(1, GLA_QK), norm_g.reshape(1, GLA_V))


def _mlstm_kernel(y_ref, sm_ref, cw_ref, cb_ref, gb_ref, g_ref, o_ref, c_ref, n_ref, m_ref, tail_ref, *, nb):
    L, H, DH = MLSTM_CHUNK, MLSTM_HEADS, MLSTM_HEAD_DIM
    W2 = 2 * ML_W
    HALO = 8

    @pl.when(pl.program_id(0) == 0)
    def _():
        c_ref[...] = jnp.zeros_like(c_ref)
        n_ref[...] = jnp.zeros_like(n_ref)
        m_ref[...] = jnp.zeros_like(m_ref)
        tail_ref[...] = jnp.zeros_like(tail_ref)

    tril = (_iota2((L, L), 0) >= _iota2((L, L), 1))
    tril_f = tril.astype(F32)
    for b in range(nb):
        y = y_ref[b]
        qk_raw = y[:, :W2]
        ext = jnp.concatenate([tail_ref[b], qk_raw], axis=0)
        tail_ref[b] = qk_raw[L - HALO:, :]
        conv = cb_ref[...]
        for j in range(MLSTM_CONV):
            s0 = HALO - (MLSTM_CONV - 1) + j
            conv = conv + cw_ref[j:j + 1, :] * ext[s0:s0 + L, :]
        qk = _silu(conv)
        gates = sm_ref[b] + gb_ref[...]
        bcum = jnp.dot(tril_f, _log_sigmoid(gates), precision=HI, preferred_element_type=F32)
        outs = []
        for h in range(H):
            q = qk[:, h * DH:(h + 1) * DH]
            k = qk[:, ML_W + h * DH:ML_W + (h + 1) * DH] * (DH ** -0.5)
            v = y[:, W2 + h * DH:W2 + (h + 1) * DH]
            b_col = bcum[:, _SM_F + h:_SM_F + h + 1]
            li_col = gates[:, _SM_I + h:_SM_I + h + 1]
            b_row = _col_to_row(b_col, L)
            li_row = _col_to_row(li_col, L)
            m_prev = m_ref[b * H + h][:, 0:1]
            dmat = jnp.where(tril, b_col - b_row + li_row, NEG_INF)
            inter = b_col + m_prev
            m_t = jnp.maximum(inter, jnp.max(dmat, axis=-1, keepdims=True))
            w_in = jnp.exp(dmat - m_t)
            w_st = jnp.exp(inter - m_t)
            sc = lax.dot_general(q, k, (((1,), (1,)), ((), ())), preferred_element_type=F32) * w_in
            cst = c_ref[b * H + h]
            nst = n_ref[b * H + h]
            num = (jnp.dot(sc, v, preferred_element_type=F32)
                   + w_st * jnp.dot(q, cst, preferred_element_type=F32))
            den = jnp.sum(sc, axis=-1, keepdims=True) + w_st * jnp.sum(q * nst, axis=-1, keepdims=True)
            hh = num / jnp.maximum(jnp.abs(den), jnp.exp(-m_t))
            b_last = b_col[L - 1:L, :]
            dec = b_last - b_col + li_col
            m_new = jnp.maximum(b_last + m_prev, jnp.max(dec, axis=0, keepdims=True))
            wk = jnp.exp(dec - m_new)
            keep = jnp.exp(b_last + m_prev - m_new)
            wkk = wk * k
            c_ref[b * H + h] = keep * cst + lax.dot_general(
                wkk, v, (((0,), (0,)), ((), ())), preferred_element_type=F32)
            n_ref[b * H + h] = keep * nst + jnp.sum(wkk, axis=0, keepdims=True)
            m_ref[b * H + h] = jnp.broadcast_to(m_new, (1, LANES))
            o_pre = y[:, W2 + ML_W + h * DH:W2 + ML_W + (h + 1) * DH]
            hh = _sigmoid(o_pre) * hh
            hh = hh * lax.rsqrt(jnp.mean(hh * hh, axis=-1, keepdims=True) + LN_EPS)
            outs.append(hh)
        o_ref[b] = (jnp.concatenate(outs, axis=-1) * g_ref[...]).astype(o_ref.dtype)


def _mlstm(yd3, sm3, conv_w, conv_b, b_i, b_f, norm_g):
    nb, s, _ = yd3.shape
    L = MLSTM_CHUNK
    gate_bias = jnp.zeros((1, LANES), F32)
    gate_bias = gate_bias.at[0, _SM_I:_SM_I + MLSTM_HEADS].set(b_i).at[0, _SM_F:_SM_F + MLSTM_HEADS].set(b_f)
    return pl.pallas_call(
        functools.partial(_mlstm_kernel, nb=nb),
        grid=(s // L,),
        in_specs=[pl.BlockSpec((nb, L, _W_D), lambda n: (0, n, 0)),
                  pl.BlockSpec((nb, L, LANES), lambda n: (0, n, 0)),
                  pl.BlockSpec((MLSTM_CONV, 2 * ML_W), lambda n: (0, 0)),
                  pl.BlockSpec((1, 2 * ML_W), lambda n: (0, 0)),
                  pl.BlockSpec((1, LANES), lambda n: (0, 0)),
                  pl.BlockSpec((1, ML_W), lambda n: (0, 0))],
        out_specs=pl.BlockSpec((nb, L, ML_W), lambda n: (0, n, 0)),
        out_shape=jax.ShapeDtypeStruct((nb, s, ML_W), BF16),
        scratch_shapes=[pltpu.VMEM((nb * MLSTM_HEADS, MLSTM_HEAD_DIM, MLSTM_HEAD_DIM), F32),
                        pltpu.VMEM((nb * MLSTM_HEADS, 1, MLSTM_HEAD_DIM), F32),
                        pltpu.VMEM((nb * MLSTM_HEADS, 1, LANES), F32),
                        pltpu.VMEM((nb, 8, 2 * ML_W), F32)],
        compiler_params=_cparams(("arbitrary",)),
        name="mlstm",
    )(yd3, sm3, conv_w, conv_b.reshape(1, 2 * ML_W), gate_bias, norm_g.reshape(1, ML_W))


def _gelu(x):
    return 0.5 * x * (1.0 + lax.erf(x * (0.5 ** 0.5)))


def _sgu_kernel(y_ref, lg_ref, lb_ref, ws_ref, bst_ref, o_ref, *, nchunk):
    C, G, GC = SGU_CHUNK, SGU_GROUPS, SGU_GROUP_CH
    y = y_ref[...]
    zu = _gelu(y[:, :SGU_W])
    zv = _gelu(y[:, SGU_W:])
    mu = jnp.mean(zv, axis=-1, keepdims=True)
    var = jnp.mean(jnp.square(zv - mu), axis=-1, keepdims=True)
    vn = (zv - mu) * lax.rsqrt(var + LN_EPS) * lg_ref[...] + lb_ref[...]
    tril = _iota2((C, C), 0) >= _iota2((C, C), 1)
    for g in range(G):
        wc = jnp.where(tril, ws_ref[g], 0.0)
        bias = bst_ref[:, g:g + 1]
        for c in range(nchunk):
            rows = slice(c * C, (c + 1) * C)
            cols = slice(g * GC, (g + 1) * GC)
            s = jnp.dot(wc, vn[rows, cols], preferred_element_type=F32) + bias
            o_ref[rows, cols] = (zu[rows, cols] * s).astype(o_ref.dtype)


def _sgu(yc, ln_g, ln_b, ws, bs, nchunk=2):
    n = yc.shape[0]
    t = nchunk * SGU_CHUNK
    bst = jnp.zeros((SGU_CHUNK, LANES), F32).at[:, :SGU_GROUPS].set(bs.T)
    return pl.pallas_call(
        functools.partial(_sgu_kernel, nchunk=nchunk),
        grid=(n // t,),
        in_specs=[pl.BlockSpec((t, _W_C), lambda i: (i, 0)),
                  pl.BlockSpec((1, SGU_W), lambda i: (0, 0)),
                  pl.BlockSpec((1, SGU_W), lambda i: (0, 0)),
                  pl.BlockSpec((SGU_GROUPS, SGU_CHUNK, SGU_CHUNK), lambda i: (0, 0, 0)),
                  pl.BlockSpec((SGU_CHUNK, LANES), lambda i: (0, 0))],
        out_specs=pl.BlockSpec((t, SGU_W), lambda i: (i, 0)),
        out_shape=jax.ShapeDtypeStruct((n, SGU_W), BF16),
        compiler_params=_cparams(("parallel",)),
        name="sgu",
    )(yc, ln_g.reshape(1, SGU_W), ln_b.reshape(1, SGU_W), ws, bst)


def _rope_table_kernel(pos_ref, inv_ref, cos_ref, sin_ref):
    ang = pos_ref[...].astype(F32) * inv_ref[...]
    half = DIL_HEAD_DIM // 2
    sign = jnp.where((_iota2(ang.shape, 1) % DIL_HEAD_DIM) < half, -1.0, 1.0)
    cos_ref[...] = jnp.cos(ang)
    sin_ref[...] = jnp.sin(ang) * sign


def _rope_tables(positions):
    n = positions.size
    half = DIL_HEAD_DIM // 2
    inv = ROPE_THETA ** (-jnp.arange(half, dtype=F32) * 2.0 / DIL_HEAD_DIM)
    inv = jnp.tile(inv, LANES // half).reshape(1, LANES)
    t = 1024
    return pl.pallas_call(
        _rope_table_kernel,
        grid=(n // t,),
        in_specs=[pl.BlockSpec((t, 1), lambda i: (i, 0)),
                  pl.BlockSpec((1, LANES), lambda i: (0, 0))],
        out_specs=[pl.BlockSpec((t, LANES), lambda i: (i, 0)),
                   pl.BlockSpec((t, LANES), lambda i: (i, 0))],
        out_shape=[jax.ShapeDtypeStruct((n, LANES), F32)] * 2,
        compiler_params=_cparams(("parallel",)),
        name="rope_tables",
    )(positions.reshape(n, 1), inv)


def _dil_kernel(q0, q1, q2, k0, k1, k2, v0, v1, v2, cos_ref, sin_ref, o_ref,
                qs_ref, ks_ref, num_ref, m_ref, den_ref, *, seq):
    DH, BLK = DIL_HEAD_DIM, DIL_BLOCK
    half = DH // 2
    q_refs, k_refs, v_refs = (q0, q1, q2), (k0, k1, k2), (v0, v1, v2)
    cos = cos_ref[...]
    sin = sin_ref[...]
    first_half = (_iota2((seq, LANES), 1) % DH) < half

    def rope(x):
        swapped = jnp.where(first_half, pltpu.roll(x, LANES - half, 1), pltpu.roll(x, half, 1))
        return x * cos + swapped * sin

    for g in range(len(DIL_PAIRS)):
        qs_ref[g] = rope(q_refs[g][...]) * (DH ** -0.5)
        ks_ref[g] = rope(k_refs[g][...])

    ii = _iota2((BLK, BLK), 0)
    jj = _iota2((BLK, BLK), 1)
    mask_cur = jj <= ii
    mask_prev = jj >= ii

    for g, (window, dil) in enumerate(DIL_PAIRS):
        lsub = seq // dil
        nblk = lsub // BLK
        assert window // dil == BLK and lsub % BLK == 0
        v_ref = v_refs[g]

        def unit(u, carry, g=g, dil=dil, nblk=nblk, v_ref=v_ref):
            r = u % dil
            n = u // dil
            rows = pl.ds(n * (BLK * dil) + r, BLK, stride=dil)
            qb = qs_ref[g, rows, :]
            kc = ks_ref[g, rows, :]
            vc = v_ref[rows, :]
            if nblk > 1:
                prow = pl.ds(jnp.maximum(n - 1, 0) * (BLK * dil) + r, BLK, stride=dil)
                kp = ks_ref[g, prow, :]
                vp = v_ref[prow, :]
                has_prev = n > 0
            nums, ms, dens = [], [], []
            for h in range(LANES // DH):
                ls = slice(h * DH, (h + 1) * DH)
                s_c = lax.dot_general(qb[:, ls], kc[:, ls], (((1,), (1,)), ((), ())), preferred_element_type=F32)
                s_c = jnp.where(mask_cur, s_c, NEG_INF)
                mx = jnp.max(s_c, axis=-1, keepdims=True)
                if nblk > 1:
                    s_p = lax.dot_general(qb[:, ls], kp[:, ls], (((1,), (1,)), ((), ())), preferred_element_type=F32)
                    s_p = jnp.where(jnp.logical_and(mask_prev, has_prev), s_p, NEG_INF)
                    mx = jnp.maximum(mx, jnp.max(s_p, axis=-1, keepdims=True))
                p_c = jnp.exp(s_c - mx)
                den = jnp.sum(p_c, axis=-1, keepdims=True)
                num = jnp.dot(p_c, vc[:, ls], preferred_element_type=F32)
                if nblk > 1:
                    p_p = jnp.exp(s_p - mx)
                    den = den + jnp.sum(p_p, axis=-1, keepdims=True)
                    num = num + jnp.dot(p_p, vp[:, ls], preferred_element_type=F32)
                nums.append(num)
                ms.append(jnp.broadcast_to(mx, (BLK, DH)))
                dens.append(jnp.broadcast_to(den, (BLK, DH)))
            num_ref[g, rows, :] = jnp.concatenate(nums, axis=-1)
            m_ref[g, rows, :] = jnp.concatenate(ms, axis=-1)
            den_ref[g, rows, :] = jnp.concatenate(dens, axis=-1)
            return carry

        lax.fori_loop(0, dil * nblk, unit, 0, unroll=4)

    m_all = jnp.maximum(jnp.maximum(m_ref[0], m_ref[1]), m_ref[2])
    num = jnp.zeros((seq, LANES), F32)
    den = jnp.zeros((seq, LANES), F32)
    for g in range(len(DIL_PAIRS)):
        e = jnp.exp(m_ref[g] - m_all)
        num = num + e * num_ref[g]
        den = den + e * den_ref[g]
    o_ref[...] = (num / den).astype(o_ref.dtype)


def _dilated(yb, cos_t, sin_t, nbatch):
    n = yb.shape[0]
    seq = n // nbatch
    npair = DIL_HEADS_PER_GROUP * DIL_HEAD_DIM // LANES
    nblk_cols = DIL_W // LANES

    def spec(section, g):
        return pl.BlockSpec((seq, LANES), lambda b, p: (b, section * nblk_cols + g * npair + p))

    in_specs = ([spec(0, g) for g in range(3)] + [spec(1, g) for g in range(3)] + [spec(2, g) for g in range(3)]
                + [pl.BlockSpec((seq, LANES), lambda b, p: (b, 0))] * 2)
    return pl.pallas_call(
        functools.partial(_dil_kernel, seq=seq),
        grid=(nbatch, npair),
        in_specs=in_specs,
        out_specs=pl.BlockSpec((seq, LANES), lambda b, p: (b, p)),
        out_shape=jax.ShapeDtypeStruct((n, DIL_OUT), BF16),
        scratch_shapes=[pltpu.VMEM((3, seq, LANES), F32)] * 5,
        compiler_params=_cparams(("parallel", "parallel")),
        name="dilated_attn",
    )(*([yb] * 9), cos_t, sin_t)


def _merge_kernel(x_ref, g0, g1, g2, g3, ya, yb, yc, yd, pa, pb, pc, pd, o_ref):
    x = x_ref[...]
    acc = None
    for wg, y, p in ((g0, ya, pa), (g1, yb, pb), (g2, yc, pc), (g3, yd, pd)):
        gate = _sigmoid(jnp.dot(x, wg[...], preferred_element_type=F32))
        term = gate * jnp.dot(y[...], p[...], preferred_element_type=F32)
        acc = term if acc is None else acc + term
    o_ref[...] = acc.astype(o_ref.dtype)


def _merge(xb, wg, ys, ps, tm=1024, tn=512):
    n, d = xb.shape
    ncol = d // tn

    def gate_spec(br):
        return pl.BlockSpec((d, tn), lambda i, j: (0, br * ncol + j))

    in_specs = ([pl.BlockSpec((tm, d), lambda i, j: (i, 0))]
                + [gate_spec(br) for br in range(N_BRANCH)]
                + [pl.BlockSpec((tm, y.shape[1]), lambda i, j: (i, 0)) for y in ys]
                + [pl.BlockSpec((p.shape[0], tn), lambda i, j: (0, j)) for p in ps])
    return pl.pallas_call(
        _merge_kernel,
        grid=(n // tm, ncol),
        in_specs=in_specs,
        out_specs=pl.BlockSpec((tm, tn), lambda i, j: (i, j)),
        out_shape=jax.ShapeDtypeStruct((n, d), BF16),
        compiler_params=_cparams(("parallel", "arbitrary")),
        name="gated_merge",
    )(xb, wg, wg, wg, wg, *ys, *ps)


def _layernorm_rows(z, g, b):
    mu = jnp.mean(z, axis=-1, keepdims=True)
    var = jnp.mean(jnp.square(z - mu), axis=-1, keepdims=True)
    return (z - mu) * lax.rsqrt(var + LN_EPS) * g + b


def _pack_bf16_pairs(y):
    half = y.shape[1] // 2
    lo = lax.bitcast_convert_type(y[:, :half].astype(BF16).astype(F32), jnp.uint32)
    hi = lax.bitcast_convert_type(y[:, half:].astype(BF16).astype(F32), jnp.uint32)
    return (lo >> 16) | (hi & jnp.uint32(0xFFFF0000))


def _unpack_bf16_pairs(w):
    lo = lax.bitcast_convert_type(w << 16, F32).astype(BF16)
    hi = lax.bitcast_convert_type(w & jnp.uint32(0xFFFF0000), F32).astype(BF16)
    return lo, hi


def _store_token_major(ref, val):
    t, w = val.shape
    c = w // LANES
    for s in range(c):
        ref[pl.ds(s, t, stride=c), :] = val[:, s * LANES:(s + 1) * LANES]


def _load_token_major(ref, start, t, c):
    return [ref[pl.ds(start + s, t, stride=c), :] for s in range(c)]


def _outproj_ln_kernel(m_ref, w_ref, x_ref, g_ref, b_ref, o_ref, p_ref):
    h = jnp.dot(m_ref[...], w_ref[...], preferred_element_type=F32)
    y = _layernorm_rows(DN_ALPHA * x_ref[...] + h, g_ref[...], b_ref[...])
    o_ref[...] = y
    _store_token_major(p_ref, _pack_bf16_pairs(y))


def _outproj_ln(merged, w_out, x, g, b, tm=512):
    n, d = x.shape
    c = d // 2 // LANES
    return pl.pallas_call(
        _outproj_ln_kernel,
        grid=(n // tm,),
        in_specs=[pl.BlockSpec((tm, d), lambda i: (i, 0)),
                  pl.BlockSpec((d, d), lambda i: (0, 0)),
                  pl.BlockSpec((tm, d), lambda i: (i, 0)),
                  pl.BlockSpec((1, d), lambda i: (0, 0)),
                  pl.BlockSpec((1, d), lambda i: (0, 0))],
        out_specs=[pl.BlockSpec((tm, d), lambda i: (i, 0)),
                   pl.BlockSpec((tm * c, LANES), lambda i: (i, 0))],
        out_shape=[jax.ShapeDtypeStruct((n, d), F32), jax.ShapeDtypeStruct((n * c, LANES), jnp.uint32)],
        compiler_params=_cparams(("parallel",)),
        name="outproj_ln",
    )(merged, w_out, x, g.reshape(1, d), b.reshape(1, d))


def _router_kernel(x_ref, wr_ref, rb_ref, idx_ref, wts_ref, rank_ref, cnt_ref, carry_ref, *, tr):
    gsz = N_EXPERTS // N_GROUPS

    @pl.when(pl.program_id(0) == 0)
    def _():
        carry_ref[...] = jnp.zeros_like(carry_ref)

    lane = _iota2((tr, LANES), 1)
    real = lane < N_EXPERTS
    logits = jnp.dot(x_ref[...], wr_ref[...], precision=HI, preferred_element_type=F32)
    scores = _sigmoid(logits)
    choice = jnp.where(real, scores + rb_ref[...], NEG_INF)

    def group_allreduce(v, op):
        s = 1
        while s < gsz:
            partner = jnp.where((lane & s) == 0, pltpu.roll(v, LANES - s, 1), pltpu.roll(v, s, 1))
            v = op(v, partner)
            s *= 2
        return v

    max1 = group_allreduce(choice, jnp.maximum)
    first = group_allreduce(jnp.where(choice == max1, lane, LANES), jnp.minimum)
    max2 = group_allreduce(jnp.where(lane == first, NEG_INF, choice), jnp.maximum)
    gscore = max1 + max2
    gid = lane // gsz
    gsel = jnp.zeros((tr, LANES), jnp.bool_)
    for _ in range(TOPK_GROUPS):
        gmax = jnp.max(gscore, axis=-1, keepdims=True)
        pick = jnp.min(jnp.where(gscore == gmax, gid, LANES), axis=-1, keepdims=True)
        hit = gid == pick
        gsel = jnp.logical_or(gsel, hit)
        gscore = jnp.where(hit, NEG_INF, gscore)

    cand = jnp.where(gsel, choice, NEG_INF)
    idx_out = jnp.zeros((tr, LANES), jnp.int32)
    w_out = jnp.zeros((tr, LANES), F32)
    sel = jnp.zeros((tr, LANES), F32)
    hits = []
    for k in range(TOP_K):
        vmax = jnp.max(cand, axis=-1, keepdims=True)
        pick = jnp.min(jnp.where(cand == vmax, lane, LANES), axis=-1, keepdims=True)
        hit = lane == pick
        hits.append(hit)
        wk = jnp.sum(jnp.where(hit, scores, 0.0), axis=-1, keepdims=True)
        idx_out = jnp.where(lane == k, pick, idx_out)
        w_out = jnp.where(lane == k, wk, w_out)
        sel = jnp.where(hit, 1.0, sel)
        cand = jnp.where(hit, NEG_INF, cand)
    w_out = w_out / jnp.sum(w_out, axis=-1, keepdims=True) * ROUTED_SCALE

    strict = (_iota2((tr, tr), 0) > _iota2((tr, tr), 1)).astype(BF16)
    rank = jnp.dot(strict, sel.astype(BF16), preferred_element_type=F32) + carry_ref[0:1, :]
    rank_out = jnp.zeros((tr, LANES), F32)
    for k in range(TOP_K):
        rk = jnp.sum(jnp.where(hits[k], rank, 0.0), axis=-1, keepdims=True)
        rank_out = jnp.where(lane == k, rk, rank_out)
    carry_ref[...] = carry_ref[...] + jnp.sum(sel, axis=0, keepdims=True)
    idx_ref[...] = idx_out
    wts_ref[...] = w_out
    rank_ref[...] = rank_out.astype(jnp.int32)
    cnt_ref[...] = carry_ref[...]


def _router(x1, w_router, router_bias, tr=512):
    n, d = x1.shape
    wr = jnp.zeros((d, LANES), F32).at[:, :N_EXPERTS].set(w_router)
    rb = jnp.zeros((1, LANES), F32).at[0, :N_EXPERTS].set(router_bias)
    row = pl.BlockSpec((tr, LANES), lambda i: (i, 0))
    return pl.pallas_call(
        functools.partial(_router_kernel, tr=tr),
        grid=(n // tr,),
        in_specs=[pl.BlockSpec((tr, d), lambda i: (i, 0)),
                  pl.BlockSpec((d, LANES), lambda i: (0, 0)),
                  pl.BlockSpec((1, LANES), lambda i: (0, 0))],
        out_specs=[row, row, row, pl.BlockSpec((8, LANES), lambda i: (0, 0))],
        out_shape=[jax.ShapeDtypeStruct((n, LANES), jnp.int32), jax.ShapeDtypeStruct((n, LANES), F32),
                   jax.ShapeDtypeStruct((n, LANES), jnp.int32), jax.ShapeDtypeStruct((8, LANES), F32)],
        scratch_shapes=[pltpu.VMEM((8, LANES), F32)],
        compiler_params=_cparams(("arbitrary",)),
        name="router",
    )(x1, wr, rb)


def _group_offsets(cnt, tm):
    padded = jnp.floor((cnt + (tm - 1)) * (1.0 / tm)) * tm
    upper = (_iota2((LANES, LANES), 0) < _iota2((LANES, LANES), 1)).astype(F32)
    offs = jnp.dot(padded, upper, precision=HI, preferred_element_type=F32)
    return padded, offs


def _pos_kernel(idx_ref, rank_ref, cnt_ref, pos_ref, *, tm):
    _, offs = _group_offsets(cnt_ref[...], tm)
    offs = offs[0:1, :]
    idx = idx_ref[...]
    lane = _iota2(idx.shape, 1)
    out = jnp.zeros(idx.shape, jnp.int32)
    for k in range(TOP_K):
        ok = jnp.sum(jnp.where(lane == idx[:, k:k + 1], offs, 0.0), axis=-1, keepdims=True)
        out = jnp.where(lane == k, ok.astype(jnp.int32), out)
    pos_ref[...] = out + rank_ref[...]


def _positions(idx, rank, cnt, tm, tp=1024):
    n = idx.shape[0]
    row = pl.BlockSpec((tp, LANES), lambda i: (i, 0))
    return pl.pallas_call(
        functools.partial(_pos_kernel, tm=tm),
        grid=(n // tp,),
        in_specs=[row, row, pl.BlockSpec((8, LANES), lambda i: (0, 0))],
        out_specs=row,
        out_shape=jax.ShapeDtypeStruct((n, LANES), jnp.int32),
        compiler_params=_cparams(("parallel",)),
        name="dispatch_positions",
    )(idx, rank, cnt)


def _tile_map_kernel(cnt_ref, te_ref, *, tm, width):
    padded, offs = _group_offsets(cnt_ref[...], tm)
    ends = _row_to_col((offs + padded)[0:1, :], LANES)
    expert = _iota2((LANES, width), 0)
    start = (_iota2((LANES, width), 1) * tm).astype(F32)
    done = jnp.logical_and(ends <= start, expert < N_EXPERTS)
    te = jnp.sum(jnp.where(done, 1, 0), axis=0, keepdims=True)
    total = jnp.max(jnp.where(expert < N_EXPERTS, ends, 0.0), axis=0, keepdims=True)
    ntile = (total * (1.0 / tm)).astype(jnp.int32)
    vend = _row_to_col((offs + cnt_ref[...])[0:1, :], LANES)
    mine = jnp.sum(jnp.where(expert == te, vend, 0.0), axis=0, keepdims=True)
    valid = jnp.clip(mine - start[0:1, :], 0.0, float(tm)).astype(jnp.int32)
    row = _iota2((8, width), 0)
    te_ref[...] = jnp.where(row == 0, te, jnp.where(row == 1, ntile, jnp.where(row == 2, valid, 0)))


def _tile_map(cnt, tm, ntile_max):
    width = -(-ntile_max // LANES) * LANES
    out = pl.pallas_call(
        functools.partial(_tile_map_kernel, tm=tm, width=width),
        in_specs=[pl.BlockSpec((8, LANES), lambda: (0, 0))],
        out_specs=pl.BlockSpec((8, width), lambda: (0, 0)),
        out_shape=jax.ShapeDtypeStruct((8, width), jnp.int32),
        name="tile_map",
    )(cnt)
    return out[0, :ntile_max], out[1, :1], out[2, :ntile_max]


SC_SCATTER_ROWS = 64
SC_GATHER_ROWS = 32


def _sc_mesh():
    info = plsc.get_sparse_core_info()
    mesh = plsc.VectorSubcoreMesh(core_axis_name="c", subcore_axis_name="s")
    return mesh, info.num_cores, info.num_subcores


def _sc_scatter_rows(src3, idx3, nrows):
    n, c, _ = src3.shape
    _, nk, chunk = idx3.shape
    mesh, ncore, nsub = _sc_mesh()
    per_w = n // (ncore * nsub)
    assert chunk == SC_SCATTER_ROWS and per_w % chunk == 0

    @functools.partial(
        pl.kernel, mesh=mesh,
        out_type=jax.ShapeDtypeStruct((nrows, c, LANES), src3.dtype),
        scratch_types=[pltpu.VMEM((nk, chunk), jnp.int32),
                       pltpu.VMEM((chunk, c, LANES), src3.dtype),
                       pltpu.SemaphoreType.DMA],
    )
    def scatter_kernel(src_hbm, idx_hbm, out_hbm, idx_v, rows_v, sem):
        base = (lax.axis_index("s") * ncore + lax.axis_index("c")) * per_w

        @pl.loop(0, per_w // chunk)
        def _(j):
            off = pl.multiple_of(base + j * chunk, chunk)
            pltpu.sync_copy(idx_hbm.at[base // chunk + j], idx_v)
            pltpu.sync_copy(src_hbm.at[pl.ds(off, chunk)], rows_v)
            copies = [pltpu.async_copy(rows_v, out_hbm.at[idx_v.at[k]], sem) for k in range(nk)]
            for cp in copies:
                cp.wait()

    return scatter_kernel(src3, idx3)


def _sc_gather_rows(table3, idx2):
    _, c, _ = table3.shape
    nchunk, chunk = idx2.shape
    mesh, ncore, nsub = _sc_mesh()
    per_w = nchunk // (ncore * nsub)
    assert chunk == SC_GATHER_ROWS and per_w % 2 == 0

    @functools.partial(
        pl.kernel, mesh=mesh,
        out_type=jax.ShapeDtypeStruct((nchunk * chunk, c, LANES), table3.dtype),
        scratch_types=[pltpu.VMEM((per_w, chunk), jnp.int32),
                       pltpu.VMEM((chunk, c, LANES), table3.dtype),
                       pltpu.VMEM((chunk, c, LANES), table3.dtype),
                       pltpu.SemaphoreType.DMA, pltpu.SemaphoreType.DMA,
                       pltpu.SemaphoreType.DMA, pltpu.SemaphoreType.DMA],
    )
    def gather_kernel(table_hbm, idx_hbm, out_hbm, idx_v, rows0, rows1, g0, g1, w0, w1):
        first = (lax.axis_index("s") * ncore + lax.axis_index("c")) * per_w
        pltpu.sync_copy(idx_hbm.at[pl.ds(pl.multiple_of(first, per_w), per_w)], idx_v)

        @pl.loop(0, per_w, step=2)
        def _(j):
            ga = pltpu.async_copy(table_hbm.at[idx_v.at[j]], rows0, g0)
            gb = pltpu.async_copy(table_hbm.at[idx_v.at[j + 1]], rows1, g1)
            ga.wait()
            wa = pltpu.async_copy(rows0, out_hbm.at[pl.ds(pl.multiple_of((first + j) * chunk, chunk), chunk)], w0)
            gb.wait()
            wb = pltpu.async_copy(rows1, out_hbm.at[pl.ds(pl.multiple_of((first + j + 1) * chunk, chunk), chunk)], w1)
            wa.wait()
            wb.wait()

    return gather_kernel(table3, idx2)


def _ffn_kernel(te_ref, nt_ref, nv_ref, xs_ref, w1_ref, w3_ref, w2_ref, o_ref, w1b, w3b, w2b, *, tm):
    i = pl.program_id(0)

    @pl.when(i < nt_ref[0])
    def _():
        e = te_ref[i]
        prev = te_ref[jnp.maximum(i - 1, 0)]

        @pl.when(jnp.logical_or(i == 0, e != prev))
        def _():
            w1b[...] = w1_ref[0, 0].astype(BF16)
            w3b[...] = w3_ref[0, 0].astype(BF16)
            w2b[...] = w2_ref[0, 0].astype(BF16)

        c = xs_ref.shape[0] // tm
        live = _iota2((tm, LANES), 0) < nv_ref[i]
        parts = [_unpack_bf16_pairs(jnp.where(live, p, jnp.uint32(0)))
                 for p in _load_token_major(xs_ref, 0, tm, c)]
        lo = jnp.concatenate([p[0] for p in parts], axis=1)
        hi = jnp.concatenate([p[1] for p in parts], axis=1)
        half = lo.shape[1]
        h1 = (jnp.dot(lo, w1b[:half, :], preferred_element_type=F32)
              + jnp.dot(hi, w1b[half:, :], preferred_element_type=F32))
        h3 = (jnp.dot(lo, w3b[:half, :], preferred_element_type=F32)
              + jnp.dot(hi, w3b[half:, :], preferred_element_type=F32))
        h = (_silu(h1) * h3).astype(BF16)
        y = jnp.dot(h, w2b[...], preferred_element_type=F32)
        _store_token_major(o_ref, _pack_bf16_pairs(y))

    @pl.when(i >= nt_ref[0])
    def _():
        o_ref[...] = jnp.zeros_like(o_ref)


def _expert_ffn(te, nt, nv, xs, w1, w3, w2, layer, nrows, tm):
    d, f = w1.shape[2], w1.shape[3]
    c = xs.shape[0] // nrows

    def tile(i, te_r, nt_r, nv_r):
        return (jnp.minimum(i, nt_r[0] - 1), 0)

    def expert(i, te_r, nt_r, nv_r):
        return (layer, te_r[jnp.minimum(i, nt_r[0] - 1)], 0, 0)

    grid_spec = pltpu.PrefetchScalarGridSpec(
        num_scalar_prefetch=3,
        grid=(nrows // tm,),
        in_specs=[pl.BlockSpec((tm * c, LANES), tile),
                  pl.BlockSpec((1, 1, d, f), expert),
                  pl.BlockSpec((1, 1, d, f), expert),
                  pl.BlockSpec((1, 1, f, d), expert)],
        out_specs=pl.BlockSpec((tm * c, LANES), lambda i, te_r, nt_r, nv_r: (i, 0)),
        scratch_shapes=[pltpu.VMEM((d, f), BF16), pltpu.VMEM((d, f), BF16), pltpu.VMEM((f, d), BF16)],
    )
    return pl.pallas_call(
        functools.partial(_ffn_kernel, tm=tm),
        grid_spec=grid_spec,
        out_shape=jax.ShapeDtypeStruct((nrows * c, LANES), jnp.uint32),
        compiler_params=_cparams(("arbitrary",)),
        name="expert_ffn",
    )(te, nt, nv, xs, w1, w3, w2)


def _combine_kernel(*refs, tc):
    y_refs = refs[:TOP_K]
    wts_ref, x_ref, ws1_ref, ws3_ref, ws2_ref, g_ref, b_ref, o_ref, ob_ref = refs[TOP_K:]
    c = y_refs[0].shape[0] // tc
    x = x_ref[...]
    xb = x.astype(BF16)
    s1 = jnp.dot(xb, ws1_ref[...], preferred_element_type=F32)
    s3 = jnp.dot(xb, ws3_ref[...], preferred_element_type=F32)
    acc = jnp.dot((_silu(s1) * s3).astype(BF16), ws2_ref[...], preferred_element_type=F32)
    wts = wts_ref[...]
    for k in range(TOP_K):
        parts = [_unpack_bf16_pairs(p) for p in _load_token_major(y_refs[k], 0, tc, c)]
        yk = jnp.concatenate([p[0] for p in parts] + [p[1] for p in parts], axis=1).astype(F32)
        acc = acc + wts[:, k:k + 1] * yk
    y = _layernorm_rows(DN_ALPHA * x + acc, g_ref[...], b_ref[...])
    o_ref[...] = y
    ob_ref[...] = y.astype(BF16)


def _combine(yg, wts, x1, ws1, ws3, ws2, g, b, tc=256):
    n, d = x1.shape
    sf = ws1.shape[1]
    c = yg.shape[0] // (TOP_K * n)
    nblk = n // tc

    def slot_spec(k):
        return pl.BlockSpec((tc * c, LANES), lambda i: (k * nblk + i, 0))

    return pl.pallas_call(
        functools.partial(_combine_kernel, tc=tc),
        grid=(nblk,),
        in_specs=[slot_spec(k) for k in range(TOP_K)] + [
            pl.BlockSpec((tc, LANES), lambda i: (i, 0)),
            pl.BlockSpec((tc, d), lambda i: (i, 0)),
            pl.BlockSpec((d, sf), lambda i: (0, 0)),
            pl.BlockSpec((d, sf), lambda i: (0, 0)),
            pl.BlockSpec((sf, d), lambda i: (0, 0)),
            pl.BlockSpec((1, d), lambda i: (0, 0)),
            pl.BlockSpec((1, d), lambda i: (0, 0))],
        out_specs=[pl.BlockSpec((tc, d), lambda i: (i, 0)), pl.BlockSpec((tc, d), lambda i: (i, 0))],
        out_shape=[jax.ShapeDtypeStruct((n, d), F32), jax.ShapeDtypeStruct((n, d), BF16)],
        compiler_params=_cparams(("parallel",)),
        name="moe_combine",
    )(*([yg] * TOP_K), wts, x1, ws1, ws3, ws2, g.reshape(1, d), b.reshape(1, d))


FFN_TILE = 256


def _moe(x1, x1p, w_router, router_bias, w1, w3, w2, layer, ws1, ws3, ws2, g, b):
    n = x1.shape[0]
    tm = FFN_TILE
    nrows = n * TOP_K + N_EXPERTS * tm
    idx, wts, rank, cnt = _router(x1, w_router, router_bias)
    pos = _positions(idx, rank, cnt, tm)
    te, nt, nv = _tile_map(cnt, tm, nrows // tm)
    c = x1p.shape[0] // n
    pos_t = pos[:, :TOP_K].T
    pos_s = pos_t.reshape(TOP_K, n // SC_SCATTER_ROWS, SC_SCATTER_ROWS).transpose(1, 0, 2)
    xs = _sc_scatter_rows(x1p.reshape(n, c, LANES), pos_s, nrows)
    ys = _expert_ffn(te, nt, nv, xs.reshape(nrows * c, LANES), w1, w3, w2, layer, nrows, tm)
    yg = _sc_gather_rows(ys.reshape(nrows, c, LANES), pos_t.reshape(-1, SC_GATHER_ROWS))
    return _combine(yg.reshape(TOP_K * n * c, LANES), wts, x1,
                    ws1.astype(BF16), ws3.astype(BF16), ws2.astype(BF16), g, b)


def kernel(x, positions, w_in, gla_wa2, gla_ba, gla_norm, mlstm_conv_w, mlstm_conv_b, mlstm_bi, mlstm_bf,
           mlstm_norm, sgu_ln_g, sgu_ln_b, sgu_ws, sgu_bs, w_pa, w_pb, w_pc, w_pd, w_out, ln1_g, ln1_b,
           w_router, router_bias, w1, w3, w2, ws1, ws3, ws2, ln2_g, ln2_b):
    nbatch, seq, d = x.shape
    n = nbatch * seq
    xf = x.reshape(n, d)
    xb = xf.astype(BF16)
    cos_t, sin_t = _rope_tables(positions)
    for l in range(DEPTH):
        wl = w_in[l]
        w_a = wl[:, _OFF_A:_OFF_A + _W_A].astype(BF16)
        w_b = wl[:, _OFF_B:_OFF_B + _W_B].astype(BF16)
        w_c = wl[:, _OFF_C:_OFF_C + _W_C].astype(BF16)
        w_d = wl[:, _OFF_D:_OFF_D + _W_D].astype(BF16)
        w_g = wl[:, _OFF_G:_OFF_G + _W_G].astype(BF16)
        w_s = jnp.concatenate(
            [wl[:, _OFF_LOW:_OFF_LOW + GLA_RANK], wl[:, _OFF_IF:_OFF_IF + 2 * MLSTM_HEADS],
             jnp.zeros((d, LANES - GLA_RANK - 2 * MLSTM_HEADS), F32)], axis=1).astype(BF16)
        y_a = _matmul(xb, w_a, 1024, 512)
        y_b = _matmul(xb, w_b, 1024, 768)
        y_c = _matmul(xb, w_c, 1024, 512)
        y_d = _matmul(xb, w_d, 1024, 512)
        y_s = _matmul(xb, w_s, 1024, LANES)
        sm3 = y_s.reshape(nbatch, seq, LANES)
        o_a = _gla(y_a.reshape(nbatch, seq, _W_A), sm3, gla_wa2[l], gla_ba[l], gla_norm[l]).reshape(n, GLA_V)
        o_b = _dilated(y_b, cos_t, sin_t, nbatch)
        o_c = _sgu(y_c, sgu_ln_g[l], sgu_ln_b[l], sgu_ws[l], sgu_bs[l])
        o_d = _mlstm(y_d.reshape(nbatch, seq, _W_D), sm3, mlstm_conv_w[l], mlstm_conv_b[l], mlstm_bi[l],
                     mlstm_bf[l], mlstm_norm[l]).reshape(n, ML_W)
        merged = _merge(xb, w_g, (o_a, o_b, o_c, o_d),
                        (w_pa[l].astype(BF16), w_pb[l].astype(BF16), w_pc[l].astype(BF16), w_pd[l].astype(BF16)))
        x1, x1p = _outproj_ln(merged, w_out[l].astype(BF16), xf, ln1_g[l], ln1_b[l])
        xf, xb = _moe(x1, x1p, w_router[l], router_bias[l], w1, w3, w2, l, ws1[l], ws3[l], ws2[l],
                      ln2_g[l], ln2_b[l])
    return xf.reshape(nbatch, seq, d)
```

```python
import functools
import math

import jax
import jax.numpy as jnp
from jax import lax
from jax.experimental import pallas as pl
from jax.experimental.pallas import tpu as pltpu
from jax.experimental.pallas import tpu_sc as plsc

D_MODEL = 2048
DEPTH = 2

GLA_HEADS = 4
GLA_DK = 64
GLA_DV = 128
GLA_RANK = 16
GLA_TAU = 16.0
GLA_CHUNK = 64
GLA_QK = GLA_HEADS * GLA_DK
GLA_V = GLA_HEADS * GLA_DV

DIL_PAIRS = ((128, 1), (512, 4), (2048, 16))
DIL_HEADS_PER_GROUP = 4
DIL_HEAD_DIM = 64
DIL_HEADS = len(DIL_PAIRS) * DIL_HEADS_PER_GROUP
DIL_W = DIL_HEADS * DIL_HEAD_DIM
DIL_OUT = DIL_HEADS_PER_GROUP * DIL_HEAD_DIM
DIL_BLOCK = 128
ROPE_THETA = 10000.0

SGU_CHUNK = 128
SGU_GROUPS = 6
SGU_GROUP_CH = 128
SGU_W = SGU_GROUPS * SGU_GROUP_CH

MLSTM_HEADS = 4
MLSTM_HEAD_DIM = 128
MLSTM_CHUNK = 64
MLSTM_CONV = 4
ML_W = MLSTM_HEADS * MLSTM_HEAD_DIM

N_EXPERTS = 64
TOP_K = 8
N_GROUPS = 8
TOPK_GROUPS = 4
EXPERT_FF = 512
SHARED_FF = 512
ROUTED_SCALE = 2.5

N_BRANCH = 4
DN_ALPHA = (2 * DEPTH) ** 0.25
LN_EPS = 1e-5

_OFF_A = 0
_W_A = 2 * GLA_QK + 2 * GLA_V
_OFF_LOW = _OFF_A + _W_A
_OFF_B = _OFF_LOW + GLA_RANK
_W_B = 3 * DIL_W
_OFF_C = _OFF_B + _W_B
_W_C = 2 * SGU_W
_OFF_D = _OFF_C + _W_C
_W_D = 4 * ML_W
_OFF_IF = _OFF_D + _W_D
_OFF_G = _OFF_IF + 2 * MLSTM_HEADS
_W_G = N_BRANCH * D_MODEL

LANES = 128
VMEM_LIMIT = 56 * 1024 * 1024

_SM_LOW = 0
_SM_I = GLA_RANK
_SM_F = GLA_RANK + MLSTM_HEADS

HI = lax.Precision.HIGHEST
F32 = jnp.float32
BF16 = jnp.bfloat16
NEG_INF = float("-inf")


def _cparams(sem):
    return pltpu.CompilerParams(dimension_semantics=sem, vmem_limit_bytes=VMEM_LIMIT)


def _log_sigmoid(x):
    return jnp.minimum(x, 0.0) - jnp.log1p(jnp.exp(-jnp.abs(x)))


def _sigmoid(x):
    return 1.0 / (1.0 + jnp.exp(-x))


def _silu(x):
    return x * _sigmoid(x)


def _iota2(shape, dim):
    return lax.broadcasted_iota(jnp.int32, shape, dim)


def _col_to_row(col, n):
    eye = _iota2((n, n), 0) == _iota2((n, n), 1)
    return jnp.sum(jnp.where(eye, col, 0.0), axis=0, keepdims=True)


def _row_to_col(row, n):
    eye = _iota2((n, n), 0) == _iota2((n, n), 1)
    return jnp.sum(jnp.where(eye, row, 0.0), axis=1, keepdims=True)


def _mm_kernel(x_ref, w_ref, o_ref):
    o_ref[...] = jnp.dot(x_ref[...], w_ref[...], preferred_element_type=F32).astype(o_ref.dtype)


def _matmul(x, w, tm, tn, out_dtype=F32):
    n, k = x.shape
    m = w.shape[1]
    return pl.pallas_call(
        _mm_kernel,
        grid=(n // tm, m // tn),
        in_specs=[pl.BlockSpec((tm, k), lambda i, j: (i, 0)),
                  pl.BlockSpec((k, tn), lambda i, j: (0, j))],
        out_specs=pl.BlockSpec((tm, tn), lambda i, j: (i, j)),
        out_shape=jax.ShapeDtypeStruct((n, m), out_dtype),
        compiler_params=_cparams(("parallel", "arbitrary")),
        name="in_proj",
    )(x, w)


def _gla_kernel(y_ref, sm_ref, wa2_ref, ba_ref, g_ref, o_ref, state_ref, *, nb):
    L, H, DK, DV = GLA_CHUNK, GLA_HEADS, GLA_DK, GLA_DV

    @pl.when(pl.program_id(0) == 0)
    def _():
        state_ref[...] = jnp.zeros_like(state_ref)

    tril = (_iota2((L, L), 0) >= _iota2((L, L), 1))
    tril_f = tril.astype(F32)
    for b in range(nb):
        y = y_ref[b]
        a_low = sm_ref[b][:, _SM_LOW:_SM_LOW + GLA_RANK]
        glog = jnp.dot(a_low, wa2_ref[...], preferred_element_type=F32) + ba_ref[...]
        g = _log_sigmoid(glog) * (1.0 / GLA_TAU)
        bc = jnp.dot(tril_f, g, precision=HI, preferred_element_type=F32)
        outs = []
        for h in range(H):
            q = y[:, h * DK:(h + 1) * DK] * (DK ** -0.5)
            k = y[:, GLA_QK + h * DK:GLA_QK + (h + 1) * DK]
            v = y[:, 2 * GLA_QK + h * DV:2 * GLA_QK + (h + 1) * DV]
            bh = bc[:, h * DK:(h + 1) * DK]
            qe = q * jnp.exp(bh)
            ke = k * jnp.exp(-bh)
            att = lax.dot_general(qe, ke, (((1,), (1,)), ((), ())), preferred_element_type=F32)
            att = jnp.where(tril, att, 0.0)
            st = state_ref[b * H + h]
            o = (jnp.dot(att, v, preferred_element_type=F32)
                 + jnp.dot(qe, st, preferred_element_type=F32))
            b_last = bh[L - 1:L, :]
            kd = k * jnp.exp(b_last - bh)
            decay = _row_to_col(jnp.exp(b_last), DK)
            state_ref[b * H + h] = decay * st + lax.dot_general(
                kd, v, (((0,), (0,)), ((), ())), preferred_element_type=F32)
            o = o * lax.rsqrt(jnp.mean(o * o, axis=-1, keepdims=True) + LN_EPS)
            outs.append(o)
        o_all = jnp.concatenate(outs, axis=-1) * g_ref[...]
        r = y[:, 2 * GLA_QK + GLA_V:2 * GLA_QK + 2 * GLA_V]
        o_ref[b] = (o_all * _silu(r)).astype(o_ref.dtype)


def _gla(ya3, sm3, wa2, ba, norm_g):
    nb, s, _ = ya3.shape
    L = GLA_CHUNK
    return pl.pallas_call(
        functools.partial(_gla_kernel, nb=nb),
        grid=(s // L,),
        in_specs=[pl.BlockSpec((nb, L, _W_A), lambda n: (0, n, 0)),
                  pl.BlockSpec((nb, L, LANES), lambda n: (0, n, 0)),
                  pl.BlockSpec((GLA_RANK, GLA_QK), lambda n: (0, 0)),
                  pl.BlockSpec((1, GLA_QK), lambda n: (0, 0)),
                  pl.BlockSpec((1, GLA_V), lambda n: (0, 0))],
        out_specs=pl.BlockSpec((nb, L, GLA_V), lambda n: (0, n, 0)),
        out_shape=jax.ShapeDtypeStruct((nb, s, GLA_V), BF16),
        scratch_shapes=[pltpu.VMEM((nb * GLA_HEADS, GLA_DK, GLA_DV), F32)],
        compiler_params=_cparams(("arbitrary",)),
        name="gla",
    )(ya3, sm3, wa2, ba.reshape(1, GLA_QK), norm_g.reshape(1, GLA_V))


def _mlstm_kernel(y_ref, sm_ref, cw_ref, cb_ref, gb_ref, g_ref, o_ref, c_ref, n_ref, m_ref, tail_ref, *, nb):
    L, H, DH = MLSTM_CHUNK, MLSTM_HEADS, MLSTM_HEAD_DIM
    W2 = 2 * ML_W
    HALO = 8

    @pl.when(pl.program_id(0) == 0)
    def _():
        c_ref[...] = jnp.zeros_like(c_ref)
        n_ref[...] = jnp.zeros_like(n_ref)
        m_ref[...] = jnp.zeros_like(m_ref)
        tail_ref[...] = jnp.zeros_like(tail_ref)

    tril = (_iota2((L, L), 0) >= _iota2((L, L), 1))
    tril_f = tril.astype(F32)
    for b in range(nb):
        y = y_ref[b]
        qk_raw = y[:, :W2]
        ext = jnp.concatenate([tail_ref[b], qk_raw], axis=0)
        tail_ref[b] = qk_raw[L - HALO:, :]
        conv = cb_ref[...]
        for j in range(MLSTM_CONV):
            s0 = HALO - (MLSTM_CONV - 1) + j
            conv = conv + cw_ref[j:j + 1, :] * ext[s0:s0 + L, :]
        qk = _silu(conv)
        gates = sm_ref[b] + gb_ref[...]
        bcum = jnp.dot(tril_f, _log_sigmoid(gates), precision=HI, preferred_element_type=F32)
        outs = []
        for h in range(H):
            q = qk[:, h * DH:(h + 1) * DH]
            k = qk[:, ML_W + h * DH:ML_W + (h + 1) * DH] * (DH ** -0.5)
            v = y[:, W2 + h * DH:W2 + (h + 1) * DH]
            b_col = bcum[:, _SM_F + h:_SM_F + h + 1]
            li_col = gates[:, _SM_I + h:_SM_I + h + 1]
            b_row = _col_to_row(b_col, L)
            li_row = _col_to_row(li_col, L)
            m_prev = m_ref[b * H + h][:, 0:1]
            dmat = jnp.where(tril, b_col - b_row + li_row, NEG_INF)
            inter = b_col + m_prev
            m_t = jnp.maximum(inter, jnp.max(dmat, axis=-1, keepdims=True))
            w_in = jnp.exp(dmat - m_t)
            w_st = jnp.exp(inter - m_t)
            sc = lax.dot_general(q, k, (((1,), (1,)), ((), ())), preferred_element_type=F32) * w_in
            cst = c_ref[b * H + h]
            nst = n_ref[b * H + h]
            num = (jnp.dot(sc, v, preferred_element_type=F32)
                   + w_st * jnp.dot(q, cst, preferred_element_type=F32))
            den = jnp.sum(sc, axis=-1, keepdims=True) + w_st * jnp.sum(q * nst, axis=-1, keepdims=True)
            hh = num / jnp.maximum(jnp.abs(den), jnp.exp(-m_t))
            b_last = b_col[L - 1:L, :]
            dec = b_last - b_col + li_col
            m_new = jnp.maximum(b_last + m_prev, jnp.max(dec, axis=0, keepdims=True))
            wk = jnp.exp(dec - m_new)
            keep = jnp.exp(b_last + m_prev - m_new)
            wkk = wk * k
            c_ref[b * H + h] = keep * cst + lax.dot_general(
                wkk, v, (((0,), (0,)), ((), ())), preferred_element_type=F32)
            n_ref[b * H + h] = keep * nst + jnp.sum(wkk, axis=0, keepdims=True)
            m_ref[b * H + h] = jnp.broadcast_to(m_new, (1, LANES))
            o_pre = y[:, W2 + ML_W + h * DH:W2 + ML_W + (h + 1) * DH]
            hh = _sigmoid(o_pre) * hh
            hh = hh * lax.rsqrt(jnp.mean(hh * hh, axis=-1, keepdims=True) + LN_EPS)
            outs.append(hh)
        o_ref[b] = (jnp.concatenate(outs, axis=-1) * g_ref[...]).astype(o_ref.dtype)


def _mlstm(yd3, sm3, conv_w, conv_b, b_i, b_f, norm_g):
    nb, s, _ = yd3.shape
    L = MLSTM_CHUNK
    gate_bias = jnp.zeros((1, LANES), F32)
    gate_bias = gate_bias.at[0, _SM_I:_SM_I + MLSTM_HEADS].set(b_i).at[0, _SM_F:_SM_F + MLSTM_HEADS].set(b_f)
    return pl.pallas_call(
        functools.partial(_mlstm_kernel, nb=nb),
        grid=(s // L,),
        in_specs=[pl.BlockSpec((nb, L, _W_D), lambda n: (0, n, 0)),
                  pl.BlockSpec((nb, L, LANES), lambda n: (0, n, 0)),
                  pl.BlockSpec((MLSTM_CONV, 2 * ML_W), lambda n: (0, 0)),
                  pl.BlockSpec((1, 2 * ML_W), lambda n: (0, 0)),
                  pl.BlockSpec((1, LANES), lambda n: (0, 0)),
                  pl.BlockSpec((1, ML_W), lambda n: (0, 0))],
        out_specs=pl.BlockSpec((nb, L, ML_W), lambda n: (0, n, 0)),
        out_shape=jax.ShapeDtypeStruct((nb, s, ML_W), BF16),
        scratch_shapes=[pltpu.VMEM((nb * MLSTM_HEADS, MLSTM_HEAD_DIM, MLSTM_HEAD_DIM), F32),
                        pltpu.VMEM((nb * MLSTM_HEADS, 1, MLSTM_HEAD_DIM), F32),
                        pltpu.VMEM((nb * MLSTM_HEADS, 1, LANES), F32),
                        pltpu.VMEM((nb, 8, 2 * ML_W), F32)],
        compiler_params=_cparams(("arbitrary",)),
        name="mlstm",
    )(yd3, sm3, conv_w, conv_b.reshape(1, 2 * ML_W), gate_bias, norm_g.reshape(1, ML_W))


def _gelu(x):
    return 0.5 * x * (1.0 + lax.erf(x * (0.5 ** 0.5)))


def _sgu_kernel(y_ref, lg_ref, lb_ref, ws_ref, bst_ref, o_ref, *, nchunk):
    C, G, GC = SGU_CHUNK, SGU_GROUPS, SGU_GROUP_CH
    y = y_ref[...]
    zu = _gelu(y[:, :SGU_W])
    zv = _gelu(y[:, SGU_W:])
    mu = jnp.mean(zv, axis=-1, keepdims=True)
    var = jnp.mean(jnp.square(zv - mu), axis=-1, keepdims=True)
    vn = (zv - mu) * lax.rsqrt(var + LN_EPS) * lg_ref[...] + lb_ref[...]
    tril = _iota2((C, C), 0) >= _iota2((C, C), 1)
    for g in range(G):
        wc = jnp.where(tril, ws_ref[g], 0.0)
        bias = bst_ref[:, g:g + 1]
        for c in range(nchunk):
            rows = slice(c * C, (c + 1) * C)
            cols = slice(g * GC, (g + 1) * GC)
            s = jnp.dot(wc, vn[rows, cols], preferred_element_type=F32) + bias
            o_ref[rows, cols] = (zu[rows, cols] * s).astype(o_ref.dtype)


def _sgu(yc, ln_g, ln_b, ws, bs, nchunk=2):
    n = yc.shape[0]
    t = nchunk * SGU_CHUNK
    bst = jnp.zeros((SGU_CHUNK, LANES), F32).at[:, :SGU_GROUPS].set(bs.T)
    return pl.pallas_call(
        functools.partial(_sgu_kernel, nchunk=nchunk),
        grid=(n // t,),
        in_specs=[pl.BlockSpec((t, _W_C), lambda i: (i, 0)),
                  pl.BlockSpec((1, SGU_W), lambda i: (0, 0)),
                  pl.BlockSpec((1, SGU_W), lambda i: (0, 0)),
                  pl.BlockSpec((SGU_GROUPS, SGU_CHUNK, SGU_CHUNK), lambda i: (0, 0, 0)),
                  pl.BlockSpec((SGU_CHUNK, LANES), lambda i: (0, 0))],
        out_specs=pl.BlockSpec((t, SGU_W), lambda i: (i, 0)),
        out_shape=jax.ShapeDtypeStruct((n, SGU_W), BF16),
        compiler_params=_cparams(("parallel",)),
        name="sgu",
    )(yc, ln_g.reshape(1, SGU_W), ln_b.reshape(1, SGU_W), ws, bst)


def _rope_table_kernel(pos_ref, inv_ref, cos_ref, sin_ref):
    ang = pos_ref[...].astype(F32) * inv_ref[...]
    half = DIL_HEAD_DIM // 2
    sign = jnp.where((_iota2(ang.shape, 1) % DIL_HEAD_DIM) < half, -1.0, 1.0)
    cos_ref[...] = jnp.cos(ang)
    sin_ref[...] = jnp.sin(ang) * sign


def _rope_tables(positions):
    n = positions.size
    half = DIL_HEAD_DIM // 2
    inv = ROPE_THETA ** (-jnp.arange(half, dtype=F32) * 2.0 / DIL_HEAD_DIM)
    inv = jnp.tile(inv, LANES // half).reshape(1, LANES)
    t = 1024
    return pl.pallas_call(
        _rope_table_kernel,
        grid=(n // t,),
        in_specs=[pl.BlockSpec((t, 1), lambda i: (i, 0)),
                  pl.BlockSpec((1, LANES), lambda i: (0, 0))],
        out_specs=[pl.BlockSpec((t, LANES), lambda i: (i, 0)),
                   pl.BlockSpec((t, LANES), lambda i: (i, 0))],
        out_shape=[jax.ShapeDtypeStruct((n, LANES), F32)] * 2,
        compiler_params=_cparams(("parallel",)),
        name="rope_tables",
    )(positions.reshape(n, 1), inv)


def _dil_kernel(q0, q1, q2, k0, k1, k2, v0, v1, v2, cos_ref, sin_ref, o_ref,
                qs_ref, ks_ref, num_ref, m_ref, den_ref, *, seq):
    DH, BLK = DIL_HEAD_DIM, DIL_BLOCK
    half = DH // 2
    q_refs, k_refs, v_refs = (q0, q1, q2), (k0, k1, k2), (v0, v1, v2)
    cos = cos_ref[...]
    sin = sin_ref[...]
    first_half = (_iota2((seq, LANES), 1) % DH) < half

    def rope(x):
        swapped = jnp.where(first_half, pltpu.roll(x, LANES - half, 1), pltpu.roll(x, half, 1))
        return x * cos + swapped * sin

    for g in range(len(DIL_PAIRS)):
        qs_ref[g] = rope(q_refs[g][...]) * (DH ** -0.5)
        ks_ref[g] = rope(k_refs[g][...])

    ii = _iota2((BLK, BLK), 0)
    jj = _iota2((BLK, BLK), 1)
    mask_cur = jj <= ii
    mask_prev = jj >= ii

    for g, (window, dil) in enumerate(DIL_PAIRS):
        lsub = seq // dil
        nblk = lsub // BLK
        assert window // dil == BLK and lsub % BLK == 0
        v_ref = v_refs[g]

        def unit(u, carry, g=g, dil=dil, nblk=nblk, v_ref=v_ref):
            r = u % dil
            n = u // dil
            rows = pl.ds(n * (BLK * dil) + r, BLK, stride=dil)
            qb = qs_ref[g, rows, :]
            kc = ks_ref[g, rows, :]
            vc = v_ref[rows, :]
            if nblk > 1:
                prow = pl.ds(jnp.maximum(n - 1, 0) * (BLK * dil) + r, BLK, stride=dil)
                kp = ks_ref[g, prow, :]
                vp = v_ref[prow, :]
                has_prev = n > 0
            nums, ms, dens = [], [], []
            for h in range(LANES // DH):
                ls = slice(h * DH, (h + 1) * DH)
                s_c = lax.dot_general(qb[:, ls], kc[:, ls], (((1,), (1,)), ((), ())), preferred_element_type=F32)
                s_c = jnp.where(mask_cur, s_c, NEG_INF)
                mx = jnp.max(s_c, axis=-1, keepdims=True)
                if nblk > 1:
                    s_p = lax.dot_general(qb[:, ls], kp[:, ls], (((1,), (1,)), ((), ())), preferred_element_type=F32)
                    s_p = jnp.where(jnp.logical_and(mask_prev, has_prev), s_p, NEG_INF)
                    mx = jnp.maximum(mx, jnp.max(s_p, axis=-1, keepdims=True))
                p_c = jnp.exp(s_c - mx)
                den = jnp.sum(p_c, axis=-1, keepdims=True)
                num = jnp.dot(p_c, vc[:, ls], preferred_element_type=F32)
                if nblk > 1:
                    p_p = jnp.exp(s_p - mx)
                    den = den + jnp.sum(p_p, axis=-1, keepdims=True)
                    num = num + jnp.dot(p_p, vp[:, ls], preferred_element_type=F32)
                nums.append(num)
                ms.append(jnp.broadcast_to(mx, (BLK, DH)))
                dens.append(jnp.broadcast_to(den, (BLK, DH)))
            num_ref[g, rows, :] = jnp.concatenate(nums, axis=-1)
            m_ref[g, rows, :] = jnp.concatenate(ms, axis=-1)
            den_ref[g, rows, :] = jnp.concatenate(dens, axis=-1)
            return carry

        lax.fori_loop(0, dil * nblk, unit, 0, unroll=4)

    m_all = jnp.maximum(jnp.maximum(m_ref[0], m_ref[1]), m_ref[2])
    num = jnp.zeros((seq, LANES), F32)
    den = jnp.zeros((seq, LANES), F32)
    for g in range(len(DIL_PAIRS)):
        e = jnp.exp(m_ref[g] - m_all)
        num = num + e * num_ref[g]
        den = den + e * den_ref[g]
    o_ref[...] = (num / den).astype(o_ref.dtype)


def _dilated(yb, cos_t, sin_t, nbatch):
    n = yb.shape[0]
    seq = n // nbatch
    npair = DIL_HEADS_PER_GROUP * DIL_HEAD_DIM // LANES
    nblk_cols = DIL_W // LANES

    def spec(section, g):
        return pl.BlockSpec((seq, LANES), lambda b, p: (b, section * nblk_cols + g * npair + p))

    in_specs = ([spec(0, g) for g in range(3)] + [spec(1, g) for g in range(3)] + [spec(2, g) for g in range(3)]
                + [pl.BlockSpec((seq, LANES), lambda b, p: (b, 0))] * 2)
    return pl.pallas_call(
        functools.partial(_dil_kernel, seq=seq),
        grid=(nbatch, npair),
        in_specs=in_specs,
        out_specs=pl.BlockSpec((seq, LANES), lambda b, p: (b, p)),
        out_shape=jax.ShapeDtypeStruct((n, DIL_OUT), BF16),
        scratch_shapes=[pltpu.VMEM((3, seq, LANES), F32)] * 5,
        compiler_params=_cparams(("parallel", "parallel")),
        name="dilated_attn",
    )(*([yb] * 9), cos_t, sin_t)


def _merge_kernel(x_ref, g0, g1, g2, g3, ya, yb, yc, yd, pa, pb, pc, pd, o_ref):
    x = x_ref[...]
    acc = None
    for wg, y, p in ((g0, ya, pa), (g1, yb, pb), (g2, yc, pc), (g3, yd, pd)):
        gate = _sigmoid(jnp.dot(x, wg[...], preferred_element_type=F32))
        term = gate * jnp.dot(y[...], p[...], preferred_element_type=F32)
        acc = term if acc is None else acc + term
    o_ref[...] = acc.astype(o_ref.dtype)


def _merge(xb, wg, ys, ps, tm=1024, tn=512):
    n, d = xb.shape
    ncol = d // tn

    def gate_spec(br):
        return pl.BlockSpec((d, tn), lambda i, j: (0, br * ncol + j))

    in_specs = ([pl.BlockSpec((tm, d), lambda i, j: (i, 0))]
                + [gate_spec(br) for br in range(N_BRANCH)]
                + [pl.BlockSpec((tm, y.shape[1]), lambda i, j: (i, 0)) for y in ys]
                + [pl.BlockSpec((p.shape[0], tn), lambda i, j: (0, j)) for p in ps])
    return pl.pallas_call(
        _merge_kernel,
        grid=(n // tm, ncol),
        in_specs=in_specs,
        out_specs=pl.BlockSpec((tm, tn), lambda i, j: (i, j)),
        out_shape=jax.ShapeDtypeStruct((n, d), BF16),
        compiler_params=_cparams(("parallel", "arbitrary")),
        name="gated_merge",
    )(xb, wg, wg, wg, wg, *ys, *ps)


def _layernorm_rows(z, g, b):
    mu = jnp.mean(z, axis=-1, keepdims=True)
    var = jnp.mean(jnp.square(z - mu), axis=-1, keepdims=True)
    return (z - mu) * lax.rsqrt(var + LN_EPS) * g + b


def _pack_bf16_pairs(y):
    half = y.shape[1] // 2
    lo = lax.bitcast_convert_type(y[:, :half].astype(BF16).astype(F32), jnp.uint32)
    hi = lax.bitcast_convert_type(y[:, half:].astype(BF16).astype(F32), jnp.uint32)
    return (lo >> 16) | (hi & jnp.uint32(0xFFFF0000))


def _unpack_bf16_pairs(w):
    lo = lax.bitcast_convert_type(w << 16, F32).astype(BF16)
    hi = lax.bitcast_convert_type(w & jnp.uint32(0xFFFF0000), F32).astype(BF16)
    return lo, hi


def _store_token_major(ref, val):
    t, w = val.shape
    c = w // LANES
    for s in range(c):
        ref[pl.ds(s, t, stride=c), :] = val[:, s * LANES:(s + 1) * LANES]


def _load_token_major(ref, start, t, c):
    return [ref[pl.ds(start + s, t, stride=c), :] for s in range(c)]


def _outproj_ln_kernel(m_ref, w_ref, x_ref, g_ref, b_ref, o_ref, p_ref):
    h = jnp.dot(m_ref[...], w_ref[...], preferred_element_type=F32)
    y = _layernorm_rows(DN_ALPHA * x_ref[...] + h, g_ref[...], b_ref[...])
    o_ref[...] = y
    _store_token_major(p_ref, _pack_bf16_pairs(y))


def _outproj_ln(merged, w_out, x, g, b, tm=512):
    n, d = x.shape
    c = d // 2 // LANES
    return pl.pallas_call(
        _outproj_ln_kernel,
        grid=(n // tm,),
        in_specs=[pl.BlockSpec((tm, d), lambda i: (i, 0)),
                  pl.BlockSpec((d, d), lambda i: (0, 0)),
                  pl.BlockSpec((tm, d), lambda i: (i, 0)),
                  pl.BlockSpec((1, d), lambda i: (0, 0)),
                  pl.BlockSpec((1, d), lambda i: (0, 0))],
        out_specs=[pl.BlockSpec((tm, d), lambda i: (i, 0)),
                   pl.BlockSpec((tm * c, LANES), lambda i: (i, 0))],
        out_shape=[jax.ShapeDtypeStruct((n, d), F32), jax.ShapeDtypeStruct((n * c, LANES), jnp.uint32)],
        compiler_params=_cparams(("parallel",)),
        name="outproj_ln",
    )(merged, w_out, x, g.reshape(1, d), b.reshape(1, d))


def _router_kernel(x_ref, wr_ref, rb_ref, idx_ref, wts_ref, rank_ref, cnt_ref, carry_ref, *, tr):
    gsz = N_EXPERTS // N_GROUPS

    @pl.when(pl.program_id(0) == 0)
    def _():
        carry_ref[...] = jnp.zeros_like(carry_ref)

    lane = _iota2((tr, LANES), 1)
    real = lane < N_EXPERTS
    logits = jnp.dot(x_ref[...], wr_ref[...], precision=HI, preferred_element_type=F32)
    scores = _sigmoid(logits)
    choice = jnp.where(real, scores + rb_ref[...], NEG_INF)

    def group_allreduce(v, op):
        s = 1
        while s < gsz:
            partner = jnp.where((lane & s) == 0, pltpu.roll(v, LANES - s, 1), pltpu.roll(v, s, 1))
            v = op(v, partner)
            s *= 2
        return v

    max1 = group_allreduce(choice, jnp.maximum)
    first = group_allreduce(jnp.where(choice == max1, lane, LANES), jnp.minimum)
    max2 = group_allreduce(jnp.where(lane == first, NEG_INF, choice), jnp.maximum)
    gscore = max1 + max2
    gid = lane // gsz
    gsel = jnp.zeros((tr, LANES), jnp.bool_)
    for _ in range(TOPK_GROUPS):
        gmax = jnp.max(gscore, axis=-1, keepdims=True)
        pick = jnp.min(jnp.where(gscore == gmax, gid, LANES), axis=-1, keepdims=True)
        hit = gid == pick
        gsel = jnp.logical_or(gsel, hit)
        gscore = jnp.where(hit, NEG_INF, gscore)

    cand = jnp.where(gsel, choice, NEG_INF)
    idx_out = jnp.zeros((tr, LANES), jnp.int32)
    w_out = jnp.zeros((tr, LANES), F32)
    sel = jnp.zeros((tr, LANES), F32)
    hits = []
    for k in range(TOP_K):
        vmax = jnp.max(cand, axis=-1, keepdims=True)
        pick = jnp.min(jnp.where(cand == vmax, lane, LANES), axis=-1, keepdims=True)
        hit = lane == pick
        hits.append(hit)
        wk = jnp.sum(jnp.where(hit, scores, 0.0), axis=-1, keepdims=True)
        idx_out = jnp.where(lane == k, pick, idx_out)
        w_out = jnp.where(lane == k, wk, w_out)
        sel = jnp.where(hit, 1.0, sel)
        cand = jnp.where(hit, NEG_INF, cand)
    w_out = w_out / jnp.sum(w_out, axis=-1, keepdims=True) * ROUTED_SCALE

    strict = (_iota2((tr, tr), 0) > _iota2((tr, tr), 1)).astype(BF16)
    rank = jnp.dot(strict, sel.astype(BF16), preferred_element_type=F32) + carry_ref[0:1, :]
    rank_out = jnp.zeros((tr, LANES), F32)
    for k in range(TOP_K):
        rk = jnp.sum(jnp.where(hits[k], rank, 0.0), axis=-1, keepdims=True)
        rank_out = jnp.where(lane == k, rk, rank_out)
    carry_ref[...] = carry_ref[...] + jnp.sum(sel, axis=0, keepdims=True)
    idx_ref[...] = idx_out
    wts_ref[...] = w_out
    rank_ref[...] = rank_out.astype(jnp.int32)
    cnt_ref[...] = carry_ref[...]


def _router(x1, w_router, router_bias, tr=512):
    n, d = x1.shape
    wr = jnp.zeros((d, LANES), F32).at[:, :N_EXPERTS].set(w_router)
    rb = jnp.zeros((1, LANES), F32).at[0, :N_EXPERTS].set(router_bias)
    row = pl.BlockSpec((tr, LANES), lambda i: (i, 0))
    return pl.pallas_call(
        functools.partial(_router_kernel, tr=tr),
        grid=(n // tr,),
        in_specs=[pl.BlockSpec((tr, d), lambda i: (i, 0)),
                  pl.BlockSpec((d, LANES), lambda i: (0, 0)),
                  pl.BlockSpec((1, LANES), lambda i: (0, 0))],
        out_specs=[row, row, row, pl.BlockSpec((8, LANES), lambda i: (0, 0))],
        out_shape=[jax.ShapeDtypeStruct((n, LANES), jnp.int32), jax.ShapeDtypeStruct((n, LANES), F32),
                   jax.ShapeDtypeStruct((n, LANES), jnp.int32), jax.ShapeDtypeStruct((8, LANES), F32)],
        scratch_shapes=[pltpu.VMEM((8, LANES), F32)],
        compiler_params=_cparams(("arbitrary",)),
        name="router",
    )(x1, wr, rb)


def _group_offsets(cnt, tm):
    padded = jnp.floor((cnt + (tm - 1)) * (1.0 / tm)) * tm
    upper = (_iota2((LANES, LANES), 0) < _iota2((LANES, LANES), 1)).astype(F32)
    offs = jnp.dot(padded, upper, precision=HI, preferred_element_type=F32)
    return padded, offs


def _pos_kernel(idx_ref, rank_ref, cnt_ref, pos_ref, *, tm):
    _, offs = _group_offsets(cnt_ref[...], tm)
    offs = offs[0:1, :]
    idx = idx_ref[...]
    lane = _iota2(idx.shape, 1)
    out = jnp.zeros(idx.shape, jnp.int32)
    for k in range(TOP_K):
        ok = jnp.sum(jnp.where(lane == idx[:, k:k + 1], offs, 0.0), axis=-1, keepdims=True)
        out = jnp.where(lane == k, ok.astype(jnp.int32), out)
    pos_ref[...] = out + rank_ref[...]


def _positions(idx, rank, cnt, tm, tp=1024):
    n = idx.shape[0]
    row = pl.BlockSpec((tp, LANES), lambda i: (i, 0))
    return pl.pallas_call(
        functools.partial(_pos_kernel, tm=tm),
        grid=(n // tp,),
        in_specs=[row, row, pl.BlockSpec((8, LANES), lambda i: (0, 0))],
        out_specs=row,
        out_shape=jax.ShapeDtypeStruct((n, LANES), jnp.int32),
        compiler_params=_cparams(("parallel",)),
        name="dispatch_positions",
    )(idx, rank, cnt)


def _tile_map_kernel(cnt_ref, te_ref, *, tm, width):
    padded, offs = _group_offsets(cnt_ref[...], tm)
    ends = _row_to_col((offs + padded)[0:1, :], LANES)
    expert = _iota2((LANES, width), 0)
    start = (_iota2((LANES, width), 1) * tm).astype(F32)
    done = jnp.logical_and(ends <= start, expert < N_EXPERTS)
    te = jnp.sum(jnp.where(done, 1, 0), axis=0, keepdims=True)
    total = jnp.max(jnp.where(expert < N_EXPERTS, ends, 0.0), axis=0, keepdims=True)
    ntile = (total * (1.0 / tm)).astype(jnp.int32)
    vend = _row_to_col((offs + cnt_ref[...])[0:1, :], LANES)
    mine = jnp.sum(jnp.where(expert == te, vend, 0.0), axis=0, keepdims=True)
    valid = jnp.clip(mine - start[0:1, :], 0.0, float(tm)).astype(jnp.int32)
    row = _iota2((8, width), 0)
    te_ref[...] = jnp.where(row == 0, te, jnp.where(row == 1, ntile, jnp.where(row == 2, valid, 0)))


def _tile_map(cnt, tm, ntile_max):
    width = -(-ntile_max // LANES) * LANES
    out = pl.pallas_call(
        functools.partial(_tile_map_kernel, tm=tm, width=width),
        in_specs=[pl.BlockSpec((8, LANES), lambda: (0, 0))],
        out_specs=pl.BlockSpec((8, width), lambda: (0, 0)),
        out_shape=jax.ShapeDtypeStruct((8, width), jnp.int32),
        name="tile_map",
    )(cnt)
    return out[0, :ntile_max], out[1, :1], out[2, :ntile_max]


SC_SCATTER_ROWS = 64
SC_GATHER_ROWS = 32


def _sc_mesh():
    info = plsc.get_sparse_core_info()
    mesh = plsc.VectorSubcoreMesh(core_axis_name="c", subcore_axis_name="s")
    return mesh, info.num_cores, info.num_subcores


def _sc_scatter_rows(src3, idx3, nrows):
    n, c, _ = src3.shape
    _, nk, chunk = idx3.shape
    mesh, ncore, nsub = _sc_mesh()
    per_w = n // (ncore * nsub)
    assert chunk == SC_SCATTER_ROWS and per_w % chunk == 0

    @functools.partial(
        pl.kernel, mesh=mesh,
        out_type=jax.ShapeDtypeStruct((nrows, c, LANES), src3.dtype),
        scratch_types=[pltpu.VMEM((nk, chunk), jnp.int32),
                       pltpu.VMEM((chunk, c, LANES), src3.dtype),
                       pltpu.SemaphoreType.DMA],
    )
    def scatter_kernel(src_hbm, idx_hbm, out_hbm, idx_v, rows_v, sem):
        base = (lax.axis_index("s") * ncore + lax.axis_index("c")) * per_w

        @pl.loop(0, per_w // chunk)
        def _(j):
            off = pl.multiple_of(base + j * chunk, chunk)
            pltpu.sync_copy(idx_hbm.at[base // chunk + j], idx_v)
            pltpu.sync_copy(src_hbm.at[pl.ds(off, chunk)], rows_v)
            copies = [pltpu.async_copy(rows_v, out_hbm.at[idx_v.at[k]], sem) for k in range(nk)]
            for cp in copies:
                cp.wait()

    return scatter_kernel(src3, idx3)


def _sc_gather_rows(table3, idx2):
    _, c, _ = table3.shape
    nchunk, chunk = idx2.shape
    mesh, ncore, nsub = _sc_mesh()
    per_w = nchunk // (ncore * nsub)
    assert chunk == SC_GATHER_ROWS and per_w % 2 == 0

    @functools.partial(
        pl.kernel, mesh=mesh,
        out_type=jax.ShapeDtypeStruct((nchunk * chunk, c, LANES), table3.dtype),
        scratch_types=[pltpu.VMEM((per_w, chunk), jnp.int32),
                       pltpu.VMEM((chunk, c, LANES), table3.dtype),
                       pltpu.VMEM((chunk, c, LANES), table3.dtype),
                       pltpu.SemaphoreType.DMA, pltpu.SemaphoreType.DMA,
                       pltpu.SemaphoreType.DMA, pltpu.SemaphoreType.DMA],
    )
    def gather_kernel(table_hbm, idx_hbm, out_hbm, idx_v, rows0, rows1, g0, g1, w0, w1):
        first = (lax.axis_index("s") * ncore + lax.axis_index("c")) * per_w
        pltpu.sync_copy(idx_hbm.at[pl.ds(pl.multiple_of(first, per_w), per_w)], idx_v)

        @pl.loop(0, per_w, step=2)
        def _(j):
            ga = pltpu.async_copy(table_hbm.at[idx_v.at[j]], rows0, g0)
            gb = pltpu.async_copy(table_hbm.at[idx_v.at[j + 1]], rows1, g1)
            ga.wait()
            wa = pltpu.async_copy(rows0, out_hbm.at[pl.ds(pl.multiple_of((first + j) * chunk, chunk), chunk)], w0)
            gb.wait()
            wb = pltpu.async_copy(rows1, out_hbm.at[pl.ds(pl.multiple_of((first + j + 1) * chunk, chunk), chunk)], w1)
            wa.wait()
            wb.wait()

    return gather_kernel(table3, idx2)


FFN_LOOKAHEAD = 2
FFN_SLOTS = FFN_LOOKAHEAD + 1


def _ffn_kernel(te_ref, nt_ref, nv_ref, xs_ref, w1_ref, w3_ref, w2_ref, o_ref, w1b, w3b, w2b, ord_ref, *, tm):
    j = pl.program_id(0)
    i = j - FFN_LOOKAHEAD
    nt = nt_ref[0]

    @pl.when(j == 0)
    def _():
        ord_ref[0] = 0
        ord_ref[1] = 0

    jc = jnp.minimum(j, nt - 1)
    arrived = jnp.logical_or(j == 0, jnp.logical_and(j < nt, te_ref[jc] != te_ref[jnp.maximum(jc - 1, 0)]))

    @pl.when(arrived)
    def _():
        slot = ord_ref[0] % FFN_SLOTS
        w1b[slot] = w1_ref[0, 0].astype(BF16)
        w3b[slot] = w3_ref[0, 0].astype(BF16)
        w2b[slot] = w2_ref[0, 0].astype(BF16)
        ord_ref[0] = ord_ref[0] + 1

    @pl.when(jnp.logical_and(i >= 0, i < nt))
    def _():
        ic = jnp.maximum(i, 0)

        @pl.when(jnp.logical_and(i > 0, te_ref[ic] != te_ref[jnp.maximum(ic - 1, 0)]))
        def _():
            ord_ref[1] = ord_ref[1] + 1

        slot = ord_ref[1] % FFN_SLOTS
        c = xs_ref.shape[0] // tm
        live = _iota2((tm, LANES), 0) < nv_ref[ic]
        parts = [_unpack_bf16_pairs(jnp.where(live, p, jnp.uint32(0)))
                 for p in _load_token_major(xs_ref, 0, tm, c)]
        lo = jnp.concatenate([p[0] for p in parts], axis=1)
        hi = jnp.concatenate([p[1] for p in parts], axis=1)
        half = lo.shape[1]
        h1 = (jnp.dot(lo, w1b[slot, :half, :], preferred_element_type=F32)
              + jnp.dot(hi, w1b[slot, half:, :], preferred_element_type=F32))
        h3 = (jnp.dot(lo, w3b[slot, :half, :], preferred_element_type=F32)
              + jnp.dot(hi, w3b[slot, half:, :], preferred_element_type=F32))
        h = (_silu(h1) * h3).astype(BF16)
        y = jnp.dot(h, w2b[slot], preferred_element_type=F32)
        _store_token_major(o_ref, _pack_bf16_pairs(y))

    @pl.when(i >= nt)
    def _():
        o_ref[...] = jnp.zeros_like(o_ref)


def _expert_ffn(te, nt, nv, xs, w1, w3, w2, layer, nrows, tm):
    d, f = w1.shape[2], w1.shape[3]
    c = xs.shape[0] // nrows
    ntile = nrows // tm

    def tile(j, te_r, nt_r, nv_r):
        return (jnp.clip(j - FFN_LOOKAHEAD, 0, nt_r[0] - 1), 0)

    def out_tile(j, te_r, nt_r, nv_r):
        return (jnp.maximum(j - FFN_LOOKAHEAD, 0), 0)

    def expert(j, te_r, nt_r, nv_r):
        return (layer, te_r[jnp.minimum(j, nt_r[0] - 1)], 0, 0)

    grid_spec = pltpu.PrefetchScalarGridSpec(
        num_scalar_prefetch=3,
        grid=(ntile + FFN_LOOKAHEAD,),
        in_specs=[pl.BlockSpec((tm * c, LANES), tile),
                  pl.BlockSpec((1, 1, d, f), expert),
                  pl.BlockSpec((1, 1, d, f), expert),
                  pl.BlockSpec((1, 1, f, d), expert)],
        out_specs=pl.BlockSpec((tm * c, LANES), out_tile),
        scratch_shapes=[pltpu.VMEM((FFN_SLOTS, d, f), BF16), pltpu.VMEM((FFN_SLOTS, d, f), BF16),
                        pltpu.VMEM((FFN_SLOTS, f, d), BF16), pltpu.SMEM((2,), jnp.int32)],
    )
    return pl.pallas_call(
        functools.partial(_ffn_kernel, tm=tm),
        grid_spec=grid_spec,
        out_shape=jax.ShapeDtypeStruct((nrows * c, LANES), jnp.uint32),
        compiler_params=_cparams(("arbitrary",)),
        name="expert_ffn",
    )(te, nt, nv, xs, w1, w3, w2)


def _combine_kernel(*refs, tc):
    y_refs = refs[:TOP_K]
    wts_ref, x_ref, ws1_ref, ws3_ref, ws2_ref, g_ref, b_ref, o_ref, ob_ref = refs[TOP_K:]
    c = y_refs[0].shape[0] // tc
    x = x_ref[...]
    xb = x.astype(BF16)
    s1 = jnp.dot(xb, ws1_ref[...], preferred_element_type=F32)
    s3 = jnp.dot(xb, ws3_ref[...], preferred_element_type=F32)
    acc = jnp.dot((_silu(s1) * s3).astype(BF16), ws2_ref[...], preferred_element_type=F32)
    wts = wts_ref[...]
    for k in range(TOP_K):
        parts = [_unpack_bf16_pairs(p) for p in _load_token_major(y_refs[k], 0, tc, c)]
        yk = jnp.concatenate([p[0] for p in parts] + [p[1] for p in parts], axis=1).astype(F32)
        acc = acc + wts[:, k:k + 1] * yk
    y = _layernorm_rows(DN_ALPHA * x + acc, g_ref[...], b_ref[...])
    o_ref[...] = y
    ob_ref[...] = y.astype(BF16)


def _combine(yg, wts, x1, ws1, ws3, ws2, g, b, tc=256):
    n, d = x1.shape
    sf = ws1.shape[1]
    c = yg.shape[0] // (TOP_K * n)
    nblk = n // tc

    def slot_spec(k):
        return pl.BlockSpec((tc * c, LANES), lambda i: (k * nblk + i, 0))

    return pl.pallas_call(
        functools.partial(_combine_kernel, tc=tc),
        grid=(nblk,),
        in_specs=[slot_spec(k) for k in range(TOP_K)] + [
            pl.BlockSpec((tc, LANES), lambda i: (i, 0)),
            pl.BlockSpec((tc, d), lambda i: (i, 0)),
            pl.BlockSpec((d, sf), lambda i: (0, 0)),
            pl.BlockSpec((d, sf), lambda i: (0, 0)),
            pl.BlockSpec((sf, d), lambda i: (0, 0)),
            pl.BlockSpec((1, d), lambda i: (0, 0)),
            pl.BlockSpec((1, d), lambda i: (0, 0))],
        out_specs=[pl.BlockSpec((tc, d), lambda i: (i, 0)), pl.BlockSpec((tc, d), lambda i: (i, 0))],
        out_shape=[jax.ShapeDtypeStruct((n, d), F32), jax.ShapeDtypeStruct((n, d), BF16)],
        compiler_params=_cparams(("parallel",)),
        name="moe_combine",
    )(*([yg] * TOP_K), wts, x1, ws1, ws3, ws2, g.reshape(1, d), b.reshape(1, d))


FFN_TILE = 256


def _moe(x1, x1p, w_router, router_bias, w1, w3, w2, layer, ws1, ws3, ws2, g, b):
    n = x1.shape[0]
    tm = FFN_TILE
    nrows = n * TOP_K + N_EXPERTS * tm
    idx, wts, rank, cnt = _router(x1, w_router, router_bias)
    pos = _positions(idx, rank, cnt, tm)
    te, nt, nv = _tile_map(cnt, tm, nrows // tm)
    c = x1p.shape[0] // n
    pos_t = pos[:, :TOP_K].T
    pos_s = pos_t.reshape(TOP_K, n // SC_SCATTER_ROWS, SC_SCATTER_ROWS).transpose(1, 0, 2)
    xs = _sc_scatter_rows(x1p.reshape(n, c, LANES), pos_s, nrows)
    ys = _expert_ffn(te, nt, nv, xs.reshape(nrows * c, LANES), w1, w3, w2, layer, nrows, tm)
    yg = _sc_gather_rows(ys.reshape(nrows, c, LANES), pos_t.reshape(-1, SC_GATHER_ROWS))
    return _combine(yg.reshape(TOP_K * n * c, LANES), wts, x1,
                    ws1.astype(BF16), ws3.astype(BF16), ws2.astype(BF16), g, b)


def kernel(x, positions, w_in, gla_wa2, gla_ba, gla_norm, mlstm_conv_w, mlstm_conv_b, mlstm_bi, mlstm_bf,
           mlstm_norm, sgu_ln_g, sgu_ln_b, sgu_ws, sgu_bs, w_pa, w_pb, w_pc, w_pd, w_out, ln1_g, ln1_b,
           w_router, router_bias, w1, w3, w2, ws1, ws3, ws2, ln2_g, ln2_b):
    nbatch, seq, d = x.shape
    n = nbatch * seq
    xf = x.reshape(n, d)
    xb = xf.astype(BF16)
    cos_t, sin_t = _rope_tables(positions)
    for l in range(DEPTH):
        wl = w_in[l]
        w_a = wl[:, _OFF_A:_OFF_A + _W_A].astype(BF16)
        w_b = wl[:, _OFF_B:_OFF_B + _W_B].astype(BF16)
        w_c = wl[:, _OFF_C:_OFF_C + _W_C].astype(BF16)
        w_d = wl[:, _OFF_D:_OFF_D + _W_D].astype(BF16)
        w_g = wl[:, _OFF_G:_OFF_G + _W_G].astype(BF16)
        w_s = jnp.concatenate(
            [wl[:, _OFF_LOW:_OFF_LOW + GLA_RANK], wl[:, _OFF_IF:_OFF_IF + 2 * MLSTM_HEADS],
             jnp.zeros((d, LANES - GLA_RANK - 2 * MLSTM_HEADS), F32)], axis=1).astype(BF16)
        y_a = _matmul(xb, w_a, 1024, 512)
        y_b = _matmul(xb, w_b, 1024, 768)
        y_c = _matmul(xb, w_c, 1024, 512)
        y_d = _matmul(xb, w_d, 1024, 512)
        y_s = _matmul(xb, w_s, 1024, LANES)
        sm3 = y_s.reshape(nbatch, seq, LANES)
        o_a = _gla(y_a.reshape(nbatch, seq, _W_A), sm3, gla_wa2[l], gla_ba[l], gla_norm[l]).reshape(n, GLA_V)
        o_b = _dilated(y_b, cos_t, sin_t, nbatch)
        o_c = _sgu(y_c, sgu_ln_g[l], sgu_ln_b[l], sgu_ws[l], sgu_bs[l])
        o_d = _mlstm(y_d.reshape(nbatch, seq, _W_D), sm3, mlstm_conv_w[l], mlstm_conv_b[l], mlstm_bi[l],
                     mlstm_bf[l], mlstm_norm[l]).reshape(n, ML_W)
        merged = _merge(xb, w_g, (o_a, o_b, o_c, o_d),
                        (w_pa[l].astype(BF16), w_pb[l].astype(BF16), w_pc[l].astype(BF16), w_pd[l].astype(BF16)))
        x1, x1p = _outproj_ln(merged, w_out[l].astype(BF16), xf, ln1_g[l], ln1_b[l])
        xf, xb = _moe(x1, x1p, w_router[l], router_bias[l], w1, w3, w2, l, ws1[l], ws3[l], ws2[l],
                      ln2_g[l], ln2_b[l])
    return xf.reshape(nbatch, seq, d)
```

```python
import functools
import math

import jax
import jax.numpy as jnp
from jax import lax
from jax.experimental import pallas as pl
from jax.experimental.pallas import tpu as pltpu
from jax.experimental.pallas import tpu_sc as plsc

D_MODEL = 2048
DEPTH = 2

GLA_HEADS = 4
GLA_DK = 64
GLA_DV = 128
GLA_RANK = 16
GLA_TAU = 16.0
GLA_CHUNK = 64
GLA_QK = GLA_HEADS * GLA_DK
GLA_V = GLA_HEADS * GLA_DV

DIL_PAIRS = ((128, 1), (512, 4), (2048, 16))
DIL_HEADS_PER_GROUP = 4
DIL_HEAD_DIM = 64
DIL_HEADS = len(DIL_PAIRS) * DIL_HEADS_PER_GROUP
DIL_W = DIL_HEADS * DIL_HEAD_DIM
DIL_OUT = DIL_HEADS_PER_GROUP * DIL_HEAD_DIM
DIL_BLOCK = 128
ROPE_THETA = 10000.0

SGU_CHUNK = 128
SGU_GROUPS = 6
SGU_GROUP_CH = 128
SGU_W = SGU_GROUPS * SGU_GROUP_CH

MLSTM_HEADS = 4
MLSTM_HEAD_DIM = 128
MLSTM_CHUNK = 64
MLSTM_CONV = 4
ML_W = MLSTM_HEADS * MLSTM_HEAD_DIM

N_EXPERTS = 64
TOP_K = 8
N_GROUPS = 8
TOPK_GROUPS = 4
EXPERT_FF = 512
SHARED_FF = 512
ROUTED_SCALE = 2.5

N_BRANCH = 4
DN_ALPHA = (2 * DEPTH) ** 0.25
LN_EPS = 1e-5

_OFF_A = 0
_W_A = 2 * GLA_QK + 2 * GLA_V
_OFF_LOW = _OFF_A + _W_A
_OFF_B = _OFF_LOW + GLA_RANK
_W_B = 3 * DIL_W
_OFF_C = _OFF_B + _W_B
_W_C = 2 * SGU_W
_OFF_D = _OFF_C + _W_C
_W_D = 4 * ML_W
_OFF_IF = _OFF_D + _W_D
_OFF_G = _OFF_IF + 2 * MLSTM_HEADS
_W_G = N_BRANCH * D_MODEL

LANES = 128
VMEM_LIMIT = 56 * 1024 * 1024

_SM_LOW = 0
_SM_I = GLA_RANK
_SM_F = GLA_RANK + MLSTM_HEADS

HI = lax.Precision.HIGHEST
F32 = jnp.float32
BF16 = jnp.bfloat16
NEG_INF = float("-inf")


def _cparams(sem):
    return pltpu.CompilerParams(dimension_semantics=sem, vmem_limit_bytes=VMEM_LIMIT)


def _log_sigmoid(x):
    return jnp.minimum(x, 0.0) - jnp.log1p(jnp.exp(-jnp.abs(x)))


def _sigmoid(x):
    return 1.0 / (1.0 + jnp.exp(-x))


def _silu(x):
    return x * _sigmoid(x)


def _iota2(shape, dim):
    return lax.broadcasted_iota(jnp.int32, shape, dim)


def _col_to_row(col, n):
    eye = _iota2((n, n), 0) == _iota2((n, n), 1)
    return jnp.sum(jnp.where(eye, col, 0.0), axis=0, keepdims=True)


def _row_to_col(row, n):
    eye = _iota2((n, n), 0) == _iota2((n, n), 1)
    return jnp.sum(jnp.where(eye, row, 0.0), axis=1, keepdims=True)


def _mm_kernel(x_ref, w_ref, o_ref):
    o_ref[...] = jnp.dot(x_ref[...], w_ref[...], preferred_element_type=F32).astype(o_ref.dtype)


def _matmul(x, w, tm, tn, out_dtype=F32):
    n, k = x.shape
    m = w.shape[1]
    return pl.pallas_call(
        _mm_kernel,
        grid=(n // tm, m // tn),
        in_specs=[pl.BlockSpec((tm, k), lambda i, j: (i, 0)),
                  pl.BlockSpec((k, tn), lambda i, j: (0, j))],
        out_specs=pl.BlockSpec((tm, tn), lambda i, j: (i, j)),
        out_shape=jax.ShapeDtypeStruct((n, m), out_dtype),
        compiler_params=_cparams(("parallel", "arbitrary")),
        name="in_proj",
    )(x, w)


def _gla_chunk(y_ref, sm_ref, wa2_ref, ba_ref, g_ref, o_ref, state_ref, *, nb):
    L, H, DK, DV = GLA_CHUNK, GLA_HEADS, GLA_DK, GLA_DV
    tril = (_iota2((L, L), 0) >= _iota2((L, L), 1))
    tril_f = tril.astype(F32)
    for b in range(nb):
        y = y_ref[b]
        a_low = sm_ref[b][:, _SM_LOW:_SM_LOW + GLA_RANK]
        glog = jnp.dot(a_low, wa2_ref[...], preferred_element_type=F32) + ba_ref[...]
        g = _log_sigmoid(glog) * (1.0 / GLA_TAU)
        bc = jnp.dot(tril_f, g, precision=HI, preferred_element_type=F32)
        outs = []
        for h in range(H):
            q = y[:, h * DK:(h + 1) * DK] * (DK ** -0.5)
            k = y[:, GLA_QK + h * DK:GLA_QK + (h + 1) * DK]
            v = y[:, 2 * GLA_QK + h * DV:2 * GLA_QK + (h + 1) * DV]
            bh = bc[:, h * DK:(h + 1) * DK]
            qe = q * jnp.exp(bh)
            ke = k * jnp.exp(-bh)
            att = lax.dot_general(qe, ke, (((1,), (1,)), ((), ())), preferred_element_type=F32)
            att = jnp.where(tril, att, 0.0)
            st = state_ref[b * H + h]
            o = (jnp.dot(att, v, preferred_element_type=F32)
                 + jnp.dot(qe, st, preferred_element_type=F32))
            b_last = bh[L - 1:L, :]
            kd = k * jnp.exp(b_last - bh)
            decay = _row_to_col(jnp.exp(b_last), DK)
            state_ref[b * H + h] = decay * st + lax.dot_general(
                kd, v, (((0,), (0,)), ((), ())), preferred_element_type=F32)
            o = o * lax.rsqrt(jnp.mean(o * o, axis=-1, keepdims=True) + LN_EPS)
            outs.append(o)
        o_all = jnp.concatenate(outs, axis=-1) * g_ref[...]
        r = y[:, 2 * GLA_QK + GLA_V:2 * GLA_QK + 2 * GLA_V]
        o_ref[b] = (o_all * _silu(r)).astype(o_ref.dtype)


MLSTM_HALO = 8


def _mlstm_chunk(y_ref, sm_ref, cw_ref, cb_ref, gb_ref, g_ref, o_ref, c_ref, n_ref, m_ref, tail_ref, *, nb):
    L, H, DH = MLSTM_CHUNK, MLSTM_HEADS, MLSTM_HEAD_DIM
    W2 = 2 * ML_W
    HALO = MLSTM_HALO
    tril = (_iota2((L, L), 0) >= _iota2((L, L), 1))
    tril_f = tril.astype(F32)
    for b in range(nb):
        y = y_ref[b]
        qk_raw = y[:, :W2]
        ext = jnp.concatenate([tail_ref[b], qk_raw], axis=0)
        tail_ref[b] = qk_raw[L - HALO:, :]
        conv = cb_ref[...]
        for j in range(MLSTM_CONV):
            s0 = HALO - (MLSTM_CONV - 1) + j
            conv = conv + cw_ref[j:j + 1, :] * ext[s0:s0 + L, :]
        qk = _silu(conv)
        gates = sm_ref[b] + gb_ref[...]
        bcum = jnp.dot(tril_f, _log_sigmoid(gates), precision=HI, preferred_element_type=F32)
        outs = []
        for h in range(H):
            q = qk[:, h * DH:(h + 1) * DH]
            k = qk[:, ML_W + h * DH:ML_W + (h + 1) * DH] * (DH ** -0.5)
            v = y[:, W2 + h * DH:W2 + (h + 1) * DH]
            b_col = bcum[:, _SM_F + h:_SM_F + h + 1]
            li_col = gates[:, _SM_I + h:_SM_I + h + 1]
            b_row = _col_to_row(b_col, L)
            li_row = _col_to_row(li_col, L)
            m_prev = m_ref[b * H + h][:, 0:1]
            dmat = jnp.where(tril, b_col - b_row + li_row, NEG_INF)
            inter = b_col + m_prev
            m_t = jnp.maximum(inter, jnp.max(dmat, axis=-1, keepdims=True))
            w_in = jnp.exp(dmat - m_t)
            w_st = jnp.exp(inter - m_t)
            sc = lax.dot_general(q, k, (((1,), (1,)), ((), ())), preferred_element_type=F32) * w_in
            cst = c_ref[b * H + h]
            nst = n_ref[b * H + h]
            num = (jnp.dot(sc, v, preferred_element_type=F32)
                   + w_st * jnp.dot(q, cst, preferred_element_type=F32))
            den = jnp.sum(sc, axis=-1, keepdims=True) + w_st * jnp.sum(q * nst, axis=-1, keepdims=True)
            hh = num / jnp.maximum(jnp.abs(den), jnp.exp(-m_t))
            b_last = b_col[L - 1:L, :]
            dec = b_last - b_col + li_col
            m_new = jnp.maximum(b_last + m_prev, jnp.max(dec, axis=0, keepdims=True))
            wk = jnp.exp(dec - m_new)
            keep = jnp.exp(b_last + m_prev - m_new)
            wkk = wk * k
            c_ref[b * H + h] = keep * cst + lax.dot_general(
                wkk, v, (((0,), (0,)), ((), ())), preferred_element_type=F32)
            n_ref[b * H + h] = keep * nst + jnp.sum(wkk, axis=0, keepdims=True)
            m_ref[b * H + h] = jnp.broadcast_to(m_new, (1, LANES))
            o_pre = y[:, W2 + ML_W + h * DH:W2 + ML_W + (h + 1) * DH]
            hh = _sigmoid(o_pre) * hh
            hh = hh * lax.rsqrt(jnp.mean(hh * hh, axis=-1, keepdims=True) + LN_EPS)
            outs.append(hh)
        o_ref[b] = (jnp.concatenate(outs, axis=-1) * g_ref[...]).astype(o_ref.dtype)


def _recurrent_kernel(ya_ref, yd_ref, sm_ref, wa2_ref, ba_ref, ga_ref, cw_ref, cb_ref, gb_ref, gd_ref,
                      oa_ref, od_ref, state_ref, c_ref, n_ref, m_ref, tail_ref, *, nb):
    @pl.when(pl.program_id(0) == 0)
    def _():
        for ref in (state_ref, c_ref, n_ref, m_ref, tail_ref):
            ref[...] = jnp.zeros_like(ref)

    _gla_chunk(ya_ref, sm_ref, wa2_ref, ba_ref, ga_ref, oa_ref, state_ref, nb=nb)
    _mlstm_chunk(yd_ref, sm_ref, cw_ref, cb_ref, gb_ref, gd_ref, od_ref, c_ref, n_ref, m_ref, tail_ref, nb=nb)


def _recurrent_mixers(ya3, yd3, sm3, wa2, ba, gla_norm, conv_w, conv_b, b_i, b_f, mlstm_norm):
    nb, s, _ = ya3.shape
    assert GLA_CHUNK == MLSTM_CHUNK
    L = GLA_CHUNK
    gate_bias = jnp.zeros((1, LANES), F32)
    gate_bias = gate_bias.at[0, _SM_I:_SM_I + MLSTM_HEADS].set(b_i).at[0, _SM_F:_SM_F + MLSTM_HEADS].set(b_f)

    def chunk(width):
        return pl.BlockSpec((nb, L, width), lambda n: (0, n, 0))

    def whole(r, c):
        return pl.BlockSpec((r, c), lambda n: (0, 0))

    return pl.pallas_call(
        functools.partial(_recurrent_kernel, nb=nb),
        grid=(s // L,),
        in_specs=[chunk(_W_A), chunk(_W_D), chunk(LANES),
                  whole(GLA_RANK, GLA_QK), whole(1, GLA_QK), whole(1, GLA_V),
                  whole(MLSTM_CONV, 2 * ML_W), whole(1, 2 * ML_W), whole(1, LANES), whole(1, ML_W)],
        out_specs=[chunk(GLA_V), chunk(ML_W)],
        out_shape=[jax.ShapeDtypeStruct((nb, s, GLA_V), BF16), jax.ShapeDtypeStruct((nb, s, ML_W), BF16)],
        scratch_shapes=[pltpu.VMEM((nb * GLA_HEADS, GLA_DK, GLA_DV), F32),
                        pltpu.VMEM((nb * MLSTM_HEADS, MLSTM_HEAD_DIM, MLSTM_HEAD_DIM), F32),
                        pltpu.VMEM((nb * MLSTM_HEADS, 1, MLSTM_HEAD_DIM), F32),
                        pltpu.VMEM((nb * MLSTM_HEADS, 1, LANES), F32),
                        pltpu.VMEM((nb, MLSTM_HALO, 2 * ML_W), F32)],
        compiler_params=_cparams(("arbitrary",)),
        name="gla_mlstm",
    )(ya3, yd3, sm3, wa2, ba.reshape(1, GLA_QK), gla_norm.reshape(1, GLA_V),
      conv_w, conv_b.reshape(1, 2 * ML_W), gate_bias, mlstm_norm.reshape(1, ML_W))


def _gelu(x):
    return 0.5 * x * (1.0 + lax.erf(x * (0.5 ** 0.5)))


def _sgu_kernel(y_ref, lg_ref, lb_ref, ws_ref, bst_ref, o_ref, *, nchunk):
    C, G, GC = SGU_CHUNK, SGU_GROUPS, SGU_GROUP_CH
    y = y_ref[...]
    zu = _gelu(y[:, :SGU_W])
    zv = _gelu(y[:, SGU_W:])
    mu = jnp.mean(zv, axis=-1, keepdims=True)
    var = jnp.mean(jnp.square(zv - mu), axis=-1, keepdims=True)
    vn = (zv - mu) * lax.rsqrt(var + LN_EPS) * lg_ref[...] + lb_ref[...]
    tril = _iota2((C, C), 0) >= _iota2((C, C), 1)
    for g in range(G):
        wc = jnp.where(tril, ws_ref[g], 0.0)
        bias = bst_ref[:, g:g + 1]
        for c in range(nchunk):
            rows = slice(c * C, (c + 1) * C)
            cols = slice(g * GC, (g + 1) * GC)
            s = jnp.dot(wc, vn[rows, cols], preferred_element_type=F32) + bias
            o_ref[rows, cols] = (zu[rows, cols] * s).astype(o_ref.dtype)


def _sgu(yc, ln_g, ln_b, ws, bs, nchunk=2):
    n = yc.shape[0]
    t = nchunk * SGU_CHUNK
    bst = jnp.zeros((SGU_CHUNK, LANES), F32).at[:, :SGU_GROUPS].set(bs.T)
    return pl.pallas_call(
        functools.partial(_sgu_kernel, nchunk=nchunk),
        grid=(n // t,),
        in_specs=[pl.BlockSpec((t, _W_C), lambda i: (i, 0)),
                  pl.BlockSpec((1, SGU_W), lambda i: (0, 0)),
                  pl.BlockSpec((1, SGU_W), lambda i: (0, 0)),
                  pl.BlockSpec((SGU_GROUPS, SGU_CHUNK, SGU_CHUNK), lambda i: (0, 0, 0)),
                  pl.BlockSpec((SGU_CHUNK, LANES), lambda i: (0, 0))],
        out_specs=pl.BlockSpec((t, SGU_W), lambda i: (i, 0)),
        out_shape=jax.ShapeDtypeStruct((n, SGU_W), BF16),
        compiler_params=_cparams(("parallel",)),
        name="sgu",
    )(yc, ln_g.reshape(1, SGU_W), ln_b.reshape(1, SGU_W), ws, bst)


def _rope_table_kernel(pos_ref, inv_ref, cos_ref, sin_ref):
    ang = pos_ref[...].astype(F32) * inv_ref[...]
    half = DIL_HEAD_DIM // 2
    sign = jnp.where((_iota2(ang.shape, 1) % DIL_HEAD_DIM) < half, -1.0, 1.0)
    cos_ref[...] = jnp.cos(ang)
    sin_ref[...] = jnp.sin(ang) * sign


def _rope_tables(positions):
    n = positions.size
    half = DIL_HEAD_DIM // 2
    inv = ROPE_THETA ** (-jnp.arange(half, dtype=F32) * 2.0 / DIL_HEAD_DIM)
    inv = jnp.tile(inv, LANES // half).reshape(1, LANES)
    t = 1024
    return pl.pallas_call(
        _rope_table_kernel,
        grid=(n // t,),
        in_specs=[pl.BlockSpec((t, 1), lambda i: (i, 0)),
                  pl.BlockSpec((1, LANES), lambda i: (0, 0))],
        out_specs=[pl.BlockSpec((t, LANES), lambda i: (i, 0)),
                   pl.BlockSpec((t, LANES), lambda i: (i, 0))],
        out_shape=[jax.ShapeDtypeStruct((n, LANES), F32)] * 2,
        compiler_params=_cparams(("parallel",)),
        name="rope_tables",
    )(positions.reshape(n, 1), inv)


def _dil_kernel(q0, q1, q2, k0, k1, k2, v0, v1, v2, cos_ref, sin_ref, o_ref,
                qs_ref, ks_ref, num_ref, m_ref, den_ref, *, seq):
    DH, BLK = DIL_HEAD_DIM, DIL_BLOCK
    half = DH // 2
    q_refs, k_refs, v_refs = (q0, q1, q2), (k0, k1, k2), (v0, v1, v2)
    cos = cos_ref[...]
    sin = sin_ref[...]
    first_half = (_iota2((seq, LANES), 1) % DH) < half

    def rope(x):
        swapped = jnp.where(first_half, pltpu.roll(x, LANES - half, 1), pltpu.roll(x, half, 1))
        return x * cos + swapped * sin

    for g in range(len(DIL_PAIRS)):
        qs_ref[g] = rope(q_refs[g][...]) * (DH ** -0.5)
        ks_ref[g] = rope(k_refs[g][...])

    ii = _iota2((BLK, BLK), 0)
    jj = _iota2((BLK, BLK), 1)
    mask_cur = jj <= ii
    mask_prev = jj >= ii

    for g, (window, dil) in enumerate(DIL_PAIRS):
        lsub = seq // dil
        nblk = lsub // BLK
        assert window // dil == BLK and lsub % BLK == 0
        v_ref = v_refs[g]

        def unit(u, carry, g=g, dil=dil, nblk=nblk, v_ref=v_ref):
            r = u % dil
            n = u // dil
            rows = pl.ds(n * (BLK * dil) + r, BLK, stride=dil)
            qb = qs_ref[g, rows, :]
            kc = ks_ref[g, rows, :]
            vc = v_ref[rows, :]
            if nblk > 1:
                prow = pl.ds(jnp.maximum(n - 1, 0) * (BLK * dil) + r, BLK, stride=dil)
                kp = ks_ref[g, prow, :]
                vp = v_ref[prow, :]
                has_prev = n > 0
            nums, ms, dens = [], [], []
            for h in range(LANES // DH):
                ls = slice(h * DH, (h + 1) * DH)
                s_c = lax.dot_general(qb[:, ls], kc[:, ls], (((1,), (1,)), ((), ())), preferred_element_type=F32)
                s_c = jnp.where(mask_cur, s_c, NEG_INF)
                mx = jnp.max(s_c, axis=-1, keepdims=True)
                if nblk > 1:
                    s_p = lax.dot_general(qb[:, ls], kp[:, ls], (((1,), (1,)), ((), ())), preferred_element_type=F32)
                    s_p = jnp.where(jnp.logical_and(mask_prev, has_prev), s_p, NEG_INF)
                    mx = jnp.maximum(mx, jnp.max(s_p, axis=-1, keepdims=True))
                p_c = jnp.exp(s_c - mx)
                den = jnp.sum(p_c, axis=-1, keepdims=True)
                num = jnp.dot(p_c, vc[:, ls], preferred_element_type=F32)
                if nblk > 1:
                    p_p = jnp.exp(s_p - mx)
                    den = den + jnp.sum(p_p, axis=-1, keepdims=True)
                    num = num + jnp.dot(p_p, vp[:, ls], preferred_element_type=F32)
                nums.append(num)
                ms.append(jnp.broadcast_to(mx, (BLK, DH)))
                dens.append(jnp.broadcast_to(den, (BLK, DH)))
            num_ref[g, rows, :] = jnp.concatenate(nums, axis=-1)
            m_ref[g, rows, :] = jnp.concatenate(ms, axis=-1)
            den_ref[g, rows, :] = jnp.concatenate(dens, axis=-1)
            return carry

        lax.fori_loop(0, dil * nblk, unit, 0, unroll=4)

    m_all = jnp.maximum(jnp.maximum(m_ref[0], m_ref[1]), m_ref[2])
    num = jnp.zeros((seq, LANES), F32)
    den = jnp.zeros((seq, LANES), F32)
    for g in range(len(DIL_PAIRS)):
        e = jnp.exp(m_ref[g] - m_all)
        num = num + e * num_ref[g]
        den = den + e * den_ref[g]
    o_ref[...] = (num / den).astype(o_ref.dtype)


def _dilated(yb, cos_t, sin_t, nbatch):
    n = yb.shape[0]
    seq = n // nbatch
    npair = DIL_HEADS_PER_GROUP * DIL_HEAD_DIM // LANES
    nblk_cols = DIL_W // LANES

    def spec(section, g):
        return pl.BlockSpec((seq, LANES), lambda b, p: (b, section * nblk_cols + g * npair + p))

    in_specs = ([spec(0, g) for g in range(3)] + [spec(1, g) for g in range(3)] + [spec(2, g) for g in range(3)]
                + [pl.BlockSpec((seq, LANES), lambda b, p: (b, 0))] * 2)
    return pl.pallas_call(
        functools.partial(_dil_kernel, seq=seq),
        grid=(nbatch, npair),
        in_specs=in_specs,
        out_specs=pl.BlockSpec((seq, LANES), lambda b, p: (b, p)),
        out_shape=jax.ShapeDtypeStruct((n, DIL_OUT), BF16),
        scratch_shapes=[pltpu.VMEM((3, seq, LANES), F32)] * 5,
        compiler_params=_cparams(("parallel", "parallel")),
        name="dilated_attn",
    )(*([yb] * 9), cos_t, sin_t)


def _merge_kernel(x_ref, g0, g1, g2, g3, ya, yb, yc, yd, pa, pb, pc, pd, o_ref):
    x = x_ref[...]
    acc = None
    for wg, y, p in ((g0, ya, pa), (g1, yb, pb), (g2, yc, pc), (g3, yd, pd)):
        gate = _sigmoid(jnp.dot(x, wg[...], preferred_element_type=F32))
        term = gate * jnp.dot(y[...], p[...], preferred_element_type=F32)
        acc = term if acc is None else acc + term
    o_ref[...] = acc.astype(o_ref.dtype)


def _merge(xb, wg, ys, ps, tm=1024, tn=512):
    n, d = xb.shape
    ncol = d // tn

    def gate_spec(br):
        return pl.BlockSpec((d, tn), lambda i, j: (0, br * ncol + j))

    in_specs = ([pl.BlockSpec((tm, d), lambda i, j: (i, 0))]
                + [gate_spec(br) for br in range(N_BRANCH)]
                + [pl.BlockSpec((tm, y.shape[1]), lambda i, j: (i, 0)) for y in ys]
                + [pl.BlockSpec((p.shape[0], tn), lambda i, j: (0, j)) for p in ps])
    return pl.pallas_call(
        _merge_kernel,
        grid=(n // tm, ncol),
        in_specs=in_specs,
        out_specs=pl.BlockSpec((tm, tn), lambda i, j: (i, j)),
        out_shape=jax.ShapeDtypeStruct((n, d), BF16),
        compiler_params=_cparams(("parallel", "arbitrary")),
        name="gated_merge",
    )(xb, wg, wg, wg, wg, *ys, *ps)


def _layernorm_rows(z, g, b):
    mu = jnp.mean(z, axis=-1, keepdims=True)
    var = jnp.mean(jnp.square(z - mu), axis=-1, keepdims=True)
    return (z - mu) * lax.rsqrt(var + LN_EPS) * g + b


def _pack_bf16_pairs(y):
    half = y.shape[1] // 2
    lo = lax.bitcast_convert_type(y[:, :half].astype(BF16).astype(F32), jnp.uint32)
    hi = lax.bitcast_convert_type(y[:, half:].astype(BF16).astype(F32), jnp.uint32)
    return (lo >> 16) | (hi & jnp.uint32(0xFFFF0000))


def _unpack_bf16_pairs(w):
    lo = lax.bitcast_convert_type(w << 16, F32).astype(BF16)
    hi = lax.bitcast_convert_type(w & jnp.uint32(0xFFFF0000), F32).astype(BF16)
    return lo, hi


def _store_token_major(ref, val):
    t, w = val.shape
    c = w // LANES
    for s in range(c):
        ref[pl.ds(s, t, stride=c), :] = val[:, s * LANES:(s + 1) * LANES]


def _load_token_major(ref, start, t, c):
    return [ref[pl.ds(start + s, t, stride=c), :] for s in range(c)]


def _outproj_ln_kernel(m_ref, w_ref, x_ref, g_ref, b_ref, o_ref, p_ref):
    h = jnp.dot(m_ref[...], w_ref[...], preferred_element_type=F32)
    y = _layernorm_rows(DN_ALPHA * x_ref[...] + h, g_ref[...], b_ref[...])
    o_ref[...] = y
    _store_token_major(p_ref, _pack_bf16_pairs(y))


def _outproj_ln(merged, w_out, x, g, b, tm=512):
    n, d = x.shape
    c = d // 2 // LANES
    return pl.pallas_call(
        _outproj_ln_kernel,
        grid=(n // tm,),
        in_specs=[pl.BlockSpec((tm, d), lambda i: (i, 0)),
                  pl.BlockSpec((d, d), lambda i: (0, 0)),
                  pl.BlockSpec((tm, d), lambda i: (i, 0)),
                  pl.BlockSpec((1, d), lambda i: (0, 0)),
                  pl.BlockSpec((1, d), lambda i: (0, 0))],
        out_specs=[pl.BlockSpec((tm, d), lambda i: (i, 0)),
                   pl.BlockSpec((tm * c, LANES), lambda i: (i, 0))],
        out_shape=[jax.ShapeDtypeStruct((n, d), F32), jax.ShapeDtypeStruct((n * c, LANES), jnp.uint32)],
        compiler_params=_cparams(("parallel",)),
        name="outproj_ln",
    )(merged, w_out, x, g.reshape(1, d), b.reshape(1, d))


def _router_kernel(x_ref, wr_ref, rb_ref, idx_ref, wts_ref, rank_ref, cnt_ref, carry_ref, *, tr):
    gsz = N_EXPERTS // N_GROUPS

    @pl.when(pl.program_id(0) == 0)
    def _():
        carry_ref[...] = jnp.zeros_like(carry_ref)

    lane = _iota2((tr, LANES), 1)
    real = lane < N_EXPERTS
    logits = jnp.dot(x_ref[...], wr_ref[...], precision=HI, preferred_element_type=F32)
    scores = _sigmoid(logits)
    choice = jnp.where(real, scores + rb_ref[...], NEG_INF)

    def group_allreduce(v, op):
        s = 1
        while s < gsz:
            partner = jnp.where((lane & s) == 0, pltpu.roll(v, LANES - s, 1), pltpu.roll(v, s, 1))
            v = op(v, partner)
            s *= 2
        return v

    max1 = group_allreduce(choice, jnp.maximum)
    first = group_allreduce(jnp.where(choice == max1, lane, LANES), jnp.minimum)
    max2 = group_allreduce(jnp.where(lane == first, NEG_INF, choice), jnp.maximum)
    gscore = max1 + max2
    gid = lane // gsz
    gsel = jnp.zeros((tr, LANES), jnp.bool_)
    for _ in range(TOPK_GROUPS):
        gmax = jnp.max(gscore, axis=-1, keepdims=True)
        pick = jnp.min(jnp.where(gscore == gmax, gid, LANES), axis=-1, keepdims=True)
        hit = gid == pick
        gsel = jnp.logical_or(gsel, hit)
        gscore = jnp.where(hit, NEG_INF, gscore)

    cand = jnp.where(gsel, choice, NEG_INF)
    idx_out = jnp.zeros((tr, LANES), jnp.int32)
    w_out = jnp.zeros((tr, LANES), F32)
    sel = jnp.zeros((tr, LANES), F32)
    hits = []
    for k in range(TOP_K):
        vmax = jnp.max(cand, axis=-1, keepdims=True)
        pick = jnp.min(jnp.where(cand == vmax, lane, LANES), axis=-1, keepdims=True)
        hit = lane == pick
        hits.append(hit)
        wk = jnp.sum(jnp.where(hit, scores, 0.0), axis=-1, keepdims=True)
        idx_out = jnp.where(lane == k, pick, idx_out)
        w_out = jnp.where(lane == k, wk, w_out)
        sel = jnp.where(hit, 1.0, sel)
        cand = jnp.where(hit, NEG_INF, cand)
    w_out = w_out / jnp.sum(w_out, axis=-1, keepdims=True) * ROUTED_SCALE

    strict = (_iota2((tr, tr), 0) > _iota2((tr, tr), 1)).astype(BF16)
    rank = jnp.dot(strict, sel.astype(BF16), preferred_element_type=F32) + carry_ref[0:1, :]
    rank_out = jnp.zeros((tr, LANES), F32)
    for k in range(TOP_K):
        rk = jnp.sum(jnp.where(hits[k], rank, 0.0), axis=-1, keepdims=True)
        rank_out = jnp.where(lane == k, rk, rank_out)
    carry_ref[...] = carry_ref[...] + jnp.sum(sel, axis=0, keepdims=True)
    idx_ref[...] = idx_out
    wts_ref[...] = w_out
    rank_ref[...] = rank_out.astype(jnp.int32)
    cnt_ref[...] = carry_ref[...]


def _router(x1, w_router, router_bias, tr=512):
    n, d = x1.shape
    wr = jnp.zeros((d, LANES), F32).at[:, :N_EXPERTS].set(w_router)
    rb = jnp.zeros((1, LANES), F32).at[0, :N_EXPERTS].set(router_bias)
    row = pl.BlockSpec((tr, LANES), lambda i: (i, 0))
    return pl.pallas_call(
        functools.partial(_router_kernel, tr=tr),
        grid=(n // tr,),
        in_specs=[pl.BlockSpec((tr, d), lambda i: (i, 0)),
                  pl.BlockSpec((d, LANES), lambda i: (0, 0)),
                  pl.BlockSpec((1, LANES), lambda i: (0, 0))],
        out_specs=[row, row, row, pl.BlockSpec((8, LANES), lambda i: (0, 0))],
        out_shape=[jax.ShapeDtypeStruct((n, LANES), jnp.int32), jax.ShapeDtypeStruct((n, LANES), F32),
                   jax.ShapeDtypeStruct((n, LANES), jnp.int32), jax.ShapeDtypeStruct((8, LANES), F32)],
        scratch_shapes=[pltpu.VMEM((8, LANES), F32)],
        compiler_params=_cparams(("arbitrary",)),
        name="router",
    )(x1, wr, rb)


def _group_offsets(cnt, tm):
    padded = jnp.floor((cnt + (tm - 1)) * (1.0 / tm)) * tm
    upper = (_iota2((LANES, LANES), 0) < _iota2((LANES, LANES), 1)).astype(F32)
    offs = jnp.dot(padded, upper, precision=HI, preferred_element_type=F32)
    return padded, offs


def _pos_kernel(idx_ref, rank_ref, cnt_ref, pos_ref, *, tm):
    _, offs = _group_offsets(cnt_ref[...], tm)
    offs = offs[0:1, :]
    idx = idx_ref[...]
    lane = _iota2(idx.shape, 1)
    out = jnp.zeros(idx.shape, jnp.int32)
    for k in range(TOP_K):
        ok = jnp.sum(jnp.where(lane == idx[:, k:k + 1], offs, 0.0), axis=-1, keepdims=True)
        out = jnp.where(lane == k, ok.astype(jnp.int32), out)
    pos_ref[...] = out + rank_ref[...]


def _positions(idx, rank, cnt, tm, tp=1024):
    n = idx.shape[0]
    row = pl.BlockSpec((tp, LANES), lambda i: (i, 0))
    return pl.pallas_call(
        functools.partial(_pos_kernel, tm=tm),
        grid=(n // tp,),
        in_specs=[row, row, pl.BlockSpec((8, LANES), lambda i: (0, 0))],
        out_specs=row,
        out_shape=jax.ShapeDtypeStruct((n, LANES), jnp.int32),
        compiler_params=_cparams(("parallel",)),
        name="dispatch_positions",
    )(idx, rank, cnt)


def _tile_map_kernel(cnt_ref, te_ref, *, tm, width):
    padded, offs = _group_offsets(cnt_ref[...], tm)
    ends = _row_to_col((offs + padded)[0:1, :], LANES)
    expert = _iota2((LANES, width), 0)
    start = (_iota2((LANES, width), 1) * tm).astype(F32)
    done = jnp.logical_and(ends <= start, expert < N_EXPERTS)
    te = jnp.sum(jnp.where(done, 1, 0), axis=0, keepdims=True)
    total = jnp.max(jnp.where(expert < N_EXPERTS, ends, 0.0), axis=0, keepdims=True)
    ntile = (total * (1.0 / tm)).astype(jnp.int32)
    vend = _row_to_col((offs + cnt_ref[...])[0:1, :], LANES)
    mine = jnp.sum(jnp.where(expert == te, vend, 0.0), axis=0, keepdims=True)
    valid = jnp.clip(mine - start[0:1, :], 0.0, float(tm)).astype(jnp.int32)
    row = _iota2((8, width), 0)
    te_ref[...] = jnp.where(row == 0, te, jnp.where(row == 1, ntile, jnp.where(row == 2, valid, 0)))


def _tile_map(cnt, tm, ntile_max):
    width = -(-ntile_max // LANES) * LANES
    out = pl.pallas_call(
        functools.partial(_tile_map_kernel, tm=tm, width=width),
        in_specs=[pl.BlockSpec((8, LANES), lambda: (0, 0))],
        out_specs=pl.BlockSpec((8, width), lambda: (0, 0)),
        out_shape=jax.ShapeDtypeStruct((8, width), jnp.int32),
        name="tile_map",
    )(cnt)
    return out[0, :ntile_max], out[1, :1], out[2, :ntile_max]


SC_SCATTER_ROWS = 64
SC_GATHER_ROWS = 32


def _sc_mesh():
    info = plsc.get_sparse_core_info()
    mesh = plsc.VectorSubcoreMesh(core_axis_name="c", subcore_axis_name="s")
    return mesh, info.num_cores, info.num_subcores


def _sc_scatter_rows(src3, idx3, nrows):
    n, c, _ = src3.shape
    _, nk, chunk = idx3.shape
    mesh, ncore, nsub = _sc_mesh()
    per_w = n // (ncore * nsub)
    assert chunk == SC_SCATTER_ROWS and per_w % chunk == 0

    @functools.partial(
        pl.kernel, mesh=mesh,
        out_type=jax.ShapeDtypeStruct((nrows, c, LANES), src3.dtype),
        scratch_types=[pltpu.VMEM((nk, chunk), jnp.int32),
                       pltpu.VMEM((chunk, c, LANES), src3.dtype),
                       pltpu.SemaphoreType.DMA],
    )
    def scatter_kernel(src_hbm, idx_hbm, out_hbm, idx_v, rows_v, sem):
        base = (lax.axis_index("s") * ncore + lax.axis_index("c")) * per_w

        @pl.loop(0, per_w // chunk)
        def _(j):
            off = pl.multiple_of(base + j * chunk, chunk)
            pltpu.sync_copy(idx_hbm.at[base // chunk + j], idx_v)
            pltpu.sync_copy(src_hbm.at[pl.ds(off, chunk)], rows_v)
            copies = [pltpu.async_copy(rows_v, out_hbm.at[idx_v.at[k]], sem) for k in range(nk)]
            for cp in copies:
                cp.wait()

    return scatter_kernel(src3, idx3)


def _sc_gather_rows(table3, idx2):
    _, c, _ = table3.shape
    nchunk, chunk = idx2.shape
    mesh, ncore, nsub = _sc_mesh()
    per_w = nchunk // (ncore * nsub)
    assert chunk == SC_GATHER_ROWS and per_w % 2 == 0

    @functools.partial(
        pl.kernel, mesh=mesh,
        out_type=jax.ShapeDtypeStruct((nchunk * chunk, c, LANES), table3.dtype),
        scratch_types=[pltpu.VMEM((per_w, chunk), jnp.int32),
                       pltpu.VMEM((chunk, c, LANES), table3.dtype),
                       pltpu.VMEM((chunk, c, LANES), table3.dtype),
                       pltpu.SemaphoreType.DMA, pltpu.SemaphoreType.DMA,
                       pltpu.SemaphoreType.DMA, pltpu.SemaphoreType.DMA],
    )
    def gather_kernel(table_hbm, idx_hbm, out_hbm, idx_v, rows0, rows1, g0, g1, w0, w1):
        first = (lax.axis_index("s") * ncore + lax.axis_index("c")) * per_w
        pltpu.sync_copy(idx_hbm.at[pl.ds(pl.multiple_of(first, per_w), per_w)], idx_v)

        @pl.loop(0, per_w, step=2)
        def _(j):
            ga = pltpu.async_copy(table_hbm.at[idx_v.at[j]], rows0, g0)
            gb = pltpu.async_copy(table_hbm.at[idx_v.at[j + 1]], rows1, g1)
            ga.wait()
            wa = pltpu.async_copy(rows0, out_hbm.at[pl.ds(pl.multiple_of((first + j) * chunk, chunk), chunk)], w0)
            gb.wait()
            wb = pltpu.async_copy(rows1, out_hbm.at[pl.ds(pl.multiple_of((first + j + 1) * chunk, chunk), chunk)], w1)
            wa.wait()
            wb.wait()

    return gather_kernel(table3, idx2)


FFN_LOOKAHEAD = 2
FFN_SLOTS = FFN_LOOKAHEAD + 1


def _ffn_kernel(te_ref, nt_ref, nv_ref, xs_ref, w1_ref, w3_ref, w2_ref, o_ref, w1b, w3b, w2b, ord_ref, *, tm):
    j = pl.program_id(0)
    i = j - FFN_LOOKAHEAD
    nt = nt_ref[0]

    @pl.when(j == 0)
    def _():
        for s in range(4):
            ord_ref[s] = 0

    def starts_group(t):
        tc = jnp.clip(t, 0, nt - 1)
        changed = te_ref[tc] != te_ref[jnp.maximum(tc - 1, 0)]
        return jnp.logical_or(t == 0, jnp.logical_and(jnp.logical_and(t > 0, t < nt), changed))

    for s, (w_ref, wb) in enumerate(((w1_ref, w1b), (w3_ref, w3b), (w2_ref, w2b))):
        @pl.when(starts_group(j - s))
        def _(s=s, w_ref=w_ref, wb=wb):
            wb[ord_ref[s] % FFN_SLOTS] = w_ref[0, 0].astype(BF16)
            ord_ref[s] = ord_ref[s] + 1

    @pl.when(jnp.logical_and(i >= 0, i < nt))
    def _():
        ic = jnp.maximum(i, 0)

        @pl.when(jnp.logical_and(i > 0, starts_group(i)))
        def _():
            ord_ref[3] = ord_ref[3] + 1

        slot = ord_ref[3] % FFN_SLOTS
        c = xs_ref.shape[0] // tm
        live = _iota2((tm, LANES), 0) < nv_ref[ic]
        parts = [_unpack_bf16_pairs(jnp.where(live, p, jnp.uint32(0)))
                 for p in _load_token_major(xs_ref, 0, tm, c)]
        lo = jnp.concatenate([p[0] for p in parts], axis=1)
        hi = jnp.concatenate([p[1] for p in parts], axis=1)
        half = lo.shape[1]
        h1 = (jnp.dot(lo, w1b[slot, :half, :], preferred_element_type=F32)
              + jnp.dot(hi, w1b[slot, half:, :], preferred_element_type=F32))
        h3 = (jnp.dot(lo, w3b[slot, :half, :], preferred_element_type=F32)
              + jnp.dot(hi, w3b[slot, half:, :], preferred_element_type=F32))
        h = (_silu(h1) * h3).astype(BF16)
        y = jnp.dot(h, w2b[slot], preferred_element_type=F32)
        _store_token_major(o_ref, _pack_bf16_pairs(y))

    @pl.when(i >= nt)
    def _():
        o_ref[...] = jnp.zeros_like(o_ref)


def _expert_ffn(te, nt, nv, xs, w1, w3, w2, layer, nrows, tm):
    d, f = w1.shape[2], w1.shape[3]
    c = xs.shape[0] // nrows
    ntile = nrows // tm

    def tile(j, te_r, nt_r, nv_r):
        return (jnp.clip(j - FFN_LOOKAHEAD, 0, nt_r[0] - 1), 0)

    def out_tile(j, te_r, nt_r, nv_r):
        return (jnp.maximum(j - FFN_LOOKAHEAD, 0), 0)

    def expert(delay):
        def index_map(j, te_r, nt_r, nv_r):
            return (layer, te_r[jnp.clip(j - delay, 0, nt_r[0] - 1)], 0, 0)
        return index_map

    assert FFN_LOOKAHEAD == 2
    grid_spec = pltpu.PrefetchScalarGridSpec(
        num_scalar_prefetch=3,
        grid=(ntile + FFN_LOOKAHEAD,),
        in_specs=[pl.BlockSpec((tm * c, LANES), tile),
                  pl.BlockSpec((1, 1, d, f), expert(0)),
                  pl.BlockSpec((1, 1, d, f), expert(1)),
                  pl.BlockSpec((1, 1, f, d), expert(2))],
        out_specs=pl.BlockSpec((tm * c, LANES), out_tile),
        scratch_shapes=[pltpu.VMEM((FFN_SLOTS, d, f), BF16), pltpu.VMEM((FFN_SLOTS, d, f), BF16),
                        pltpu.VMEM((FFN_SLOTS, f, d), BF16), pltpu.SMEM((4,), jnp.int32)],
    )
    return pl.pallas_call(
        functools.partial(_ffn_kernel, tm=tm),
        grid_spec=grid_spec,
        out_shape=jax.ShapeDtypeStruct((nrows * c, LANES), jnp.uint32),
        compiler_params=_cparams(("arbitrary",)),
        name="expert_ffn",
    )(te, nt, nv, xs, w1, w3, w2)


def _combine_kernel(*refs, tc):
    y_refs = refs[:TOP_K]
    wts_ref, x_ref, ws1_ref, ws3_ref, ws2_ref, g_ref, b_ref, o_ref, ob_ref = refs[TOP_K:]
    c = y_refs[0].shape[0] // tc
    x = x_ref[...]
    xb = x.astype(BF16)
    s1 = jnp.dot(xb, ws1_ref[...], preferred_element_type=F32)
    s3 = jnp.dot(xb, ws3_ref[...], preferred_element_type=F32)
    acc = jnp.dot((_silu(s1) * s3).astype(BF16), ws2_ref[...], preferred_element_type=F32)
    wts = wts_ref[...]
    for k in range(TOP_K):
        parts = [_unpack_bf16_pairs(p) for p in _load_token_major(y_refs[k], 0, tc, c)]
        yk = jnp.concatenate([p[0] for p in parts] + [p[1] for p in parts], axis=1).astype(F32)
        acc = acc + wts[:, k:k + 1] * yk
    y = _layernorm_rows(DN_ALPHA * x + acc, g_ref[...], b_ref[...])
    o_ref[...] = y
    ob_ref[...] = y.astype(BF16)


def _combine(yg, wts, x1, ws1, ws3, ws2, g, b, tc=256):
    n, d = x1.shape
    sf = ws1.shape[1]
    c = yg.shape[0] // (TOP_K * n)
    nblk = n // tc

    def slot_spec(k):
        return pl.BlockSpec((tc * c, LANES), lambda i: (k * nblk + i, 0))

    return pl.pallas_call(
        functools.partial(_combine_kernel, tc=tc),
        grid=(nblk,),
        in_specs=[slot_spec(k) for k in range(TOP_K)] + [
            pl.BlockSpec((tc, LANES), lambda i: (i, 0)),
            pl.BlockSpec((tc, d), lambda i: (i, 0)),
            pl.BlockSpec((d, sf), lambda i: (0, 0)),
            pl.BlockSpec((d, sf), lambda i: (0, 0)),
            pl.BlockSpec((sf, d), lambda i: (0, 0)),
            pl.BlockSpec((1, d), lambda i: (0, 0)),
            pl.BlockSpec((1, d), lambda i: (0, 0))],
        out_specs=[pl.BlockSpec((tc, d), lambda i: (i, 0)), pl.BlockSpec((tc, d), lambda i: (i, 0))],
        out_shape=[jax.ShapeDtypeStruct((n, d), F32), jax.ShapeDtypeStruct((n, d), BF16)],
        compiler_params=_cparams(("parallel",)),
        name="moe_combine",
    )(*([yg] * TOP_K), wts, x1, ws1, ws3, ws2, g.reshape(1, d), b.reshape(1, d))


FFN_TILE = 256


def _moe(x1, x1p, w_router, router_bias, w1, w3, w2, layer, ws1, ws3, ws2, g, b):
    n = x1.shape[0]
    tm = FFN_TILE
    nrows = n * TOP_K + N_EXPERTS * tm
    idx, wts, rank, cnt = _router(x1, w_router, router_bias)
    pos = _positions(idx, rank, cnt, tm)
    te, nt, nv = _tile_map(cnt, tm, nrows // tm)
    c = x1p.shape[0] // n
    pos_t = pos[:, :TOP_K].T
    pos_s = pos_t.reshape(TOP_K, n // SC_SCATTER_ROWS, SC_SCATTER_ROWS).transpose(1, 0, 2)
    xs = _sc_scatter_rows(x1p.reshape(n, c, LANES), pos_s, nrows)
    ys = _expert_ffn(te, nt, nv, xs.reshape(nrows * c, LANES), w1, w3, w2, layer, nrows, tm)
    yg = _sc_gather_rows(ys.reshape(nrows, c, LANES), pos_t.reshape(-1, SC_GATHER_ROWS))
    return _combine(yg.reshape(TOP_K * n * c, LANES), wts, x1,
                    ws1.astype(BF16), ws3.astype(BF16), ws2.astype(BF16), g, b)


def kernel(x, positions, w_in, gla_wa2, gla_ba, gla_norm, mlstm_conv_w, mlstm_conv_b, mlstm_bi, mlstm_bf,
           mlstm_norm, sgu_ln_g, sgu_ln_b, sgu_ws, sgu_bs, w_pa, w_pb, w_pc, w_pd, w_out, ln1_g, ln1_b,
           w_router, router_bias, w1, w3, w2, ws1, ws3, ws2, ln2_g, ln2_b):
    nbatch, seq, d = x.shape
    n = nbatch * seq
    xf = x.reshape(n, d)
    xb = xf.astype(BF16)
    cos_t, sin_t = _rope_tables(positions)
    for l in range(DEPTH):
        wl = w_in[l]
        w_a = wl[:, _OFF_A:_OFF_A + _W_A].astype(BF16)
        w_b = wl[:, _OFF_B:_OFF_B + _W_B].astype(BF16)
        w_c = wl[:, _OFF_C:_OFF_C + _W_C].astype(BF16)
        w_d = wl[:, _OFF_D:_OFF_D + _W_D].astype(BF16)
        w_g = wl[:, _OFF_G:_OFF_G + _W_G].astype(BF16)
        w_s = jnp.concatenate(
            [wl[:, _OFF_LOW:_OFF_LOW + GLA_RANK], wl[:, _OFF_IF:_OFF_IF + 2 * MLSTM_HEADS],
             jnp.zeros((d, LANES - GLA_RANK - 2 * MLSTM_HEADS), F32)], axis=1).astype(BF16)
        y_a = _matmul(xb, w_a, 1024, 512)
        y_b = _matmul(xb, w_b, 1024, 768)
        y_c = _matmul(xb, w_c, 1024, 512)
        y_d = _matmul(xb, w_d, 1024, 512)
        y_s = _matmul(xb, w_s, 1024, LANES)
        sm3 = y_s.reshape(nbatch, seq, LANES)
        o_a, o_d = _recurrent_mixers(
            y_a.reshape(nbatch, seq, _W_A), y_d.reshape(nbatch, seq, _W_D), sm3, gla_wa2[l], gla_ba[l],
            gla_norm[l], mlstm_conv_w[l], mlstm_conv_b[l], mlstm_bi[l], mlstm_bf[l], mlstm_norm[l])
        o_a = o_a.reshape(n, GLA_V)
        o_d = o_d.reshape(n, ML_W)
        o_b = _dilated(y_b, cos_t, sin_t, nbatch)
        o_c = _sgu(y_c, sgu_ln_g[l], sgu_ln_b[l], sgu_ws[l], sgu_bs[l])
        merged = _merge(xb, w_g, (o_a, o_b, o_c, o_d),
                        (w_pa[l].astype(BF16), w_pb[l].astype(BF16), w_pc[l].astype(BF16), w_pd[l].astype(BF16)))
        x1, x1p = _outproj_ln(merged, w_out[l].astype(BF16), xf, ln1_g[l], ln1_b[l])
        xf, xb = _moe(x1, x1p, w_router[l], router_bias[l], w1, w3, w2, l, ws1[l], ws3[l], ws2[l],
                      ln2_g[l], ln2_b[l])
    return xf.reshape(nbatch, seq, d)
```

```python
import functools
import math

import jax
import jax.numpy as jnp
from jax import lax
from jax.experimental import pallas as pl
from jax.experimental.pallas import tpu as pltpu
from jax.experimental.pallas import tpu_sc as plsc

D_MODEL = 2048
DEPTH = 2

GLA_HEADS = 4
GLA_DK = 64
GLA_DV = 128
GLA_RANK = 16
GLA_TAU = 16.0
GLA_CHUNK = 64
GLA_QK = GLA_HEADS * GLA_DK
GLA_V = GLA_HEADS * GLA_DV

DIL_PAIRS = ((128, 1), (512, 4), (2048, 16))
DIL_HEADS_PER_GROUP = 4
DIL_HEAD_DIM = 64
DIL_HEADS = len(DIL_PAIRS) * DIL_HEADS_PER_GROUP
DIL_W = DIL_HEADS * DIL_HEAD_DIM
DIL_OUT = DIL_HEADS_PER_GROUP * DIL_HEAD_DIM
DIL_BLOCK = 128
ROPE_THETA = 10000.0

SGU_CHUNK = 128
SGU_GROUPS = 6
SGU_GROUP_CH = 128
SGU_W = SGU_GROUPS * SGU_GROUP_CH

MLSTM_HEADS = 4
MLSTM_HEAD_DIM = 128
MLSTM_CHUNK = 64
MLSTM_CONV = 4
ML_W = MLSTM_HEADS * MLSTM_HEAD_DIM

N_EXPERTS = 64
TOP_K = 8
N_GROUPS = 8
TOPK_GROUPS = 4
EXPERT_FF = 512
SHARED_FF = 512
ROUTED_SCALE = 2.5

N_BRANCH = 4
DN_ALPHA = (2 * DEPTH) ** 0.25
LN_EPS = 1e-5

_OFF_A = 0
_W_A = 2 * GLA_QK + 2 * GLA_V
_OFF_LOW = _OFF_A + _W_A
_OFF_B = _OFF_LOW + GLA_RANK
_W_B = 3 * DIL_W
_OFF_C = _OFF_B + _W_B
_W_C = 2 * SGU_W
_OFF_D = _OFF_C + _W_C
_W_D = 4 * ML_W
_OFF_IF = _OFF_D + _W_D
_OFF_G = _OFF_IF + 2 * MLSTM_HEADS
_W_G = N_BRANCH * D_MODEL

LANES = 128
VMEM_LIMIT = 56 * 1024 * 1024

_SM_LOW = 0
_SM_I = GLA_RANK
_SM_F = GLA_RANK + MLSTM_HEADS

HI = lax.Precision.HIGHEST
F32 = jnp.float32
BF16 = jnp.bfloat16
NEG_INF = float("-inf")


def _cparams(sem):
    return pltpu.CompilerParams(dimension_semantics=sem, vmem_limit_bytes=VMEM_LIMIT)


def _log_sigmoid(x):
    return jnp.minimum(x, 0.0) - jnp.log1p(jnp.exp(-jnp.abs(x)))


def _sigmoid(x):
    return 1.0 / (1.0 + jnp.exp(-x))


def _silu(x):
    return x * _sigmoid(x)


def _iota2(shape, dim):
    return lax.broadcasted_iota(jnp.int32, shape, dim)


def _col_to_row(col, n):
    eye = _iota2((n, n), 0) == _iota2((n, n), 1)
    return jnp.sum(jnp.where(eye, col, 0.0), axis=0, keepdims=True)


def _row_to_col(row, n):
    eye = _iota2((n, n), 0) == _iota2((n, n), 1)
    return jnp.sum(jnp.where(eye, row, 0.0), axis=1, keepdims=True)


def _mm_kernel(x_ref, w_ref, o_ref):
    o_ref[...] = jnp.dot(x_ref[...], w_ref[...], preferred_element_type=F32).astype(o_ref.dtype)


def _matmul(x, w, tm, tn, out_dtype=F32):
    n, k = x.shape
    m = w.shape[1]
    return pl.pallas_call(
        _mm_kernel,
        grid=(n // tm, m // tn),
        in_specs=[pl.BlockSpec((tm, k), lambda i, j: (i, 0)),
                  pl.BlockSpec((k, tn), lambda i, j: (0, j))],
        out_specs=pl.BlockSpec((tm, tn), lambda i, j: (i, j)),
        out_shape=jax.ShapeDtypeStruct((n, m), out_dtype),
        compiler_params=_cparams(("parallel", "arbitrary")),
        name="in_proj",
    )(x, w)


def _gla_chunk(y_ref, sm_ref, wa2_ref, ba_ref, g_ref, o_ref, state_ref, *, nb):
    L, H, DK, DV = GLA_CHUNK, GLA_HEADS, GLA_DK, GLA_DV
    tril = (_iota2((L, L), 0) >= _iota2((L, L), 1))
    tril_f = tril.astype(F32)
    for b in range(nb):
        y = y_ref[b]
        a_low = sm_ref[b][:, _SM_LOW:_SM_LOW + GLA_RANK]
        glog = jnp.dot(a_low, wa2_ref[...], preferred_element_type=F32) + ba_ref[...]
        g = _log_sigmoid(glog) * (1.0 / GLA_TAU)
        bc = jnp.dot(tril_f, g, precision=HI, preferred_element_type=F32)
        outs = []
        for h in range(H):
            q = y[:, h * DK:(h + 1) * DK] * (DK ** -0.5)
            k = y[:, GLA_QK + h * DK:GLA_QK + (h + 1) * DK]
            v = y[:, 2 * GLA_QK + h * DV:2 * GLA_QK + (h + 1) * DV]
            bh = bc[:, h * DK:(h + 1) * DK]
            qe = q * jnp.exp(bh)
            ke = k * jnp.exp(-bh)
            att = lax.dot_general(qe, ke, (((1,), (1,)), ((), ())), preferred_element_type=F32)
            att = jnp.where(tril, att, 0.0)
            st = state_ref[b * H + h]
            o = (jnp.dot(att, v, preferred_element_type=F32)
                 + jnp.dot(qe, st, preferred_element_type=F32))
            b_last = bh[L - 1:L, :]
            kd = k * jnp.exp(b_last - bh)
            decay = _row_to_col(jnp.exp(b_last), DK)
            state_ref[b * H + h] = decay * st + lax.dot_general(
                kd, v, (((0,), (0,)), ((), ())), preferred_element_type=F32)
            o = o * lax.rsqrt(jnp.mean(o * o, axis=-1, keepdims=True) + LN_EPS)
            outs.append(o)
        o_all = jnp.concatenate(outs, axis=-1) * g_ref[...]
        r = y[:, 2 * GLA_QK + GLA_V:2 * GLA_QK + 2 * GLA_V]
        o_ref[b] = (o_all * _silu(r)).astype(o_ref.dtype)


MLSTM_HALO = 8


def _mlstm_chunk(y_ref, sm_ref, cw_ref, cb_ref, gb_ref, g_ref, o_ref, c_ref, n_ref, m_ref, tail_ref, *, nb):
    L, H, DH = MLSTM_CHUNK, MLSTM_HEADS, MLSTM_HEAD_DIM
    W2 = 2 * ML_W
    HALO = MLSTM_HALO
    tril = (_iota2((L, L), 0) >= _iota2((L, L), 1))
    tril_f = tril.astype(F32)
    for b in range(nb):
        y = y_ref[b]
        qk_raw = y[:, :W2]
        ext = jnp.concatenate([tail_ref[b], qk_raw], axis=0)
        tail_ref[b] = qk_raw[L - HALO:, :]
        conv = cb_ref[...]
        for j in range(MLSTM_CONV):
            s0 = HALO - (MLSTM_CONV - 1) + j
            conv = conv + cw_ref[j:j + 1, :] * ext[s0:s0 + L, :]
        qk = _silu(conv)
        gates = sm_ref[b] + gb_ref[...]
        bcum = jnp.dot(tril_f, _log_sigmoid(gates), precision=HI, preferred_element_type=F32)
        outs = []
        for h in range(H):
            q = qk[:, h * DH:(h + 1) * DH]
            k = qk[:, ML_W + h * DH:ML_W + (h + 1) * DH] * (DH ** -0.5)
            v = y[:, W2 + h * DH:W2 + (h + 1) * DH]
            b_col = bcum[:, _SM_F + h:_SM_F + h + 1]
            li_col = gates[:, _SM_I + h:_SM_I + h + 1]
            b_row = _col_to_row(b_col, L)
            li_row = _col_to_row(li_col, L)
            m_prev = m_ref[b * H + h][:, 0:1]
            dmat = jnp.where(tril, b_col - b_row + li_row, NEG_INF)
            inter = b_col + m_prev
            m_t = jnp.maximum(inter, jnp.max(dmat, axis=-1, keepdims=True))
            w_in = jnp.exp(dmat - m_t)
            w_st = jnp.exp(inter - m_t)
            sc = lax.dot_general(q, k, (((1,), (1,)), ((), ())), preferred_element_type=F32) * w_in
            cst = c_ref[b * H + h]
            nst = n_ref[b * H + h]
            num = (jnp.dot(sc, v, preferred_element_type=F32)
                   + w_st * jnp.dot(q, cst, preferred_element_type=F32))
            den = jnp.sum(sc, axis=-1, keepdims=True) + w_st * jnp.sum(q * nst, axis=-1, keepdims=True)
            hh = num / jnp.maximum(jnp.abs(den), jnp.exp(-m_t))
            b_last = b_col[L - 1:L, :]
            dec = b_last - b_col + li_col
            m_new = jnp.maximum(b_last + m_prev, jnp.max(dec, axis=0, keepdims=True))
            wk = jnp.exp(dec - m_new)
            keep = jnp.exp(b_last + m_prev - m_new)
            wkk = wk * k
            c_ref[b * H + h] = keep * cst + lax.dot_general(
                wkk, v, (((0,), (0,)), ((), ())), preferred_element_type=F32)
            n_ref[b * H + h] = keep * nst + jnp.sum(wkk, axis=0, keepdims=True)
            m_ref[b * H + h] = jnp.broadcast_to(m_new, (1, LANES))
            o_pre = y[:, W2 + ML_W + h * DH:W2 + ML_W + (h + 1) * DH]
            hh = _sigmoid(o_pre) * hh
            hh = hh * lax.rsqrt(jnp.mean(hh * hh, axis=-1, keepdims=True) + LN_EPS)
            outs.append(hh)
        o_ref[b] = (jnp.concatenate(outs, axis=-1) * g_ref[...]).astype(o_ref.dtype)


def _recurrent_kernel(ya_ref, yd_ref, sm_ref, wa2_ref, ba_ref, ga_ref, cw_ref, cb_ref, gb_ref, gd_ref,
                      oa_ref, od_ref, state_ref, c_ref, n_ref, m_ref, tail_ref, *, nb):
    @pl.when(pl.program_id(0) == 0)
    def _():
        for ref in (state_ref, c_ref, n_ref, m_ref, tail_ref):
            ref[...] = jnp.zeros_like(ref)

    _gla_chunk(ya_ref, sm_ref, wa2_ref, ba_ref, ga_ref, oa_ref, state_ref, nb=nb)
    _mlstm_chunk(yd_ref, sm_ref, cw_ref, cb_ref, gb_ref, gd_ref, od_ref, c_ref, n_ref, m_ref, tail_ref, nb=nb)


def _recurrent_mixers(ya3, yd3, sm3, wa2, ba, gla_norm, conv_w, conv_b, b_i, b_f, mlstm_norm):
    nb, s, _ = ya3.shape
    assert GLA_CHUNK == MLSTM_CHUNK
    L = GLA_CHUNK
    gate_bias = jnp.zeros((1, LANES), F32)
    gate_bias = gate_bias.at[0, _SM_I:_SM_I + MLSTM_HEADS].set(b_i).at[0, _SM_F:_SM_F + MLSTM_HEADS].set(b_f)

    def chunk(width):
        return pl.BlockSpec((nb, L, width), lambda n: (0, n, 0))

    def whole(r, c):
        return pl.BlockSpec((r, c), lambda n: (0, 0))

    return pl.pallas_call(
        functools.partial(_recurrent_kernel, nb=nb),
        grid=(s // L,),
        in_specs=[chunk(_W_A), chunk(_W_D), chunk(LANES),
                  whole(GLA_RANK, GLA_QK), whole(1, GLA_QK), whole(1, GLA_V),
                  whole(MLSTM_CONV, 2 * ML_W), whole(1, 2 * ML_W), whole(1, LANES), whole(1, ML_W)],
        out_specs=[chunk(GLA_V), chunk(ML_W)],
        out_shape=[jax.ShapeDtypeStruct((nb, s, GLA_V), BF16), jax.ShapeDtypeStruct((nb, s, ML_W), BF16)],
        scratch_shapes=[pltpu.VMEM((nb * GLA_HEADS, GLA_DK, GLA_DV), F32),
                        pltpu.VMEM((nb * MLSTM_HEADS, MLSTM_HEAD_DIM, MLSTM_HEAD_DIM), F32),
                        pltpu.VMEM((nb * MLSTM_HEADS, 1, MLSTM_HEAD_DIM), F32),
                        pltpu.VMEM((nb * MLSTM_HEADS, 1, LANES), F32),
                        pltpu.VMEM((nb, MLSTM_HALO, 2 * ML_W), F32)],
        compiler_params=_cparams(("arbitrary",)),
        name="gla_mlstm",
    )(ya3, yd3, sm3, wa2, ba.reshape(1, GLA_QK), gla_norm.reshape(1, GLA_V),
      conv_w, conv_b.reshape(1, 2 * ML_W), gate_bias, mlstm_norm.reshape(1, ML_W))


def _gelu(x):
    return 0.5 * x * (1.0 + lax.erf(x * (0.5 ** 0.5)))


def _sgu_kernel(y_ref, lg_ref, lb_ref, ws_ref, bst_ref, o_ref, *, nchunk):
    C, G, GC = SGU_CHUNK, SGU_GROUPS, SGU_GROUP_CH
    y = y_ref[...]
    zu = _gelu(y[:, :SGU_W])
    zv = _gelu(y[:, SGU_W:])
    mu = jnp.mean(zv, axis=-1, keepdims=True)
    var = jnp.mean(jnp.square(zv - mu), axis=-1, keepdims=True)
    vn = (zv - mu) * lax.rsqrt(var + LN_EPS) * lg_ref[...] + lb_ref[...]
    tril = _iota2((C, C), 0) >= _iota2((C, C), 1)
    for g in range(G):
        wc = jnp.where(tril, ws_ref[g], 0.0)
        bias = bst_ref[:, g:g + 1]
        for c in range(nchunk):
            rows = slice(c * C, (c + 1) * C)
            cols = slice(g * GC, (g + 1) * GC)
            s = jnp.dot(wc, vn[rows, cols], preferred_element_type=F32) + bias
            o_ref[rows, cols] = (zu[rows, cols] * s).astype(o_ref.dtype)


def _sgu(yc, ln_g, ln_b, ws, bs, nchunk=2):
    n = yc.shape[0]
    t = nchunk * SGU_CHUNK
    bst = jnp.zeros((SGU_CHUNK, LANES), F32).at[:, :SGU_GROUPS].set(bs.T)
    return pl.pallas_call(
        functools.partial(_sgu_kernel, nchunk=nchunk),
        grid=(n // t,),
        in_specs=[pl.BlockSpec((t, _W_C), lambda i: (i, 0)),
                  pl.BlockSpec((1, SGU_W), lambda i: (0, 0)),
                  pl.BlockSpec((1, SGU_W), lambda i: (0, 0)),
                  pl.BlockSpec((SGU_GROUPS, SGU_CHUNK, SGU_CHUNK), lambda i: (0, 0, 0)),
                  pl.BlockSpec((SGU_CHUNK, LANES), lambda i: (0, 0))],
        out_specs=pl.BlockSpec((t, SGU_W), lambda i: (i, 0)),
        out_shape=jax.ShapeDtypeStruct((n, SGU_W), BF16),
        compiler_params=_cparams(("parallel",)),
        name="sgu",
    )(yc, ln_g.reshape(1, SGU_W), ln_b.reshape(1, SGU_W), ws, bst)


def _rope_table_kernel(pos_ref, inv_ref, cos_ref, sin_ref):
    ang = pos_ref[...].astype(F32) * inv_ref[...]
    half = DIL_HEAD_DIM // 2
    sign = jnp.where((_iota2(ang.shape, 1) % DIL_HEAD_DIM) < half, -1.0, 1.0)
    cos_ref[...] = jnp.cos(ang)
    sin_ref[...] = jnp.sin(ang) * sign


def _rope_tables(positions):
    n = positions.size
    half = DIL_HEAD_DIM // 2
    inv = ROPE_THETA ** (-jnp.arange(half, dtype=F32) * 2.0 / DIL_HEAD_DIM)
    inv = jnp.tile(inv, LANES // half).reshape(1, LANES)
    t = 1024
    return pl.pallas_call(
        _rope_table_kernel,
        grid=(n // t,),
        in_specs=[pl.BlockSpec((t, 1), lambda i: (i, 0)),
                  pl.BlockSpec((1, LANES), lambda i: (0, 0))],
        out_specs=[pl.BlockSpec((t, LANES), lambda i: (i, 0)),
                   pl.BlockSpec((t, LANES), lambda i: (i, 0))],
        out_shape=[jax.ShapeDtypeStruct((n, LANES), F32)] * 2,
        compiler_params=_cparams(("parallel",)),
        name="rope_tables",
    )(positions.reshape(n, 1), inv)


def _dil_kernel(q0, q1, q2, k0, k1, k2, v0, v1, v2, cos_ref, sin_ref, o_ref,
                qs_ref, ks_ref, num_ref, m_ref, den_ref, *, seq):
    DH, BLK = DIL_HEAD_DIM, DIL_BLOCK
    half = DH // 2
    q_refs, k_refs, v_refs = (q0, q1, q2), (k0, k1, k2), (v0, v1, v2)
    cos = cos_ref[...]
    sin = sin_ref[...]
    first_half = (_iota2((seq, LANES), 1) % DH) < half

    def rope(x):
        swapped = jnp.where(first_half, pltpu.roll(x, LANES - half, 1), pltpu.roll(x, half, 1))
        return x * cos + swapped * sin

    for g in range(len(DIL_PAIRS)):
        qs_ref[g] = rope(q_refs[g][...]) * (DH ** -0.5)
        ks_ref[g] = rope(k_refs[g][...])

    ii = _iota2((BLK, BLK), 0)
    jj = _iota2((BLK, BLK), 1)
    mask_cur = jj <= ii
    mask_prev = jj >= ii
    assert LANES == 2 * DH and BLK == LANES
    head_lanes = [(jj // DH) == h for h in range(LANES // DH)]
    head_ones = [hl.astype(F32) for hl in head_lanes]

    for g, (window, dil) in enumerate(DIL_PAIRS):
        lsub = seq // dil
        nblk = lsub // BLK
        assert window // dil == BLK and lsub % BLK == 0
        v_ref = v_refs[g]

        def unit(u, carry, g=g, dil=dil, nblk=nblk, v_ref=v_ref):
            r = u % dil
            n = u // dil
            rows = pl.ds(n * (BLK * dil) + r, BLK, stride=dil)
            qb = qs_ref[g, rows, :]
            kc = ks_ref[g, rows, :]
            vc = v_ref[rows, :]
            if nblk > 1:
                prow = pl.ds(jnp.maximum(n - 1, 0) * (BLK * dil) + r, BLK, stride=dil)
                kp = ks_ref[g, prow, :]
                vp = v_ref[prow, :]
                has_prev = n > 0
            num = jnp.zeros((BLK, LANES), F32)
            den = jnp.zeros((BLK, LANES), F32)
            mxs = []
            for h in range(LANES // DH):
                qh = jnp.where(head_lanes[h], qb, 0.0)
                s_c = lax.dot_general(qh, kc, (((1,), (1,)), ((), ())), preferred_element_type=F32)
                s_c = jnp.where(mask_cur, s_c, NEG_INF)
                if nblk > 1:
                    s_p = lax.dot_general(qh, kp, (((1,), (1,)), ((), ())), preferred_element_type=F32)
                    s_p = jnp.where(jnp.logical_and(mask_prev, has_prev), s_p, NEG_INF)
                    mx = jnp.max(jnp.maximum(s_c, s_p), axis=-1, keepdims=True)
                else:
                    mx = jnp.max(s_c, axis=-1, keepdims=True)
                p_c = jnp.exp(s_c - mx)
                num = num + jnp.dot(p_c, jnp.where(head_lanes[h], vc, 0.0), preferred_element_type=F32)
                den = den + jnp.dot(p_c, head_ones[h], preferred_element_type=F32)
                if nblk > 1:
                    p_p = jnp.exp(s_p - mx)
                    num = num + jnp.dot(p_p, jnp.where(head_lanes[h], vp, 0.0), preferred_element_type=F32)
                    den = den + jnp.dot(p_p, head_ones[h], preferred_element_type=F32)
                mxs.append(mx)
            num_ref[g, rows, :] = num
            m_ref[g, rows, :] = jnp.where(head_lanes[0], mxs[0], mxs[1])
            den_ref[g, rows, :] = den
            return carry

        lax.fori_loop(0, dil * nblk, unit, 0, unroll=4)

    m_all = jnp.maximum(jnp.maximum(m_ref[0], m_ref[1]), m_ref[2])
    num = jnp.zeros((seq, LANES), F32)
    den = jnp.zeros((seq, LANES), F32)
    for g in range(len(DIL_PAIRS)):
        e = jnp.exp(m_ref[g] - m_all)
        num = num + e * num_ref[g]
        den = den + e * den_ref[g]
    o_ref[...] = (num / den).astype(o_ref.dtype)


def _dilated(yb, cos_t, sin_t, nbatch):
    n = yb.shape[0]
    seq = n // nbatch
    npair = DIL_HEADS_PER_GROUP * DIL_HEAD_DIM // LANES
    nblk_cols = DIL_W // LANES

    def spec(section, g):
        return pl.BlockSpec((seq, LANES), lambda b, p: (b, section * nblk_cols + g * npair + p))

    in_specs = ([spec(0, g) for g in range(3)] + [spec(1, g) for g in range(3)] + [spec(2, g) for g in range(3)]
                + [pl.BlockSpec((seq, LANES), lambda b, p: (b, 0))] * 2)
    return pl.pallas_call(
        functools.partial(_dil_kernel, seq=seq),
        grid=(nbatch, npair),
        in_specs=in_specs,
        out_specs=pl.BlockSpec((seq, LANES), lambda b, p: (b, p)),
        out_shape=jax.ShapeDtypeStruct((n, DIL_OUT), BF16),
        scratch_shapes=[pltpu.VMEM((3, seq, LANES), F32)] * 5,
        compiler_params=_cparams(("parallel", "parallel")),
        name="dilated_attn",
    )(*([yb] * 9), cos_t, sin_t)


def _merge_kernel(x_ref, g0, g1, g2, g3, ya, yb, yc, yd, pa, pb, pc, pd, o_ref):
    x = x_ref[...]
    acc = None
    for wg, y, p in ((g0, ya, pa), (g1, yb, pb), (g2, yc, pc), (g3, yd, pd)):
        gate = _sigmoid(jnp.dot(x, wg[...], preferred_element_type=F32))
        term = gate * jnp.dot(y[...], p[...], preferred_element_type=F32)
        acc = term if acc is None else acc + term
    o_ref[...] = acc.astype(o_ref.dtype)


def _merge(xb, wg, ys, ps, tm=1024, tn=512):
    n, d = xb.shape
    ncol = d // tn

    def gate_spec(br):
        return pl.BlockSpec((d, tn), lambda i, j: (0, br * ncol + j))

    in_specs = ([pl.BlockSpec((tm, d), lambda i, j: (i, 0))]
                + [gate_spec(br) for br in range(N_BRANCH)]
                + [pl.BlockSpec((tm, y.shape[1]), lambda i, j: (i, 0)) for y in ys]
                + [pl.BlockSpec((p.shape[0], tn), lambda i, j: (0, j)) for p in ps])
    return pl.pallas_call(
        _merge_kernel,
        grid=(n // tm, ncol),
        in_specs=in_specs,
        out_specs=pl.BlockSpec((tm, tn), lambda i, j: (i, j)),
        out_shape=jax.ShapeDtypeStruct((n, d), BF16),
        compiler_params=_cparams(("parallel", "arbitrary")),
        name="gated_merge",
    )(xb, wg, wg, wg, wg, *ys, *ps)


def _layernorm_rows(z, g, b):
    mu = jnp.mean(z, axis=-1, keepdims=True)
    var = jnp.mean(jnp.square(z - mu), axis=-1, keepdims=True)
    return (z - mu) * lax.rsqrt(var + LN_EPS) * g + b


def _pack_bf16_pairs(y):
    half = y.shape[1] // 2
    lo = lax.bitcast_convert_type(y[:, :half].astype(BF16).astype(F32), jnp.uint32)
    hi = lax.bitcast_convert_type(y[:, half:].astype(BF16).astype(F32), jnp.uint32)
    return (lo >> 16) | (hi & jnp.uint32(0xFFFF0000))


def _unpack_bf16_pairs(w):
    lo = lax.bitcast_convert_type(w << 16, F32).astype(BF16)
    hi = lax.bitcast_convert_type(w & jnp.uint32(0xFFFF0000), F32).astype(BF16)
    return lo, hi


def _store_token_major(ref, val):
    t, w = val.shape
    c = w // LANES
    for s in range(c):
        ref[pl.ds(s, t, stride=c), :] = val[:, s * LANES:(s + 1) * LANES]


def _load_token_major(ref, start, t, c):
    return [ref[pl.ds(start + s, t, stride=c), :] for s in range(c)]


def _outproj_ln_kernel(m_ref, w_ref, x_ref, g_ref, b_ref, o_ref, p_ref):
    h = jnp.dot(m_ref[...], w_ref[...], preferred_element_type=F32)
    y = _layernorm_rows(DN_ALPHA * x_ref[...] + h, g_ref[...], b_ref[...])
    o_ref[...] = y
    _store_token_major(p_ref, _pack_bf16_pairs(y))


def _outproj_ln(merged, w_out, x, g, b, tm=512):
    n, d = x.shape
    c = d // 2 // LANES
    return pl.pallas_call(
        _outproj_ln_kernel,
        grid=(n // tm,),
        in_specs=[pl.BlockSpec((tm, d), lambda i: (i, 0)),
                  pl.BlockSpec((d, d), lambda i: (0, 0)),
                  pl.BlockSpec((tm, d), lambda i: (i, 0)),
                  pl.BlockSpec((1, d), lambda i: (0, 0)),
                  pl.BlockSpec((1, d), lambda i: (0, 0))],
        out_specs=[pl.BlockSpec((tm, d), lambda i: (i, 0)),
                   pl.BlockSpec((tm * c, LANES), lambda i: (i, 0))],
        out_shape=[jax.ShapeDtypeStruct((n, d), F32), jax.ShapeDtypeStruct((n * c, LANES), jnp.uint32)],
        compiler_params=_cparams(("parallel",)),
        name="outproj_ln",
    )(merged, w_out, x, g.reshape(1, d), b.reshape(1, d))


def _router_kernel(x_ref, wr_ref, rb_ref, idx_ref, wts_ref, rank_ref, cnt_ref, carry_ref, *, tr):
    E, G = N_EXPERTS, N_GROUPS
    gsz = E // G

    @pl.when(pl.program_id(0) == 0)
    def _():
        carry_ref[...] = jnp.zeros_like(carry_ref)

    logits = jnp.dot(x_ref[...], wr_ref[...], precision=HI, preferred_element_type=F32)
    scores = _sigmoid(logits.T[:E, :])
    choice = scores + rb_ref[...]
    eidx = _iota2((E, tr), 0)

    c3 = choice.reshape(G, gsz, tr)
    e3 = eidx.reshape(G, gsz, tr)
    max1 = jnp.max(c3, axis=1, keepdims=True)
    first = jnp.min(jnp.where(c3 == max1, e3, E), axis=1, keepdims=True)
    max2 = jnp.max(jnp.where(e3 == first, NEG_INF, c3), axis=1, keepdims=True)
    gscore = (max1 + max2).reshape(G, tr)
    gidx = _iota2((G, tr), 0)
    gsel = jnp.zeros((G, tr), jnp.bool_)
    for _ in range(TOPK_GROUPS):
        gmax = jnp.max(gscore, axis=0, keepdims=True)
        pick = jnp.min(jnp.where(gscore == gmax, gidx, G), axis=0, keepdims=True)
        hit = gidx == pick
        gsel = jnp.logical_or(gsel, hit)
        gscore = jnp.where(hit, NEG_INF, gscore)
    esel = jnp.broadcast_to(gsel.reshape(G, 1, tr), (G, gsz, tr)).reshape(E, tr)

    cand = jnp.where(esel, choice, NEG_INF)
    sel = jnp.zeros((E, tr), F32)
    picks, weights, hits = [], [], []
    for k in range(TOP_K):
        vmax = jnp.max(cand, axis=0, keepdims=True)
        pick = jnp.min(jnp.where(cand == vmax, eidx, E), axis=0, keepdims=True)
        hit = eidx == pick
        picks.append(pick)
        hits.append(hit)
        weights.append(jnp.sum(jnp.where(hit, scores, 0.0), axis=0, keepdims=True))
        sel = jnp.where(hit, 1.0, sel)
        cand = jnp.where(hit, NEG_INF, cand)
    wsum = weights[0]
    for w in weights[1:]:
        wsum = wsum + w
    w_rows = jnp.concatenate([w / wsum * ROUTED_SCALE for w in weights], axis=0)

    before = (_iota2((tr, tr), 0) < _iota2((tr, tr), 1)).astype(BF16)
    rank = jnp.dot(sel.astype(BF16), before, preferred_element_type=F32) + carry_ref[:, 0:1]
    rank_rows = [jnp.sum(jnp.where(hits[k], rank, 0.0), axis=0, keepdims=True) for k in range(TOP_K)]
    carry_ref[...] = carry_ref[...] + jnp.sum(sel, axis=1, keepdims=True)

    idx_ref[...] = jnp.concatenate(picks, axis=0)
    rank_ref[...] = jnp.concatenate(rank_rows, axis=0).astype(jnp.int32)
    wts_ref[...] = jnp.concatenate([w_rows, jnp.zeros((LANES - TOP_K, tr), F32)], axis=0).T
    counts = _col_to_row(carry_ref[:, 0:1], E)
    cnt_ref[...] = jnp.broadcast_to(jnp.concatenate([counts, jnp.zeros((1, LANES - E), F32)], axis=1), (8, LANES))


def _router(x1, w_router, router_bias, tr=512):
    n, d = x1.shape
    wr = jnp.zeros((d, LANES), F32).at[:, :N_EXPERTS].set(w_router)
    slot_rows = pl.BlockSpec((TOP_K, tr), lambda i: (0, i))
    return pl.pallas_call(
        functools.partial(_router_kernel, tr=tr),
        grid=(n // tr,),
        in_specs=[pl.BlockSpec((tr, d), lambda i: (i, 0)),
                  pl.BlockSpec((d, LANES), lambda i: (0, 0)),
                  pl.BlockSpec((N_EXPERTS, 1), lambda i: (0, 0))],
        out_specs=[slot_rows, pl.BlockSpec((tr, LANES), lambda i: (i, 0)), slot_rows,
                   pl.BlockSpec((8, LANES), lambda i: (0, 0))],
        out_shape=[jax.ShapeDtypeStruct((TOP_K, n), jnp.int32), jax.ShapeDtypeStruct((n, LANES), F32),
                   jax.ShapeDtypeStruct((TOP_K, n), jnp.int32), jax.ShapeDtypeStruct((8, LANES), F32)],
        scratch_shapes=[pltpu.VMEM((N_EXPERTS, LANES), F32)],
        compiler_params=_cparams(("arbitrary",)),
        name="router",
    )(x1, wr, router_bias.reshape(N_EXPERTS, 1))


def _group_offsets(cnt, tm):
    padded = jnp.floor((cnt + (tm - 1)) * (1.0 / tm)) * tm
    upper = (_iota2((LANES, LANES), 0) < _iota2((LANES, LANES), 1)).astype(F32)
    offs = jnp.dot(padded, upper, precision=HI, preferred_element_type=F32)
    return padded, offs


def _pos_kernel(idx_ref, rank_ref, cnt_ref, pos_ref, *, tm):
    _, offs = _group_offsets(cnt_ref[...], tm)
    starts = _row_to_col(offs[0:1, :], LANES)[:N_EXPERTS, :]
    idx = idx_ref[...]
    eidx = _iota2((N_EXPERTS, idx.shape[1]), 0)
    rows = [jnp.sum(jnp.where(eidx == idx[k:k + 1, :], starts, 0.0), axis=0, keepdims=True)
            for k in range(TOP_K)]
    pos_ref[...] = jnp.concatenate(rows, axis=0).astype(jnp.int32) + rank_ref[...]


def _positions(idx_t, rank_t, cnt, tm, tp=2048):
    n = idx_t.shape[1]
    slot_rows = pl.BlockSpec((TOP_K, tp), lambda i: (0, i))
    return pl.pallas_call(
        functools.partial(_pos_kernel, tm=tm),
        grid=(n // tp,),
        in_specs=[slot_rows, slot_rows, pl.BlockSpec((8, LANES), lambda i: (0, 0))],
        out_specs=slot_rows,
        out_shape=jax.ShapeDtypeStruct((TOP_K, n), jnp.int32),
        compiler_params=_cparams(("parallel",)),
        name="dispatch_positions",
    )(idx_t, rank_t, cnt)


def _tile_map_kernel(cnt_ref, te_ref, *, tm, width):
    padded, offs = _group_offsets(cnt_ref[...], tm)
    ends = _row_to_col((offs + padded)[0:1, :], LANES)
    expert = _iota2((LANES, width), 0)
    start = (_iota2((LANES, width), 1) * tm).astype(F32)
    done = jnp.logical_and(ends <= start, expert < N_EXPERTS)
    te = jnp.sum(jnp.where(done, 1, 0), axis=0, keepdims=True)
    total = jnp.max(jnp.where(expert < N_EXPERTS, ends, 0.0), axis=0, keepdims=True)
    ntile = (total * (1.0 / tm)).astype(jnp.int32)
    vend = _row_to_col((offs + cnt_ref[...])[0:1, :], LANES)
    mine = jnp.sum(jnp.where(expert == te, vend, 0.0), axis=0, keepdims=True)
    valid = jnp.clip(mine - start[0:1, :], 0.0, float(tm)).astype(jnp.int32)
    row = _iota2((8, width), 0)
    te_ref[...] = jnp.where(row == 0, te, jnp.where(row == 1, ntile, jnp.where(row == 2, valid, 0)))


def _tile_map(cnt, tm, ntile_max):
    width = -(-ntile_max // LANES) * LANES
    out = pl.pallas_call(
        functools.partial(_tile_map_kernel, tm=tm, width=width),
        in_specs=[pl.BlockSpec((8, LANES), lambda: (0, 0))],
        out_specs=pl.BlockSpec((8, width), lambda: (0, 0)),
        out_shape=jax.ShapeDtypeStruct((8, width), jnp.int32),
        name="tile_map",
    )(cnt)
    return out[0, :ntile_max], out[1, :1], out[2, :ntile_max]


SC_SCATTER_ROWS = 64
SC_GATHER_ROWS = 32


def _sc_mesh():
    info = plsc.get_sparse_core_info()
    mesh = plsc.VectorSubcoreMesh(core_axis_name="c", subcore_axis_name="s")
    return mesh, info.num_cores, info.num_subcores


def _sc_scatter_rows(src3, idx3, nrows):
    n, c, _ = src3.shape
    _, nk, chunk = idx3.shape
    mesh, ncore, nsub = _sc_mesh()
    per_w = n // (ncore * nsub)
    assert chunk == SC_SCATTER_ROWS and per_w % chunk == 0

    @functools.partial(
        pl.kernel, mesh=mesh,
        out_type=jax.ShapeDtypeStruct((nrows, c, LANES), src3.dtype),
        scratch_types=[pltpu.VMEM((nk, chunk), jnp.int32),
                       pltpu.VMEM((chunk, c, LANES), src3.dtype),
                       pltpu.SemaphoreType.DMA],
    )
    def scatter_kernel(src_hbm, idx_hbm, out_hbm, idx_v, rows_v, sem):
        base = (lax.axis_index("s") * ncore + lax.axis_index("c")) * per_w

        @pl.loop(0, per_w // chunk)
        def _(j):
            off = pl.multiple_of(base + j * chunk, chunk)
            pltpu.sync_copy(idx_hbm.at[base // chunk + j], idx_v)
            pltpu.sync_copy(src_hbm.at[pl.ds(off, chunk)], rows_v)
            copies = [pltpu.async_copy(rows_v, out_hbm.at[idx_v.at[k]], sem) for k in range(nk)]
            for cp in copies:
                cp.wait()

    return scatter_kernel(src3, idx3)


def _sc_gather_rows(table3, idx2):
    _, c, _ = table3.shape
    nchunk, chunk = idx2.shape
    mesh, ncore, nsub = _sc_mesh()
    per_w = nchunk // (ncore * nsub)
    assert chunk == SC_GATHER_ROWS and per_w % 2 == 0

    @functools.partial(
        pl.kernel, mesh=mesh,
        out_type=jax.ShapeDtypeStruct((nchunk * chunk, c, LANES), table3.dtype),
        scratch_types=[pltpu.VMEM((per_w, chunk), jnp.int32),
                       pltpu.VMEM((chunk, c, LANES), table3.dtype),
                       pltpu.VMEM((chunk, c, LANES), table3.dtype),
                       pltpu.SemaphoreType.DMA, pltpu.SemaphoreType.DMA,
                       pltpu.SemaphoreType.DMA, pltpu.SemaphoreType.DMA],
    )
    def gather_kernel(table_hbm, idx_hbm, out_hbm, idx_v, rows0, rows1, g0, g1, w0, w1):
        first = (lax.axis_index("s") * ncore + lax.axis_index("c")) * per_w
        pltpu.sync_copy(idx_hbm.at[pl.ds(pl.multiple_of(first, per_w), per_w)], idx_v)

        @pl.loop(0, per_w, step=2)
        def _(j):
            ga = pltpu.async_copy(table_hbm.at[idx_v.at[j]], rows0, g0)
            gb = pltpu.async_copy(table_hbm.at[idx_v.at[j + 1]], rows1, g1)
            ga.wait()
            wa = pltpu.async_copy(rows0, out_hbm.at[pl.ds(pl.multiple_of((first + j) * chunk, chunk), chunk)], w0)
            gb.wait()
            wb = pltpu.async_copy(rows1, out_hbm.at[pl.ds(pl.multiple_of((first + j + 1) * chunk, chunk), chunk)], w1)
            wa.wait()
            wb.wait()

    return gather_kernel(table3, idx2)


FFN_LOOKAHEAD = 2
FFN_SLOTS = FFN_LOOKAHEAD + 1


def _ffn_kernel(te_ref, nt_ref, nv_ref, xs_ref, w1_ref, w3_ref, w2_ref, o_ref, w1b, w3b, w2b, ord_ref, *, tm):
    j = pl.program_id(0)
    i = j - FFN_LOOKAHEAD
    nt = nt_ref[0]

    @pl.when(j == 0)
    def _():
        for s in range(4):
            ord_ref[s] = 0

    def starts_group(t):
        tc = jnp.clip(t, 0, nt - 1)
        changed = te_ref[tc] != te_ref[jnp.maximum(tc - 1, 0)]
        return jnp.logical_or(t == 0, jnp.logical_and(jnp.logical_and(t > 0, t < nt), changed))

    for s, (w_ref, wb) in enumerate(((w1_ref, w1b), (w3_ref, w3b), (w2_ref, w2b))):
        @pl.when(starts_group(j - s))
        def _(s=s, w_ref=w_ref, wb=wb):
            wb[ord_ref[s] % FFN_SLOTS] = w_ref[0, 0].astype(BF16)
            ord_ref[s] = ord_ref[s] + 1

    @pl.when(jnp.logical_and(i >= 0, i < nt))
    def _():
        ic = jnp.maximum(i, 0)

        @pl.when(jnp.logical_and(i > 0, starts_group(i)))
        def _():
            ord_ref[3] = ord_ref[3] + 1

        slot = ord_ref[3] % FFN_SLOTS
        c = xs_ref.shape[0] // tm
        live = _iota2((tm, LANES), 0) < nv_ref[ic]
        parts = [_unpack_bf16_pairs(jnp.where(live, p, jnp.uint32(0)))
                 for p in _load_token_major(xs_ref, 0, tm, c)]
        lo = jnp.concatenate([p[0] for p in parts], axis=1)
        hi = jnp.concatenate([p[1] for p in parts], axis=1)
        half = lo.shape[1]
        h1 = (jnp.dot(lo, w1b[slot, :half, :], preferred_element_type=F32)
              + jnp.dot(hi, w1b[slot, half:, :], preferred_element_type=F32))
        h3 = (jnp.dot(lo, w3b[slot, :half, :], preferred_element_type=F32)
              + jnp.dot(hi, w3b[slot, half:, :], preferred_element_type=F32))
        h = (_silu(h1) * h3).astype(BF16)
        y = jnp.dot(h, w2b[slot], preferred_element_type=F32)
        _store_token_major(o_ref, _pack_bf16_pairs(y))

    @pl.when(i >= nt)
    def _():
        o_ref[...] = jnp.zeros_like(o_ref)


def _expert_ffn(te, nt, nv, xs, w1, w3, w2, layer, nrows, tm):
    d, f = w1.shape[2], w1.shape[3]
    c = xs.shape[0] // nrows
    ntile = nrows // tm

    def tile(j, te_r, nt_r, nv_r):
        return (jnp.clip(j - FFN_LOOKAHEAD, 0, nt_r[0] - 1), 0)

    def out_tile(j, te_r, nt_r, nv_r):
        return (jnp.maximum(j - FFN_LOOKAHEAD, 0), 0)

    def expert(delay):
        def index_map(j, te_r, nt_r, nv_r):
            return (layer, te_r[jnp.clip(j - delay, 0, nt_r[0] - 1)], 0, 0)
        return index_map

    assert FFN_LOOKAHEAD == 2
    grid_spec = pltpu.PrefetchScalarGridSpec(
        num_scalar_prefetch=3,
        grid=(ntile + FFN_LOOKAHEAD,),
        in_specs=[pl.BlockSpec((tm * c, LANES), tile),
                  pl.BlockSpec((1, 1, d, f), expert(0)),
                  pl.BlockSpec((1, 1, d, f), expert(1)),
                  pl.BlockSpec((1, 1, f, d), expert(2))],
        out_specs=pl.BlockSpec((tm * c, LANES), out_tile),
        scratch_shapes=[pltpu.VMEM((FFN_SLOTS, d, f), BF16), pltpu.VMEM((FFN_SLOTS, d, f), BF16),
                        pltpu.VMEM((FFN_SLOTS, f, d), BF16), pltpu.SMEM((4,), jnp.int32)],
    )
    return pl.pallas_call(
        functools.partial(_ffn_kernel, tm=tm),
        grid_spec=grid_spec,
        out_shape=jax.ShapeDtypeStruct((nrows * c, LANES), jnp.uint32),
        compiler_params=_cparams(("arbitrary",)),
        name="expert_ffn",
    )(te, nt, nv, xs, w1, w3, w2)


def _combine_kernel(*refs, tc):
    y_refs = refs[:TOP_K]
    wts_ref, x_ref, ws1_ref, ws3_ref, ws2_ref, g_ref, b_ref, o_ref, ob_ref = refs[TOP_K:]
    c = y_refs[0].shape[0] // tc
    x = x_ref[...]
    xb = x.astype(BF16)
    s1 = jnp.dot(xb, ws1_ref[...], preferred_element_type=F32)
    s3 = jnp.dot(xb, ws3_ref[...], preferred_element_type=F32)
    acc = jnp.dot((_silu(s1) * s3).astype(BF16), ws2_ref[...], preferred_element_type=F32)
    wts = wts_ref[...]
    for k in range(TOP_K):
        parts = [_unpack_bf16_pairs(p) for p in _load_token_major(y_refs[k], 0, tc, c)]
        yk = jnp.concatenate([p[0] for p in parts] + [p[1] for p in parts], axis=1).astype(F32)
        acc = acc + wts[:, k:k + 1] * yk
    y = _layernorm_rows(DN_ALPHA * x + acc, g_ref[...], b_ref[...])
    o_ref[...] = y
    ob_ref[...] = y.astype(BF16)


def _combine(yg, wts, x1, ws1, ws3, ws2, g, b, tc=256):
    n, d = x1.shape
    sf = ws1.shape[1]
    c = yg.shape[0] // (TOP_K * n)
    nblk = n // tc

    def slot_spec(k):
        return pl.BlockSpec((tc * c, LANES), lambda i: (k * nblk + i, 0))

    return pl.pallas_call(
        functools.partial(_combine_kernel, tc=tc),
        grid=(nblk,),
        in_specs=[slot_spec(k) for k in range(TOP_K)] + [
            pl.BlockSpec((tc, LANES), lambda i: (i, 0)),
            pl.BlockSpec((tc, d), lambda i: (i, 0)),
            pl.BlockSpec((d, sf), lambda i: (0, 0)),
            pl.BlockSpec((d, sf), lambda i: (0, 0)),
            pl.BlockSpec((sf, d), lambda i: (0, 0)),
            pl.BlockSpec((1, d), lambda i: (0, 0)),
            pl.BlockSpec((1, d), lambda i: (0, 0))],
        out_specs=[pl.BlockSpec((tc, d), lambda i: (i, 0)), pl.BlockSpec((tc, d), lambda i: (i, 0))],
        out_shape=[jax.ShapeDtypeStruct((n, d), F32), jax.ShapeDtypeStruct((n, d), BF16)],
        compiler_params=_cparams(("parallel",)),
        name="moe_combine",
    )(*([yg] * TOP_K), wts, x1, ws1, ws3, ws2, g.reshape(1, d), b.reshape(1, d))


FFN_TILE = 256


def _moe(x1, x1p, w_router, router_bias, w1, w3, w2, layer, ws1, ws3, ws2, g, b):
    n = x1.shape[0]
    tm = FFN_TILE
    nrows = n * TOP_K + N_EXPERTS * tm
    idx, wts, rank, cnt = _router(x1, w_router, router_bias)
    pos_t = _positions(idx, rank, cnt, tm)
    te, nt, nv = _tile_map(cnt, tm, nrows // tm)
    c = x1p.shape[0] // n
    pos_s = pos_t.reshape(TOP_K, n // SC_SCATTER_ROWS, SC_SCATTER_ROWS).transpose(1, 0, 2)
    xs = _sc_scatter_rows(x1p.reshape(n, c, LANES), pos_s, nrows)
    ys = _expert_ffn(te, nt, nv, xs.reshape(nrows * c, LANES), w1, w3, w2, layer, nrows, tm)
    yg = _sc_gather_rows(ys.reshape(nrows, c, LANES), pos_t.reshape(-1, SC_GATHER_ROWS))
    return _combine(yg.reshape(TOP_K * n * c, LANES), wts, x1,
                    ws1.astype(BF16), ws3.astype(BF16), ws2.astype(BF16), g, b)


def kernel(x, positions, w_in, gla_wa2, gla_ba, gla_norm, mlstm_conv_w, mlstm_conv_b, mlstm_bi, mlstm_bf,
           mlstm_norm, sgu_ln_g, sgu_ln_b, sgu_ws, sgu_bs, w_pa, w_pb, w_pc, w_pd, w_out, ln1_g, ln1_b,
           w_router, router_bias, w1, w3, w2, ws1, ws3, ws2, ln2_g, ln2_b):
    nbatch, seq, d = x.shape
    n = nbatch * seq
    xf = x.reshape(n, d)
    xb = xf.astype(BF16)
    cos_t, sin_t = _rope_tables(positions)
    for l in range(DEPTH):
        wl = w_in[l]
        w_a = wl[:, _OFF_A:_OFF_A + _W_A].astype(BF16)
        w_b = wl[:, _OFF_B:_OFF_B + _W_B].astype(BF16)
        w_c = wl[:, _OFF_C:_OFF_C + _W_C].astype(BF16)
        w_d = wl[:, _OFF_D:_OFF_D + _W_D].astype(BF16)
        w_g = wl[:, _OFF_G:_OFF_G + _W_G].astype(BF16)
        w_s = jnp.concatenate(
            [wl[:, _OFF_LOW:_OFF_LOW + GLA_RANK], wl[:, _OFF_IF:_OFF_IF + 2 * MLSTM_HEADS],
             jnp.zeros((d, LANES - GLA_RANK - 2 * MLSTM_HEADS), F32)], axis=1).astype(BF16)
        y_a = _matmul(xb, w_a, 2048, 768)
        y_b = _matmul(xb, w_b, 2048, 768)
        y_c = _matmul(xb, w_c, 2048, 768)
        y_d = _matmul(xb, w_d, 2048, 1024)
        y_s = _matmul(xb, w_s, 2048, LANES)
        sm3 = y_s.reshape(nbatch, seq, LANES)
        o_a, o_d = _recurrent_mixers(
            y_a.reshape(nbatch, seq, _W_A), y_d.reshape(nbatch, seq, _W_D), sm3, gla_wa2[l], gla_ba[l],
            gla_norm[l], mlstm_conv_w[l], mlstm_conv_b[l], mlstm_bi[l], mlstm_bf[l], mlstm_norm[l])
        o_a = o_a.reshape(n, GLA_V)
        o_d = o_d.reshape(n, ML_W)
        o_b = _dilated(y_b, cos_t, sin_t, nbatch)
        o_c = _sgu(y_c, sgu_ln_g[l], sgu_ln_b[l], sgu_ws[l], sgu_bs[l])
        merged = _merge(xb, w_g, (o_a, o_b, o_c, o_d),
                        (w_pa[l].astype(BF16), w_pb[l].astype(BF16), w_pc[l].astype(BF16), w_pd[l].astype(BF16)))
        x1, x1p = _outproj_ln(merged, w_out[l].astype(BF16), xf, ln1_g[l], ln1_b[l])
        xf, xb = _moe(x1, x1p, w_router[l], router_bias[l], w1, w3, w2, l, ws1[l], ws3[l], ws2[l],
                      ln2_g[l], ln2_b[l])
    return xf.reshape(nbatch, seq, d)
```

```python
import functools
import math

import jax
import jax.numpy as jnp
from jax import lax
from jax.experimental import pallas as pl
from jax.experimental.pallas import tpu as pltpu
from jax.experimental.pallas import tpu_sc as plsc

D_MODEL = 2048
DEPTH = 2

GLA_HEADS = 4
GLA_DK = 64
GLA_DV = 128
GLA_RANK = 16
GLA_TAU = 16.0
GLA_CHUNK = 64
GLA_QK = GLA_HEADS * GLA_DK
GLA_V = GLA_HEADS * GLA_DV

DIL_PAIRS = ((128, 1), (512, 4), (2048, 16))
DIL_HEADS_PER_GROUP = 4
DIL_HEAD_DIM = 64
DIL_HEADS = len(DIL_PAIRS) * DIL_HEADS_PER_GROUP
DIL_W = DIL_HEADS * DIL_HEAD_DIM
DIL_OUT = DIL_HEADS_PER_GROUP * DIL_HEAD_DIM
DIL_BLOCK = 128
ROPE_THETA = 10000.0

SGU_CHUNK = 128
SGU_GROUPS = 6
SGU_GROUP_CH = 128
SGU_W = SGU_GROUPS * SGU_GROUP_CH

MLSTM_HEADS = 4
MLSTM_HEAD_DIM = 128
MLSTM_CHUNK = 64
MLSTM_CONV = 4
ML_W = MLSTM_HEADS * MLSTM_HEAD_DIM

N_EXPERTS = 64
TOP_K = 8
N_GROUPS = 8
TOPK_GROUPS = 4
EXPERT_FF = 512
SHARED_FF = 512
ROUTED_SCALE = 2.5

N_BRANCH = 4
DN_ALPHA = (2 * DEPTH) ** 0.25
LN_EPS = 1e-5

_OFF_A = 0
_W_A = 2 * GLA_QK + 2 * GLA_V
_OFF_LOW = _OFF_A + _W_A
_OFF_B = _OFF_LOW + GLA_RANK
_W_B = 3 * DIL_W
_OFF_C = _OFF_B + _W_B
_W_C = 2 * SGU_W
_OFF_D = _OFF_C + _W_C
_W_D = 4 * ML_W
_OFF_IF = _OFF_D + _W_D
_OFF_G = _OFF_IF + 2 * MLSTM_HEADS
_W_G = N_BRANCH * D_MODEL

LANES = 128
VMEM_LIMIT = 56 * 1024 * 1024

_SM_LOW = 0
_SM_I = GLA_RANK
_SM_F = GLA_RANK + MLSTM_HEADS

HI = lax.Precision.HIGHEST
F32 = jnp.float32
BF16 = jnp.bfloat16
NEG_INF = float("-inf")


def _cparams(sem):
    return pltpu.CompilerParams(dimension_semantics=sem, vmem_limit_bytes=VMEM_LIMIT)


def _log_sigmoid(x):
    return jnp.minimum(x, 0.0) - jnp.log1p(jnp.exp(-jnp.abs(x)))


def _sigmoid(x):
    return 1.0 / (1.0 + jnp.exp(-x))


def _silu(x):
    return x * _sigmoid(x)


def _iota2(shape, dim):
    return lax.broadcasted_iota(jnp.int32, shape, dim)


def _col_to_row(col, n):
    eye = _iota2((n, n), 0) == _iota2((n, n), 1)
    return jnp.sum(jnp.where(eye, col, 0.0), axis=0, keepdims=True)


def _row_to_col(row, n):
    eye = _iota2((n, n), 0) == _iota2((n, n), 1)
    return jnp.sum(jnp.where(eye, row, 0.0), axis=1, keepdims=True)


def _mm_kernel(x_ref, w_ref, o_ref):
    o_ref[...] = jnp.dot(x_ref[...], w_ref[...], preferred_element_type=F32).astype(o_ref.dtype)


def _matmul(x, w, tm, tn, out_dtype=F32):
    n, k = x.shape
    m = w.shape[1]
    return pl.pallas_call(
        _mm_kernel,
        grid=(n // tm, m // tn),
        in_specs=[pl.BlockSpec((tm, k), lambda i, j: (i, 0)),
                  pl.BlockSpec((k, tn), lambda i, j: (0, j))],
        out_specs=pl.BlockSpec((tm, tn), lambda i, j: (i, j)),
        out_shape=jax.ShapeDtypeStruct((n, m), out_dtype),
        compiler_params=_cparams(("parallel", "arbitrary")),
        name="in_proj",
    )(x, w)


def _gla_chunk(y_ref, sm_ref, wa2_ref, ba_ref, g_ref, o_ref, state_ref, *, nb):
    L, H, DK, DV = GLA_CHUNK, GLA_HEADS, GLA_DK, GLA_DV
    tril = (_iota2((L, L), 0) >= _iota2((L, L), 1))
    tril_f = tril.astype(F32)
    for b in range(nb):
        y = y_ref[b]
        a_low = sm_ref[b][:, _SM_LOW:_SM_LOW + GLA_RANK]
        glog = jnp.dot(a_low, wa2_ref[...], preferred_element_type=F32) + ba_ref[...]
        g = _log_sigmoid(glog) * (1.0 / GLA_TAU)
        bc = jnp.dot(tril_f, g, precision=HI, preferred_element_type=F32)
        outs = []
        for h in range(H):
            q = y[:, h * DK:(h + 1) * DK] * (DK ** -0.5)
            k = y[:, GLA_QK + h * DK:GLA_QK + (h + 1) * DK]
            v = y[:, 2 * GLA_QK + h * DV:2 * GLA_QK + (h + 1) * DV]
            bh = bc[:, h * DK:(h + 1) * DK]
            qe = q * jnp.exp(bh)
            ke = k * jnp.exp(-bh)
            att = lax.dot_general(qe, ke, (((1,), (1,)), ((), ())), preferred_element_type=F32)
            att = jnp.where(tril, att, 0.0)
            st = state_ref[b * H + h]
            o = (jnp.dot(att, v, preferred_element_type=F32)
                 + jnp.dot(qe, st, preferred_element_type=F32))
            b_last = bh[L - 1:L, :]
            kd = k * jnp.exp(b_last - bh)
            decay = _row_to_col(jnp.exp(b_last), DK)
            state_ref[b * H + h] = decay * st + lax.dot_general(
                kd, v, (((0,), (0,)), ((), ())), preferred_element_type=F32)
            o = o * lax.rsqrt(jnp.mean(o * o, axis=-1, keepdims=True) + LN_EPS)
            outs.append(o)
        o_all = jnp.concatenate(outs, axis=-1) * g_ref[...]
        r = y[:, 2 * GLA_QK + GLA_V:2 * GLA_QK + 2 * GLA_V]
        o_ref[b] = (o_all * _silu(r)).astype(o_ref.dtype)


MLSTM_HALO = 8


def _mlstm_chunk(y_ref, sm_ref, cw_ref, cb_ref, gb_ref, g_ref, o_ref, c_ref, n_ref, m_ref, tail_ref, *, nb):
    L, H, DH = MLSTM_CHUNK, MLSTM_HEADS, MLSTM_HEAD_DIM
    W2 = 2 * ML_W
    HALO = MLSTM_HALO
    tril = (_iota2((L, L), 0) >= _iota2((L, L), 1))
    tril_f = tril.astype(F32)
    for b in range(nb):
        y = y_ref[b]
        qk_raw = y[:, :W2]
        ext = jnp.concatenate([tail_ref[b], qk_raw], axis=0)
        tail_ref[b] = qk_raw[L - HALO:, :]
        conv = cb_ref[...]
        for j in range(MLSTM_CONV):
            s0 = HALO - (MLSTM_CONV - 1) + j
            conv = conv + cw_ref[j:j + 1, :] * ext[s0:s0 + L, :]
        qk = _silu(conv)
        gates = sm_ref[b] + gb_ref[...]
        bcum = jnp.dot(tril_f, _log_sigmoid(gates), precision=HI, preferred_element_type=F32)
        outs = []
        for h in range(H):
            q = qk[:, h * DH:(h + 1) * DH]
            k = qk[:, ML_W + h * DH:ML_W + (h + 1) * DH] * (DH ** -0.5)
            v = y[:, W2 + h * DH:W2 + (h + 1) * DH]
            b_col = bcum[:, _SM_F + h:_SM_F + h + 1]
            li_col = gates[:, _SM_I + h:_SM_I + h + 1]
            b_row = _col_to_row(b_col, L)
            li_row = _col_to_row(li_col, L)
            m_prev = m_ref[b * H + h][:, 0:1]
            dmat = jnp.where(tril, b_col - b_row + li_row, NEG_INF)
            inter = b_col + m_prev
            m_t = jnp.maximum(inter, jnp.max(dmat, axis=-1, keepdims=True))
            w_in = jnp.exp(dmat - m_t)
            w_st = jnp.exp(inter - m_t)
            sc = lax.dot_general(q, k, (((1,), (1,)), ((), ())), preferred_element_type=F32) * w_in
            cst = c_ref[b * H + h]
            nst = n_ref[b * H + h]
            num = (jnp.dot(sc, v, preferred_element_type=F32)
                   + w_st * jnp.dot(q, cst, preferred_element_type=F32))
            den = jnp.sum(sc, axis=-1, keepdims=True) + w_st * jnp.sum(q * nst, axis=-1, keepdims=True)
            hh = num / jnp.maximum(jnp.abs(den), jnp.exp(-m_t))
            b_last = b_col[L - 1:L, :]
            dec = b_last - b_col + li_col
            m_new = jnp.maximum(b_last + m_prev, jnp.max(dec, axis=0, keepdims=True))
            wk = jnp.exp(dec - m_new)
            keep = jnp.exp(b_last + m_prev - m_new)
            wkk = wk * k
            c_ref[b * H + h] = keep * cst + lax.dot_general(
                wkk, v, (((0,), (0,)), ((), ())), preferred_element_type=F32)
            n_ref[b * H + h] = keep * nst + jnp.sum(wkk, axis=0, keepdims=True)
            m_ref[b * H + h] = jnp.broadcast_to(m_new, (1, LANES))
            o_pre = y[:, W2 + ML_W + h * DH:W2 + ML_W + (h + 1) * DH]
            hh = _sigmoid(o_pre) * hh
            hh = hh * lax.rsqrt(jnp.mean(hh * hh, axis=-1, keepdims=True) + LN_EPS)
            outs.append(hh)
        o_ref[b] = (jnp.concatenate(outs, axis=-1) * g_ref[...]).astype(o_ref.dtype)


def _mixers_kernel(ya_ref, yd_ref, sm_ref, yc_ref, wa2_ref, ba_ref, ga_ref, cw_ref, cb_ref, gb_ref, gd_ref,
                   lg_ref, lb_ref, ws_ref, bst_ref, oa_ref, od_ref, oc_ref,
                   state_ref, c_ref, n_ref, m_ref, tail_ref, *, nb, sgu_chunks):
    @pl.when(pl.program_id(0) == 0)
    def _():
        for ref in (state_ref, c_ref, n_ref, m_ref, tail_ref):
            ref[...] = jnp.zeros_like(ref)

    _gla_chunk(ya_ref, sm_ref, wa2_ref, ba_ref, ga_ref, oa_ref, state_ref, nb=nb)
    _mlstm_chunk(yd_ref, sm_ref, cw_ref, cb_ref, gb_ref, gd_ref, od_ref, c_ref, n_ref, m_ref, tail_ref, nb=nb)
    _sgu_kernel(yc_ref, lg_ref, lb_ref, ws_ref, bst_ref, oc_ref, nchunk=sgu_chunks)


def _mixers_acd(ya3, yd3, sm3, yc, wa2, ba, gla_norm, conv_w, conv_b, b_i, b_f, mlstm_norm,
                sgu_ln_g, sgu_ln_b, sgu_ws, sgu_bs):
    nb, s, _ = ya3.shape
    assert GLA_CHUNK == MLSTM_CHUNK
    L = GLA_CHUNK
    nstep = s // L
    n = yc.shape[0]
    sgu_rows = n // nstep
    assert sgu_rows % SGU_CHUNK == 0 and sgu_rows * nstep == n
    gate_bias = jnp.zeros((1, LANES), F32)
    gate_bias = gate_bias.at[0, _SM_I:_SM_I + MLSTM_HEADS].set(b_i).at[0, _SM_F:_SM_F + MLSTM_HEADS].set(b_f)
    bst = jnp.zeros((SGU_CHUNK, LANES), F32).at[:, :SGU_GROUPS].set(sgu_bs.T)

    def chunk(width):
        return pl.BlockSpec((nb, L, width), lambda i: (0, i, 0))

    def whole(*shape):
        return pl.BlockSpec(shape, lambda i: (0,) * len(shape))

    return pl.pallas_call(
        functools.partial(_mixers_kernel, nb=nb, sgu_chunks=sgu_rows // SGU_CHUNK),
        grid=(nstep,),
        in_specs=[chunk(_W_A), chunk(_W_D), chunk(LANES), pl.BlockSpec((sgu_rows, _W_C), lambda i: (i, 0)),
                  whole(GLA_RANK, GLA_QK), whole(1, GLA_QK), whole(1, GLA_V),
                  whole(MLSTM_CONV, 2 * ML_W), whole(1, 2 * ML_W), whole(1, LANES), whole(1, ML_W),
                  whole(1, SGU_W), whole(1, SGU_W), whole(SGU_GROUPS, SGU_CHUNK, SGU_CHUNK),
                  whole(SGU_CHUNK, LANES)],
        out_specs=[chunk(GLA_V), chunk(ML_W), pl.BlockSpec((sgu_rows, SGU_W), lambda i: (i, 0))],
        out_shape=[jax.ShapeDtypeStruct((nb, s, GLA_V), BF16), jax.ShapeDtypeStruct((nb, s, ML_W), BF16),
                   jax.ShapeDtypeStruct((n, SGU_W), BF16)],
        scratch_shapes=[pltpu.VMEM((nb * GLA_HEADS, GLA_DK, GLA_DV), F32),
                        pltpu.VMEM((nb * MLSTM_HEADS, MLSTM_HEAD_DIM, MLSTM_HEAD_DIM), F32),
                        pltpu.VMEM((nb * MLSTM_HEADS, 1, MLSTM_HEAD_DIM), F32),
                        pltpu.VMEM((nb * MLSTM_HEADS, 1, LANES), F32),
                        pltpu.VMEM((nb, MLSTM_HALO, 2 * ML_W), F32)],
        compiler_params=_cparams(("arbitrary",)),
        name="gla_mlstm_sgu",
    )(ya3, yd3, sm3, yc, wa2, ba.reshape(1, GLA_QK), gla_norm.reshape(1, GLA_V),
      conv_w, conv_b.reshape(1, 2 * ML_W), gate_bias, mlstm_norm.reshape(1, ML_W),
      sgu_ln_g.reshape(1, SGU_W), sgu_ln_b.reshape(1, SGU_W), sgu_ws, bst)


def _gelu(x):
    return 0.5 * x * (1.0 + lax.erf(x * (0.5 ** 0.5)))


def _sgu_kernel(y_ref, lg_ref, lb_ref, ws_ref, bst_ref, o_ref, *, nchunk):
    C, G, GC = SGU_CHUNK, SGU_GROUPS, SGU_GROUP_CH
    y = y_ref[...]
    zu = _gelu(y[:, :SGU_W])
    zv = _gelu(y[:, SGU_W:])
    mu = jnp.mean(zv, axis=-1, keepdims=True)
    var = jnp.mean(jnp.square(zv - mu), axis=-1, keepdims=True)
    vn = (zv - mu) * lax.rsqrt(var + LN_EPS) * lg_ref[...] + lb_ref[...]
    tril = _iota2((C, C), 0) >= _iota2((C, C), 1)
    for g in range(G):
        wc = jnp.where(tril, ws_ref[g], 0.0)
        bias = bst_ref[:, g:g + 1]
        for c in range(nchunk):
            rows = slice(c * C, (c + 1) * C)
            cols = slice(g * GC, (g + 1) * GC)
            s = jnp.dot(wc, vn[rows, cols], preferred_element_type=F32) + bias
            o_ref[rows, cols] = (zu[rows, cols] * s).astype(o_ref.dtype)


def _rope_table_kernel(pos_ref, inv_ref, cos_ref, sin_ref):
    ang = pos_ref[...].astype(F32) * inv_ref[...]
    half = DIL_HEAD_DIM // 2
    sign = jnp.where((_iota2(ang.shape, 1) % DIL_HEAD_DIM) < half, -1.0, 1.0)
    cos_ref[...] = jnp.cos(ang)
    sin_ref[...] = jnp.sin(ang) * sign


def _rope_tables(positions):
    n = positions.size
    half = DIL_HEAD_DIM // 2
    inv = ROPE_THETA ** (-jnp.arange(half, dtype=F32) * 2.0 / DIL_HEAD_DIM)
    inv = jnp.tile(inv, LANES // half).reshape(1, LANES)
    t = 1024
    return pl.pallas_call(
        _rope_table_kernel,
        grid=(n // t,),
        in_specs=[pl.BlockSpec((t, 1), lambda i: (i, 0)),
                  pl.BlockSpec((1, LANES), lambda i: (0, 0))],
        out_specs=[pl.BlockSpec((t, LANES), lambda i: (i, 0)),
                   pl.BlockSpec((t, LANES), lambda i: (i, 0))],
        out_shape=[jax.ShapeDtypeStruct((n, LANES), F32)] * 2,
        compiler_params=_cparams(("parallel",)),
        name="rope_tables",
    )(positions.reshape(n, 1), inv)


def _dil_kernel(q0, q1, q2, k0, k1, k2, v0, v1, v2, cos_ref, sin_ref, o_ref,
                qs_ref, ks_ref, num_ref, m_ref, den_ref, *, seq):
    DH, BLK = DIL_HEAD_DIM, DIL_BLOCK
    half = DH // 2
    q_refs, k_refs, v_refs = (q0, q1, q2), (k0, k1, k2), (v0, v1, v2)
    cos = cos_ref[...]
    sin = sin_ref[...]
    first_half = (_iota2((seq, LANES), 1) % DH) < half

    def rope(x):
        swapped = jnp.where(first_half, pltpu.roll(x, LANES - half, 1), pltpu.roll(x, half, 1))
        return x * cos + swapped * sin

    for g in range(len(DIL_PAIRS)):
        qs_ref[g] = rope(q_refs[g][...]) * (DH ** -0.5)
        ks_ref[g] = rope(k_refs[g][...])

    ii = _iota2((BLK, BLK), 0)
    jj = _iota2((BLK, BLK), 1)
    mask_cur = jj <= ii
    mask_prev = jj >= ii
    assert LANES == 2 * DH and BLK == LANES
    head_lanes = [(jj // DH) == h for h in range(LANES // DH)]
    head_ones = [hl.astype(F32) for hl in head_lanes]

    for g, (window, dil) in enumerate(DIL_PAIRS):
        lsub = seq // dil
        nblk = lsub // BLK
        assert window // dil == BLK and lsub % BLK == 0
        v_ref = v_refs[g]

        def unit(u, carry, g=g, dil=dil, nblk=nblk, v_ref=v_ref):
            r = u % dil
            n = u // dil
            rows = pl.ds(n * (BLK * dil) + r, BLK, stride=dil)
            qb = qs_ref[g, rows, :]
            kc = ks_ref[g, rows, :]
            vc = v_ref[rows, :]
            if nblk > 1:
                prow = pl.ds(jnp.maximum(n - 1, 0) * (BLK * dil) + r, BLK, stride=dil)
                kp = ks_ref[g, prow, :]
                vp = v_ref[prow, :]
                has_prev = n > 0
            num = jnp.zeros((BLK, LANES), F32)
            den = jnp.zeros((BLK, LANES), F32)
            mxs = []
            for h in range(LANES // DH):
                qh = jnp.where(head_lanes[h], qb, 0.0)
                s_c = lax.dot_general(qh, kc, (((1,), (1,)), ((), ())), preferred_element_type=F32)
                s_c = jnp.where(mask_cur, s_c, NEG_INF)
                if nblk > 1:
                    s_p = lax.dot_general(qh, kp, (((1,), (1,)), ((), ())), preferred_element_type=F32)
                    s_p = jnp.where(jnp.logical_and(mask_prev, has_prev), s_p, NEG_INF)
                    mx = jnp.max(jnp.maximum(s_c, s_p), axis=-1, keepdims=True)
                else:
                    mx = jnp.max(s_c, axis=-1, keepdims=True)
                p_c = jnp.exp(s_c - mx)
                num = num + jnp.dot(p_c, jnp.where(head_lanes[h], vc, 0.0), preferred_element_type=F32)
                den = den + jnp.dot(p_c, head_ones[h], preferred_element_type=F32)
                if nblk > 1:
                    p_p = jnp.exp(s_p - mx)
                    num = num + jnp.dot(p_p, jnp.where(head_lanes[h], vp, 0.0), preferred_element_type=F32)
                    den = den + jnp.dot(p_p, head_ones[h], preferred_element_type=F32)
                mxs.append(mx)
            num_ref[g, rows, :] = num
            m_ref[g, rows, :] = jnp.where(head_lanes[0], mxs[0], mxs[1])
            den_ref[g, rows, :] = den
            return carry

        lax.fori_loop(0, dil * nblk, unit, 0, unroll=4)

    m_all = jnp.maximum(jnp.maximum(m_ref[0], m_ref[1]), m_ref[2])
    num = jnp.zeros((seq, LANES), F32)
    den = jnp.zeros((seq, LANES), F32)
    for g in range(len(DIL_PAIRS)):
        e = jnp.exp(m_ref[g] - m_all)
        num = num + e * num_ref[g]
        den = den + e * den_ref[g]
    o_ref[...] = (num / den).astype(o_ref.dtype)


def _dilated(yb, cos_t, sin_t, nbatch):
    n = yb.shape[0]
    seq = n // nbatch
    npair = DIL_HEADS_PER_GROUP * DIL_HEAD_DIM // LANES
    nblk_cols = DIL_W // LANES

    def spec(section, g):
        return pl.BlockSpec((seq, LANES), lambda b, p: (b, section * nblk_cols + g * npair + p))

    in_specs = ([spec(0, g) for g in range(3)] + [spec(1, g) for g in range(3)] + [spec(2, g) for g in range(3)]
                + [pl.BlockSpec((seq, LANES), lambda b, p: (b, 0))] * 2)
    return pl.pallas_call(
        functools.partial(_dil_kernel, seq=seq),
        grid=(nbatch, npair),
        in_specs=in_specs,
        out_specs=pl.BlockSpec((seq, LANES), lambda b, p: (b, p)),
        out_shape=jax.ShapeDtypeStruct((n, DIL_OUT), BF16),
        scratch_shapes=[pltpu.VMEM((3, seq, LANES), F32)] * 5,
        compiler_params=_cparams(("parallel", "parallel")),
        name="dilated_attn",
    )(*([yb] * 9), cos_t, sin_t)


def _merge_kernel(x_ref, g0, g1, g2, g3, ya, yb, yc, yd, pa, pb, pc, pd, o_ref):
    x = x_ref[...]
    acc = None
    for wg, y, p in ((g0, ya, pa), (g1, yb, pb), (g2, yc, pc), (g3, yd, pd)):
        gate = _sigmoid(jnp.dot(x, wg[...], preferred_element_type=F32))
        term = gate * jnp.dot(y[...], p[...], preferred_element_type=F32)
        acc = term if acc is None else acc + term
    o_ref[...] = acc.astype(o_ref.dtype)


def _merge(xb, wg, ys, ps, tm=1024, tn=512):
    n, d = xb.shape
    ncol = d // tn

    def gate_spec(br):
        return pl.BlockSpec((d, tn), lambda i, j: (0, br * ncol + j))

    in_specs = ([pl.BlockSpec((tm, d), lambda i, j: (i, 0))]
                + [gate_spec(br) for br in range(N_BRANCH)]
                + [pl.BlockSpec((tm, y.shape[1]), lambda i, j: (i, 0)) for y in ys]
                + [pl.BlockSpec((p.shape[0], tn), lambda i, j: (0, j)) for p in ps])
    return pl.pallas_call(
        _merge_kernel,
        grid=(n // tm, ncol),
        in_specs=in_specs,
        out_specs=pl.BlockSpec((tm, tn), lambda i, j: (i, j)),
        out_shape=jax.ShapeDtypeStruct((n, d), BF16),
        compiler_params=_cparams(("parallel", "arbitrary")),
        name="gated_merge",
    )(xb, wg, wg, wg, wg, *ys, *ps)


def _layernorm_rows(z, g, b):
    mu = jnp.mean(z, axis=-1, keepdims=True)
    var = jnp.mean(jnp.square(z - mu), axis=-1, keepdims=True)
    return (z - mu) * lax.rsqrt(var + LN_EPS) * g + b


def _pack_bf16_pairs(y):
    half = y.shape[1] // 2
    lo = lax.bitcast_convert_type(y[:, :half].astype(BF16).astype(F32), jnp.uint32)
    hi = lax.bitcast_convert_type(y[:, half:].astype(BF16).astype(F32), jnp.uint32)
    return (lo >> 16) | (hi & jnp.uint32(0xFFFF0000))


def _unpack_bf16_pairs(w):
    lo = lax.bitcast_convert_type(w << 16, F32).astype(BF16)
    hi = lax.bitcast_convert_type(w & jnp.uint32(0xFFFF0000), F32).astype(BF16)
    return lo, hi


def _store_token_major(ref, val):
    t, w = val.shape
    c = w // LANES
    for s in range(c):
        ref[pl.ds(s, t, stride=c), :] = val[:, s * LANES:(s + 1) * LANES]


def _load_token_major(ref, start, t, c):
    return [ref[pl.ds(start + s, t, stride=c), :] for s in range(c)]


def _outproj_ln_kernel(m_ref, w_ref, x_ref, g_ref, b_ref, o_ref, p_ref):
    h = jnp.dot(m_ref[...], w_ref[...], preferred_element_type=F32)
    y = _layernorm_rows(DN_ALPHA * x_ref[...] + h, g_ref[...], b_ref[...])
    o_ref[...] = y
    _store_token_major(p_ref, _pack_bf16_pairs(y))


def _outproj_ln(merged, w_out, x, g, b, tm=512):
    n, d = x.shape
    c = d // 2 // LANES
    return pl.pallas_call(
        _outproj_ln_kernel,
        grid=(n // tm,),
        in_specs=[pl.BlockSpec((tm, d), lambda i: (i, 0)),
                  pl.BlockSpec((d, d), lambda i: (0, 0)),
                  pl.BlockSpec((tm, d), lambda i: (i, 0)),
                  pl.BlockSpec((1, d), lambda i: (0, 0)),
                  pl.BlockSpec((1, d), lambda i: (0, 0))],
        out_specs=[pl.BlockSpec((tm, d), lambda i: (i, 0)),
                   pl.BlockSpec((tm * c, LANES), lambda i: (i, 0))],
        out_shape=[jax.ShapeDtypeStruct((n, d), F32), jax.ShapeDtypeStruct((n * c, LANES), jnp.uint32)],
        compiler_params=_cparams(("parallel",)),
        name="outproj_ln",
    )(merged, w_out, x, g.reshape(1, d), b.reshape(1, d))


def _router_kernel(x_ref, wr_ref, rb_ref, idx_ref, wts_ref, rank_ref, cnt_ref, carry_ref, *, tr):
    E, G = N_EXPERTS, N_GROUPS
    gsz = E // G

    @pl.when(pl.program_id(0) == 0)
    def _():
        carry_ref[...] = jnp.zeros_like(carry_ref)

    logits = jnp.dot(x_ref[...], wr_ref[...], precision=HI, preferred_element_type=F32)
    scores = _sigmoid(logits.T[:E, :])
    choice = scores + rb_ref[...]
    eidx = _iota2((E, tr), 0)

    c3 = choice.reshape(G, gsz, tr)
    e3 = eidx.reshape(G, gsz, tr)
    max1 = jnp.max(c3, axis=1, keepdims=True)
    first = jnp.min(jnp.where(c3 == max1, e3, E), axis=1, keepdims=True)
    max2 = jnp.max(jnp.where(e3 == first, NEG_INF, c3), axis=1, keepdims=True)
    gscore = (max1 + max2).reshape(G, tr)
    gidx = _iota2((G, tr), 0)
    gsel = jnp.zeros((G, tr), jnp.bool_)
    for _ in range(TOPK_GROUPS):
        gmax = jnp.max(gscore, axis=0, keepdims=True)
        pick = jnp.min(jnp.where(gscore == gmax, gidx, G), axis=0, keepdims=True)
        hit = gidx == pick
        gsel = jnp.logical_or(gsel, hit)
        gscore = jnp.where(hit, NEG_INF, gscore)
    esel = jnp.broadcast_to(gsel.reshape(G, 1, tr), (G, gsz, tr)).reshape(E, tr)

    cand = jnp.where(esel, choice, NEG_INF)
    sel = jnp.zeros((E, tr), F32)
    picks, weights, hits = [], [], []
    for k in range(TOP_K):
        vmax = jnp.max(cand, axis=0, keepdims=True)
        pick = jnp.min(jnp.where(cand == vmax, eidx, E), axis=0, keepdims=True)
        hit = eidx == pick
        picks.append(pick)
        hits.append(hit)
        weights.append(jnp.sum(jnp.where(hit, scores, 0.0), axis=0, keepdims=True))
        sel = jnp.where(hit, 1.0, sel)
        cand = jnp.where(hit, NEG_INF, cand)
    wsum = weights[0]
    for w in weights[1:]:
        wsum = wsum + w
    w_rows = jnp.concatenate([w / wsum * ROUTED_SCALE for w in weights], axis=0)

    before = (_iota2((tr, tr), 0) < _iota2((tr, tr), 1)).astype(BF16)
    rank = jnp.dot(sel.astype(BF16), before, preferred_element_type=F32) + carry_ref[:, 0:1]
    rank_rows = [jnp.sum(jnp.where(hits[k], rank, 0.0), axis=0, keepdims=True) for k in range(TOP_K)]
    carry_ref[...] = carry_ref[...] + jnp.sum(sel, axis=1, keepdims=True)

    idx_ref[...] = jnp.concatenate(picks, axis=0)
    rank_ref[...] = jnp.concatenate(rank_rows, axis=0).astype(jnp.int32)
    wts_ref[...] = jnp.concatenate([w_rows, jnp.zeros((LANES - TOP_K, tr), F32)], axis=0).T
    counts = _col_to_row(carry_ref[:, 0:1], E)
    cnt_ref[...] = jnp.broadcast_to(jnp.concatenate([counts, jnp.zeros((1, LANES - E), F32)], axis=1), (8, LANES))


def _router(x1, w_router, router_bias, tr=512):
    n, d = x1.shape
    wr = jnp.zeros((d, LANES), F32).at[:, :N_EXPERTS].set(w_router)
    slot_rows = pl.BlockSpec((TOP_K, tr), lambda i: (0, i))
    return pl.pallas_call(
        functools.partial(_router_kernel, tr=tr),
        grid=(n // tr,),
        in_specs=[pl.BlockSpec((tr, d), lambda i: (i, 0)),
                  pl.BlockSpec((d, LANES), lambda i: (0, 0)),
                  pl.BlockSpec((N_EXPERTS, 1), lambda i: (0, 0))],
        out_specs=[slot_rows, pl.BlockSpec((tr, LANES), lambda i: (i, 0)), slot_rows,
                   pl.BlockSpec((8, LANES), lambda i: (0, 0))],
        out_shape=[jax.ShapeDtypeStruct((TOP_K, n), jnp.int32), jax.ShapeDtypeStruct((n, LANES), F32),
                   jax.ShapeDtypeStruct((TOP_K, n), jnp.int32), jax.ShapeDtypeStruct((8, LANES), F32)],
        scratch_shapes=[pltpu.VMEM((N_EXPERTS, LANES), F32)],
        compiler_params=_cparams(("arbitrary",)),
        name="router",
    )(x1, wr, router_bias.reshape(N_EXPERTS, 1))


def _group_offsets(cnt, tm):
    padded = jnp.floor((cnt + (tm - 1)) * (1.0 / tm)) * tm
    upper = (_iota2((LANES, LANES), 0) < _iota2((LANES, LANES), 1)).astype(F32)
    offs = jnp.dot(padded, upper, precision=HI, preferred_element_type=F32)
    return padded, offs


def _pos_kernel(idx_ref, rank_ref, cnt_ref, pos_ref, *, tm):
    _, offs = _group_offsets(cnt_ref[...], tm)
    starts = _row_to_col(offs[0:1, :], LANES)[:N_EXPERTS, :]
    idx = idx_ref[...]
    eidx = _iota2((N_EXPERTS, idx.shape[1]), 0)
    rows = [jnp.sum(jnp.where(eidx == idx[k:k + 1, :], starts, 0.0), axis=0, keepdims=True)
            for k in range(TOP_K)]
    pos_ref[...] = jnp.concatenate(rows, axis=0).astype(jnp.int32) + rank_ref[...]


def _positions(idx_t, rank_t, cnt, tm, tp=2048):
    n = idx_t.shape[1]
    slot_rows = pl.BlockSpec((TOP_K, tp), lambda i: (0, i))
    return pl.pallas_call(
        functools.partial(_pos_kernel, tm=tm),
        grid=(n // tp,),
        in_specs=[slot_rows, slot_rows, pl.BlockSpec((8, LANES), lambda i: (0, 0))],
        out_specs=slot_rows,
        out_shape=jax.ShapeDtypeStruct((TOP_K, n), jnp.int32),
        compiler_params=_cparams(("parallel",)),
        name="dispatch_positions",
    )(idx_t, rank_t, cnt)


def _tile_map_kernel(cnt_ref, te_ref, *, tm, width):
    padded, offs = _group_offsets(cnt_ref[...], tm)
    ends = _row_to_col((offs + padded)[0:1, :], LANES)
    expert = _iota2((LANES, width), 0)
    start = (_iota2((LANES, width), 1) * tm).astype(F32)
    done = jnp.logical_and(ends <= start, expert < N_EXPERTS)
    te = jnp.sum(jnp.where(done, 1, 0), axis=0, keepdims=True)
    total = jnp.max(jnp.where(expert < N_EXPERTS, ends, 0.0), axis=0, keepdims=True)
    ntile = (total * (1.0 / tm)).astype(jnp.int32)
    vend = _row_to_col((offs + cnt_ref[...])[0:1, :], LANES)
    mine = jnp.sum(jnp.where(expert == te, vend, 0.0), axis=0, keepdims=True)
    valid = jnp.clip(mine - start[0:1, :], 0.0, float(tm)).astype(jnp.int32)
    row = _iota2((8, width), 0)
    te_ref[...] = jnp.where(row == 0, te, jnp.where(row == 1, ntile, jnp.where(row == 2, valid, 0)))


def _tile_map(cnt, tm, ntile_max):
    width = -(-ntile_max // LANES) * LANES
    out = pl.pallas_call(
        functools.partial(_tile_map_kernel, tm=tm, width=width),
        in_specs=[pl.BlockSpec((8, LANES), lambda: (0, 0))],
        out_specs=pl.BlockSpec((8, width), lambda: (0, 0)),
        out_shape=jax.ShapeDtypeStruct((8, width), jnp.int32),
        name="tile_map",
    )(cnt)
    return out[0, :ntile_max], out[1, :1], out[2, :ntile_max]


SC_SCATTER_ROWS = 64
SC_GATHER_ROWS = 32


def _sc_mesh():
    info = plsc.get_sparse_core_info()
    mesh = plsc.VectorSubcoreMesh(core_axis_name="c", subcore_axis_name="s")
    return mesh, info.num_cores, info.num_subcores


def _sc_scatter_rows(src3, idx3, nrows):
    n, c, _ = src3.shape
    _, nk, chunk = idx3.shape
    mesh, ncore, nsub = _sc_mesh()
    per_w = n // (ncore * nsub)
    assert chunk == SC_SCATTER_ROWS and per_w % chunk == 0

    @functools.partial(
        pl.kernel, mesh=mesh,
        out_type=jax.ShapeDtypeStruct((nrows, c, LANES), src3.dtype),
        scratch_types=[pltpu.VMEM((nk, chunk), jnp.int32),
                       pltpu.VMEM((chunk, c, LANES), src3.dtype),
                       pltpu.SemaphoreType.DMA],
    )
    def scatter_kernel(src_hbm, idx_hbm, out_hbm, idx_v, rows_v, sem):
        base = (lax.axis_index("s") * ncore + lax.axis_index("c")) * per_w

        @pl.loop(0, per_w // chunk)
        def _(j):
            off = pl.multiple_of(base + j * chunk, chunk)
            pltpu.sync_copy(idx_hbm.at[base // chunk + j], idx_v)
            pltpu.sync_copy(src_hbm.at[pl.ds(off, chunk)], rows_v)
            copies = [pltpu.async_copy(rows_v, out_hbm.at[idx_v.at[k]], sem) for k in range(nk)]
            for cp in copies:
                cp.wait()

    return scatter_kernel(src3, idx3)


def _sc_gather_rows(table3, idx2):
    _, c, _ = table3.shape
    nchunk, chunk = idx2.shape
    mesh, ncore, nsub = _sc_mesh()
    per_w = nchunk // (ncore * nsub)
    assert chunk == SC_GATHER_ROWS and per_w % 2 == 0

    @functools.partial(
        pl.kernel, mesh=mesh,
        out_type=jax.ShapeDtypeStruct((nchunk * chunk, c, LANES), table3.dtype),
        scratch_types=[pltpu.VMEM((per_w, chunk), jnp.int32),
                       pltpu.VMEM((chunk, c, LANES), table3.dtype),
                       pltpu.VMEM((chunk, c, LANES), table3.dtype),
                       pltpu.SemaphoreType.DMA, pltpu.SemaphoreType.DMA,
                       pltpu.SemaphoreType.DMA, pltpu.SemaphoreType.DMA],
    )
    def gather_kernel(table_hbm, idx_hbm, out_hbm, idx_v, rows0, rows1, g0, g1, w0, w1):
        first = (lax.axis_index("s") * ncore + lax.axis_index("c")) * per_w
        pltpu.sync_copy(idx_hbm.at[pl.ds(pl.multiple_of(first, per_w), per_w)], idx_v)

        @pl.loop(0, per_w, step=2)
        def _(j):
            ga = pltpu.async_copy(table_hbm.at[idx_v.at[j]], rows0, g0)
            gb = pltpu.async_copy(table_hbm.at[idx_v.at[j + 1]], rows1, g1)
            ga.wait()
            wa = pltpu.async_copy(rows0, out_hbm.at[pl.ds(pl.multiple_of((first + j) * chunk, chunk), chunk)], w0)
            gb.wait()
            wb = pltpu.async_copy(rows1, out_hbm.at[pl.ds(pl.multiple_of((first + j + 1) * chunk, chunk), chunk)], w1)
            wa.wait()
            wb.wait()

    return gather_kernel(table3, idx2)


FFN_LOOKAHEAD = 2
FFN_SLOTS = FFN_LOOKAHEAD + 1


def _ffn_kernel(te_ref, nt_ref, nv_ref, xs_ref, w1_ref, w3_ref, w2_ref, o_ref, w1b, w3b, w2b, ord_ref, *, tm):
    j = pl.program_id(0)
    i = j - FFN_LOOKAHEAD
    nt = nt_ref[0]

    @pl.when(j == 0)
    def _():
        for s in range(4):
            ord_ref[s] = 0

    def starts_group(t):
        tc = jnp.clip(t, 0, nt - 1)
        changed = te_ref[tc] != te_ref[jnp.maximum(tc - 1, 0)]
        return jnp.logical_or(t == 0, jnp.logical_and(jnp.logical_and(t > 0, t < nt), changed))

    for s, (w_ref, wb) in enumerate(((w1_ref, w1b), (w3_ref, w3b), (w2_ref, w2b))):
        @pl.when(starts_group(j - s))
        def _(s=s, w_ref=w_ref, wb=wb):
            wb[ord_ref[s] % FFN_SLOTS] = w_ref[0, 0].astype(BF16)
            ord_ref[s] = ord_ref[s] + 1

    @pl.when(jnp.logical_and(i >= 0, i < nt))
    def _():
        ic = jnp.maximum(i, 0)

        @pl.when(jnp.logical_and(i > 0, starts_group(i)))
        def _():
            ord_ref[3] = ord_ref[3] + 1

        slot = ord_ref[3] % FFN_SLOTS
        c = xs_ref.shape[0] // tm
        live = _iota2((tm, LANES), 0) < nv_ref[ic]
        parts = [_unpack_bf16_pairs(jnp.where(live, p, jnp.uint32(0)))
                 for p in _load_token_major(xs_ref, 0, tm, c)]
        lo = jnp.concatenate([p[0] for p in parts], axis=1)
        hi = jnp.concatenate([p[1] for p in parts], axis=1)
        half = lo.shape[1]
        h1 = (jnp.dot(lo, w1b[slot, :half, :], preferred_element_type=F32)
              + jnp.dot(hi, w1b[slot, half:, :], preferred_element_type=F32))
        h3 = (jnp.dot(lo, w3b[slot, :half, :], preferred_element_type=F32)
              + jnp.dot(hi, w3b[slot, half:, :], preferred_element_type=F32))
        h = (_silu(h1) * h3).astype(BF16)
        y = jnp.dot(h, w2b[slot], preferred_element_type=F32)
        _store_token_major(o_ref, _pack_bf16_pairs(y))

    @pl.when(i >= nt)
    def _():
        o_ref[...] = jnp.zeros_like(o_ref)


def _expert_ffn(te, nt, nv, xs, w1, w3, w2, layer, nrows, tm):
    d, f = w1.shape[2], w1.shape[3]
    c = xs.shape[0] // nrows
    ntile = nrows // tm

    def tile(j, te_r, nt_r, nv_r):
        return (jnp.clip(j - FFN_LOOKAHEAD, 0, nt_r[0] - 1), 0)

    def out_tile(j, te_r, nt_r, nv_r):
        return (jnp.maximum(j - FFN_LOOKAHEAD, 0), 0)

    def expert(delay):
        def index_map(j, te_r, nt_r, nv_r):
            return (layer, te_r[jnp.clip(j - delay, 0, nt_r[0] - 1)], 0, 0)
        return index_map

    assert FFN_LOOKAHEAD == 2
    grid_spec = pltpu.PrefetchScalarGridSpec(
        num_scalar_prefetch=3,
        grid=(ntile + FFN_LOOKAHEAD,),
        in_specs=[pl.BlockSpec((tm * c, LANES), tile),
                  pl.BlockSpec((1, 1, d, f), expert(0)),
                  pl.BlockSpec((1, 1, d, f), expert(1)),
                  pl.BlockSpec((1, 1, f, d), expert(2))],
        out_specs=pl.BlockSpec((tm * c, LANES), out_tile),
        scratch_shapes=[pltpu.VMEM((FFN_SLOTS, d, f), BF16), pltpu.VMEM((FFN_SLOTS, d, f), BF16),
                        pltpu.VMEM((FFN_SLOTS, f, d), BF16), pltpu.SMEM((4,), jnp.int32)],
    )
    return pl.pallas_call(
        functools.partial(_ffn_kernel, tm=tm),
        grid_spec=grid_spec,
        out_shape=jax.ShapeDtypeStruct((nrows * c, LANES), jnp.uint32),
        compiler_params=_cparams(("arbitrary",)),
        name="expert_ffn",
    )(te, nt, nv, xs, w1, w3, w2)


def _shared_expert_kernel(x_ref, ws1_ref, ws3_ref, ws2_ref, o_ref):
    xb = x_ref[...].astype(BF16)
    s1 = jnp.dot(xb, ws1_ref[...], preferred_element_type=F32)
    s3 = jnp.dot(xb, ws3_ref[...], preferred_element_type=F32)
    o_ref[...] = jnp.dot((_silu(s1) * s3).astype(BF16), ws2_ref[...], preferred_element_type=F32)


def _shared_expert(x1, ws1, ws3, ws2, tm=512):
    n, d = x1.shape
    sf = ws1.shape[1]
    return pl.pallas_call(
        _shared_expert_kernel,
        grid=(n // tm,),
        in_specs=[pl.BlockSpec((tm, d), lambda i: (i, 0)),
                  pl.BlockSpec((d, sf), lambda i: (0, 0)),
                  pl.BlockSpec((d, sf), lambda i: (0, 0)),
                  pl.BlockSpec((sf, d), lambda i: (0, 0))],
        out_specs=pl.BlockSpec((tm, d), lambda i: (i, 0)),
        out_shape=jax.ShapeDtypeStruct((n, d), F32),
        compiler_params=_cparams(("parallel",)),
        name="shared_expert",
    )(x1, ws1, ws3, ws2)


def _combine_kernel(*refs, tc):
    y_refs = refs[:TOP_K]
    wts_ref, x_ref, sh_ref, g_ref, b_ref, o_ref, ob_ref = refs[TOP_K:]
    c = y_refs[0].shape[0] // tc
    acc = sh_ref[...]
    wts = wts_ref[...]
    for k in range(TOP_K):
        parts = [_unpack_bf16_pairs(p) for p in _load_token_major(y_refs[k], 0, tc, c)]
        yk = jnp.concatenate([p[0] for p in parts] + [p[1] for p in parts], axis=1).astype(F32)
        acc = acc + wts[:, k:k + 1] * yk
    y = _layernorm_rows(DN_ALPHA * x_ref[...] + acc, g_ref[...], b_ref[...])
    o_ref[...] = y
    ob_ref[...] = y.astype(BF16)


def _combine(yg, wts, x1, shared, g, b, tc=256):
    n, d = x1.shape
    c = yg.shape[0] // (TOP_K * n)
    nblk = n // tc

    def slot_spec(k):
        return pl.BlockSpec((tc * c, LANES), lambda i: (k * nblk + i, 0))

    rows = pl.BlockSpec((tc, d), lambda i: (i, 0))
    return pl.pallas_call(
        functools.partial(_combine_kernel, tc=tc),
        grid=(nblk,),
        in_specs=[slot_spec(k) for k in range(TOP_K)] + [
            pl.BlockSpec((tc, LANES), lambda i: (i, 0)), rows, rows,
            pl.BlockSpec((1, d), lambda i: (0, 0)),
            pl.BlockSpec((1, d), lambda i: (0, 0))],
        out_specs=[rows, rows],
        out_shape=[jax.ShapeDtypeStruct((n, d), F32), jax.ShapeDtypeStruct((n, d), BF16)],
        compiler_params=_cparams(("parallel",)),
        name="moe_combine",
    )(*([yg] * TOP_K), wts, x1, shared, g.reshape(1, d), b.reshape(1, d))


def _cast_kernel(x_ref, o_ref):
    o_ref[...] = x_ref[...].astype(o_ref.dtype)


def _to_bf16(x, tm=1024):
    n, d = x.shape
    return pl.pallas_call(
        _cast_kernel,
        grid=(n // tm,),
        in_specs=[pl.BlockSpec((tm, d), lambda i: (i, 0))],
        out_specs=pl.BlockSpec((tm, d), lambda i: (i, 0)),
        out_shape=jax.ShapeDtypeStruct((n, d), BF16),
        compiler_params=_cparams(("parallel",)),
        name="cast_bf16",
    )(x)


FFN_TILE = 256


def _moe(x1, x1p, w_router, router_bias, w1, w3, w2, layer, ws1, ws3, ws2, g, b):
    n = x1.shape[0]
    tm = FFN_TILE
    nrows = n * TOP_K + N_EXPERTS * tm
    idx, wts, rank, cnt = _router(x1, w_router, router_bias)
    pos_t = _positions(idx, rank, cnt, tm)
    te, nt, nv = _tile_map(cnt, tm, nrows // tm)
    c = x1p.shape[0] // n
    pos_s = pos_t.reshape(TOP_K, n // SC_SCATTER_ROWS, SC_SCATTER_ROWS).transpose(1, 0, 2)
    xs = _sc_scatter_rows(x1p.reshape(n, c, LANES), pos_s, nrows)
    shared = _shared_expert(x1, ws1.astype(BF16), ws3.astype(BF16), ws2.astype(BF16))
    ys = _expert_ffn(te, nt, nv, xs.reshape(nrows * c, LANES), w1, w3, w2, layer, nrows, tm)
    yg = _sc_gather_rows(ys.reshape(nrows, c, LANES), pos_t.reshape(-1, SC_GATHER_ROWS))
    return _combine(yg.reshape(TOP_K * n * c, LANES), wts, x1, shared, g, b)


def kernel(x, positions, w_in, gla_wa2, gla_ba, gla_norm, mlstm_conv_w, mlstm_conv_b, mlstm_bi, mlstm_bf,
           mlstm_norm, sgu_ln_g, sgu_ln_b, sgu_ws, sgu_bs, w_pa, w_pb, w_pc, w_pd, w_out, ln1_g, ln1_b,
           w_router, router_bias, w1, w3, w2, ws1, ws3, ws2, ln2_g, ln2_b):
    nbatch, seq, d = x.shape
    n = nbatch * seq
    xf = x.reshape(n, d)
    xb = _to_bf16(xf)
    cos_t, sin_t = _rope_tables(positions)
    for l in range(DEPTH):
        wl = w_in[l]
        w_a = wl[:, _OFF_A:_OFF_A + _W_A].astype(BF16)
        w_b = wl[:, _OFF_B:_OFF_B + _W_B].astype(BF16)
        w_c = wl[:, _OFF_C:_OFF_C + _W_C].astype(BF16)
        w_d = wl[:, _OFF_D:_OFF_D + _W_D].astype(BF16)
        w_g = wl[:, _OFF_G:_OFF_G + _W_G].astype(BF16)
        w_s = jnp.concatenate(
            [wl[:, _OFF_LOW:_OFF_LOW + GLA_RANK], wl[:, _OFF_IF:_OFF_IF + 2 * MLSTM_HEADS],
             jnp.zeros((d, LANES - GLA_RANK - 2 * MLSTM_HEADS), F32)], axis=1).astype(BF16)
        y_a = _matmul(xb, w_a, 2048, 768)
        y_b = _matmul(xb, w_b, 2048, 768)
        y_c = _matmul(xb, w_c, 2048, 768)
        y_d = _matmul(xb, w_d, 2048, 1024)
        y_s = _matmul(xb, w_s, 2048, LANES)
        sm3 = y_s.reshape(nbatch, seq, LANES)
        o_a, o_d, o_c = _mixers_acd(
            y_a.reshape(nbatch, seq, _W_A), y_d.reshape(nbatch, seq, _W_D), sm3, y_c, gla_wa2[l], gla_ba[l],
            gla_norm[l], mlstm_conv_w[l], mlstm_conv_b[l], mlstm_bi[l], mlstm_bf[l], mlstm_norm[l],
            sgu_ln_g[l], sgu_ln_b[l], sgu_ws[l], sgu_bs[l])
        o_a = o_a.reshape(n, GLA_V)
        o_d = o_d.reshape(n, ML_W)
        o_b = _dilated(y_b, cos_t, sin_t, nbatch)
        merged = _merge(xb, w_g, (o_a, o_b, o_c, o_d),
                        (w_pa[l].astype(BF16), w_pb[l].astype(BF16), w_pc[l].astype(BF16), w_pd[l].astype(BF16)))
        x1, x1p = _outproj_ln(merged, w_out[l].astype(BF16), xf, ln1_g[l], ln1_b[l])
        xf, xb = _moe(x1, x1p, w_router[l], router_bias[l], w1, w3, w2, l, ws1[l], ws3[l], ws2[l],
                      ln2_g[l], ln2_b[l])
    return xf.reshape(nbatch, seq, d)
```

```python
import functools
import math

import jax
import jax.numpy as jnp
from jax import lax
from jax.experimental import pallas as pl
from jax.experimental.pallas import tpu as pltpu
from jax.experimental.pallas import tpu_sc as plsc

D_MODEL = 2048
DEPTH = 2

GLA_HEADS = 4
GLA_DK = 64
GLA_DV = 128
GLA_RANK = 16
GLA_TAU = 16.0
GLA_CHUNK = 64
GLA_QK = GLA_HEADS * GLA_DK
GLA_V = GLA_HEADS * GLA_DV

DIL_PAIRS = ((128, 1), (512, 4), (2048, 16))
DIL_HEADS_PER_GROUP = 4
DIL_HEAD_DIM = 64
DIL_HEADS = len(DIL_PAIRS) * DIL_HEADS_PER_GROUP
DIL_W = DIL_HEADS * DIL_HEAD_DIM
DIL_OUT = DIL_HEADS_PER_GROUP * DIL_HEAD_DIM
DIL_BLOCK = 128
ROPE_THETA = 10000.0

SGU_CHUNK = 128
SGU_GROUPS = 6
SGU_GROUP_CH = 128
SGU_W = SGU_GROUPS * SGU_GROUP_CH

MLSTM_HEADS = 4
MLSTM_HEAD_DIM = 128
MLSTM_CHUNK = 64
MLSTM_CONV = 4
ML_W = MLSTM_HEADS * MLSTM_HEAD_DIM

N_EXPERTS = 64
TOP_K = 8
N_GROUPS = 8
TOPK_GROUPS = 4
EXPERT_FF = 512
SHARED_FF = 512
ROUTED_SCALE = 2.5

N_BRANCH = 4
DN_ALPHA = (2 * DEPTH) ** 0.25
LN_EPS = 1e-5

_OFF_A = 0
_W_A = 2 * GLA_QK + 2 * GLA_V
_OFF_LOW = _OFF_A + _W_A
_OFF_B = _OFF_LOW + GLA_RANK
_W_B = 3 * DIL_W
_OFF_C = _OFF_B + _W_B
_W_C = 2 * SGU_W
_OFF_D = _OFF_C + _W_C
_W_D = 4 * ML_W
_OFF_IF = _OFF_D + _W_D
_OFF_G = _OFF_IF + 2 * MLSTM_HEADS
_W_G = N_BRANCH * D_MODEL

LANES = 128
VMEM_LIMIT = 56 * 1024 * 1024

_SM_LOW = 0
_SM_I = GLA_RANK
_SM_F = GLA_RANK + MLSTM_HEADS

HI = lax.Precision.HIGHEST
F32 = jnp.float32
BF16 = jnp.bfloat16
NEG_INF = float("-inf")


def _cparams(sem):
    return pltpu.CompilerParams(dimension_semantics=sem, vmem_limit_bytes=VMEM_LIMIT)


def _log_sigmoid(x):
    return jnp.minimum(x, 0.0) - jnp.log1p(jnp.exp(-jnp.abs(x)))


def _sigmoid(x):
    return 1.0 / (1.0 + jnp.exp(-x))


def _silu(x):
    return x * _sigmoid(x)


def _iota2(shape, dim):
    return lax.broadcasted_iota(jnp.int32, shape, dim)


def _col_to_row(col, n):
    eye = _iota2((n, n), 0) == _iota2((n, n), 1)
    return jnp.sum(jnp.where(eye, col, 0.0), axis=0, keepdims=True)


def _row_to_col(row, n):
    eye = _iota2((n, n), 0) == _iota2((n, n), 1)
    return jnp.sum(jnp.where(eye, row, 0.0), axis=1, keepdims=True)


def _mm_kernel(x_ref, w_ref, o_ref):
    o_ref[...] = jnp.dot(x_ref[...], w_ref[...], preferred_element_type=F32).astype(o_ref.dtype)


def _matmul(x, w, tm, tn, out_dtype=F32):
    n, k = x.shape
    m = w.shape[1]
    return pl.pallas_call(
        _mm_kernel,
        grid=(n // tm, m // tn),
        in_specs=[pl.BlockSpec((tm, k), lambda i, j: (i, 0)),
                  pl.BlockSpec((k, tn), lambda i, j: (0, j))],
        out_specs=pl.BlockSpec((tm, tn), lambda i, j: (i, j)),
        out_shape=jax.ShapeDtypeStruct((n, m), out_dtype),
        compiler_params=_cparams(("parallel", "arbitrary")),
        name="in_proj",
    )(x, w)


def _gla_chunk(y_ref, sm_ref, wa2_ref, ba_ref, g_ref, o_ref, state_ref, *, nb):
    L, H, DK, DV = GLA_CHUNK, GLA_HEADS, GLA_DK, GLA_DV
    tril = (_iota2((L, L), 0) >= _iota2((L, L), 1))
    tril_f = tril.astype(F32)
    for b in range(nb):
        y = y_ref[b]
        a_low = sm_ref[b][:, _SM_LOW:_SM_LOW + GLA_RANK]
        glog = jnp.dot(a_low, wa2_ref[...], preferred_element_type=F32) + ba_ref[...]
        g = _log_sigmoid(glog) * (1.0 / GLA_TAU)
        bc = jnp.dot(tril_f, g, precision=HI, preferred_element_type=F32)
        outs = []
        for h in range(H):
            q = y[:, h * DK:(h + 1) * DK] * (DK ** -0.5)
            k = y[:, GLA_QK + h * DK:GLA_QK + (h + 1) * DK]
            v = y[:, 2 * GLA_QK + h * DV:2 * GLA_QK + (h + 1) * DV]
            bh = bc[:, h * DK:(h + 1) * DK]
            qe = q * jnp.exp(bh)
            ke = k * jnp.exp(-bh)
            att = lax.dot_general(qe, ke, (((1,), (1,)), ((), ())), preferred_element_type=F32)
            att = jnp.where(tril, att, 0.0)
            st = state_ref[b * H + h]
            o = (jnp.dot(att, v, preferred_element_type=F32)
                 + jnp.dot(qe, st, preferred_element_type=F32))
            b_last = bh[L - 1:L, :]
            kd = k * jnp.exp(b_last - bh)
            decay = _row_to_col(jnp.exp(b_last), DK)
            state_ref[b * H + h] = decay * st + lax.dot_general(
                kd, v, (((0,), (0,)), ((), ())), preferred_element_type=F32)
            o = o * lax.rsqrt(jnp.mean(o * o, axis=-1, keepdims=True) + LN_EPS)
            outs.append(o)
        o_all = jnp.concatenate(outs, axis=-1) * g_ref[...]
        r = y[:, 2 * GLA_QK + GLA_V:2 * GLA_QK + 2 * GLA_V]
        o_ref[b] = (o_all * _silu(r)).astype(o_ref.dtype)


MLSTM_HALO = 8


def _mlstm_chunk(y_ref, sm_ref, cw_ref, cb_ref, gb_ref, g_ref, o_ref, c_ref, n_ref, m_ref, tail_ref, *, nb):
    L, H, DH = MLSTM_CHUNK, MLSTM_HEADS, MLSTM_HEAD_DIM
    W2 = 2 * ML_W
    HALO = MLSTM_HALO
    tril = (_iota2((L, L), 0) >= _iota2((L, L), 1))
    tril_f = tril.astype(F32)
    for b in range(nb):
        y = y_ref[b]
        qk_raw = y[:, :W2]
        ext = jnp.concatenate([tail_ref[b], qk_raw], axis=0)
        tail_ref[b] = qk_raw[L - HALO:, :]
        conv = cb_ref[...]
        for j in range(MLSTM_CONV):
            s0 = HALO - (MLSTM_CONV - 1) + j
            conv = conv + cw_ref[j:j + 1, :] * ext[s0:s0 + L, :]
        qk = _silu(conv)
        gates = sm_ref[b] + gb_ref[...]
        bcum = jnp.dot(tril_f, _log_sigmoid(gates), precision=HI, preferred_element_type=F32)
        outs = []
        for h in range(H):
            q = qk[:, h * DH:(h + 1) * DH]
            k = qk[:, ML_W + h * DH:ML_W + (h + 1) * DH] * (DH ** -0.5)
            v = y[:, W2 + h * DH:W2 + (h + 1) * DH]
            b_col = bcum[:, _SM_F + h:_SM_F + h + 1]
            li_col = gates[:, _SM_I + h:_SM_I + h + 1]
            b_row = _col_to_row(b_col, L)
            li_row = _col_to_row(li_col, L)
            m_prev = m_ref[b * H + h][:, 0:1]
            dmat = jnp.where(tril, b_col - b_row + li_row, NEG_INF)
            inter = b_col + m_prev
            m_t = jnp.maximum(inter, jnp.max(dmat, axis=-1, keepdims=True))
            w_in = jnp.exp(dmat - m_t)
            w_st = jnp.exp(inter - m_t)
            sc = lax.dot_general(q, k, (((1,), (1,)), ((), ())), preferred_element_type=F32) * w_in
            cst = c_ref[b * H + h]
            nst = n_ref[b * H + h]
            num = (jnp.dot(sc, v, preferred_element_type=F32)
                   + w_st * jnp.dot(q, cst, preferred_element_type=F32))
            den = jnp.sum(sc, axis=-1, keepdims=True) + w_st * jnp.sum(q * nst, axis=-1, keepdims=True)
            hh = num / jnp.maximum(jnp.abs(den), jnp.exp(-m_t))
            b_last = b_col[L - 1:L, :]
            dec = b_last - b_col + li_col
            m_new = jnp.maximum(b_last + m_prev, jnp.max(dec, axis=0, keepdims=True))
            wk = jnp.exp(dec - m_new)
            keep = jnp.exp(b_last + m_prev - m_new)
            wkk = wk * k
            c_ref[b * H + h] = keep * cst + lax.dot_general(
                wkk, v, (((0,), (0,)), ((), ())), preferred_element_type=F32)
            n_ref[b * H + h] = keep * nst + jnp.sum(wkk, axis=0, keepdims=True)
            m_ref[b * H + h] = jnp.broadcast_to(m_new, (1, LANES))
            o_pre = y[:, W2 + ML_W + h * DH:W2 + ML_W + (h + 1) * DH]
            hh = _sigmoid(o_pre) * hh
            hh = hh * lax.rsqrt(jnp.mean(hh * hh, axis=-1, keepdims=True) + LN_EPS)
            outs.append(hh)
        o_ref[b] = (jnp.concatenate(outs, axis=-1) * g_ref[...]).astype(o_ref.dtype)


def _mixers_kernel(ya_ref, yd_ref, sm_ref, yc_ref, wa2_ref, ba_ref, ga_ref, cw_ref, cb_ref, gb_ref, gd_ref,
                   lg_ref, lb_ref, ws_ref, bst_ref, oa_ref, od_ref, oc_ref,
                   state_ref, c_ref, n_ref, m_ref, tail_ref, *, nb, sgu_chunks):
    @pl.when(pl.program_id(0) == 0)
    def _():
        for ref in (state_ref, c_ref, n_ref, m_ref, tail_ref):
            ref[...] = jnp.zeros_like(ref)

    _gla_chunk(ya_ref, sm_ref, wa2_ref, ba_ref, ga_ref, oa_ref, state_ref, nb=nb)
    _mlstm_chunk(yd_ref, sm_ref, cw_ref, cb_ref, gb_ref, gd_ref, od_ref, c_ref, n_ref, m_ref, tail_ref, nb=nb)
    _sgu_kernel(yc_ref, lg_ref, lb_ref, ws_ref, bst_ref, oc_ref, nchunk=sgu_chunks)


def _mixers_acd(ya3, yd3, sm3, yc, wa2, ba, gla_norm, conv_w, conv_b, b_i, b_f, mlstm_norm,
                sgu_ln_g, sgu_ln_b, sgu_ws, sgu_bs):
    nb, s, _ = ya3.shape
    assert GLA_CHUNK == MLSTM_CHUNK
    L = GLA_CHUNK
    nstep = s // L
    n = yc.shape[0]
    sgu_rows = n // nstep
    assert sgu_rows % SGU_CHUNK == 0 and sgu_rows * nstep == n
    gate_bias = jnp.zeros((1, LANES), F32)
    gate_bias = gate_bias.at[0, _SM_I:_SM_I + MLSTM_HEADS].set(b_i).at[0, _SM_F:_SM_F + MLSTM_HEADS].set(b_f)
    bst = jnp.zeros((SGU_CHUNK, LANES), F32).at[:, :SGU_GROUPS].set(sgu_bs.T)

    def chunk(width):
        return pl.BlockSpec((nb, L, width), lambda i: (0, i, 0))

    def whole(*shape):
        return pl.BlockSpec(shape, lambda i: (0,) * len(shape))

    return pl.pallas_call(
        functools.partial(_mixers_kernel, nb=nb, sgu_chunks=sgu_rows // SGU_CHUNK),
        grid=(nstep,),
        in_specs=[chunk(_W_A), chunk(_W_D), chunk(LANES), pl.BlockSpec((sgu_rows, _W_C), lambda i: (i, 0)),
                  whole(GLA_RANK, GLA_QK), whole(1, GLA_QK), whole(1, GLA_V),
                  whole(MLSTM_CONV, 2 * ML_W), whole(1, 2 * ML_W), whole(1, LANES), whole(1, ML_W),
                  whole(1, SGU_W), whole(1, SGU_W), whole(SGU_GROUPS, SGU_CHUNK, SGU_CHUNK),
                  whole(SGU_CHUNK, LANES)],
        out_specs=[chunk(GLA_V), chunk(ML_W), pl.BlockSpec((sgu_rows, SGU_W), lambda i: (i, 0))],
        out_shape=[jax.ShapeDtypeStruct((nb, s, GLA_V), BF16), jax.ShapeDtypeStruct((nb, s, ML_W), BF16),
                   jax.ShapeDtypeStruct((n, SGU_W), BF16)],
        scratch_shapes=[pltpu.VMEM((nb * GLA_HEADS, GLA_DK, GLA_DV), F32),
                        pltpu.VMEM((nb * MLSTM_HEADS, MLSTM_HEAD_DIM, MLSTM_HEAD_DIM), F32),
                        pltpu.VMEM((nb * MLSTM_HEADS, 1, MLSTM_HEAD_DIM), F32),
                        pltpu.VMEM((nb * MLSTM_HEADS, 1, LANES), F32),
                        pltpu.VMEM((nb, MLSTM_HALO, 2 * ML_W), F32)],
        compiler_params=_cparams(("arbitrary",)),
        name="gla_mlstm_sgu",
    )(ya3, yd3, sm3, yc, wa2, ba.reshape(1, GLA_QK), gla_norm.reshape(1, GLA_V),
      conv_w, conv_b.reshape(1, 2 * ML_W), gate_bias, mlstm_norm.reshape(1, ML_W),
      sgu_ln_g.reshape(1, SGU_W), sgu_ln_b.reshape(1, SGU_W), sgu_ws, bst)


def _gelu(x):
    return 0.5 * x * (1.0 + lax.erf(x * (0.5 ** 0.5)))


def _sgu_kernel(y_ref, lg_ref, lb_ref, ws_ref, bst_ref, o_ref, *, nchunk):
    C, G, GC = SGU_CHUNK, SGU_GROUPS, SGU_GROUP_CH
    y = y_ref[...]
    zu = _gelu(y[:, :SGU_W])
    zv = _gelu(y[:, SGU_W:])
    mu = jnp.mean(zv, axis=-1, keepdims=True)
    var = jnp.mean(jnp.square(zv - mu), axis=-1, keepdims=True)
    vn = (zv - mu) * lax.rsqrt(var + LN_EPS) * lg_ref[...] + lb_ref[...]
    tril = _iota2((C, C), 0) >= _iota2((C, C), 1)
    for g in range(G):
        wc = jnp.where(tril, ws_ref[g], 0.0)
        bias = bst_ref[:, g:g + 1]
        for c in range(nchunk):
            rows = slice(c * C, (c + 1) * C)
            cols = slice(g * GC, (g + 1) * GC)
            s = jnp.dot(wc, vn[rows, cols], preferred_element_type=F32) + bias
            o_ref[rows, cols] = (zu[rows, cols] * s).astype(o_ref.dtype)


def _rope_table_kernel(pos_ref, inv_ref, cos_ref, sin_ref):
    ang = pos_ref[...].astype(F32) * inv_ref[...]
    half = DIL_HEAD_DIM // 2
    sign = jnp.where((_iota2(ang.shape, 1) % DIL_HEAD_DIM) < half, -1.0, 1.0)
    cos_ref[...] = jnp.cos(ang)
    sin_ref[...] = jnp.sin(ang) * sign


def _rope_tables(positions):
    n = positions.size
    half = DIL_HEAD_DIM // 2
    inv = ROPE_THETA ** (-jnp.arange(half, dtype=F32) * 2.0 / DIL_HEAD_DIM)
    inv = jnp.tile(inv, LANES // half).reshape(1, LANES)
    t = 1024
    return pl.pallas_call(
        _rope_table_kernel,
        grid=(n // t,),
        in_specs=[pl.BlockSpec((t, 1), lambda i: (i, 0)),
                  pl.BlockSpec((1, LANES), lambda i: (0, 0))],
        out_specs=[pl.BlockSpec((t, LANES), lambda i: (i, 0)),
                   pl.BlockSpec((t, LANES), lambda i: (i, 0))],
        out_shape=[jax.ShapeDtypeStruct((n, LANES), F32)] * 2,
        compiler_params=_cparams(("parallel",)),
        name="rope_tables",
    )(positions.reshape(n, 1), inv)


DIL_QROWS = DIL_BLOCK


def _dil_kernel(q0, q1, q2, k0, k1, k2, v0, v1, v2, cos_ref, sin_ref, o_ref,
                qs_ref, ks_ref, num_ref, m_ref, den_ref, *, seq):
    DH, BLK = DIL_HEAD_DIM, DIL_BLOCK
    half = DH // 2
    q_refs, k_refs, v_refs = (q0, q1, q2), (k0, k1, k2), (v0, v1, v2)
    cos = cos_ref[...]
    sin = sin_ref[...]
    first_half = (_iota2((seq, LANES), 1) % DH) < half

    def rope(x):
        swapped = jnp.where(first_half, pltpu.roll(x, LANES - half, 1), pltpu.roll(x, half, 1))
        return x * cos + swapped * sin

    for g in range(len(DIL_PAIRS)):
        qs_ref[g] = rope(q_refs[g][...]) * (DH ** -0.5)
        ks_ref[g] = rope(k_refs[g][...])

    assert LANES == 2 * DH and BLK == LANES and BLK % DIL_QROWS == 0
    jj = _iota2((BLK, LANES), 1)
    head_lanes = [(jj // DH) == h for h in range(LANES // DH)]
    head_ones = [hl.astype(F32) for hl in head_lanes]
    iq = _iota2((DIL_QROWS, BLK), 0)
    jq = _iota2((DIL_QROWS, BLK), 1)
    head_lanes_q = [(jq // DH) == h for h in range(LANES // DH)]

    for g, (window, dil) in enumerate(DIL_PAIRS):
        lsub = seq // dil
        nblk = lsub // BLK
        assert window // dil == BLK and lsub % BLK == 0
        v_ref = v_refs[g]

        def unit(u, carry, g=g, dil=dil, nblk=nblk, v_ref=v_ref):
            r = u % dil
            n = u // dil
            rows = pl.ds(n * (BLK * dil) + r, BLK, stride=dil)
            qb = qs_ref[g, rows, :]
            kc = ks_ref[g, rows, :]
            vc = v_ref[rows, :]
            if nblk > 1:
                prow = pl.ds(jnp.maximum(n - 1, 0) * (BLK * dil) + r, BLK, stride=dil)
                kp = ks_ref[g, prow, :]
                vp = v_ref[prow, :]
                has_prev = n > 0
            nums, dens, ms = [], [], []
            for q0 in range(0, BLK, DIL_QROWS):
                qr = slice(q0, q0 + DIL_QROWS)
                num = jnp.zeros((DIL_QROWS, LANES), F32)
                den = jnp.zeros((DIL_QROWS, LANES), F32)
                mxs = []
                for h in range(LANES // DH):
                    qh = jnp.where(head_lanes_q[h], qb[qr], 0.0)
                    s_c = lax.dot_general(qh, kc, (((1,), (1,)), ((), ())), preferred_element_type=F32)
                    s_c = jnp.where(jq <= iq + q0, s_c, NEG_INF)
                    if nblk > 1:
                        s_p = lax.dot_general(qh, kp, (((1,), (1,)), ((), ())), preferred_element_type=F32)
                        s_p = jnp.where(jnp.logical_and(jq >= iq + q0, has_prev), s_p, NEG_INF)
                        mx = jnp.max(jnp.maximum(s_c, s_p), axis=-1, keepdims=True)
                    else:
                        mx = jnp.max(s_c, axis=-1, keepdims=True)
                    p_c = jnp.exp(s_c - mx)
                    num = num + jnp.dot(p_c, jnp.where(head_lanes[h], vc, 0.0), preferred_element_type=F32)
                    den = den + jnp.dot(p_c, head_ones[h], preferred_element_type=F32)
                    if nblk > 1:
                        p_p = jnp.exp(s_p - mx)
                        num = num + jnp.dot(p_p, jnp.where(head_lanes[h], vp, 0.0), preferred_element_type=F32)
                        den = den + jnp.dot(p_p, head_ones[h], preferred_element_type=F32)
                    mxs.append(mx)
                nums.append(num)
                dens.append(den)
                ms.append(jnp.where(head_lanes_q[0], mxs[0], mxs[1]))
            num_ref[g, rows, :] = jnp.concatenate(nums, axis=0)
            m_ref[g, rows, :] = jnp.concatenate(ms, axis=0)
            den_ref[g, rows, :] = jnp.concatenate(dens, axis=0)
            return carry

        lax.fori_loop(0, dil * nblk, unit, 0, unroll=4)

    m_all = jnp.maximum(jnp.maximum(m_ref[0], m_ref[1]), m_ref[2])
    num = jnp.zeros((seq, LANES), F32)
    den = jnp.zeros((seq, LANES), F32)
    for g in range(len(DIL_PAIRS)):
        e = jnp.exp(m_ref[g] - m_all)
        num = num + e * num_ref[g]
        den = den + e * den_ref[g]
    o_ref[...] = (num / den).astype(o_ref.dtype)


def _dilated(yb, cos_t, sin_t, nbatch):
    n = yb.shape[0]
    seq = n // nbatch
    npair = DIL_HEADS_PER_GROUP * DIL_HEAD_DIM // LANES
    nblk_cols = DIL_W // LANES

    def spec(section, g):
        return pl.BlockSpec((seq, LANES), lambda b, p: (b, section * nblk_cols + g * npair + p))

    in_specs = ([spec(0, g) for g in range(3)] + [spec(1, g) for g in range(3)] + [spec(2, g) for g in range(3)]
                + [pl.BlockSpec((seq, LANES), lambda b, p: (b, 0))] * 2)
    return pl.pallas_call(
        functools.partial(_dil_kernel, seq=seq),
        grid=(nbatch, npair),
        in_specs=in_specs,
        out_specs=pl.BlockSpec((seq, LANES), lambda b, p: (b, p)),
        out_shape=jax.ShapeDtypeStruct((n, DIL_OUT), BF16),
        scratch_shapes=[pltpu.VMEM((3, seq, LANES), F32)] * 5,
        compiler_params=_cparams(("parallel", "parallel")),
        name="dilated_attn",
    )(*([yb] * 9), cos_t, sin_t)


def _merge_kernel(x_ref, g0, g1, g2, g3, ya, yb, yc, yd, pa, pb, pc, pd, o_ref):
    x = x_ref[...]
    acc = None
    for wg, y, p in ((g0, ya, pa), (g1, yb, pb), (g2, yc, pc), (g3, yd, pd)):
        gate = _sigmoid(jnp.dot(x, wg[...], preferred_element_type=F32))
        term = gate * jnp.dot(y[...], p[...], preferred_element_type=F32)
        acc = term if acc is None else acc + term
    o_ref[...] = acc.astype(o_ref.dtype)


def _merge(xb, wg, ys, ps, tm=1024, tn=512):
    n, d = xb.shape
    ncol = d // tn

    def gate_spec(br):
        return pl.BlockSpec((d, tn), lambda i, j: (0, br * ncol + j))

    in_specs = ([pl.BlockSpec((tm, d), lambda i, j: (i, 0))]
                + [gate_spec(br) for br in range(N_BRANCH)]
                + [pl.BlockSpec((tm, y.shape[1]), lambda i, j: (i, 0)) for y in ys]
                + [pl.BlockSpec((p.shape[0], tn), lambda i, j: (0, j)) for p in ps])
    return pl.pallas_call(
        _merge_kernel,
        grid=(n // tm, ncol),
        in_specs=in_specs,
        out_specs=pl.BlockSpec((tm, tn), lambda i, j: (i, j)),
        out_shape=jax.ShapeDtypeStruct((n, d), BF16),
        compiler_params=_cparams(("parallel", "arbitrary")),
        name="gated_merge",
    )(xb, wg, wg, wg, wg, *ys, *ps)


def _layernorm_rows(z, g, b):
    mu = jnp.mean(z, axis=-1, keepdims=True)
    var = jnp.mean(jnp.square(z - mu), axis=-1, keepdims=True)
    return (z - mu) * lax.rsqrt(var + LN_EPS) * g + b


def _pack_bf16_pairs(y):
    half = y.shape[1] // 2
    lo = lax.bitcast_convert_type(y[:, :half].astype(BF16).astype(F32), jnp.uint32)
    hi = lax.bitcast_convert_type(y[:, half:].astype(BF16).astype(F32), jnp.uint32)
    return (lo >> 16) | (hi & jnp.uint32(0xFFFF0000))


def _unpack_bf16_pairs(w):
    lo = lax.bitcast_convert_type(w << 16, F32).astype(BF16)
    hi = lax.bitcast_convert_type(w & jnp.uint32(0xFFFF0000), F32).astype(BF16)
    return lo, hi


def _store_token_major(ref, val):
    t, w = val.shape
    c = w // LANES
    for s in range(c):
        ref[pl.ds(s, t, stride=c), :] = val[:, s * LANES:(s + 1) * LANES]


def _load_token_major(ref, start, t, c):
    return [ref[pl.ds(start + s, t, stride=c), :] for s in range(c)]


def _outproj_ln_kernel(m_ref, w_ref, x_ref, g_ref, b_ref, o_ref, p_ref):
    h = jnp.dot(m_ref[...], w_ref[...], preferred_element_type=F32)
    y = _layernorm_rows(DN_ALPHA * x_ref[...] + h, g_ref[...], b_ref[...])
    o_ref[...] = y
    _store_token_major(p_ref, _pack_bf16_pairs(y))


def _outproj_ln(merged, w_out, x, g, b, tm=512):
    n, d = x.shape
    c = d // 2 // LANES
    return pl.pallas_call(
        _outproj_ln_kernel,
        grid=(n // tm,),
        in_specs=[pl.BlockSpec((tm, d), lambda i: (i, 0)),
                  pl.BlockSpec((d, d), lambda i: (0, 0)),
                  pl.BlockSpec((tm, d), lambda i: (i, 0)),
                  pl.BlockSpec((1, d), lambda i: (0, 0)),
                  pl.BlockSpec((1, d), lambda i: (0, 0))],
        out_specs=[pl.BlockSpec((tm, d), lambda i: (i, 0)),
                   pl.BlockSpec((tm * c, LANES), lambda i: (i, 0))],
        out_shape=[jax.ShapeDtypeStruct((n, d), F32), jax.ShapeDtypeStruct((n * c, LANES), jnp.uint32)],
        compiler_params=_cparams(("parallel",)),
        name="outproj_ln",
    )(merged, w_out, x, g.reshape(1, d), b.reshape(1, d))


def _router_kernel(x_ref, wr_ref, rb_ref, idx_ref, wts_ref, rank_ref, cnt_ref, carry_ref, *, tr):
    E, G = N_EXPERTS, N_GROUPS
    gsz = E // G

    @pl.when(pl.program_id(0) == 0)
    def _():
        carry_ref[...] = jnp.zeros_like(carry_ref)

    logits = jnp.dot(x_ref[...], wr_ref[...], precision=HI, preferred_element_type=F32)
    scores = _sigmoid(logits.T[:E, :])
    choice = scores + rb_ref[...]
    eidx = _iota2((E, tr), 0)

    c3 = choice.reshape(G, gsz, tr)
    e3 = eidx.reshape(G, gsz, tr)
    max1 = jnp.max(c3, axis=1, keepdims=True)
    first = jnp.min(jnp.where(c3 == max1, e3, E), axis=1, keepdims=True)
    max2 = jnp.max(jnp.where(e3 == first, NEG_INF, c3), axis=1, keepdims=True)
    gscore = (max1 + max2).reshape(G, tr)
    gidx = _iota2((G, tr), 0)
    gsel = jnp.zeros((G, tr), jnp.bool_)
    for _ in range(TOPK_GROUPS):
        gmax = jnp.max(gscore, axis=0, keepdims=True)
        pick = jnp.min(jnp.where(gscore == gmax, gidx, G), axis=0, keepdims=True)
        hit = gidx == pick
        gsel = jnp.logical_or(gsel, hit)
        gscore = jnp.where(hit, NEG_INF, gscore)
    esel = jnp.broadcast_to(gsel.reshape(G, 1, tr), (G, gsz, tr)).reshape(E, tr)

    cand = jnp.where(esel, choice, NEG_INF)
    sel = jnp.zeros((E, tr), F32)
    picks, weights, hits = [], [], []
    for k in range(TOP_K):
        vmax = jnp.max(cand, axis=0, keepdims=True)
        pick = jnp.min(jnp.where(cand == vmax, eidx, E), axis=0, keepdims=True)
        hit = eidx == pick
        picks.append(pick)
        hits.append(hit)
        weights.append(jnp.sum(jnp.where(hit, scores, 0.0), axis=0, keepdims=True))
        sel = jnp.where(hit, 1.0, sel)
        cand = jnp.where(hit, NEG_INF, cand)
    wsum = weights[0]
    for w in weights[1:]:
        wsum = wsum + w
    w_rows = jnp.concatenate([w / wsum * ROUTED_SCALE for w in weights], axis=0)

    before = (_iota2((tr, tr), 0) < _iota2((tr, tr), 1)).astype(BF16)
    rank = jnp.dot(sel.astype(BF16), before, preferred_element_type=F32) + carry_ref[:, 0:1]
    rank_rows = [jnp.sum(jnp.where(hits[k], rank, 0.0), axis=0, keepdims=True) for k in range(TOP_K)]
    carry_ref[...] = carry_ref[...] + jnp.sum(sel, axis=1, keepdims=True)

    idx_ref[...] = jnp.concatenate(picks, axis=0)
    rank_ref[...] = jnp.concatenate(rank_rows, axis=0).astype(jnp.int32)
    wts_ref[...] = jnp.concatenate([w_rows, jnp.zeros((LANES - TOP_K, tr), F32)], axis=0).T
    counts = _col_to_row(carry_ref[:, 0:1], E)
    cnt_ref[...] = jnp.broadcast_to(jnp.concatenate([counts, jnp.zeros((1, LANES - E), F32)], axis=1), (8, LANES))


def _router(x1, w_router, router_bias, tr=512):
    n, d = x1.shape
    wr = jnp.zeros((d, LANES), F32).at[:, :N_EXPERTS].set(w_router)
    slot_rows = pl.BlockSpec((TOP_K, tr), lambda i: (0, i))
    return pl.pallas_call(
        functools.partial(_router_kernel, tr=tr),
        grid=(n // tr,),
        in_specs=[pl.BlockSpec((tr, d), lambda i: (i, 0)),
                  pl.BlockSpec((d, LANES), lambda i: (0, 0)),
                  pl.BlockSpec((N_EXPERTS, 1), lambda i: (0, 0))],
        out_specs=[slot_rows, pl.BlockSpec((tr, LANES), lambda i: (i, 0)), slot_rows,
                   pl.BlockSpec((8, LANES), lambda i: (0, 0))],
        out_shape=[jax.ShapeDtypeStruct((TOP_K, n), jnp.int32), jax.ShapeDtypeStruct((n, LANES), F32),
                   jax.ShapeDtypeStruct((TOP_K, n), jnp.int32), jax.ShapeDtypeStruct((8, LANES), F32)],
        scratch_shapes=[pltpu.VMEM((N_EXPERTS, LANES), F32)],
        compiler_params=_cparams(("arbitrary",)),
        name="router",
    )(x1, wr, router_bias.reshape(N_EXPERTS, 1))


def _group_offsets(cnt, tm):
    padded = jnp.floor((cnt + (tm - 1)) * (1.0 / tm)) * tm
    upper = (_iota2((LANES, LANES), 0) < _iota2((LANES, LANES), 1)).astype(F32)
    offs = jnp.dot(padded, upper, precision=HI, preferred_element_type=F32)
    return padded, offs


def _pos_kernel(idx_ref, rank_ref, cnt_ref, pos_ref, *, tm):
    _, offs = _group_offsets(cnt_ref[...], tm)
    starts = _row_to_col(offs[0:1, :], LANES)[:N_EXPERTS, :]
    idx = idx_ref[...]
    eidx = _iota2((N_EXPERTS, idx.shape[1]), 0)
    rows = [jnp.sum(jnp.where(eidx == idx[k:k + 1, :], starts, 0.0), axis=0, keepdims=True)
            for k in range(TOP_K)]
    pos_ref[...] = jnp.concatenate(rows, axis=0).astype(jnp.int32) + rank_ref[...]


def _positions(idx_t, rank_t, cnt, tm, tp=2048):
    n = idx_t.shape[1]
    slot_rows = pl.BlockSpec((TOP_K, tp), lambda i: (0, i))
    return pl.pallas_call(
        functools.partial(_pos_kernel, tm=tm),
        grid=(n // tp,),
        in_specs=[slot_rows, slot_rows, pl.BlockSpec((8, LANES), lambda i: (0, 0))],
        out_specs=slot_rows,
        out_shape=jax.ShapeDtypeStruct((TOP_K, n), jnp.int32),
        compiler_params=_cparams(("parallel",)),
        name="dispatch_positions",
    )(idx_t, rank_t, cnt)


def _tile_map_kernel(cnt_ref, te_ref, *, tm, width):
    padded, offs = _group_offsets(cnt_ref[...], tm)
    ends = _row_to_col((offs + padded)[0:1, :], LANES)
    expert = _iota2((LANES, width), 0)
    start = (_iota2((LANES, width), 1) * tm).astype(F32)
    done = jnp.logical_and(ends <= start, expert < N_EXPERTS)
    te = jnp.sum(jnp.where(done, 1, 0), axis=0, keepdims=True)
    total = jnp.max(jnp.where(expert < N_EXPERTS, ends, 0.0), axis=0, keepdims=True)
    ntile = (total * (1.0 / tm)).astype(jnp.int32)
    vend = _row_to_col((offs + cnt_ref[...])[0:1, :], LANES)
    mine = jnp.sum(jnp.where(expert == te, vend, 0.0), axis=0, keepdims=True)
    valid = jnp.clip(mine - start[0:1, :], 0.0, float(tm)).astype(jnp.int32)
    row = _iota2((8, width), 0)
    te_ref[...] = jnp.where(row == 0, te, jnp.where(row == 1, ntile, jnp.where(row == 2, valid, 0)))


def _tile_map(cnt, tm, ntile_max):
    width = -(-ntile_max // LANES) * LANES
    out = pl.pallas_call(
        functools.partial(_tile_map_kernel, tm=tm, width=width),
        in_specs=[pl.BlockSpec((8, LANES), lambda: (0, 0))],
        out_specs=pl.BlockSpec((8, width), lambda: (0, 0)),
        out_shape=jax.ShapeDtypeStruct((8, width), jnp.int32),
        name="tile_map",
    )(cnt)
    return out[0, :ntile_max], out[1, :1], out[2, :ntile_max]


SC_SCATTER_ROWS = 64
SC_GATHER_ROWS = 32


def _sc_mesh():
    info = plsc.get_sparse_core_info()
    mesh = plsc.VectorSubcoreMesh(core_axis_name="c", subcore_axis_name="s")
    return mesh, info.num_cores, info.num_subcores


def _sc_scatter_rows(src3, idx3, nrows):
    n, c, _ = src3.shape
    _, nk, chunk = idx3.shape
    mesh, ncore, nsub = _sc_mesh()
    per_w = n // (ncore * nsub)
    assert chunk == SC_SCATTER_ROWS and per_w % chunk == 0

    @functools.partial(
        pl.kernel, mesh=mesh,
        out_type=jax.ShapeDtypeStruct((nrows, c, LANES), src3.dtype),
        scratch_types=[pltpu.VMEM((nk, chunk), jnp.int32),
                       pltpu.VMEM((chunk, c, LANES), src3.dtype),
                       pltpu.SemaphoreType.DMA],
    )
    def scatter_kernel(src_hbm, idx_hbm, out_hbm, idx_v, rows_v, sem):
        base = (lax.axis_index("s") * ncore + lax.axis_index("c")) * per_w

        @pl.loop(0, per_w // chunk)
        def _(j):
            off = pl.multiple_of(base + j * chunk, chunk)
            pltpu.sync_copy(idx_hbm.at[base // chunk + j], idx_v)
            pltpu.sync_copy(src_hbm.at[pl.ds(off, chunk)], rows_v)
            copies = [pltpu.async_copy(rows_v, out_hbm.at[idx_v.at[k]], sem) for k in range(nk)]
            for cp in copies:
                cp.wait()

    return scatter_kernel(src3, idx3)


def _sc_gather_rows(table3, idx2):
    _, c, _ = table3.shape
    nchunk, chunk = idx2.shape
    mesh, ncore, nsub = _sc_mesh()
    per_w = nchunk // (ncore * nsub)
    assert chunk == SC_GATHER_ROWS and per_w % 2 == 0

    @functools.partial(
        pl.kernel, mesh=mesh,
        out_type=jax.ShapeDtypeStruct((nchunk * chunk, c, LANES), table3.dtype),
        scratch_types=[pltpu.VMEM((per_w, chunk), jnp.int32),
                       pltpu.VMEM((chunk, c, LANES), table3.dtype),
                       pltpu.VMEM((chunk, c, LANES), table3.dtype),
                       pltpu.SemaphoreType.DMA, pltpu.SemaphoreType.DMA,
                       pltpu.SemaphoreType.DMA, pltpu.SemaphoreType.DMA],
    )
    def gather_kernel(table_hbm, idx_hbm, out_hbm, idx_v, rows0, rows1, g0, g1, w0, w1):
        first = (lax.axis_index("s") * ncore + lax.axis_index("c")) * per_w
        pltpu.sync_copy(idx_hbm.at[pl.ds(pl.multiple_of(first, per_w), per_w)], idx_v)

        @pl.loop(0, per_w, step=2)
        def _(j):
            ga = pltpu.async_copy(table_hbm.at[idx_v.at[j]], rows0, g0)
            gb = pltpu.async_copy(table_hbm.at[idx_v.at[j + 1]], rows1, g1)
            ga.wait()
            wa = pltpu.async_copy(rows0, out_hbm.at[pl.ds(pl.multiple_of((first + j) * chunk, chunk), chunk)], w0)
            gb.wait()
            wb = pltpu.async_copy(rows1, out_hbm.at[pl.ds(pl.multiple_of((first + j + 1) * chunk, chunk), chunk)], w1)
            wa.wait()
            wb.wait()

    return gather_kernel(table3, idx2)


FFN_LOOKAHEAD = 2
FFN_SLOTS = FFN_LOOKAHEAD + 1


def _ffn_kernel(te_ref, nt_ref, nv_ref, xs_ref, w1_ref, w3_ref, w2_ref, o_ref, w1b, w3b, w2b, ord_ref, *, tm):
    j = pl.program_id(0)
    i = j - FFN_LOOKAHEAD
    nt = nt_ref[0]

    @pl.when(j == 0)
    def _():
        for s in range(4):
            ord_ref[s] = 0

    def starts_group(t):
        tc = jnp.clip(t, 0, nt - 1)
        changed = te_ref[tc] != te_ref[jnp.maximum(tc - 1, 0)]
        return jnp.logical_or(t == 0, jnp.logical_and(jnp.logical_and(t > 0, t < nt), changed))

    for s, (w_ref, wb) in enumerate(((w1_ref, w1b), (w3_ref, w3b), (w2_ref, w2b))):
        @pl.when(starts_group(j - s))
        def _(s=s, w_ref=w_ref, wb=wb):
            wb[ord_ref[s] % FFN_SLOTS] = w_ref[0, 0].astype(BF16)
            ord_ref[s] = ord_ref[s] + 1

    @pl.when(jnp.logical_and(i >= 0, i < nt))
    def _():
        ic = jnp.maximum(i, 0)

        @pl.when(jnp.logical_and(i > 0, starts_group(i)))
        def _():
            ord_ref[3] = ord_ref[3] + 1

        slot = ord_ref[3] % FFN_SLOTS
        c = xs_ref.shape[0] // tm
        live = _iota2((tm, LANES), 0) < nv_ref[ic]
        parts = [_unpack_bf16_pairs(jnp.where(live, p, jnp.uint32(0)))
                 for p in _load_token_major(xs_ref, 0, tm, c)]
        lo = jnp.concatenate([p[0] for p in parts], axis=1)
        hi = jnp.concatenate([p[1] for p in parts], axis=1)
        half = lo.shape[1]
        h1 = (jnp.dot(lo, w1b[slot, :half, :], preferred_element_type=F32)
              + jnp.dot(hi, w1b[slot, half:, :], preferred_element_type=F32))
        h3 = (jnp.dot(lo, w3b[slot, :half, :], preferred_element_type=F32)
              + jnp.dot(hi, w3b[slot, half:, :], preferred_element_type=F32))
        h = (_silu(h1) * h3).astype(BF16)
        y = jnp.dot(h, w2b[slot], preferred_element_type=F32)
        _store_token_major(o_ref, _pack_bf16_pairs(y))

    @pl.when(i >= nt)
    def _():
        o_ref[...] = jnp.zeros_like(o_ref)


def _expert_ffn(te, nt, nv, xs, w1, w3, w2, layer, nrows, tm):
    d, f = w1.shape[2], w1.shape[3]
    c = xs.shape[0] // nrows
    ntile = nrows // tm

    def tile(j, te_r, nt_r, nv_r):
        return (jnp.clip(j - FFN_LOOKAHEAD, 0, nt_r[0] - 1), 0)

    def out_tile(j, te_r, nt_r, nv_r):
        return (jnp.maximum(j - FFN_LOOKAHEAD, 0), 0)

    def expert(delay):
        def index_map(j, te_r, nt_r, nv_r):
            return (layer, te_r[jnp.clip(j - delay, 0, nt_r[0] - 1)], 0, 0)
        return index_map

    assert FFN_LOOKAHEAD == 2
    grid_spec = pltpu.PrefetchScalarGridSpec(
        num_scalar_prefetch=3,
        grid=(ntile + FFN_LOOKAHEAD,),
        in_specs=[pl.BlockSpec((tm * c, LANES), tile),
                  pl.BlockSpec((1, 1, d, f), expert(0)),
                  pl.BlockSpec((1, 1, d, f), expert(1)),
                  pl.BlockSpec((1, 1, f, d), expert(2))],
        out_specs=pl.BlockSpec((tm * c, LANES), out_tile),
        scratch_shapes=[pltpu.VMEM((FFN_SLOTS, d, f), BF16), pltpu.VMEM((FFN_SLOTS, d, f), BF16),
                        pltpu.VMEM((FFN_SLOTS, f, d), BF16), pltpu.SMEM((4,), jnp.int32)],
    )
    return pl.pallas_call(
        functools.partial(_ffn_kernel, tm=tm),
        grid_spec=grid_spec,
        out_shape=jax.ShapeDtypeStruct((nrows * c, LANES), jnp.uint32),
        compiler_params=_cparams(("arbitrary",)),
        name="expert_ffn",
    )(te, nt, nv, xs, w1, w3, w2)


def _shared_expert_kernel(x_ref, ws1_ref, ws3_ref, ws2_ref, o_ref):
    xb = x_ref[...].astype(BF16)
    s1 = jnp.dot(xb, ws1_ref[...], preferred_element_type=F32)
    s3 = jnp.dot(xb, ws3_ref[...], preferred_element_type=F32)
    o_ref[...] = jnp.dot((_silu(s1) * s3).astype(BF16), ws2_ref[...], preferred_element_type=F32)


def _shared_expert(x1, ws1, ws3, ws2, tm=512):
    n, d = x1.shape
    sf = ws1.shape[1]
    return pl.pallas_call(
        _shared_expert_kernel,
        grid=(n // tm,),
        in_specs=[pl.BlockSpec((tm, d), lambda i: (i, 0)),
                  pl.BlockSpec((d, sf), lambda i: (0, 0)),
                  pl.BlockSpec((d, sf), lambda i: (0, 0)),
                  pl.BlockSpec((sf, d), lambda i: (0, 0))],
        out_specs=pl.BlockSpec((tm, d), lambda i: (i, 0)),
        out_shape=jax.ShapeDtypeStruct((n, d), F32),
        compiler_params=_cparams(("parallel",)),
        name="shared_expert",
    )(x1, ws1, ws3, ws2)


def _combine_kernel(*refs, tc, has_prev):
    y_refs = refs[:TOP_K]
    wts_ref, x_ref, sh_ref, g_ref, b_ref = refs[TOP_K:TOP_K + 5]
    o_ref, ob_ref = refs[-2:]
    c = y_refs[0].shape[0] // tc
    acc = sh_ref[...]
    wts = wts_ref[...]
    for k in range(TOP_K):
        parts = [_unpack_bf16_pairs(p) for p in _load_token_major(y_refs[k], 0, tc, c)]
        yk = jnp.concatenate([p[0] for p in parts] + [p[1] for p in parts], axis=1).astype(F32)
        acc = acc + wts[:, k:k + 1] * yk
    y = _layernorm_rows(DN_ALPHA * x_ref[...] + acc, g_ref[...], b_ref[...])
    o_ref[...] = y
    ob_ref[...] = y.astype(BF16)


COMBINE_PARTS = 2


def _combine(yg, wts, x1, shared, g, b, part, prev, tc=256):
    n, d = x1.shape
    m = n // COMBINE_PARTS
    c = yg.shape[0] // (TOP_K * m)
    nblk = m // tc
    base = part * nblk

    def slot_spec(k):
        return pl.BlockSpec((tc * c, LANES), lambda i: (k * nblk + i, 0))

    rows = pl.BlockSpec((tc, d), lambda i: (base + i, 0))
    in_specs = [slot_spec(k) for k in range(TOP_K)] + [
        pl.BlockSpec((tc, LANES), lambda i: (base + i, 0)), rows, rows,
        pl.BlockSpec((1, d), lambda i: (0, 0)),
        pl.BlockSpec((1, d), lambda i: (0, 0))]
    args = [yg] * TOP_K + [wts, x1, shared, g.reshape(1, d), b.reshape(1, d)]
    aliases = {}
    if prev is not None:
        aliases = {len(args): 0, len(args) + 1: 1}
        in_specs += [pl.BlockSpec(memory_space=pl.ANY)] * 2
        args += list(prev)
    return pl.pallas_call(
        functools.partial(_combine_kernel, tc=tc, has_prev=prev is not None),
        grid=(nblk,),
        in_specs=in_specs,
        out_specs=[rows, rows],
        out_shape=[jax.ShapeDtypeStruct((n, d), F32), jax.ShapeDtypeStruct((n, d), BF16)],
        input_output_aliases=aliases,
        compiler_params=_cparams(("parallel",)),
        name="moe_combine",
    )(*args)


def _cast_kernel(x_ref, o_ref):
    o_ref[...] = x_ref[...].astype(o_ref.dtype)


def _to_bf16(x, tm=1024):
    n, d = x.shape
    return pl.pallas_call(
        _cast_kernel,
        grid=(n // tm,),
        in_specs=[pl.BlockSpec((tm, d), lambda i: (i, 0))],
        out_specs=pl.BlockSpec((tm, d), lambda i: (i, 0)),
        out_shape=jax.ShapeDtypeStruct((n, d), BF16),
        compiler_params=_cparams(("parallel",)),
        name="cast_bf16",
    )(x)


FFN_TILE = 256


def _moe(x1, x1p, w_router, router_bias, w1, w3, w2, layer, ws1, ws3, ws2, g, b):
    n = x1.shape[0]
    tm = FFN_TILE
    nrows = n * TOP_K + N_EXPERTS * tm
    idx, wts, rank, cnt = _router(x1, w_router, router_bias)
    pos_t = _positions(idx, rank, cnt, tm)
    te, nt, nv = _tile_map(cnt, tm, nrows // tm)
    c = x1p.shape[0] // n
    pos_s = pos_t.reshape(TOP_K, n // SC_SCATTER_ROWS, SC_SCATTER_ROWS).transpose(1, 0, 2)
    xs = _sc_scatter_rows(x1p.reshape(n, c, LANES), pos_s, nrows)
    shared = _shared_expert(x1, ws1.astype(BF16), ws3.astype(BF16), ws2.astype(BF16))
    ys = _expert_ffn(te, nt, nv, xs.reshape(nrows * c, LANES), w1, w3, w2, layer, nrows, tm)
    ys3 = ys.reshape(nrows, c, LANES)
    m = n // COMBINE_PARTS
    out = None
    for part in range(COMBINE_PARTS):
        idx_part = pos_t[:, part * m:(part + 1) * m].reshape(-1, SC_GATHER_ROWS)
        yg = _sc_gather_rows(ys3, idx_part)
        out = _combine(yg.reshape(TOP_K * m * c, LANES), wts, x1, shared, g, b, part, out)
    return out


def kernel(x, positions, w_in, gla_wa2, gla_ba, gla_norm, mlstm_conv_w, mlstm_conv_b, mlstm_bi, mlstm_bf,
           mlstm_norm, sgu_ln_g, sgu_ln_b, sgu_ws, sgu_bs, w_pa, w_pb, w_pc, w_pd, w_out, ln1_g, ln1_b,
           w_router, router_bias, w1, w3, w2, ws1, ws3, ws2, ln2_g, ln2_b):
    nbatch, seq, d = x.shape
    n = nbatch * seq
    xf = x.reshape(n, d)
    xb = _to_bf16(xf)
    cos_t, sin_t = _rope_tables(positions)
    for l in range(DEPTH):
        wl = w_in[l]
        w_a = wl[:, _OFF_A:_OFF_A + _W_A].astype(BF16)
        w_b = wl[:, _OFF_B:_OFF_B + _W_B].astype(BF16)
        w_c = wl[:, _OFF_C:_OFF_C + _W_C].astype(BF16)
        w_d = wl[:, _OFF_D:_OFF_D + _W_D].astype(BF16)
        w_g = wl[:, _OFF_G:_OFF_G + _W_G].astype(BF16)
        w_s = jnp.concatenate(
            [wl[:, _OFF_LOW:_OFF_LOW + GLA_RANK], wl[:, _OFF_IF:_OFF_IF + 2 * MLSTM_HEADS],
             jnp.zeros((d, LANES - GLA_RANK - 2 * MLSTM_HEADS), F32)], axis=1).astype(BF16)
        y_a = _matmul(xb, w_a, 2048, 768)
        y_b = _matmul(xb, w_b, 2048, 768)
        y_c = _matmul(xb, w_c, 2048, 768)
        y_d = _matmul(xb, w_d, 2048, 1024)
        y_s = _matmul(xb, w_s, 2048, LANES)
        sm3 = y_s.reshape(nbatch, seq, LANES)
        o_a, o_d, o_c = _mixers_acd(
            y_a.reshape(nbatch, seq, _W_A), y_d.reshape(nbatch, seq, _W_D), sm3, y_c, gla_wa2[l], gla_ba[l],
            gla_norm[l], mlstm_conv_w[l], mlstm_conv_b[l], mlstm_bi[l], mlstm_bf[l], mlstm_norm[l],
            sgu_ln_g[l], sgu_ln_b[l], sgu_ws[l], sgu_bs[l])
        o_a = o_a.reshape(n, GLA_V)
        o_d = o_d.reshape(n, ML_W)
        o_b = _dilated(y_b, cos_t, sin_t, nbatch)
        merged = _merge(xb, w_g, (o_a, o_b, o_c, o_d),
                        (w_pa[l].astype(BF16), w_pb[l].astype(BF16), w_pc[l].astype(BF16), w_pd[l].astype(BF16)))
        x1, x1p = _outproj_ln(merged, w_out[l].astype(BF16), xf, ln1_g[l], ln1_b[l])
        xf, xb = _moe(x1, x1p, w_router[l], router_bias[l], w1, w3, w2, l, ws1[l], ws3[l], ws2[l],
                      ln2_g[l], ln2_b[l])
    return xf.reshape(nbatch, seq, d)
```

```python
import functools
import math

import jax
import jax.numpy as jnp
from jax import lax
from jax.experimental import pallas as pl
from jax.experimental.pallas import tpu as pltpu
from jax.experimental.pallas import tpu_sc as plsc

D_MODEL = 2048
DEPTH = 2

GLA_HEADS = 4
GLA_DK = 64
GLA_DV = 128
GLA_RANK = 16
GLA_TAU = 16.0
GLA_CHUNK = 64
GLA_QK = GLA_HEADS * GLA_DK
GLA_V = GLA_HEADS * GLA_DV

DIL_PAIRS = ((128, 1), (512, 4), (2048, 16))
DIL_HEADS_PER_GROUP = 4
DIL_HEAD_DIM = 64
DIL_HEADS = len(DIL_PAIRS) * DIL_HEADS_PER_GROUP
DIL_W = DIL_HEADS * DIL_HEAD_DIM
DIL_OUT = DIL_HEADS_PER_GROUP * DIL_HEAD_DIM
DIL_BLOCK = 128
ROPE_THETA = 10000.0

SGU_CHUNK = 128
SGU_GROUPS = 6
SGU_GROUP_CH = 128
SGU_W = SGU_GROUPS * SGU_GROUP_CH

MLSTM_HEADS = 4
MLSTM_HEAD_DIM = 128
MLSTM_CHUNK = 64
MLSTM_CONV = 4
ML_W = MLSTM_HEADS * MLSTM_HEAD_DIM

N_EXPERTS = 64
TOP_K = 8
N_GROUPS = 8
TOPK_GROUPS = 4
EXPERT_FF = 512
SHARED_FF = 512
ROUTED_SCALE = 2.5

N_BRANCH = 4
DN_ALPHA = (2 * DEPTH) ** 0.25
LN_EPS = 1e-5

_OFF_A = 0
_W_A = 2 * GLA_QK + 2 * GLA_V
_OFF_LOW = _OFF_A + _W_A
_OFF_B = _OFF_LOW + GLA_RANK
_W_B = 3 * DIL_W
_OFF_C = _OFF_B + _W_B
_W_C = 2 * SGU_W
_OFF_D = _OFF_C + _W_C
_W_D = 4 * ML_W
_OFF_IF = _OFF_D + _W_D
_OFF_G = _OFF_IF + 2 * MLSTM_HEADS
_W_G = N_BRANCH * D_MODEL

LANES = 128
VMEM_LIMIT = 56 * 1024 * 1024

_SM_LOW = 0
_SM_I = GLA_RANK
_SM_F = GLA_RANK + MLSTM_HEADS

HI = lax.Precision.HIGHEST
F32 = jnp.float32
BF16 = jnp.bfloat16
NEG_INF = float("-inf")


def _cparams(sem):
    return pltpu.CompilerParams(dimension_semantics=sem, vmem_limit_bytes=VMEM_LIMIT)


def _log_sigmoid(x):
    return jnp.minimum(x, 0.0) - jnp.log1p(jnp.exp(-jnp.abs(x)))


def _sigmoid(x):
    return 1.0 / (1.0 + jnp.exp(-x))


def _silu(x):
    return x * _sigmoid(x)


def _iota2(shape, dim):
    return lax.broadcasted_iota(jnp.int32, shape, dim)


def _col_to_row(col, n):
    eye = _iota2((n, n), 0) == _iota2((n, n), 1)
    return jnp.sum(jnp.where(eye, col, 0.0), axis=0, keepdims=True)


def _row_to_col(row, n):
    eye = _iota2((n, n), 0) == _iota2((n, n), 1)
    return jnp.sum(jnp.where(eye, row, 0.0), axis=1, keepdims=True)


def _mm_kernel(x_ref, w_ref, o_ref):
    o_ref[...] = jnp.dot(x_ref[...], w_ref[...], preferred_element_type=F32).astype(o_ref.dtype)


def _matmul(x, w, tm, tn, out_dtype=F32):
    n, k = x.shape
    m = w.shape[1]
    return pl.pallas_call(
        _mm_kernel,
        grid=(n // tm, m // tn),
        in_specs=[pl.BlockSpec((tm, k), lambda i, j: (i, 0)),
                  pl.BlockSpec((k, tn), lambda i, j: (0, j))],
        out_specs=pl.BlockSpec((tm, tn), lambda i, j: (i, j)),
        out_shape=jax.ShapeDtypeStruct((n, m), out_dtype),
        compiler_params=_cparams(("parallel", "arbitrary")),
        name="in_proj",
    )(x, w)


def _gla_chunk(y_ref, sm_ref, wa2_ref, ba_ref, g_ref, o_ref, state_ref, *, nb):
    L, H, DK, DV = GLA_CHUNK, GLA_HEADS, GLA_DK, GLA_DV
    tril = (_iota2((L, L), 0) >= _iota2((L, L), 1))
    tril_f = tril.astype(F32)
    for b in range(nb):
        y = y_ref[b]
        a_low = sm_ref[b][:, _SM_LOW:_SM_LOW + GLA_RANK]
        glog = jnp.dot(a_low, wa2_ref[...], preferred_element_type=F32) + ba_ref[...]
        g = _log_sigmoid(glog) * (1.0 / GLA_TAU)
        bc = jnp.dot(tril_f, g, precision=HI, preferred_element_type=F32)
        outs = []
        for h in range(H):
            q = y[:, h * DK:(h + 1) * DK] * (DK ** -0.5)
            k = y[:, GLA_QK + h * DK:GLA_QK + (h + 1) * DK]
            v = y[:, 2 * GLA_QK + h * DV:2 * GLA_QK + (h + 1) * DV]
            bh = bc[:, h * DK:(h + 1) * DK]
            qe = q * jnp.exp(bh)
            ke = k * jnp.exp(-bh)
            att = lax.dot_general(qe, ke, (((1,), (1,)), ((), ())), preferred_element_type=F32)
            att = jnp.where(tril, att, 0.0)
            st = state_ref[b * H + h]
            o = (jnp.dot(att, v, preferred_element_type=F32)
                 + jnp.dot(qe, st, preferred_element_type=F32))
            b_last = bh[L - 1:L, :]
            kd = k * jnp.exp(b_last - bh)
            decay = _row_to_col(jnp.exp(b_last), DK)
            state_ref[b * H + h] = decay * st + lax.dot_general(
                kd, v, (((0,), (0,)), ((), ())), preferred_element_type=F32)
            o = o * lax.rsqrt(jnp.mean(o * o, axis=-1, keepdims=True) + LN_EPS)
            outs.append(o)
        o_all = jnp.concatenate(outs, axis=-1) * g_ref[...]
        r = y[:, 2 * GLA_QK + GLA_V:2 * GLA_QK + 2 * GLA_V]
        o_ref[b] = (o_all * _silu(r)).astype(o_ref.dtype)


MLSTM_HALO = 8


def _mlstm_chunk(y_ref, sm_ref, cw_ref, cb_ref, gb_ref, g_ref, o_ref, c_ref, n_ref, m_ref, tail_ref, *, nb):
    L, H, DH = MLSTM_CHUNK, MLSTM_HEADS, MLSTM_HEAD_DIM
    W2 = 2 * ML_W
    HALO = MLSTM_HALO
    tril = (_iota2((L, L), 0) >= _iota2((L, L), 1))
    tril_f = tril.astype(F32)
    for b in range(nb):
        y = y_ref[b]
        qk_raw = y[:, :W2]
        ext = jnp.concatenate([tail_ref[b], qk_raw], axis=0)
        tail_ref[b] = qk_raw[L - HALO:, :]
        conv = cb_ref[...]
        for j in range(MLSTM_CONV):
            s0 = HALO - (MLSTM_CONV - 1) + j
            conv = conv + cw_ref[j:j + 1, :] * ext[s0:s0 + L, :]
        qk = _silu(conv)
        gates = sm_ref[b] + gb_ref[...]
        bcum = jnp.dot(tril_f, _log_sigmoid(gates), precision=HI, preferred_element_type=F32)
        outs = []
        for h in range(H):
            q = qk[:, h * DH:(h + 1) * DH]
            k = qk[:, ML_W + h * DH:ML_W + (h + 1) * DH] * (DH ** -0.5)
            v = y[:, W2 + h * DH:W2 + (h + 1) * DH]
            b_col = bcum[:, _SM_F + h:_SM_F + h + 1]
            li_col = gates[:, _SM_I + h:_SM_I + h + 1]
            b_row = _col_to_row(b_col, L)
            li_row = _col_to_row(li_col, L)
            m_prev = m_ref[b * H + h][:, 0:1]
            dmat = jnp.where(tril, b_col - b_row + li_row, NEG_INF)
            inter = b_col + m_prev
            m_t = jnp.maximum(inter, jnp.max(dmat, axis=-1, keepdims=True))
            w_in = jnp.exp(dmat - m_t)
            w_st = jnp.exp(inter - m_t)
            sc = lax.dot_general(q, k, (((1,), (1,)), ((), ())), preferred_element_type=F32) * w_in
            cst = c_ref[b * H + h]
            nst = n_ref[b * H + h]
            num = (jnp.dot(sc, v, preferred_element_type=F32)
                   + w_st * jnp.dot(q, cst, preferred_element_type=F32))
            den = jnp.sum(sc, axis=-1, keepdims=True) + w_st * jnp.sum(q * nst, axis=-1, keepdims=True)
            hh = num / jnp.maximum(jnp.abs(den), jnp.exp(-m_t))
            b_last = b_col[L - 1:L, :]
            dec = b_last - b_col + li_col
            m_new = jnp.maximum(b_last + m_prev, jnp.max(dec, axis=0, keepdims=True))
            wk = jnp.exp(dec - m_new)
            keep = jnp.exp(b_last + m_prev - m_new)
            wkk = wk * k
            c_ref[b * H + h] = keep * cst + lax.dot_general(
                wkk, v, (((0,), (0,)), ((), ())), preferred_element_type=F32)
            n_ref[b * H + h] = keep * nst + jnp.sum(wkk, axis=0, keepdims=True)
            m_ref[b * H + h] = jnp.broadcast_to(m_new, (1, LANES))
            o_pre = y[:, W2 + ML_W + h * DH:W2 + ML_W + (h + 1) * DH]
            hh = _sigmoid(o_pre) * hh
            hh = hh * lax.rsqrt(jnp.mean(hh * hh, axis=-1, keepdims=True) + LN_EPS)
            outs.append(hh)
        o_ref[b] = (jnp.concatenate(outs, axis=-1) * g_ref[...]).astype(o_ref.dtype)


def _mixers_kernel(ya_ref, yd_ref, sm_ref, yc_ref, wa2_ref, ba_ref, ga_ref, cw_ref, cb_ref, gb_ref, gd_ref,
                   lg_ref, lb_ref, ws_ref, bst_ref, oa_ref, od_ref, oc_ref,
                   state_ref, c_ref, n_ref, m_ref, tail_ref, *, nb, sgu_chunks):
    @pl.when(pl.program_id(0) == 0)
    def _():
        for ref in (state_ref, c_ref, n_ref, m_ref, tail_ref):
            ref[...] = jnp.zeros_like(ref)

    _gla_chunk(ya_ref, sm_ref, wa2_ref, ba_ref, ga_ref, oa_ref, state_ref, nb=nb)
    _mlstm_chunk(yd_ref, sm_ref, cw_ref, cb_ref, gb_ref, gd_ref, od_ref, c_ref, n_ref, m_ref, tail_ref, nb=nb)
    _sgu_kernel(yc_ref, lg_ref, lb_ref, ws_ref, bst_ref, oc_ref, nchunk=sgu_chunks)


def _mixers_acd(ya3, yd3, sm3, yc, wa2, ba, gla_norm, conv_w, conv_b, b_i, b_f, mlstm_norm,
                sgu_ln_g, sgu_ln_b, sgu_ws, sgu_bs):
    nb, s, _ = ya3.shape
    assert GLA_CHUNK == MLSTM_CHUNK
    L = GLA_CHUNK
    nstep = s // L
    n = yc.shape[0]
    sgu_rows = n // nstep
    assert sgu_rows % SGU_CHUNK == 0 and sgu_rows * nstep == n
    gate_bias = jnp.zeros((1, LANES), F32)
    gate_bias = gate_bias.at[0, _SM_I:_SM_I + MLSTM_HEADS].set(b_i).at[0, _SM_F:_SM_F + MLSTM_HEADS].set(b_f)
    bst = jnp.zeros((SGU_CHUNK, LANES), F32).at[:, :SGU_GROUPS].set(sgu_bs.T)

    def chunk(width):
        return pl.BlockSpec((nb, L, width), lambda i: (0, i, 0))

    def whole(*shape):
        return pl.BlockSpec(shape, lambda i: (0,) * len(shape))

    return pl.pallas_call(
        functools.partial(_mixers_kernel, nb=nb, sgu_chunks=sgu_rows // SGU_CHUNK),
        grid=(nstep,),
        in_specs=[chunk(_W_A), chunk(_W_D), chunk(LANES), pl.BlockSpec((sgu_rows, _W_C), lambda i: (i, 0)),
                  whole(GLA_RANK, GLA_QK), whole(1, GLA_QK), whole(1, GLA_V),
                  whole(MLSTM_CONV, 2 * ML_W), whole(1, 2 * ML_W), whole(1, LANES), whole(1, ML_W),
                  whole(1, SGU_W), whole(1, SGU_W), whole(SGU_GROUPS, SGU_CHUNK, SGU_CHUNK),
                  whole(SGU_CHUNK, LANES)],
        out_specs=[chunk(GLA_V), chunk(ML_W), pl.BlockSpec((sgu_rows, SGU_W), lambda i: (i, 0))],
        out_shape=[jax.ShapeDtypeStruct((nb, s, GLA_V), BF16), jax.ShapeDtypeStruct((nb, s, ML_W), BF16),
                   jax.ShapeDtypeStruct((n, SGU_W), BF16)],
        scratch_shapes=[pltpu.VMEM((nb * GLA_HEADS, GLA_DK, GLA_DV), F32),
                        pltpu.VMEM((nb * MLSTM_HEADS, MLSTM_HEAD_DIM, MLSTM_HEAD_DIM), F32),
                        pltpu.VMEM((nb * MLSTM_HEADS, 1, MLSTM_HEAD_DIM), F32),
                        pltpu.VMEM((nb * MLSTM_HEADS, 1, LANES), F32),
                        pltpu.VMEM((nb, MLSTM_HALO, 2 * ML_W), F32)],
        compiler_params=_cparams(("arbitrary",)),
        name="gla_mlstm_sgu",
    )(ya3, yd3, sm3, yc, wa2, ba.reshape(1, GLA_QK), gla_norm.reshape(1, GLA_V),
      conv_w, conv_b.reshape(1, 2 * ML_W), gate_bias, mlstm_norm.reshape(1, ML_W),
      sgu_ln_g.reshape(1, SGU_W), sgu_ln_b.reshape(1, SGU_W), sgu_ws, bst)


def _gelu(x):
    return 0.5 * x * (1.0 + lax.erf(x * (0.5 ** 0.5)))


def _sgu_kernel(y_ref, lg_ref, lb_ref, ws_ref, bst_ref, o_ref, *, nchunk):
    C, G, GC = SGU_CHUNK, SGU_GROUPS, SGU_GROUP_CH
    y = y_ref[...]
    zu = _gelu(y[:, :SGU_W])
    zv = _gelu(y[:, SGU_W:])
    mu = jnp.mean(zv, axis=-1, keepdims=True)
    var = jnp.mean(jnp.square(zv - mu), axis=-1, keepdims=True)
    vn = (zv - mu) * lax.rsqrt(var + LN_EPS) * lg_ref[...] + lb_ref[...]
    tril = _iota2((C, C), 0) >= _iota2((C, C), 1)
    for g in range(G):
        wc = jnp.where(tril, ws_ref[g], 0.0)
        bias = bst_ref[:, g:g + 1]
        for c in range(nchunk):
            rows = slice(c * C, (c + 1) * C)
            cols = slice(g * GC, (g + 1) * GC)
            s = jnp.dot(wc, vn[rows, cols], preferred_element_type=F32) + bias
            o_ref[rows, cols] = (zu[rows, cols] * s).astype(o_ref.dtype)


def _rope_table_kernel(pos_ref, inv_ref, cos_ref, sin_ref):
    ang = pos_ref[...].astype(F32) * inv_ref[...]
    half = DIL_HEAD_DIM // 2
    sign = jnp.where((_iota2(ang.shape, 1) % DIL_HEAD_DIM) < half, -1.0, 1.0)
    cos_ref[...] = jnp.cos(ang)
    sin_ref[...] = jnp.sin(ang) * sign


def _rope_tables(positions):
    n = positions.size
    half = DIL_HEAD_DIM // 2
    inv = ROPE_THETA ** (-jnp.arange(half, dtype=F32) * 2.0 / DIL_HEAD_DIM)
    inv = jnp.tile(inv, LANES // half).reshape(1, LANES)
    t = 1024
    return pl.pallas_call(
        _rope_table_kernel,
        grid=(n // t,),
        in_specs=[pl.BlockSpec((t, 1), lambda i: (i, 0)),
                  pl.BlockSpec((1, LANES), lambda i: (0, 0))],
        out_specs=[pl.BlockSpec((t, LANES), lambda i: (i, 0)),
                   pl.BlockSpec((t, LANES), lambda i: (i, 0))],
        out_shape=[jax.ShapeDtypeStruct((n, LANES), F32)] * 2,
        compiler_params=_cparams(("parallel",)),
        name="rope_tables",
    )(positions.reshape(n, 1), inv)


def _dil_kernel(q0, q1, q2, k0, k1, k2, v0, v1, v2, cos_ref, sin_ref, o_ref,
                qs_ref, ks_ref, num_ref, m_ref, den_ref, *, seq):
    DH, BLK = DIL_HEAD_DIM, DIL_BLOCK
    half = DH // 2
    q_refs, k_refs, v_refs = (q0, q1, q2), (k0, k1, k2), (v0, v1, v2)
    cos = cos_ref[...]
    sin = sin_ref[...]
    first_half = (_iota2((seq, LANES), 1) % DH) < half

    def rope(x):
        swapped = jnp.where(first_half, pltpu.roll(x, LANES - half, 1), pltpu.roll(x, half, 1))
        return x * cos + swapped * sin

    for g in range(len(DIL_PAIRS)):
        qs_ref[g] = rope(q_refs[g][...]) * (DH ** -0.5)
        ks_ref[g] = rope(k_refs[g][...])

    ii = _iota2((BLK, BLK), 0)
    jj = _iota2((BLK, BLK), 1)
    mask_cur = jj <= ii
    mask_prev = jj >= ii
    assert LANES == 2 * DH and BLK == LANES
    head_lanes = [(jj // DH) == h for h in range(LANES // DH)]
    head_ones = [hl.astype(F32) for hl in head_lanes]

    for g, (window, dil) in enumerate(DIL_PAIRS):
        lsub = seq // dil
        nblk = lsub // BLK
        assert window // dil == BLK and lsub % BLK == 0
        v_ref = v_refs[g]

        def unit(u, g=g, dil=dil, v_ref=v_ref):
            r = u % dil
            n = u // dil
            rows = pl.ds(n * (BLK * dil) + r, BLK, stride=dil)
            qb = qs_ref[g, rows, :]
            kc = ks_ref[g, rows, :]
            vc = v_ref[rows, :]
            nblk = 2 if n > 0 else 1
            if nblk > 1:
                prow = pl.ds((n - 1) * (BLK * dil) + r, BLK, stride=dil)
                kp = ks_ref[g, prow, :]
                vp = v_ref[prow, :]
                has_prev = True
            num = jnp.zeros((BLK, LANES), F32)
            den = jnp.zeros((BLK, LANES), F32)
            mxs = []
            for h in range(LANES // DH):
                qh = jnp.where(head_lanes[h], qb, 0.0)
                s_c = lax.dot_general(qh, kc, (((1,), (1,)), ((), ())), preferred_element_type=F32)
                s_c = jnp.where(mask_cur, s_c, NEG_INF)
                if nblk > 1:
                    s_p = lax.dot_general(qh, kp, (((1,), (1,)), ((), ())), preferred_element_type=F32)
                    s_p = jnp.where(jnp.logical_and(mask_prev, has_prev), s_p, NEG_INF)
                    mx = jnp.max(jnp.maximum(s_c, s_p), axis=-1, keepdims=True)
                else:
                    mx = jnp.max(s_c, axis=-1, keepdims=True)
                p_c = jnp.exp(s_c - mx)
                num = num + jnp.dot(p_c, jnp.where(head_lanes[h], vc, 0.0), preferred_element_type=F32)
                den = den + jnp.dot(p_c, head_ones[h], preferred_element_type=F32)
                if nblk > 1:
                    p_p = jnp.exp(s_p - mx)
                    num = num + jnp.dot(p_p, jnp.where(head_lanes[h], vp, 0.0), preferred_element_type=F32)
                    den = den + jnp.dot(p_p, head_ones[h], preferred_element_type=F32)
                mxs.append(mx)
            num_ref[g, rows, :] = num
            m_ref[g, rows, :] = jnp.where(head_lanes[0], mxs[0], mxs[1])
            den_ref[g, rows, :] = den

        for u in range(dil * nblk):
            unit(u)

    m_all = jnp.maximum(jnp.maximum(m_ref[0], m_ref[1]), m_ref[2])
    num = jnp.zeros((seq, LANES), F32)
    den = jnp.zeros((seq, LANES), F32)
    for g in range(len(DIL_PAIRS)):
        e = jnp.exp(m_ref[g] - m_all)
        num = num + e * num_ref[g]
        den = den + e * den_ref[g]
    o_ref[...] = (num / den).astype(o_ref.dtype)


def _dilated(yb, cos_t, sin_t, nbatch):
    n = yb.shape[0]
    seq = n // nbatch
    npair = DIL_HEADS_PER_GROUP * DIL_HEAD_DIM // LANES
    nblk_cols = DIL_W // LANES

    def spec(section, g):
        return pl.BlockSpec((seq, LANES), lambda b, p: (b, section * nblk_cols + g * npair + p))

    in_specs = ([spec(0, g) for g in range(3)] + [spec(1, g) for g in range(3)] + [spec(2, g) for g in range(3)]
                + [pl.BlockSpec((seq, LANES), lambda b, p: (b, 0))] * 2)
    return pl.pallas_call(
        functools.partial(_dil_kernel, seq=seq),
        grid=(nbatch, npair),
        in_specs=in_specs,
        out_specs=pl.BlockSpec((seq, LANES), lambda b, p: (b, p)),
        out_shape=jax.ShapeDtypeStruct((n, DIL_OUT), BF16),
        scratch_shapes=[pltpu.VMEM((3, seq, LANES), F32)] * 5,
        compiler_params=_cparams(("parallel", "parallel")),
        name="dilated_attn",
    )(*([yb] * 9), cos_t, sin_t)


def _merge_kernel(x_ref, g0, g1, g2, g3, ya, yb, yc, yd, pa, pb, pc, pd, o_ref):
    x = x_ref[...]
    acc = None
    for wg, y, p in ((g0, ya, pa), (g1, yb, pb), (g2, yc, pc), (g3, yd, pd)):
        gate = _sigmoid(jnp.dot(x, wg[...], preferred_element_type=F32))
        term = gate * jnp.dot(y[...], p[...], preferred_element_type=F32)
        acc = term if acc is None else acc + term
    o_ref[...] = acc.astype(o_ref.dtype)


def _merge(xb, wg, ys, ps, tm=1024, tn=512):
    n, d = xb.shape
    ncol = d // tn

    def gate_spec(br):
        return pl.BlockSpec((d, tn), lambda i, j: (0, br * ncol + j))

    in_specs = ([pl.BlockSpec((tm, d), lambda i, j: (i, 0))]
                + [gate_spec(br) for br in range(N_BRANCH)]
                + [pl.BlockSpec((tm, y.shape[1]), lambda i, j: (i, 0)) for y in ys]
                + [pl.BlockSpec((p.shape[0], tn), lambda i, j: (0, j)) for p in ps])
    return pl.pallas_call(
        _merge_kernel,
        grid=(n // tm, ncol),
        in_specs=in_specs,
        out_specs=pl.BlockSpec((tm, tn), lambda i, j: (i, j)),
        out_shape=jax.ShapeDtypeStruct((n, d), BF16),
        compiler_params=_cparams(("parallel", "arbitrary")),
        name="gated_merge",
    )(xb, wg, wg, wg, wg, *ys, *ps)


def _layernorm_rows(z, g, b):
    mu = jnp.mean(z, axis=-1, keepdims=True)
    var = jnp.mean(jnp.square(z - mu), axis=-1, keepdims=True)
    return (z - mu) * lax.rsqrt(var + LN_EPS) * g + b


def _pack_bf16_pairs(y):
    half = y.shape[1] // 2
    lo = lax.bitcast_convert_type(y[:, :half].astype(BF16).astype(F32), jnp.uint32)
    hi = lax.bitcast_convert_type(y[:, half:].astype(BF16).astype(F32), jnp.uint32)
    return (lo >> 16) | (hi & jnp.uint32(0xFFFF0000))


def _unpack_bf16_pairs(w):
    lo = lax.bitcast_convert_type(w << 16, F32).astype(BF16)
    hi = lax.bitcast_convert_type(w & jnp.uint32(0xFFFF0000), F32).astype(BF16)
    return lo, hi


def _store_token_major(ref, val):
    t, w = val.shape
    c = w // LANES
    for s in range(c):
        ref[pl.ds(s, t, stride=c), :] = val[:, s * LANES:(s + 1) * LANES]


def _load_token_major(ref, start, t, c):
    return [ref[pl.ds(start + s, t, stride=c), :] for s in range(c)]


def _outproj_ln_kernel(m_ref, w_ref, x_ref, g_ref, b_ref, o_ref, p_ref):
    h = jnp.dot(m_ref[...], w_ref[...], preferred_element_type=F32)
    y = _layernorm_rows(DN_ALPHA * x_ref[...] + h, g_ref[...], b_ref[...])
    o_ref[...] = y
    _store_token_major(p_ref, _pack_bf16_pairs(y))


def _outproj_ln(merged, w_out, x, g, b, tm=512):
    n, d = x.shape
    c = d // 2 // LANES
    return pl.pallas_call(
        _outproj_ln_kernel,
        grid=(n // tm,),
        in_specs=[pl.BlockSpec((tm, d), lambda i: (i, 0)),
                  pl.BlockSpec((d, d), lambda i: (0, 0)),
                  pl.BlockSpec((tm, d), lambda i: (i, 0)),
                  pl.BlockSpec((1, d), lambda i: (0, 0)),
                  pl.BlockSpec((1, d), lambda i: (0, 0))],
        out_specs=[pl.BlockSpec((tm, d), lambda i: (i, 0)),
                   pl.BlockSpec((tm * c, LANES), lambda i: (i, 0))],
        out_shape=[jax.ShapeDtypeStruct((n, d), F32), jax.ShapeDtypeStruct((n * c, LANES), jnp.uint32)],
        compiler_params=_cparams(("parallel",)),
        name="outproj_ln",
    )(merged, w_out, x, g.reshape(1, d), b.reshape(1, d))


def _router_kernel(x_ref, wr_ref, rb_ref, idx_ref, wts_ref, rank_ref, cnt_ref, carry_ref, *, tr):
    E, G = N_EXPERTS, N_GROUPS
    gsz = E // G

    @pl.when(pl.program_id(0) == 0)
    def _():
        carry_ref[...] = jnp.zeros_like(carry_ref)

    logits = jnp.dot(x_ref[...], wr_ref[...], precision=HI, preferred_element_type=F32)
    scores = _sigmoid(logits.T[:E, :])
    choice = scores + rb_ref[...]
    eidx = _iota2((E, tr), 0)

    c3 = choice.reshape(G, gsz, tr)
    e3 = eidx.reshape(G, gsz, tr)
    max1 = jnp.max(c3, axis=1, keepdims=True)
    first = jnp.min(jnp.where(c3 == max1, e3, E), axis=1, keepdims=True)
    max2 = jnp.max(jnp.where(e3 == first, NEG_INF, c3), axis=1, keepdims=True)
    gscore = (max1 + max2).reshape(G, tr)
    gidx = _iota2((G, tr), 0)
    gsel = jnp.zeros((G, tr), jnp.bool_)
    for _ in range(TOPK_GROUPS):
        gmax = jnp.max(gscore, axis=0, keepdims=True)
        pick = jnp.min(jnp.where(gscore == gmax, gidx, G), axis=0, keepdims=True)
        hit = gidx == pick
        gsel = jnp.logical_or(gsel, hit)
        gscore = jnp.where(hit, NEG_INF, gscore)
    esel = jnp.broadcast_to(gsel.reshape(G, 1, tr), (G, gsz, tr)).reshape(E, tr)

    cand = jnp.where(esel, choice, NEG_INF)
    sel = jnp.zeros((E, tr), F32)
    picks, weights, hits = [], [], []
    for k in range(TOP_K):
        vmax = jnp.max(cand, axis=0, keepdims=True)
        pick = jnp.min(jnp.where(cand == vmax, eidx, E), axis=0, keepdims=True)
        hit = eidx == pick
        picks.append(pick)
        hits.append(hit)
        weights.append(jnp.sum(jnp.where(hit, scores, 0.0), axis=0, keepdims=True))
        sel = jnp.where(hit, 1.0, sel)
        cand = jnp.where(hit, NEG_INF, cand)
    wsum = weights[0]
    for w in weights[1:]:
        wsum = wsum + w
    w_rows = jnp.concatenate([w / wsum * ROUTED_SCALE for w in weights], axis=0)

    before = (_iota2((tr, tr), 0) < _iota2((tr, tr), 1)).astype(BF16)
    rank = jnp.dot(sel.astype(BF16), before, preferred_element_type=F32) + carry_ref[:, 0:1]
    rank_rows = [jnp.sum(jnp.where(hits[k], rank, 0.0), axis=0, keepdims=True) for k in range(TOP_K)]
    carry_ref[...] = carry_ref[...] + jnp.sum(sel, axis=1, keepdims=True)

    idx_ref[...] = jnp.concatenate(picks, axis=0)
    rank_ref[...] = jnp.concatenate(rank_rows, axis=0).astype(jnp.int32)
    wts_ref[...] = jnp.concatenate([w_rows, jnp.zeros((LANES - TOP_K, tr), F32)], axis=0).T
    counts = _col_to_row(carry_ref[:, 0:1], E)
    cnt_ref[...] = jnp.broadcast_to(jnp.concatenate([counts, jnp.zeros((1, LANES - E), F32)], axis=1), (8, LANES))


def _router(x1, w_router, router_bias, tr=512):
    n, d = x1.shape
    wr = jnp.zeros((d, LANES), F32).at[:, :N_EXPERTS].set(w_router)
    slot_rows = pl.BlockSpec((TOP_K, tr), lambda i: (0, i))
    return pl.pallas_call(
        functools.partial(_router_kernel, tr=tr),
        grid=(n // tr,),
        in_specs=[pl.BlockSpec((tr, d), lambda i: (i, 0)),
                  pl.BlockSpec((d, LANES), lambda i: (0, 0)),
                  pl.BlockSpec((N_EXPERTS, 1), lambda i: (0, 0))],
        out_specs=[slot_rows, pl.BlockSpec((tr, LANES), lambda i: (i, 0)), slot_rows,
                   pl.BlockSpec((8, LANES), lambda i: (0, 0))],
        out_shape=[jax.ShapeDtypeStruct((TOP_K, n), jnp.int32), jax.ShapeDtypeStruct((n, LANES), F32),
                   jax.ShapeDtypeStruct((TOP_K, n), jnp.int32), jax.ShapeDtypeStruct((8, LANES), F32)],
        scratch_shapes=[pltpu.VMEM((N_EXPERTS, LANES), F32)],
        compiler_params=_cparams(("arbitrary",)),
        name="router",
    )(x1, wr, router_bias.reshape(N_EXPERTS, 1))


def _group_offsets(cnt, tm):
    padded = jnp.floor((cnt + (tm - 1)) * (1.0 / tm)) * tm
    upper = (_iota2((LANES, LANES), 0) < _iota2((LANES, LANES), 1)).astype(F32)
    offs = jnp.dot(padded, upper, precision=HI, preferred_element_type=F32)
    return padded, offs


def _pos_kernel(idx_ref, rank_ref, cnt_ref, pos_ref, *, tm):
    _, offs = _group_offsets(cnt_ref[...], tm)
    starts = _row_to_col(offs[0:1, :], LANES)[:N_EXPERTS, :]
    idx = idx_ref[...]
    eidx = _iota2((N_EXPERTS, idx.shape[1]), 0)
    rows = [jnp.sum(jnp.where(eidx == idx[k:k + 1, :], starts, 0.0), axis=0, keepdims=True)
            for k in range(TOP_K)]
    pos_ref[...] = jnp.concatenate(rows, axis=0).astype(jnp.int32) + rank_ref[...]


def _positions(idx_t, rank_t, cnt, tm, tp=2048):
    n = idx_t.shape[1]
    slot_rows = pl.BlockSpec((TOP_K, tp), lambda i: (0, i))
    return pl.pallas_call(
        functools.partial(_pos_kernel, tm=tm),
        grid=(n // tp,),
        in_specs=[slot_rows, slot_rows, pl.BlockSpec((8, LANES), lambda i: (0, 0))],
        out_specs=slot_rows,
        out_shape=jax.ShapeDtypeStruct((TOP_K, n), jnp.int32),
        compiler_params=_cparams(("parallel",)),
        name="dispatch_positions",
    )(idx_t, rank_t, cnt)


def _tile_map_kernel(cnt_ref, te_ref, *, tm, width):
    padded, offs = _group_offsets(cnt_ref[...], tm)
    ends = _row_to_col((offs + padded)[0:1, :], LANES)
    expert = _iota2((LANES, width), 0)
    start = (_iota2((LANES, width), 1) * tm).astype(F32)
    done = jnp.logical_and(ends <= start, expert < N_EXPERTS)
    te = jnp.sum(jnp.where(done, 1, 0), axis=0, keepdims=True)
    total = jnp.max(jnp.where(expert < N_EXPERTS, ends, 0.0), axis=0, keepdims=True)
    ntile = (total * (1.0 / tm)).astype(jnp.int32)
    vend = _row_to_col((offs + cnt_ref[...])[0:1, :], LANES)
    mine = jnp.sum(jnp.where(expert == te, vend, 0.0), axis=0, keepdims=True)
    valid = jnp.clip(mine - start[0:1, :], 0.0, float(tm)).astype(jnp.int32)
    row = _iota2((8, width), 0)
    te_ref[...] = jnp.where(row == 0, te, jnp.where(row == 1, ntile, jnp.where(row == 2, valid, 0)))


def _tile_map(cnt, tm, ntile_max):
    width = -(-ntile_max // LANES) * LANES
    out = pl.pallas_call(
        functools.partial(_tile_map_kernel, tm=tm, width=width),
        in_specs=[pl.BlockSpec((8, LANES), lambda: (0, 0))],
        out_specs=pl.BlockSpec((8, width), lambda: (0, 0)),
        out_shape=jax.ShapeDtypeStruct((8, width), jnp.int32),
        name="tile_map",
    )(cnt)
    return out[0, :ntile_max], out[1, :1], out[2, :ntile_max]


SC_SCATTER_ROWS = 64
SC_GATHER_ROWS = 32


def _sc_mesh():
    info = plsc.get_sparse_core_info()
    mesh = plsc.VectorSubcoreMesh(core_axis_name="c", subcore_axis_name="s")
    return mesh, info.num_cores, info.num_subcores


def _sc_scatter_rows(src3, idx3, nrows):
    n, c, _ = src3.shape
    _, nk, chunk = idx3.shape
    mesh, ncore, nsub = _sc_mesh()
    per_w = n // (ncore * nsub)
    assert chunk == SC_SCATTER_ROWS and per_w % chunk == 0

    @functools.partial(
        pl.kernel, mesh=mesh,
        out_type=jax.ShapeDtypeStruct((nrows, c, LANES), src3.dtype),
        scratch_types=[pltpu.VMEM((nk, chunk), jnp.int32),
                       pltpu.VMEM((chunk, c, LANES), src3.dtype),
                       pltpu.SemaphoreType.DMA],
    )
    def scatter_kernel(src_hbm, idx_hbm, out_hbm, idx_v, rows_v, sem):
        base = (lax.axis_index("s") * ncore + lax.axis_index("c")) * per_w

        @pl.loop(0, per_w // chunk)
        def _(j):
            off = pl.multiple_of(base + j * chunk, chunk)
            pltpu.sync_copy(idx_hbm.at[base // chunk + j], idx_v)
            pltpu.sync_copy(src_hbm.at[pl.ds(off, chunk)], rows_v)
            copies = [pltpu.async_copy(rows_v, out_hbm.at[idx_v.at[k]], sem) for k in range(nk)]
            for cp in copies:
                cp.wait()

    return scatter_kernel(src3, idx3)


def _sc_gather_rows(table3, idx2):
    _, c, _ = table3.shape
    nchunk, chunk = idx2.shape
    mesh, ncore, nsub = _sc_mesh()
    per_w = nchunk // (ncore * nsub)
    assert chunk == SC_GATHER_ROWS and per_w % 2 == 0

    @functools.partial(
        pl.kernel, mesh=mesh,
        out_type=jax.ShapeDtypeStruct((nchunk * chunk, c, LANES), table3.dtype),
        scratch_types=[pltpu.VMEM((per_w, chunk), jnp.int32),
                       pltpu.VMEM((chunk, c, LANES), table3.dtype),
                       pltpu.VMEM((chunk, c, LANES), table3.dtype),
                       pltpu.SemaphoreType.DMA, pltpu.SemaphoreType.DMA,
                       pltpu.SemaphoreType.DMA, pltpu.SemaphoreType.DMA],
    )
    def gather_kernel(table_hbm, idx_hbm, out_hbm, idx_v, rows0, rows1, g0, g1, w0, w1):
        first = (lax.axis_index("s") * ncore + lax.axis_index("c")) * per_w
        pltpu.sync_copy(idx_hbm.at[pl.ds(pl.multiple_of(first, per_w), per_w)], idx_v)

        @pl.loop(0, per_w, step=2)
        def _(j):
            ga = pltpu.async_copy(table_hbm.at[idx_v.at[j]], rows0, g0)
            gb = pltpu.async_copy(table_hbm.at[idx_v.at[j + 1]], rows1, g1)
            ga.wait()
            wa = pltpu.async_copy(rows0, out_hbm.at[pl.ds(pl.multiple_of((first + j) * chunk, chunk), chunk)], w0)
            gb.wait()
            wb = pltpu.async_copy(rows1, out_hbm.at[pl.ds(pl.multiple_of((first + j + 1) * chunk, chunk), chunk)], w1)
            wa.wait()
            wb.wait()

    return gather_kernel(table3, idx2)


FFN_LOOKAHEAD = 2
FFN_SLOTS = FFN_LOOKAHEAD + 1


def _ffn_kernel(te_ref, nt_ref, nv_ref, xs_ref, w1_ref, w3_ref, w2_ref, o_ref, w1b, w3b, w2b, ord_ref, *, tm):
    j = pl.program_id(0)
    i = j - FFN_LOOKAHEAD
    nt = nt_ref[0]

    @pl.when(j == 0)
    def _():
        for s in range(4):
            ord_ref[s] = 0

    def starts_group(t):
        tc = jnp.clip(t, 0, nt - 1)
        changed = te_ref[tc] != te_ref[jnp.maximum(tc - 1, 0)]
        return jnp.logical_or(t == 0, jnp.logical_and(jnp.logical_and(t > 0, t < nt), changed))

    for s, (w_ref, wb) in enumerate(((w1_ref, w1b), (w3_ref, w3b), (w2_ref, w2b))):
        @pl.when(starts_group(j - s))
        def _(s=s, w_ref=w_ref, wb=wb):
            wb[ord_ref[s] % FFN_SLOTS] = w_ref[0, 0].astype(BF16)
            ord_ref[s] = ord_ref[s] + 1

    @pl.when(jnp.logical_and(i >= 0, i < nt))
    def _():
        ic = jnp.maximum(i, 0)

        @pl.when(jnp.logical_and(i > 0, starts_group(i)))
        def _():
            ord_ref[3] = ord_ref[3] + 1

        slot = ord_ref[3] % FFN_SLOTS
        c = xs_ref.shape[0] // tm
        live = _iota2((tm, LANES), 0) < nv_ref[ic]
        parts = [_unpack_bf16_pairs(jnp.where(live, p, jnp.uint32(0)))
                 for p in _load_token_major(xs_ref, 0, tm, c)]
        lo = jnp.concatenate([p[0] for p in parts], axis=1)
        hi = jnp.concatenate([p[1] for p in parts], axis=1)
        half = lo.shape[1]
        h1 = (jnp.dot(lo, w1b[slot, :half, :], preferred_element_type=F32)
              + jnp.dot(hi, w1b[slot, half:, :], preferred_element_type=F32))
        h3 = (jnp.dot(lo, w3b[slot, :half, :], preferred_element_type=F32)
              + jnp.dot(hi, w3b[slot, half:, :], preferred_element_type=F32))
        h = (_silu(h1) * h3).astype(BF16)
        y = jnp.dot(h, w2b[slot], preferred_element_type=F32)
        _store_token_major(o_ref, _pack_bf16_pairs(y))

    @pl.when(i >= nt)
    def _():
        o_ref[...] = jnp.zeros_like(o_ref)


def _expert_ffn(te, nt, nv, xs, w1, w3, w2, layer, nrows, tm):
    d, f = w1.shape[2], w1.shape[3]
    c = xs.shape[0] // nrows
    ntile = nrows // tm

    def tile(j, te_r, nt_r, nv_r):
        return (jnp.clip(j - FFN_LOOKAHEAD, 0, nt_r[0] - 1), 0)

    def out_tile(j, te_r, nt_r, nv_r):
        return (jnp.maximum(j - FFN_LOOKAHEAD, 0), 0)

    def expert(delay):
        def index_map(j, te_r, nt_r, nv_r):
            return (layer, te_r[jnp.clip(j - delay, 0, nt_r[0] - 1)], 0, 0)
        return index_map

    assert FFN_LOOKAHEAD == 2
    grid_spec = pltpu.PrefetchScalarGridSpec(
        num_scalar_prefetch=3,
        grid=(ntile + FFN_LOOKAHEAD,),
        in_specs=[pl.BlockSpec((tm * c, LANES), tile),
                  pl.BlockSpec((1, 1, d, f), expert(0)),
                  pl.BlockSpec((1, 1, d, f), expert(1)),
                  pl.BlockSpec((1, 1, f, d), expert(2))],
        out_specs=pl.BlockSpec((tm * c, LANES), out_tile),
        scratch_shapes=[pltpu.VMEM((FFN_SLOTS, d, f), BF16), pltpu.VMEM((FFN_SLOTS, d, f), BF16),
                        pltpu.VMEM((FFN_SLOTS, f, d), BF16), pltpu.SMEM((4,), jnp.int32)],
    )
    return pl.pallas_call(
        functools.partial(_ffn_kernel, tm=tm),
        grid_spec=grid_spec,
        out_shape=jax.ShapeDtypeStruct((nrows * c, LANES), jnp.uint32),
        compiler_params=_cparams(("arbitrary",)),
        name="expert_ffn",
    )(te, nt, nv, xs, w1, w3, w2)


def _shared_expert_kernel(x_ref, ws1_ref, ws3_ref, ws2_ref, o_ref):
    xb = x_ref[...].astype(BF16)
    s1 = jnp.dot(xb, ws1_ref[...], preferred_element_type=F32)
    s3 = jnp.dot(xb, ws3_ref[...], preferred_element_type=F32)
    o_ref[...] = jnp.dot((_silu(s1) * s3).astype(BF16), ws2_ref[...], preferred_element_type=F32)


def _shared_expert(x1, ws1, ws3, ws2, tm=512):
    n, d = x1.shape
    sf = ws1.shape[1]
    return pl.pallas_call(
        _shared_expert_kernel,
        grid=(n // tm,),
        in_specs=[pl.BlockSpec((tm, d), lambda i: (i, 0)),
                  pl.BlockSpec((d, sf), lambda i: (0, 0)),
                  pl.BlockSpec((d, sf), lambda i: (0, 0)),
                  pl.BlockSpec((sf, d), lambda i: (0, 0))],
        out_specs=pl.BlockSpec((tm, d), lambda i: (i, 0)),
        out_shape=jax.ShapeDtypeStruct((n, d), F32),
        compiler_params=_cparams(("parallel",)),
        name="shared_expert",
    )(x1, ws1, ws3, ws2)


def _combine_kernel(*refs, tc):
    y_refs = refs[:TOP_K]
    wts_ref, x_ref, sh_ref, g_ref, b_ref, o_ref, ob_ref = refs[TOP_K:]
    c = y_refs[0].shape[0] // tc
    acc = sh_ref[...]
    wts = wts_ref[...]
    for k in range(TOP_K):
        parts = [_unpack_bf16_pairs(p) for p in _load_token_major(y_refs[k], 0, tc, c)]
        yk = jnp.concatenate([p[0] for p in parts] + [p[1] for p in parts], axis=1).astype(F32)
        acc = acc + wts[:, k:k + 1] * yk
    y = _layernorm_rows(DN_ALPHA * x_ref[...] + acc, g_ref[...], b_ref[...])
    o_ref[...] = y
    ob_ref[...] = y.astype(BF16)


def _combine(yg, wts, x1, shared, g, b, tc=256):
    n, d = x1.shape
    c = yg.shape[0] // (TOP_K * n)
    nblk = n // tc

    def slot_spec(k):
        return pl.BlockSpec((tc * c, LANES), lambda i: (k * nblk + i, 0))

    rows = pl.BlockSpec((tc, d), lambda i: (i, 0))
    return pl.pallas_call(
        functools.partial(_combine_kernel, tc=tc),
        grid=(nblk,),
        in_specs=[slot_spec(k) for k in range(TOP_K)] + [
            pl.BlockSpec((tc, LANES), lambda i: (i, 0)), rows, rows,
            pl.BlockSpec((1, d), lambda i: (0, 0)),
            pl.BlockSpec((1, d), lambda i: (0, 0))],
        out_specs=[rows, rows],
        out_shape=[jax.ShapeDtypeStruct((n, d), F32), jax.ShapeDtypeStruct((n, d), BF16)],
        compiler_params=_cparams(("parallel",)),
        name="moe_combine",
    )(*([yg] * TOP_K), wts, x1, shared, g.reshape(1, d), b.reshape(1, d))


def _cast_kernel(x_ref, o_ref):
    o_ref[...] = x_ref[...].astype(o_ref.dtype)


def _to_bf16(x, tm=1024):
    n, d = x.shape
    return pl.pallas_call(
        _cast_kernel,
        grid=(n // tm,),
        in_specs=[pl.BlockSpec((tm, d), lambda i: (i, 0))],
        out_specs=pl.BlockSpec((tm, d), lambda i: (i, 0)),
        out_shape=jax.ShapeDtypeStruct((n, d), BF16),
        compiler_params=_cparams(("parallel",)),
        name="cast_bf16",
    )(x)


FFN_TILE = 256


def _moe(x1, x1p, w_router, router_bias, w1, w3, w2, layer, ws1, ws3, ws2, g, b):
    n = x1.shape[0]
    tm = FFN_TILE
    nrows = n * TOP_K + N_EXPERTS * tm
    idx, wts, rank, cnt = _router(x1, w_router, router_bias)
    pos_t = _positions(idx, rank, cnt, tm)
    te, nt, nv = _tile_map(cnt, tm, nrows // tm)
    c = x1p.shape[0] // n
    pos_s = pos_t.reshape(TOP_K, n // SC_SCATTER_ROWS, SC_SCATTER_ROWS).transpose(1, 0, 2)
    xs = _sc_scatter_rows(x1p.reshape(n, c, LANES), pos_s, nrows)
    shared = _shared_expert(x1, ws1.astype(BF16), ws3.astype(BF16), ws2.astype(BF16))
    ys = _expert_ffn(te, nt, nv, xs.reshape(nrows * c, LANES), w1, w3, w2, layer, nrows, tm)
    yg = _sc_gather_rows(ys.reshape(nrows, c, LANES), pos_t.reshape(-1, SC_GATHER_ROWS))
    return _combine(yg.reshape(TOP_K * n * c, LANES), wts, x1, shared, g, b)


def kernel(x, positions, w_in, gla_wa2, gla_ba, gla_norm, mlstm_conv_w, mlstm_conv_b, mlstm_bi, mlstm_bf,
           mlstm_norm, sgu_ln_g, sgu_ln_b, sgu_ws, sgu_bs, w_pa, w_pb, w_pc, w_pd, w_out, ln1_g, ln1_b,
           w_router, router_bias, w1, w3, w2, ws1, ws3, ws2, ln2_g, ln2_b):
    nbatch, seq, d = x.shape
    n = nbatch * seq
    xf = x.reshape(n, d)
    xb = _to_bf16(xf)
    cos_t, sin_t = _rope_tables(positions)
    for l in range(DEPTH):
        wl = w_in[l]
        w_a = wl[:, _OFF_A:_OFF_A + _W_A].astype(BF16)
        w_b = wl[:, _OFF_B:_OFF_B + _W_B].astype(BF16)
        w_c = wl[:, _OFF_C:_OFF_C + _W_C].astype(BF16)
        w_d = wl[:, _OFF_D:_OFF_D + _W_D].astype(BF16)
        w_g = wl[:, _OFF_G:_OFF_G + _W_G].astype(BF16)
        w_s = jnp.concatenate(
            [wl[:, _OFF_LOW:_OFF_LOW + GLA_RANK], wl[:, _OFF_IF:_OFF_IF + 2 * MLSTM_HEADS],
             jnp.zeros((d, LANES - GLA_RANK - 2 * MLSTM_HEADS), F32)], axis=1).astype(BF16)
        y_a = _matmul(xb, w_a, 2048, 768)
        y_b = _matmul(xb, w_b, 2048, 768)
        y_c = _matmul(xb, w_c, 2048, 768)
        y_d = _matmul(xb, w_d, 2048, 1024)
        y_s = _matmul(xb, w_s, 2048, LANES)
        sm3 = y_s.reshape(nbatch, seq, LANES)
        o_a, o_d, o_c = _mixers_acd(
            y_a.reshape(nbatch, seq, _W_A), y_d.reshape(nbatch, seq, _W_D), sm3, y_c, gla_wa2[l], gla_ba[l],
            gla_norm[l], mlstm_conv_w[l], mlstm_conv_b[l], mlstm_bi[l], mlstm_bf[l], mlstm_norm[l],
            sgu_ln_g[l], sgu_ln_b[l], sgu_ws[l], sgu_bs[l])
        o_a = o_a.reshape(n, GLA_V)
        o_d = o_d.reshape(n, ML_W)
        o_b = _dilated(y_b, cos_t, sin_t, nbatch)
        merged = _merge(xb, w_g, (o_a, o_b, o_c, o_d),
                        (w_pa[l].astype(BF16), w_pb[l].astype(BF16), w_pc[l].astype(BF16), w_pd[l].astype(BF16)))
        x1, x1p = _outproj_ln(merged, w_out[l].astype(BF16), xf, ln1_g[l], ln1_b[l])
        xf, xb = _moe(x1, x1p, w_router[l], router_bias[l], w1, w3, w2, l, ws1[l], ws3[l], ws2[l],
                      ln2_g[l], ln2_b[l])
    return xf.reshape(nbatch, seq, d)
```

```python
import functools
import math

import jax
import jax.numpy as jnp
from jax import lax
from jax.experimental import pallas as pl
from jax.experimental.pallas import tpu as pltpu
from jax.experimental.pallas import tpu_sc as plsc

D_MODEL = 2048
DEPTH = 2

GLA_HEADS = 4
GLA_DK = 64
GLA_DV = 128
GLA_RANK = 16
GLA_TAU = 16.0
GLA_CHUNK = 64
GLA_QK = GLA_HEADS * GLA_DK
GLA_V = GLA_HEADS * GLA_DV

DIL_PAIRS = ((128, 1), (512, 4), (2048, 16))
DIL_HEADS_PER_GROUP = 4
DIL_HEAD_DIM = 64
DIL_HEADS = len(DIL_PAIRS) * DIL_HEADS_PER_GROUP
DIL_W = DIL_HEADS * DIL_HEAD_DIM
DIL_OUT = DIL_HEADS_PER_GROUP * DIL_HEAD_DIM
DIL_BLOCK = 128
ROPE_THETA = 10000.0

SGU_CHUNK = 128
SGU_GROUPS = 6
SGU_GROUP_CH = 128
SGU_W = SGU_GROUPS * SGU_GROUP_CH

MLSTM_HEADS = 4
MLSTM_HEAD_DIM = 128
MLSTM_CHUNK = 64
MLSTM_CONV = 4
ML_W = MLSTM_HEADS * MLSTM_HEAD_DIM

N_EXPERTS = 64
TOP_K = 8
N_GROUPS = 8
TOPK_GROUPS = 4
EXPERT_FF = 512
SHARED_FF = 512
ROUTED_SCALE = 2.5

N_BRANCH = 4
DN_ALPHA = (2 * DEPTH) ** 0.25
LN_EPS = 1e-5

_OFF_A = 0
_W_A = 2 * GLA_QK + 2 * GLA_V
_OFF_LOW = _OFF_A + _W_A
_OFF_B = _OFF_LOW + GLA_RANK
_W_B = 3 * DIL_W
_OFF_C = _OFF_B + _W_B
_W_C = 2 * SGU_W
_OFF_D = _OFF_C + _W_C
_W_D = 4 * ML_W
_OFF_IF = _OFF_D + _W_D
_OFF_G = _OFF_IF + 2 * MLSTM_HEADS
_W_G = N_BRANCH * D_MODEL

LANES = 128
VMEM_LIMIT = 56 * 1024 * 1024

_SM_LOW = 0
_SM_I = GLA_RANK
_SM_F = GLA_RANK + MLSTM_HEADS

HI = lax.Precision.HIGHEST
F32 = jnp.float32
BF16 = jnp.bfloat16
NEG_INF = float("-inf")


def _cparams(sem):
    return pltpu.CompilerParams(dimension_semantics=sem, vmem_limit_bytes=VMEM_LIMIT)


def _log_sigmoid(x):
    return jnp.minimum(x, 0.0) - jnp.log1p(jnp.exp(-jnp.abs(x)))


def _sigmoid(x):
    return 1.0 / (1.0 + jnp.exp(-x))


def _silu(x):
    return x * _sigmoid(x)


def _iota2(shape, dim):
    return lax.broadcasted_iota(jnp.int32, shape, dim)


def _col_to_row(col, n):
    eye = _iota2((n, n), 0) == _iota2((n, n), 1)
    return jnp.sum(jnp.where(eye, col, 0.0), axis=0, keepdims=True)


def _row_to_col(row, n):
    eye = _iota2((n, n), 0) == _iota2((n, n), 1)
    return jnp.sum(jnp.where(eye, row, 0.0), axis=1, keepdims=True)


def _mm_kernel(x_ref, w_ref, o_ref):
    o_ref[...] = jnp.dot(x_ref[...], w_ref[...], preferred_element_type=F32).astype(o_ref.dtype)


def _matmul(x, w, tm, tn, out_dtype=F32):
    n, k = x.shape
    m = w.shape[1]
    return pl.pallas_call(
        _mm_kernel,
        grid=(n // tm, m // tn),
        in_specs=[pl.BlockSpec((tm, k), lambda i, j: (i, 0)),
                  pl.BlockSpec((k, tn), lambda i, j: (0, j))],
        out_specs=pl.BlockSpec((tm, tn), lambda i, j: (i, j)),
        out_shape=jax.ShapeDtypeStruct((n, m), out_dtype),
        compiler_params=_cparams(("parallel", "arbitrary")),
        name="in_proj",
    )(x, w)


def _gla_chunk(y_ref, sm_ref, wa2_ref, ba_ref, g_ref, o_ref, state_ref, *, nb):
    L, H, DK, DV = GLA_CHUNK, GLA_HEADS, GLA_DK, GLA_DV
    tril = (_iota2((L, L), 0) >= _iota2((L, L), 1))
    tril_f = tril.astype(F32)
    for b in range(nb):
        y = y_ref[b]
        a_low = sm_ref[b][:, _SM_LOW:_SM_LOW + GLA_RANK]
        glog = jnp.dot(a_low, wa2_ref[...], preferred_element_type=F32) + ba_ref[...]
        g = _log_sigmoid(glog) * (1.0 / GLA_TAU)
        bc = jnp.dot(tril_f, g, precision=HI, preferred_element_type=F32)
        outs = []
        for h in range(H):
            q = y[:, h * DK:(h + 1) * DK] * (DK ** -0.5)
            k = y[:, GLA_QK + h * DK:GLA_QK + (h + 1) * DK]
            v = y[:, 2 * GLA_QK + h * DV:2 * GLA_QK + (h + 1) * DV]
            bh = bc[:, h * DK:(h + 1) * DK]
            qe = q * jnp.exp(bh)
            ke = k * jnp.exp(-bh)
            att = lax.dot_general(qe, ke, (((1,), (1,)), ((), ())), preferred_element_type=F32)
            att = jnp.where(tril, att, 0.0)
            st = state_ref[b * H + h]
            o = (jnp.dot(att, v, preferred_element_type=F32)
                 + jnp.dot(qe, st, preferred_element_type=F32))
            b_last = bh[L - 1:L, :]
            kd = k * jnp.exp(b_last - bh)
            decay = _row_to_col(jnp.exp(b_last), DK)
            state_ref[b * H + h] = decay * st + lax.dot_general(
                kd, v, (((0,), (0,)), ((), ())), preferred_element_type=F32)
            o = o * lax.rsqrt(jnp.mean(o * o, axis=-1, keepdims=True) + LN_EPS)
            outs.append(o)
        o_all = jnp.concatenate(outs, axis=-1) * g_ref[...]
        r = y[:, 2 * GLA_QK + GLA_V:2 * GLA_QK + 2 * GLA_V]
        o_ref[b] = (o_all * _silu(r)).astype(o_ref.dtype)


MLSTM_HALO = 8


def _mlstm_chunk(y_ref, sm_ref, cw_ref, cb_ref, gb_ref, g_ref, o_ref, c_ref, n_ref, m_ref, tail_ref, *, nb):
    L, H, DH = MLSTM_CHUNK, MLSTM_HEADS, MLSTM_HEAD_DIM
    W2 = 2 * ML_W
    HALO = MLSTM_HALO
    tril = (_iota2((L, L), 0) >= _iota2((L, L), 1))
    tril_f = tril.astype(F32)
    for b in range(nb):
        y = y_ref[b]
        qk_raw = y[:, :W2]
        ext = jnp.concatenate([tail_ref[b], qk_raw], axis=0)
        tail_ref[b] = qk_raw[L - HALO:, :]
        conv = cb_ref[...]
        for j in range(MLSTM_CONV):
            s0 = HALO - (MLSTM_CONV - 1) + j
            conv = conv + cw_ref[j:j + 1, :] * ext[s0:s0 + L, :]
        qk = _silu(conv)
        gates = sm_ref[b] + gb_ref[...]
        bcum = jnp.dot(tril_f, _log_sigmoid(gates), precision=HI, preferred_element_type=F32)
        outs = []
        for h in range(H):
            q = qk[:, h * DH:(h + 1) * DH]
            k = qk[:, ML_W + h * DH:ML_W + (h + 1) * DH] * (DH ** -0.5)
            v = y[:, W2 + h * DH:W2 + (h + 1) * DH]
            b_col = bcum[:, _SM_F + h:_SM_F + h + 1]
            li_col = gates[:, _SM_I + h:_SM_I + h + 1]
            b_row = _col_to_row(b_col, L)
            li_row = _col_to_row(li_col, L)
            m_prev = m_ref[b * H + h][:, 0:1]
            dmat = jnp.where(tril, b_col - b_row + li_row, NEG_INF)
            inter = b_col + m_prev
            m_t = jnp.maximum(inter, jnp.max(dmat, axis=-1, keepdims=True))
            w_in = jnp.exp(dmat - m_t)
            w_st = jnp.exp(inter - m_t)
            sc = lax.dot_general(q, k, (((1,), (1,)), ((), ())), preferred_element_type=F32) * w_in
            cst = c_ref[b * H + h]
            nst = n_ref[b * H + h]
            num = (jnp.dot(sc, v, preferred_element_type=F32)
                   + w_st * jnp.dot(q, cst, preferred_element_type=F32))
            den = jnp.sum(sc, axis=-1, keepdims=True) + w_st * jnp.sum(q * nst, axis=-1, keepdims=True)
            hh = num / jnp.maximum(jnp.abs(den), jnp.exp(-m_t))
            b_last = b_col[L - 1:L, :]
            dec = b_last - b_col + li_col
            m_new = jnp.maximum(b_last + m_prev, jnp.max(dec, axis=0, keepdims=True))
            wk = jnp.exp(dec - m_new)
            keep = jnp.exp(b_last + m_prev - m_new)
            wkk = wk * k
            c_ref[b * H + h] = keep * cst + lax.dot_general(
                wkk, v, (((0,), (0,)), ((), ())), preferred_element_type=F32)
            n_ref[b * H + h] = keep * nst + jnp.sum(wkk, axis=0, keepdims=True)
            m_ref[b * H + h] = jnp.broadcast_to(m_new, (1, LANES))
            o_pre = y[:, W2 + ML_W + h * DH:W2 + ML_W + (h + 1) * DH]
            hh = _sigmoid(o_pre) * hh
            hh = hh * lax.rsqrt(jnp.mean(hh * hh, axis=-1, keepdims=True) + LN_EPS)
            outs.append(hh)
        o_ref[b] = (jnp.concatenate(outs, axis=-1) * g_ref[...]).astype(o_ref.dtype)


def _mixers_kernel(ya_ref, yd_ref, sm_ref, yc_ref, wa2_ref, ba_ref, ga_ref, cw_ref, cb_ref, gb_ref, gd_ref,
                   lg_ref, lb_ref, ws_ref, bst_ref, oa_ref, od_ref, oc_ref,
                   state_ref, c_ref, n_ref, m_ref, tail_ref, *, nb, sgu_chunks):
    @pl.when(pl.program_id(0) == 0)
    def _():
        for ref in (state_ref, c_ref, n_ref, m_ref, tail_ref):
            ref[...] = jnp.zeros_like(ref)

    _gla_chunk(ya_ref, sm_ref, wa2_ref, ba_ref, ga_ref, oa_ref, state_ref, nb=nb)
    _mlstm_chunk(yd_ref, sm_ref, cw_ref, cb_ref, gb_ref, gd_ref, od_ref, c_ref, n_ref, m_ref, tail_ref, nb=nb)
    _sgu_kernel(yc_ref, lg_ref, lb_ref, ws_ref, bst_ref, oc_ref, nchunk=sgu_chunks)


def _mixers_acd(ya3, yd3, sm3, yc, wa2, ba, gla_norm, conv_w, conv_b, b_i, b_f, mlstm_norm,
                sgu_ln_g, sgu_ln_b, sgu_ws, sgu_bs):
    nb, s, _ = ya3.shape
    assert GLA_CHUNK == MLSTM_CHUNK
    L = GLA_CHUNK
    nstep = s // L
    n = yc.shape[0]
    sgu_rows = n // nstep
    assert sgu_rows % SGU_CHUNK == 0 and sgu_rows * nstep == n
    gate_bias = jnp.zeros((1, LANES), F32)
    gate_bias = gate_bias.at[0, _SM_I:_SM_I + MLSTM_HEADS].set(b_i).at[0, _SM_F:_SM_F + MLSTM_HEADS].set(b_f)
    bst = jnp.zeros((SGU_CHUNK, LANES), F32).at[:, :SGU_GROUPS].set(sgu_bs.T)

    def chunk(width):
        return pl.BlockSpec((nb, L, width), lambda i: (0, i, 0))

    def whole(*shape):
        return pl.BlockSpec(shape, lambda i: (0,) * len(shape))

    return pl.pallas_call(
        functools.partial(_mixers_kernel, nb=nb, sgu_chunks=sgu_rows // SGU_CHUNK),
        grid=(nstep,),
        in_specs=[chunk(_W_A), chunk(_W_D), chunk(LANES), pl.BlockSpec((sgu_rows, _W_C), lambda i: (i, 0)),
                  whole(GLA_RANK, GLA_QK), whole(1, GLA_QK), whole(1, GLA_V),
                  whole(MLSTM_CONV, 2 * ML_W), whole(1, 2 * ML_W), whole(1, LANES), whole(1, ML_W),
                  whole(1, SGU_W), whole(1, SGU_W), whole(SGU_GROUPS, SGU_CHUNK, SGU_CHUNK),
                  whole(SGU_CHUNK, LANES)],
        out_specs=[chunk(GLA_V), chunk(ML_W), pl.BlockSpec((sgu_rows, SGU_W), lambda i: (i, 0))],
        out_shape=[jax.ShapeDtypeStruct((nb, s, GLA_V), BF16), jax.ShapeDtypeStruct((nb, s, ML_W), BF16),
                   jax.ShapeDtypeStruct((n, SGU_W), BF16)],
        scratch_shapes=[pltpu.VMEM((nb * GLA_HEADS, GLA_DK, GLA_DV), F32),
                        pltpu.VMEM((nb * MLSTM_HEADS, MLSTM_HEAD_DIM, MLSTM_HEAD_DIM), F32),
                        pltpu.VMEM((nb * MLSTM_HEADS, 1, MLSTM_HEAD_DIM), F32),
                        pltpu.VMEM((nb * MLSTM_HEADS, 1, LANES), F32),
                        pltpu.VMEM((nb, MLSTM_HALO, 2 * ML_W), F32)],
        compiler_params=_cparams(("arbitrary",)),
        name="gla_mlstm_sgu",
    )(ya3, yd3, sm3, yc, wa2, ba.reshape(1, GLA_QK), gla_norm.reshape(1, GLA_V),
      conv_w, conv_b.reshape(1, 2 * ML_W), gate_bias, mlstm_norm.reshape(1, ML_W),
      sgu_ln_g.reshape(1, SGU_W), sgu_ln_b.reshape(1, SGU_W), sgu_ws, bst)


def _gelu(x):
    return 0.5 * x * (1.0 + lax.erf(x * (0.5 ** 0.5)))


def _sgu_kernel(y_ref, lg_ref, lb_ref, ws_ref, bst_ref, o_ref, *, nchunk):
    C, G, GC = SGU_CHUNK, SGU_GROUPS, SGU_GROUP_CH
    y = y_ref[...]
    zu = _gelu(y[:, :SGU_W])
    zv = _gelu(y[:, SGU_W:])
    mu = jnp.mean(zv, axis=-1, keepdims=True)
    var = jnp.mean(jnp.square(zv - mu), axis=-1, keepdims=True)
    vn = (zv - mu) * lax.rsqrt(var + LN_EPS) * lg_ref[...] + lb_ref[...]
    tril = _iota2((C, C), 0) >= _iota2((C, C), 1)
    for g in range(G):
        wc = jnp.where(tril, ws_ref[g], 0.0)
        bias = bst_ref[:, g:g + 1]
        for c in range(nchunk):
            rows = slice(c * C, (c + 1) * C)
            cols = slice(g * GC, (g + 1) * GC)
            s = jnp.dot(wc, vn[rows, cols], preferred_element_type=F32) + bias
            o_ref[rows, cols] = (zu[rows, cols] * s).astype(o_ref.dtype)


def _rope_table_kernel(pos_ref, inv_ref, cos_ref, sin_ref):
    ang = pos_ref[...].astype(F32) * inv_ref[...]
    half = DIL_HEAD_DIM // 2
    sign = jnp.where((_iota2(ang.shape, 1) % DIL_HEAD_DIM) < half, -1.0, 1.0)
    cos_ref[...] = jnp.cos(ang)
    sin_ref[...] = jnp.sin(ang) * sign


def _rope_tables(positions):
    n = positions.size
    half = DIL_HEAD_DIM // 2
    inv = ROPE_THETA ** (-jnp.arange(half, dtype=F32) * 2.0 / DIL_HEAD_DIM)
    inv = jnp.tile(inv, LANES // half).reshape(1, LANES)
    t = 1024
    return pl.pallas_call(
        _rope_table_kernel,
        grid=(n // t,),
        in_specs=[pl.BlockSpec((t, 1), lambda i: (i, 0)),
                  pl.BlockSpec((1, LANES), lambda i: (0, 0))],
        out_specs=[pl.BlockSpec((t, LANES), lambda i: (i, 0)),
                   pl.BlockSpec((t, LANES), lambda i: (i, 0))],
        out_shape=[jax.ShapeDtypeStruct((n, LANES), F32)] * 2,
        compiler_params=_cparams(("parallel",)),
        name="rope_tables",
    )(positions.reshape(n, 1), inv)


def _dil_kernel(q0, q1, q2, k0, k1, k2, v0, v1, v2, cos_ref, sin_ref, o_ref,
                qs_ref, ks_ref, num_ref, m_ref, den_ref, *, seq):
    DH, BLK = DIL_HEAD_DIM, DIL_BLOCK
    half = DH // 2
    q_refs, k_refs, v_refs = (q0, q1, q2), (k0, k1, k2), (v0, v1, v2)
    cos = cos_ref[...]
    sin = sin_ref[...]
    first_half = (_iota2((seq, LANES), 1) % DH) < half

    def rope(x):
        swapped = jnp.where(first_half, pltpu.roll(x, LANES - half, 1), pltpu.roll(x, half, 1))
        return x * cos + swapped * sin

    for g in range(len(DIL_PAIRS)):
        qs_ref[g] = rope(q_refs[g][...]) * (DH ** -0.5)
        ks_ref[g] = rope(k_refs[g][...])

    ii = _iota2((BLK, BLK), 0)
    jj = _iota2((BLK, BLK), 1)
    i2 = _iota2((2 * BLK, BLK), 0) % BLK
    j2 = _iota2((2 * BLK, BLK), 1)
    mask_cur2 = j2 <= i2
    mask_prev2 = j2 >= i2
    assert LANES == 2 * DH and BLK == LANES
    head_lanes = [(jj // DH) == h for h in range(LANES // DH)]
    head_ones = [hl.astype(F32) for hl in head_lanes]

    for g, (window, dil) in enumerate(DIL_PAIRS):
        lsub = seq // dil
        nblk = lsub // BLK
        assert window // dil == BLK and lsub % BLK == 0
        v_ref = v_refs[g]

        def unit(u, g=g, dil=dil, v_ref=v_ref):
            r = u % dil
            n = u // dil
            rows = pl.ds(n * (BLK * dil) + r, BLK, stride=dil)
            qb = qs_ref[g, rows, :]
            kc = ks_ref[g, rows, :]
            vc = v_ref[rows, :]
            nblk = 2 if n > 0 else 1
            if nblk > 1:
                prow = pl.ds((n - 1) * (BLK * dil) + r, BLK, stride=dil)
                kp = ks_ref[g, prow, :]
                vp = v_ref[prow, :]
                has_prev = True
            nh = LANES // DH
            q2 = jnp.concatenate([jnp.where(head_lanes[h], qb, 0.0) for h in range(nh)], axis=0)
            s_c = lax.dot_general(q2, kc, (((1,), (1,)), ((), ())), preferred_element_type=F32)
            s_c = jnp.where(mask_cur2, s_c, NEG_INF)
            if nblk > 1:
                s_p = lax.dot_general(q2, kp, (((1,), (1,)), ((), ())), preferred_element_type=F32)
                s_p = jnp.where(mask_prev2, s_p, NEG_INF)
                mx = jnp.max(jnp.maximum(s_c, s_p), axis=-1, keepdims=True)
            else:
                mx = jnp.max(s_c, axis=-1, keepdims=True)
            p_c = jnp.exp(s_c - mx)
            if nblk > 1:
                p_p = jnp.exp(s_p - mx)
            num = jnp.zeros((BLK, LANES), F32)
            den = jnp.zeros((BLK, LANES), F32)
            for h in range(nh):
                hr = slice(h * BLK, (h + 1) * BLK)
                num = num + jnp.dot(p_c[hr], jnp.where(head_lanes[h], vc, 0.0), preferred_element_type=F32)
                den = den + jnp.dot(p_c[hr], head_ones[h], preferred_element_type=F32)
                if nblk > 1:
                    num = num + jnp.dot(p_p[hr], jnp.where(head_lanes[h], vp, 0.0), preferred_element_type=F32)
                    den = den + jnp.dot(p_p[hr], head_ones[h], preferred_element_type=F32)
            num_ref[g, rows, :] = num
            m_ref[g, rows, :] = jnp.where(head_lanes[0], mx[:BLK], mx[BLK:])
            den_ref[g, rows, :] = den

        for u in range(dil * nblk):
            unit(u)

    m_all = jnp.maximum(jnp.maximum(m_ref[0], m_ref[1]), m_ref[2])
    num = jnp.zeros((seq, LANES), F32)
    den = jnp.zeros((seq, LANES), F32)
    for g in range(len(DIL_PAIRS)):
        e = jnp.exp(m_ref[g] - m_all)
        num = num + e * num_ref[g]
        den = den + e * den_ref[g]
    o_ref[...] = (num / den).astype(o_ref.dtype)


def _dilated(yb, cos_t, sin_t, nbatch):
    n = yb.shape[0]
    seq = n // nbatch
    npair = DIL_HEADS_PER_GROUP * DIL_HEAD_DIM // LANES
    nblk_cols = DIL_W // LANES

    def spec(section, g):
        return pl.BlockSpec((seq, LANES), lambda b, p: (b, section * nblk_cols + g * npair + p))

    in_specs = ([spec(0, g) for g in range(3)] + [spec(1, g) for g in range(3)] + [spec(2, g) for g in range(3)]
                + [pl.BlockSpec((seq, LANES), lambda b, p: (b, 0))] * 2)
    return pl.pallas_call(
        functools.partial(_dil_kernel, seq=seq),
        grid=(nbatch, npair),
        in_specs=in_specs,
        out_specs=pl.BlockSpec((seq, LANES), lambda b, p: (b, p)),
        out_shape=jax.ShapeDtypeStruct((n, DIL_OUT), BF16),
        scratch_shapes=[pltpu.VMEM((3, seq, LANES), F32)] * 5,
        compiler_params=_cparams(("parallel", "parallel")),
        name="dilated_attn",
    )(*([yb] * 9), cos_t, sin_t)


def _merge_kernel(x_ref, g0, g1, g2, g3, ya, yb, yc, yd, pa, pb, pc, pd, o_ref):
    x = x_ref[...]
    acc = None
    for wg, y, p in ((g0, ya, pa), (g1, yb, pb), (g2, yc, pc), (g3, yd, pd)):
        gate = _sigmoid(jnp.dot(x, wg[...], preferred_element_type=F32))
        term = gate * jnp.dot(y[...], p[...], preferred_element_type=F32)
        acc = term if acc is None else acc + term
    o_ref[...] = acc.astype(o_ref.dtype)


def _merge(xb, wg, ys, ps, tm=1024, tn=512):
    n, d = xb.shape
    ncol = d // tn

    def gate_spec(br):
        return pl.BlockSpec((d, tn), lambda i, j: (0, br * ncol + j))

    in_specs = ([pl.BlockSpec((tm, d), lambda i, j: (i, 0))]
                + [gate_spec(br) for br in range(N_BRANCH)]
                + [pl.BlockSpec((tm, y.shape[1]), lambda i, j: (i, 0)) for y in ys]
                + [pl.BlockSpec((p.shape[0], tn), lambda i, j: (0, j)) for p in ps])
    return pl.pallas_call(
        _merge_kernel,
        grid=(n // tm, ncol),
        in_specs=in_specs,
        out_specs=pl.BlockSpec((tm, tn), lambda i, j: (i, j)),
        out_shape=jax.ShapeDtypeStruct((n, d), BF16),
        compiler_params=_cparams(("parallel", "arbitrary")),
        name="gated_merge",
    )(xb, wg, wg, wg, wg, *ys, *ps)


def _layernorm_rows(z, g, b):
    mu = jnp.mean(z, axis=-1, keepdims=True)
    var = jnp.mean(jnp.square(z - mu), axis=-1, keepdims=True)
    return (z - mu) * lax.rsqrt(var + LN_EPS) * g + b


def _pack_bf16_pairs(y):
    half = y.shape[1] // 2
    lo = lax.bitcast_convert_type(y[:, :half].astype(BF16).astype(F32), jnp.uint32)
    hi = lax.bitcast_convert_type(y[:, half:].astype(BF16).astype(F32), jnp.uint32)
    return (lo >> 16) | (hi & jnp.uint32(0xFFFF0000))


def _unpack_bf16_pairs(w):
    lo = lax.bitcast_convert_type(w << 16, F32).astype(BF16)
    hi = lax.bitcast_convert_type(w & jnp.uint32(0xFFFF0000), F32).astype(BF16)
    return lo, hi


def _store_token_major(ref, val):
    t, w = val.shape
    c = w // LANES
    for s in range(c):
        ref[pl.ds(s, t, stride=c), :] = val[:, s * LANES:(s + 1) * LANES]


def _load_token_major(ref, start, t, c):
    return [ref[pl.ds(start + s, t, stride=c), :] for s in range(c)]


def _outproj_ln_kernel(m_ref, w_ref, x_ref, g_ref, b_ref, o_ref, p_ref):
    h = jnp.dot(m_ref[...], w_ref[...], preferred_element_type=F32)
    y = _layernorm_rows(DN_ALPHA * x_ref[...] + h, g_ref[...], b_ref[...])
    o_ref[...] = y
    _store_token_major(p_ref, _pack_bf16_pairs(y))


def _outproj_ln(merged, w_out, x, g, b, tm=512):
    n, d = x.shape
    c = d // 2 // LANES
    return pl.pallas_call(
        _outproj_ln_kernel,
        grid=(n // tm,),
        in_specs=[pl.BlockSpec((tm, d), lambda i: (i, 0)),
                  pl.BlockSpec((d, d), lambda i: (0, 0)),
                  pl.BlockSpec((tm, d), lambda i: (i, 0)),
                  pl.BlockSpec((1, d), lambda i: (0, 0)),
                  pl.BlockSpec((1, d), lambda i: (0, 0))],
        out_specs=[pl.BlockSpec((tm, d), lambda i: (i, 0)),
                   pl.BlockSpec((tm * c, LANES), lambda i: (i, 0))],
        out_shape=[jax.ShapeDtypeStruct((n, d), F32), jax.ShapeDtypeStruct((n * c, LANES), jnp.uint32)],
        compiler_params=_cparams(("parallel",)),
        name="outproj_ln",
    )(merged, w_out, x, g.reshape(1, d), b.reshape(1, d))


def _router_kernel(x_ref, wr_ref, rb_ref, idx_ref, wts_ref, rank_ref, cnt_ref, carry_ref, *, tr):
    E, G = N_EXPERTS, N_GROUPS
    gsz = E // G

    @pl.when(pl.program_id(0) == 0)
    def _():
        carry_ref[...] = jnp.zeros_like(carry_ref)

    logits = jnp.dot(x_ref[...], wr_ref[...], precision=HI, preferred_element_type=F32)
    scores = _sigmoid(logits.T[:E, :])
    choice = scores + rb_ref[...]
    eidx = _iota2((E, tr), 0)

    c3 = choice.reshape(G, gsz, tr)
    e3 = eidx.reshape(G, gsz, tr)
    max1 = jnp.max(c3, axis=1, keepdims=True)
    first = jnp.min(jnp.where(c3 == max1, e3, E), axis=1, keepdims=True)
    max2 = jnp.max(jnp.where(e3 == first, NEG_INF, c3), axis=1, keepdims=True)
    gscore = (max1 + max2).reshape(G, tr)
    gidx = _iota2((G, tr), 0)
    gsel = jnp.zeros((G, tr), jnp.bool_)
    for _ in range(TOPK_GROUPS):
        gmax = jnp.max(gscore, axis=0, keepdims=True)
        pick = jnp.min(jnp.where(gscore == gmax, gidx, G), axis=0, keepdims=True)
        hit = gidx == pick
        gsel = jnp.logical_or(gsel, hit)
        gscore = jnp.where(hit, NEG_INF, gscore)
    esel = jnp.broadcast_to(gsel.reshape(G, 1, tr), (G, gsz, tr)).reshape(E, tr)

    cand = jnp.where(esel, choice, NEG_INF)
    sel = jnp.zeros((E, tr), F32)
    picks, weights, hits = [], [], []
    for k in range(TOP_K):
        vmax = jnp.max(cand, axis=0, keepdims=True)
        pick = jnp.min(jnp.where(cand == vmax, eidx, E), axis=0, keepdims=True)
        hit = eidx == pick
        picks.append(pick)
        hits.append(hit)
        weights.append(jnp.sum(jnp.where(hit, scores, 0.0), axis=0, keepdims=True))
        sel = jnp.where(hit, 1.0, sel)
        cand = jnp.where(hit, NEG_INF, cand)
    wsum = weights[0]
    for w in weights[1:]:
        wsum = wsum + w
    w_rows = jnp.concatenate([w / wsum * ROUTED_SCALE for w in weights], axis=0)

    before = (_iota2((tr, tr), 0) < _iota2((tr, tr), 1)).astype(BF16)
    rank = jnp.dot(sel.astype(BF16), before, preferred_element_type=F32) + carry_ref[:, 0:1]
    rank_rows = [jnp.sum(jnp.where(hits[k], rank, 0.0), axis=0, keepdims=True) for k in range(TOP_K)]
    carry_ref[...] = carry_ref[...] + jnp.sum(sel, axis=1, keepdims=True)

    idx_ref[...] = jnp.concatenate(picks, axis=0)
    rank_ref[...] = jnp.concatenate(rank_rows, axis=0).astype(jnp.int32)
    wts_ref[...] = jnp.concatenate([w_rows, jnp.zeros((LANES - TOP_K, tr), F32)], axis=0).T
    counts = _col_to_row(carry_ref[:, 0:1], E)
    cnt_ref[...] = jnp.broadcast_to(jnp.concatenate([counts, jnp.zeros((1, LANES - E), F32)], axis=1), (8, LANES))


def _router(x1, w_router, router_bias, tr=512):
    n, d = x1.shape
    wr = jnp.zeros((d, LANES), F32).at[:, :N_EXPERTS].set(w_router)
    slot_rows = pl.BlockSpec((TOP_K, tr), lambda i: (0, i))
    return pl.pallas_call(
        functools.partial(_router_kernel, tr=tr),
        grid=(n // tr,),
        in_specs=[pl.BlockSpec((tr, d), lambda i: (i, 0)),
                  pl.BlockSpec((d, LANES), lambda i: (0, 0)),
                  pl.BlockSpec((N_EXPERTS, 1), lambda i: (0, 0))],
        out_specs=[slot_rows, pl.BlockSpec((tr, LANES), lambda i: (i, 0)), slot_rows,
                   pl.BlockSpec((8, LANES), lambda i: (0, 0))],
        out_shape=[jax.ShapeDtypeStruct((TOP_K, n), jnp.int32), jax.ShapeDtypeStruct((n, LANES), F32),
                   jax.ShapeDtypeStruct((TOP_K, n), jnp.int32), jax.ShapeDtypeStruct((8, LANES), F32)],
        scratch_shapes=[pltpu.VMEM((N_EXPERTS, LANES), F32)],
        compiler_params=_cparams(("arbitrary",)),
        name="router",
    )(x1, wr, router_bias.reshape(N_EXPERTS, 1))


def _group_offsets(cnt, tm):
    padded = jnp.floor((cnt + (tm - 1)) * (1.0 / tm)) * tm
    upper = (_iota2((LANES, LANES), 0) < _iota2((LANES, LANES), 1)).astype(F32)
    offs = jnp.dot(padded, upper, precision=HI, preferred_element_type=F32)
    return padded, offs


def _pos_kernel(idx_ref, rank_ref, cnt_ref, pos_ref, *, tm):
    _, offs = _group_offsets(cnt_ref[...], tm)
    starts = _row_to_col(offs[0:1, :], LANES)[:N_EXPERTS, :]
    idx = idx_ref[...]
    eidx = _iota2((N_EXPERTS, idx.shape[1]), 0)
    rows = [jnp.sum(jnp.where(eidx == idx[k:k + 1, :], starts, 0.0), axis=0, keepdims=True)
            for k in range(TOP_K)]
    pos_ref[...] = jnp.concatenate(rows, axis=0).astype(jnp.int32) + rank_ref[...]


def _positions(idx_t, rank_t, cnt, tm, tp=2048):
    n = idx_t.shape[1]
    slot_rows = pl.BlockSpec((TOP_K, tp), lambda i: (0, i))
    return pl.pallas_call(
        functools.partial(_pos_kernel, tm=tm),
        grid=(n // tp,),
        in_specs=[slot_rows, slot_rows, pl.BlockSpec((8, LANES), lambda i: (0, 0))],
        out_specs=slot_rows,
        out_shape=jax.ShapeDtypeStruct((TOP_K, n), jnp.int32),
        compiler_params=_cparams(("parallel",)),
        name="dispatch_positions",
    )(idx_t, rank_t, cnt)


def _tile_map_kernel(cnt_ref, te_ref, *, tm, width):
    padded, offs = _group_offsets(cnt_ref[...], tm)
    ends = _row_to_col((offs + padded)[0:1, :], LANES)
    expert = _iota2((LANES, width), 0)
    start = (_iota2((LANES, width), 1) * tm).astype(F32)
    done = jnp.logical_and(ends <= start, expert < N_EXPERTS)
    te = jnp.sum(jnp.where(done, 1, 0), axis=0, keepdims=True)
    total = jnp.max(jnp.where(expert < N_EXPERTS, ends, 0.0), axis=0, keepdims=True)
    ntile = (total * (1.0 / tm)).astype(jnp.int32)
    vend = _row_to_col((offs + cnt_ref[...])[0:1, :], LANES)
    mine = jnp.sum(jnp.where(expert == te, vend, 0.0), axis=0, keepdims=True)
    valid = jnp.clip(mine - start[0:1, :], 0.0, float(tm)).astype(jnp.int32)
    row = _iota2((8, width), 0)
    te_ref[...] = jnp.where(row == 0, te, jnp.where(row == 1, ntile, jnp.where(row == 2, valid, 0)))


def _tile_map(cnt, tm, ntile_max):
    width = -(-ntile_max // LANES) * LANES
    out = pl.pallas_call(
        functools.partial(_tile_map_kernel, tm=tm, width=width),
        in_specs=[pl.BlockSpec((8, LANES), lambda: (0, 0))],
        out_specs=pl.BlockSpec((8, width), lambda: (0, 0)),
        out_shape=jax.ShapeDtypeStruct((8, width), jnp.int32),
        name="tile_map",
    )(cnt)
    return out[0, :ntile_max], out[1, :1], out[2, :ntile_max]


SC_SCATTER_ROWS = 64
SC_GATHER_ROWS = 32


def _sc_mesh():
    info = plsc.get_sparse_core_info()
    mesh = plsc.VectorSubcoreMesh(core_axis_name="c", subcore_axis_name="s")
    return mesh, info.num_cores, info.num_subcores


def _sc_scatter_rows(src3, idx3, nrows):
    n, c, _ = src3.shape
    _, nk, chunk = idx3.shape
    mesh, ncore, nsub = _sc_mesh()
    per_w = n // (ncore * nsub)
    assert chunk == SC_SCATTER_ROWS and per_w % chunk == 0

    @functools.partial(
        pl.kernel, mesh=mesh,
        out_type=jax.ShapeDtypeStruct((nrows, c, LANES), src3.dtype),
        scratch_types=[pltpu.VMEM((nk, chunk), jnp.int32),
                       pltpu.VMEM((chunk, c, LANES), src3.dtype),
                       pltpu.SemaphoreType.DMA],
    )
    def scatter_kernel(src_hbm, idx_hbm, out_hbm, idx_v, rows_v, sem):
        base = (lax.axis_index("s") * ncore + lax.axis_index("c")) * per_w

        @pl.loop(0, per_w // chunk)
        def _(j):
            off = pl.multiple_of(base + j * chunk, chunk)
            pltpu.sync_copy(idx_hbm.at[base // chunk + j], idx_v)
            pltpu.sync_copy(src_hbm.at[pl.ds(off, chunk)], rows_v)
            copies = [pltpu.async_copy(rows_v, out_hbm.at[idx_v.at[k]], sem) for k in range(nk)]
            for cp in copies:
                cp.wait()

    return scatter_kernel(src3, idx3)


def _sc_gather_rows(table3, idx2):
    _, c, _ = table3.shape
    nchunk, chunk = idx2.shape
    mesh, ncore, nsub = _sc_mesh()
    per_w = nchunk // (ncore * nsub)
    assert chunk == SC_GATHER_ROWS and per_w % 2 == 0

    @functools.partial(
        pl.kernel, mesh=mesh,
        out_type=jax.ShapeDtypeStruct((nchunk * chunk, c, LANES), table3.dtype),
        scratch_types=[pltpu.VMEM((per_w, chunk), jnp.int32),
                       pltpu.VMEM((chunk, c, LANES), table3.dtype),
                       pltpu.VMEM((chunk, c, LANES), table3.dtype),
                       pltpu.SemaphoreType.DMA, pltpu.SemaphoreType.DMA,
                       pltpu.SemaphoreType.DMA, pltpu.SemaphoreType.DMA],
    )
    def gather_kernel(table_hbm, idx_hbm, out_hbm, idx_v, rows0, rows1, g0, g1, w0, w1):
        first = (lax.axis_index("s") * ncore + lax.axis_index("c")) * per_w
        pltpu.sync_copy(idx_hbm.at[pl.ds(pl.multiple_of(first, per_w), per_w)], idx_v)

        @pl.loop(0, per_w, step=2)
        def _(j):
            ga = pltpu.async_copy(table_hbm.at[idx_v.at[j]], rows0, g0)
            gb = pltpu.async_copy(table_hbm.at[idx_v.at[j + 1]], rows1, g1)
            ga.wait()
            wa = pltpu.async_copy(rows0, out_hbm.at[pl.ds(pl.multiple_of((first + j) * chunk, chunk), chunk)], w0)
            gb.wait()
            wb = pltpu.async_copy(rows1, out_hbm.at[pl.ds(pl.multiple_of((first + j + 1) * chunk, chunk), chunk)], w1)
            wa.wait()
            wb.wait()

    return gather_kernel(table3, idx2)


FFN_LOOKAHEAD = 2
FFN_SLOTS = FFN_LOOKAHEAD + 1


def _ffn_kernel(te_ref, nt_ref, nv_ref, xs_ref, w1_ref, w3_ref, w2_ref, o_ref, w1b, w3b, w2b, ord_ref, *, tm):
    j = pl.program_id(0)
    i = j - FFN_LOOKAHEAD
    nt = nt_ref[0]

    @pl.when(j == 0)
    def _():
        for s in range(4):
            ord_ref[s] = 0

    def starts_group(t):
        tc = jnp.clip(t, 0, nt - 1)
        changed = te_ref[tc] != te_ref[jnp.maximum(tc - 1, 0)]
        return jnp.logical_or(t == 0, jnp.logical_and(jnp.logical_and(t > 0, t < nt), changed))

    for s, (w_ref, wb) in enumerate(((w1_ref, w1b), (w3_ref, w3b), (w2_ref, w2b))):
        @pl.when(starts_group(j - s))
        def _(s=s, w_ref=w_ref, wb=wb):
            wb[ord_ref[s] % FFN_SLOTS] = w_ref[0, 0].astype(BF16)
            ord_ref[s] = ord_ref[s] + 1

    @pl.when(jnp.logical_and(i >= 0, i < nt))
    def _():
        ic = jnp.maximum(i, 0)

        @pl.when(jnp.logical_and(i > 0, starts_group(i)))
        def _():
            ord_ref[3] = ord_ref[3] + 1

        slot = ord_ref[3] % FFN_SLOTS
        c = xs_ref.shape[0] // tm
        live = _iota2((tm, LANES), 0) < nv_ref[ic]
        parts = [_unpack_bf16_pairs(jnp.where(live, p, jnp.uint32(0)))
                 for p in _load_token_major(xs_ref, 0, tm, c)]
        lo = jnp.concatenate([p[0] for p in parts], axis=1)
        hi = jnp.concatenate([p[1] for p in parts], axis=1)
        half = lo.shape[1]
        h1 = (jnp.dot(lo, w1b[slot, :half, :], preferred_element_type=F32)
              + jnp.dot(hi, w1b[slot, half:, :], preferred_element_type=F32))
        h3 = (jnp.dot(lo, w3b[slot, :half, :], preferred_element_type=F32)
              + jnp.dot(hi, w3b[slot, half:, :], preferred_element_type=F32))
        h = (_silu(h1) * h3).astype(BF16)
        y = jnp.dot(h, w2b[slot], preferred_element_type=F32)
        _store_token_major(o_ref, _pack_bf16_pairs(y))

    @pl.when(i >= nt)
    def _():
        o_ref[...] = jnp.zeros_like(o_ref)


def _expert_ffn(te, nt, nv, xs, w1, w3, w2, layer, nrows, tm):
    d, f = w1.shape[2], w1.shape[3]
    c = xs.shape[0] // nrows
    ntile = nrows // tm

    def tile(j, te_r, nt_r, nv_r):
        return (jnp.clip(j - FFN_LOOKAHEAD, 0, nt_r[0] - 1), 0)

    def out_tile(j, te_r, nt_r, nv_r):
        return (jnp.maximum(j - FFN_LOOKAHEAD, 0), 0)

    def expert(delay):
        def index_map(j, te_r, nt_r, nv_r):
            return (layer, te_r[jnp.clip(j - delay, 0, nt_r[0] - 1)], 0, 0)
        return index_map

    assert FFN_LOOKAHEAD == 2
    grid_spec = pltpu.PrefetchScalarGridSpec(
        num_scalar_prefetch=3,
        grid=(ntile + FFN_LOOKAHEAD,),
        in_specs=[pl.BlockSpec((tm * c, LANES), tile),
                  pl.BlockSpec((1, 1, d, f), expert(0)),
                  pl.BlockSpec((1, 1, d, f), expert(1)),
                  pl.BlockSpec((1, 1, f, d), expert(2))],
        out_specs=pl.BlockSpec((tm * c, LANES), out_tile),
        scratch_shapes=[pltpu.VMEM((FFN_SLOTS, d, f), BF16), pltpu.VMEM((FFN_SLOTS, d, f), BF16),
                        pltpu.VMEM((FFN_SLOTS, f, d), BF16), pltpu.SMEM((4,), jnp.int32)],
    )
    return pl.pallas_call(
        functools.partial(_ffn_kernel, tm=tm),
        grid_spec=grid_spec,
        out_shape=jax.ShapeDtypeStruct((nrows * c, LANES), jnp.uint32),
        compiler_params=_cparams(("arbitrary",)),
        name="expert_ffn",
    )(te, nt, nv, xs, w1, w3, w2)


def _shared_expert_kernel(x_ref, ws1_ref, ws3_ref, ws2_ref, o_ref):
    xb = x_ref[...].astype(BF16)
    s1 = jnp.dot(xb, ws1_ref[...], preferred_element_type=F32)
    s3 = jnp.dot(xb, ws3_ref[...], preferred_element_type=F32)
    o_ref[...] = jnp.dot((_silu(s1) * s3).astype(BF16), ws2_ref[...], preferred_element_type=F32)


def _shared_expert(x1, ws1, ws3, ws2, tm=512):
    n, d = x1.shape
    sf = ws1.shape[1]
    return pl.pallas_call(
        _shared_expert_kernel,
        grid=(n // tm,),
        in_specs=[pl.BlockSpec((tm, d), lambda i: (i, 0)),
                  pl.BlockSpec((d, sf), lambda i: (0, 0)),
                  pl.BlockSpec((d, sf), lambda i: (0, 0)),
                  pl.BlockSpec((sf, d), lambda i: (0, 0))],
        out_specs=pl.BlockSpec((tm, d), lambda i: (i, 0)),
        out_shape=jax.ShapeDtypeStruct((n, d), F32),
        compiler_params=_cparams(("parallel",)),
        name="shared_expert",
    )(x1, ws1, ws3, ws2)


def _combine_kernel(*refs, tc):
    y_refs = refs[:TOP_K]
    wts_ref, x_ref, sh_ref, g_ref, b_ref, o_ref, ob_ref = refs[TOP_K:]
    c = y_refs[0].shape[0] // tc
    acc = sh_ref[...]
    wts = wts_ref[...]
    for k in range(TOP_K):
        parts = [_unpack_bf16_pairs(p) for p in _load_token_major(y_refs[k], 0, tc, c)]
        yk = jnp.concatenate([p[0] for p in parts] + [p[1] for p in parts], axis=1).astype(F32)
        acc = acc + wts[:, k:k + 1] * yk
    y = _layernorm_rows(DN_ALPHA * x_ref[...] + acc, g_ref[...], b_ref[...])
    o_ref[...] = y
    ob_ref[...] = y.astype(BF16)


def _combine(yg, wts, x1, shared, g, b, tc=256):
    n, d = x1.shape
    c = yg.shape[0] // (TOP_K * n)
    nblk = n // tc

    def slot_spec(k):
        return pl.BlockSpec((tc * c, LANES), lambda i: (k * nblk + i, 0))

    rows = pl.BlockSpec((tc, d), lambda i: (i, 0))
    return pl.pallas_call(
        functools.partial(_combine_kernel, tc=tc),
        grid=(nblk,),
        in_specs=[slot_spec(k) for k in range(TOP_K)] + [
            pl.BlockSpec((tc, LANES), lambda i: (i, 0)), rows, rows,
            pl.BlockSpec((1, d), lambda i: (0, 0)),
            pl.BlockSpec((1, d), lambda i: (0, 0))],
        out_specs=[rows, rows],
        out_shape=[jax.ShapeDtypeStruct((n, d), F32), jax.ShapeDtypeStruct((n, d), BF16)],
        compiler_params=_cparams(("parallel",)),
        name="moe_combine",
    )(*([yg] * TOP_K), wts, x1, shared, g.reshape(1, d), b.reshape(1, d))


def _cast_kernel(x_ref, o_ref):
    o_ref[...] = x_ref[...].astype(o_ref.dtype)


def _to_bf16(x, tm=1024):
    n, d = x.shape
    return pl.pallas_call(
        _cast_kernel,
        grid=(n // tm,),
        in_specs=[pl.BlockSpec((tm, d), lambda i: (i, 0))],
        out_specs=pl.BlockSpec((tm, d), lambda i: (i, 0)),
        out_shape=jax.ShapeDtypeStruct((n, d), BF16),
        compiler_params=_cparams(("parallel",)),
        name="cast_bf16",
    )(x)


FFN_TILE = 256


def _moe(x1, x1p, w_router, router_bias, w1, w3, w2, layer, ws1, ws3, ws2, g, b):
    n = x1.shape[0]
    tm = FFN_TILE
    nrows = n * TOP_K + N_EXPERTS * tm
    idx, wts, rank, cnt = _router(x1, w_router, router_bias)
    pos_t = _positions(idx, rank, cnt, tm)
    te, nt, nv = _tile_map(cnt, tm, nrows // tm)
    c = x1p.shape[0] // n
    pos_s = pos_t.reshape(TOP_K, n // SC_SCATTER_ROWS, SC_SCATTER_ROWS).transpose(1, 0, 2)
    xs = _sc_scatter_rows(x1p.reshape(n, c, LANES), pos_s, nrows)
    shared = _shared_expert(x1, ws1.astype(BF16), ws3.astype(BF16), ws2.astype(BF16))
    ys = _expert_ffn(te, nt, nv, xs.reshape(nrows * c, LANES), w1, w3, w2, layer, nrows, tm)
    yg = _sc_gather_rows(ys.reshape(nrows, c, LANES), pos_t.reshape(-1, SC_GATHER_ROWS))
    return _combine(yg.reshape(TOP_K * n * c, LANES), wts, x1, shared, g, b)


def kernel(x, positions, w_in, gla_wa2, gla_ba, gla_norm, mlstm_conv_w, mlstm_conv_b, mlstm_bi, mlstm_bf,
           mlstm_norm, sgu_ln_g, sgu_ln_b, sgu_ws, sgu_bs, w_pa, w_pb, w_pc, w_pd, w_out, ln1_g, ln1_b,
           w_router, router_bias, w1, w3, w2, ws1, ws3, ws2, ln2_g, ln2_b):
    nbatch, seq, d = x.shape
    n = nbatch * seq
    xf = x.reshape(n, d)
    xb = _to_bf16(xf)
    cos_t, sin_t = _rope_tables(positions)
    for l in range(DEPTH):
        wl = w_in[l]
        w_a = wl[:, _OFF_A:_OFF_A + _W_A].astype(BF16)
        w_b = wl[:, _OFF_B:_OFF_B + _W_B].astype(BF16)
        w_c = wl[:, _OFF_C:_OFF_C + _W_C].astype(BF16)
        w_d = wl[:, _OFF_D:_OFF_D + _W_D].astype(BF16)
        w_g = wl[:, _OFF_G:_OFF_G + _W_G].astype(BF16)
        w_s = jnp.concatenate(
            [wl[:, _OFF_LOW:_OFF_LOW + GLA_RANK], wl[:, _OFF_IF:_OFF_IF + 2 * MLSTM_HEADS],
             jnp.zeros((d, LANES - GLA_RANK - 2 * MLSTM_HEADS), F32)], axis=1).astype(BF16)
        y_a = _matmul(xb, w_a, 2048, 768)
        y_b = _matmul(xb, w_b, 2048, 768)
        y_c = _matmul(xb, w_c, 2048, 768)
        y_d = _matmul(xb, w_d, 2048, 1024)
        y_s = _matmul(xb, w_s, 2048, LANES)
        sm3 = y_s.reshape(nbatch, seq, LANES)
        o_a, o_d, o_c = _mixers_acd(
            y_a.reshape(nbatch, seq, _W_A), y_d.reshape(nbatch, seq, _W_D), sm3, y_c, gla_wa2[l], gla_ba[l],
            gla_norm[l], mlstm_conv_w[l], mlstm_conv_b[l], mlstm_bi[l], mlstm_bf[l], mlstm_norm[l],
            sgu_ln_g[l], sgu_ln_b[l], sgu_ws[l], sgu_bs[l])
        o_a = o_a.reshape(n, GLA_V)
        o_d = o_d.reshape(n, ML_W)
        o_b = _dilated(y_b, cos_t, sin_t, nbatch)
        merged = _merge(xb, w_g, (o_a, o_b, o_c, o_d),
                        (w_pa[l].astype(BF16), w_pb[l].astype(BF16), w_pc[l].astype(BF16), w_pd[l].astype(BF16)))
        x1, x1p = _outproj_ln(merged, w_out[l].astype(BF16), xf, ln1_g[l], ln1_b[l])
        xf, xb = _moe(x1, x1p, w_router[l], router_bias[l], w1, w3, w2, l, ws1[l], ws3[l], ws2[l],
                      ln2_g[l], ln2_b[l])
    return xf.reshape(nbatch, seq, d)
```

```python
import functools

import jax
import jax.numpy as jnp
from jax import lax
from jax.experimental import pallas as pl
from jax.experimental.pallas import tpu as pltpu
from jax.experimental.pallas import tpu_sc as plsc

D_MODEL = 2048
DEPTH = 2

GLA_HEADS = 4
GLA_DK = 64
GLA_DV = 128
GLA_RANK = 16
GLA_TAU = 16.0
GLA_CHUNK = 64
GLA_QK = GLA_HEADS * GLA_DK
GLA_V = GLA_HEADS * GLA_DV

DIL_PAIRS = ((128, 1), (512, 4), (2048, 16))
DIL_HEADS_PER_GROUP = 4
DIL_HEAD_DIM = 64
DIL_HEADS = len(DIL_PAIRS) * DIL_HEADS_PER_GROUP
DIL_W = DIL_HEADS * DIL_HEAD_DIM
DIL_OUT = DIL_HEADS_PER_GROUP * DIL_HEAD_DIM
DIL_BLOCK = 128
ROPE_THETA = 10000.0

SGU_CHUNK = 128
SGU_GROUPS = 6
SGU_GROUP_CH = 128
SGU_W = SGU_GROUPS * SGU_GROUP_CH

MLSTM_HEADS = 4
MLSTM_HEAD_DIM = 128
MLSTM_CHUNK = 64
MLSTM_CONV = 4
ML_W = MLSTM_HEADS * MLSTM_HEAD_DIM

N_EXPERTS = 64
TOP_K = 8
N_GROUPS = 8
TOPK_GROUPS = 4
EXPERT_FF = 512
SHARED_FF = 512
ROUTED_SCALE = 2.5

N_BRANCH = 4
DN_ALPHA = (2 * DEPTH) ** 0.25
LN_EPS = 1e-5

_OFF_A = 0
_W_A = 2 * GLA_QK + 2 * GLA_V
_OFF_LOW = _OFF_A + _W_A
_OFF_B = _OFF_LOW + GLA_RANK
_W_B = 3 * DIL_W
_OFF_C = _OFF_B + _W_B
_W_C = 2 * SGU_W
_OFF_D = _OFF_C + _W_C
_W_D = 4 * ML_W
_OFF_IF = _OFF_D + _W_D
_OFF_G = _OFF_IF + 2 * MLSTM_HEADS
_W_G = N_BRANCH * D_MODEL

LANES = 128
VMEM_LIMIT = 56 * 1024 * 1024

_SM_LOW = 0
_SM_I = GLA_RANK
_SM_F = GLA_RANK + MLSTM_HEADS

HI = lax.Precision.HIGHEST
F32 = jnp.float32
BF16 = jnp.bfloat16
NEG_INF = float("-inf")


def _cparams(sem):
    return pltpu.CompilerParams(dimension_semantics=sem, vmem_limit_bytes=VMEM_LIMIT)


def _log_sigmoid(x):
    return jnp.minimum(x, 0.0) - jnp.log1p(jnp.exp(-jnp.abs(x)))


def _sigmoid(x):
    return 1.0 / (1.0 + jnp.exp(-x))


def _silu(x):
    return x * _sigmoid(x)


def _iota2(shape, dim):
    return lax.broadcasted_iota(jnp.int32, shape, dim)


def _col_to_row(col, n):
    eye = _iota2((n, n), 0) == _iota2((n, n), 1)
    return jnp.sum(jnp.where(eye, col, 0.0), axis=0, keepdims=True)


def _row_to_col(row, n):
    eye = _iota2((n, n), 0) == _iota2((n, n), 1)
    return jnp.sum(jnp.where(eye, row, 0.0), axis=1, keepdims=True)


def _mm_kernel(x_ref, w_ref, o_ref):
    o_ref[...] = jnp.dot(x_ref[...], w_ref[...], preferred_element_type=F32).astype(o_ref.dtype)


def _matmul(x, w, tm, tn, out_dtype=F32):
    n, k = x.shape
    m = w.shape[1]
    return pl.pallas_call(
        _mm_kernel,
        grid=(n // tm, m // tn),
        in_specs=[pl.BlockSpec((tm, k), lambda i, j: (i, 0)),
                  pl.BlockSpec((k, tn), lambda i, j: (0, j))],
        out_specs=pl.BlockSpec((tm, tn), lambda i, j: (i, j)),
        out_shape=jax.ShapeDtypeStruct((n, m), out_dtype),
        compiler_params=_cparams(("parallel", "arbitrary")),
        name="in_proj",
    )(x, w)


MLSTM_HALO = 8


def _gla_chunk_stacked(y_ref, sm_ref, wa2_ref, ba_ref, g_ref, o_ref, state_ref, *, nb):
    L, H, DK, DV = GLA_CHUNK, GLA_HEADS, GLA_DK, GLA_DV
    R = nb * L
    ri = _iota2((R, R), 0)
    ci = _iota2((R, R), 1)
    causal = jnp.logical_and((ri // L) == (ci // L), ri >= ci)
    row_of = [slice(b * L, (b + 1) * L) for b in range(nb)]
    y = jnp.concatenate([y_ref[b] for b in range(nb)], axis=0)
    a_low = jnp.concatenate([sm_ref[b] for b in range(nb)], axis=0)[:, _SM_LOW:_SM_LOW + GLA_RANK]
    glog = jnp.dot(a_low, wa2_ref[...], preferred_element_type=F32) + ba_ref[...]
    g = _log_sigmoid(glog) * (1.0 / GLA_TAU)
    bc = jnp.dot(causal.astype(F32), g, precision=HI, preferred_element_type=F32)
    outs = []
    for h in range(H):
        q = y[:, h * DK:(h + 1) * DK] * (DK ** -0.5)
        k = y[:, GLA_QK + h * DK:GLA_QK + (h + 1) * DK]
        v = y[:, 2 * GLA_QK + h * DV:2 * GLA_QK + (h + 1) * DV]
        bh = bc[:, h * DK:(h + 1) * DK]
        qe = q * jnp.exp(bh)
        ke = k * jnp.exp(-bh)
        att = lax.dot_general(qe, ke, (((1,), (1,)), ((), ())), preferred_element_type=F32)
        att = jnp.where(causal, att, 0.0)
        sts = [state_ref[b * H + h] for b in range(nb)]
        o = jnp.dot(att, v, preferred_element_type=F32) + jnp.concatenate(
            [jnp.dot(qe[row_of[b]], sts[b], preferred_element_type=F32) for b in range(nb)], axis=0)
        b_lasts = [bh[(b + 1) * L - 1:(b + 1) * L, :] for b in range(nb)]
        b_last = jnp.concatenate([jnp.broadcast_to(bl, (L, DK)) for bl in b_lasts], axis=0)
        kd = k * jnp.exp(b_last - bh)
        for b in range(nb):
            decay = _row_to_col(jnp.exp(b_lasts[b]), DK)
            state_ref[b * H + h] = decay * sts[b] + lax.dot_general(
                kd[row_of[b]], v[row_of[b]], (((0,), (0,)), ((), ())), preferred_element_type=F32)
        o = o * lax.rsqrt(jnp.mean(o * o, axis=-1, keepdims=True) + LN_EPS)
        outs.append(o)
    o_all = jnp.concatenate(outs, axis=-1) * g_ref[...]
    r = y[:, 2 * GLA_QK + GLA_V:2 * GLA_QK + 2 * GLA_V]
    o_all = (o_all * _silu(r)).astype(o_ref.dtype)
    for b in range(nb):
        o_ref[b] = o_all[row_of[b]]


def _mlstm_chunk_stacked(y_ref, sm_ref, cw_ref, cb_ref, gb_ref, g_ref, o_ref, c_ref, n_ref, m_ref, tail_ref, *, nb):
    L, H, DH = MLSTM_CHUNK, MLSTM_HEADS, MLSTM_HEAD_DIM
    W2 = 2 * ML_W
    HALO = MLSTM_HALO
    R = nb * L
    ri = _iota2((R, R), 0)
    ci = _iota2((R, R), 1)
    same_batch = (ri // L) == (ci // L)
    causal = jnp.logical_and(same_batch, ri >= ci)
    ys, qks = [], []
    for b in range(nb):
        y = y_ref[b]
        qk_raw = y[:, :W2]
        ext = jnp.concatenate([tail_ref[b], qk_raw], axis=0)
        tail_ref[b] = qk_raw[L - HALO:, :]
        conv = cb_ref[...]
        for j in range(MLSTM_CONV):
            s0 = HALO - (MLSTM_CONV - 1) + j
            conv = conv + cw_ref[j:j + 1, :] * ext[s0:s0 + L, :]
        ys.append(y)
        qks.append(_silu(conv))
    y_all = jnp.concatenate(ys, axis=0)
    qk = jnp.concatenate(qks, axis=0)
    gates = jnp.concatenate([sm_ref[b] for b in range(nb)], axis=0) + gb_ref[...]
    bcum = jnp.dot(causal.astype(F32), _log_sigmoid(gates), precision=HI, preferred_element_type=F32)
    row_of = [slice(b * L, (b + 1) * L) for b in range(nb)]
    outs = []
    for h in range(H):
        q = qk[:, h * DH:(h + 1) * DH]
        k = qk[:, ML_W + h * DH:ML_W + (h + 1) * DH] * (DH ** -0.5)
        v = y_all[:, W2 + h * DH:W2 + (h + 1) * DH]
        b_col = bcum[:, _SM_F + h:_SM_F + h + 1]
        li_col = gates[:, _SM_I + h:_SM_I + h + 1]
        b_row = _col_to_row(b_col, R)
        li_row = _col_to_row(li_col, R)
        m_prevs = [m_ref[b * H + h][:, 0:1] for b in range(nb)]
        m_prev = jnp.concatenate([jnp.broadcast_to(mp, (L, 1)) for mp in m_prevs], axis=0)
        dmat = jnp.where(causal, b_col - b_row + li_row, NEG_INF)
        inter = b_col + m_prev
        m_t = jnp.maximum(inter, jnp.max(dmat, axis=-1, keepdims=True))
        w_in = jnp.exp(dmat - m_t)
        w_st = jnp.exp(inter - m_t)
        sc = lax.dot_general(q, k, (((1,), (1,)), ((), ())), preferred_element_type=F32) * w_in
        csts = [c_ref[b * H + h] for b in range(nb)]
        nsts = [n_ref[b * H + h] for b in range(nb)]
        qc = jnp.concatenate([jnp.dot(q[row_of[b]], csts[b], preferred_element_type=F32) for b in range(nb)], axis=0)
        n_rows = jnp.concatenate([jnp.broadcast_to(nsts[b], (L, DH)) for b in range(nb)], axis=0)
        num = jnp.dot(sc, v, preferred_element_type=F32) + w_st * qc
        den = jnp.sum(sc, axis=-1, keepdims=True) + w_st * jnp.sum(q * n_rows, axis=-1, keepdims=True)
        hh = num / jnp.maximum(jnp.abs(den), jnp.exp(-m_t))
        b_lasts = [b_col[(b + 1) * L - 1:(b + 1) * L, :] for b in range(nb)]
        b_last = jnp.concatenate([jnp.broadcast_to(bl, (L, 1)) for bl in b_lasts], axis=0)
        dec = b_last - b_col + li_col
        m_news = [jnp.maximum(b_lasts[b] + m_prevs[b], jnp.max(dec[row_of[b]], axis=0, keepdims=True))
                  for b in range(nb)]
        m_new = jnp.concatenate([jnp.broadcast_to(mn, (L, 1)) for mn in m_news], axis=0)
        wkk = jnp.exp(dec - m_new) * k
        for b in range(nb):
            keep = jnp.exp(b_lasts[b] + m_prevs[b] - m_news[b])
            c_ref[b * H + h] = keep * csts[b] + lax.dot_general(
                wkk[row_of[b]], v[row_of[b]], (((0,), (0,)), ((), ())), preferred_element_type=F32)
            n_ref[b * H + h] = keep * nsts[b] + jnp.sum(wkk[row_of[b]], axis=0, keepdims=True)
            m_ref[b * H + h] = jnp.broadcast_to(m_news[b], (1, LANES))
        o_pre = y_all[:, W2 + ML_W + h * DH:W2 + ML_W + (h + 1) * DH]
        hh = _sigmoid(o_pre) * hh
        hh = hh * lax.rsqrt(jnp.mean(hh * hh, axis=-1, keepdims=True) + LN_EPS)
        outs.append(hh)
    o_all = (jnp.concatenate(outs, axis=-1) * g_ref[...]).astype(o_ref.dtype)
    for b in range(nb):
        o_ref[b] = o_all[row_of[b]]


def _mixers_kernel(ya_ref, yd_ref, sm_ref, yc_ref, wa2_ref, ba_ref, ga_ref, cw_ref, cb_ref, gb_ref, gd_ref,
                   lg_ref, lb_ref, ws_ref, bst_ref, oa_ref, od_ref, oc_ref,
                   state_ref, c_ref, n_ref, m_ref, tail_ref, *, nb, sgu_chunks):
    @pl.when(pl.program_id(0) == 0)
    def _():
        for ref in (state_ref, c_ref, n_ref, m_ref, tail_ref):
            ref[...] = jnp.zeros_like(ref)

    _gla_chunk_stacked(ya_ref, sm_ref, wa2_ref, ba_ref, ga_ref, oa_ref, state_ref, nb=nb)
    _mlstm_chunk_stacked(yd_ref, sm_ref, cw_ref, cb_ref, gb_ref, gd_ref, od_ref, c_ref, n_ref, m_ref, tail_ref,
                         nb=nb)
    _sgu_kernel(yc_ref, lg_ref, lb_ref, ws_ref, bst_ref, oc_ref, nchunk=sgu_chunks)


def _mixers_acd(ya3, yd3, sm3, yc, wa2, ba, gla_norm, conv_w, conv_b, b_i, b_f, mlstm_norm,
                sgu_ln_g, sgu_ln_b, sgu_ws, sgu_bs):
    nb, s, _ = ya3.shape
    assert GLA_CHUNK == MLSTM_CHUNK
    L = GLA_CHUNK
    nstep = s // L
    n = yc.shape[0]
    sgu_rows = n // nstep
    assert sgu_rows % SGU_CHUNK == 0 and sgu_rows * nstep == n
    gate_bias = jnp.zeros((1, LANES), F32)
    gate_bias = gate_bias.at[0, _SM_I:_SM_I + MLSTM_HEADS].set(b_i).at[0, _SM_F:_SM_F + MLSTM_HEADS].set(b_f)
    bst = jnp.zeros((SGU_CHUNK, LANES), F32).at[:, :SGU_GROUPS].set(sgu_bs.T)

    def chunk(width):
        return pl.BlockSpec((nb, L, width), lambda i: (0, i, 0))

    def whole(*shape):
        return pl.BlockSpec(shape, lambda i: (0,) * len(shape))

    return pl.pallas_call(
        functools.partial(_mixers_kernel, nb=nb, sgu_chunks=sgu_rows // SGU_CHUNK),
        grid=(nstep,),
        in_specs=[chunk(_W_A), chunk(_W_D), chunk(LANES), pl.BlockSpec((sgu_rows, _W_C), lambda i: (i, 0)),
                  whole(GLA_RANK, GLA_QK), whole(1, GLA_QK), whole(1, GLA_V),
                  whole(MLSTM_CONV, 2 * ML_W), whole(1, 2 * ML_W), whole(1, LANES), whole(1, ML_W),
                  whole(1, SGU_W), whole(1, SGU_W), whole(SGU_GROUPS, SGU_CHUNK, SGU_CHUNK),
                  whole(SGU_CHUNK, LANES)],
        out_specs=[chunk(GLA_V), chunk(ML_W), pl.BlockSpec((sgu_rows, SGU_W), lambda i: (i, 0))],
        out_shape=[jax.ShapeDtypeStruct((nb, s, GLA_V), BF16), jax.ShapeDtypeStruct((nb, s, ML_W), BF16),
                   jax.ShapeDtypeStruct((n, SGU_W), BF16)],
        scratch_shapes=[pltpu.VMEM((nb * GLA_HEADS, GLA_DK, GLA_DV), F32),
                        pltpu.VMEM((nb * MLSTM_HEADS, MLSTM_HEAD_DIM, MLSTM_HEAD_DIM), F32),
                        pltpu.VMEM((nb * MLSTM_HEADS, 1, MLSTM_HEAD_DIM), F32),
                        pltpu.VMEM((nb * MLSTM_HEADS, 1, LANES), F32),
                        pltpu.VMEM((nb, MLSTM_HALO, 2 * ML_W), F32)],
        compiler_params=_cparams(("arbitrary",)),
        name="gla_mlstm_sgu",
    )(ya3, yd3, sm3, yc, wa2, ba.reshape(1, GLA_QK), gla_norm.reshape(1, GLA_V),
      conv_w, conv_b.reshape(1, 2 * ML_W), gate_bias, mlstm_norm.reshape(1, ML_W),
      sgu_ln_g.reshape(1, SGU_W), sgu_ln_b.reshape(1, SGU_W), sgu_ws, bst)


def _gelu(x):
    return 0.5 * x * (1.0 + lax.erf(x * (0.5 ** 0.5)))


def _sgu_kernel(y_ref, lg_ref, lb_ref, ws_ref, bst_ref, o_ref, *, nchunk):
    C, G, GC = SGU_CHUNK, SGU_GROUPS, SGU_GROUP_CH
    y = y_ref[...]
    zu = _gelu(y[:, :SGU_W])
    zv = _gelu(y[:, SGU_W:])
    mu = jnp.mean(zv, axis=-1, keepdims=True)
    var = jnp.mean(jnp.square(zv - mu), axis=-1, keepdims=True)
    vn = (zv - mu) * lax.rsqrt(var + LN_EPS) * lg_ref[...] + lb_ref[...]
    tril = _iota2((C, C), 0) >= _iota2((C, C), 1)
    for g in range(G):
        wc = jnp.where(tril, ws_ref[g], 0.0)
        bias = bst_ref[:, g:g + 1]
        for c in range(nchunk):
            rows = slice(c * C, (c + 1) * C)
            cols = slice(g * GC, (g + 1) * GC)
            s = jnp.dot(wc, vn[rows, cols], preferred_element_type=F32) + bias
            o_ref[rows, cols] = (zu[rows, cols] * s).astype(o_ref.dtype)


def _rope_table_kernel(pos_ref, inv_ref, cos_ref, sin_ref):
    ang = pos_ref[...].astype(F32) * inv_ref[...]
    half = DIL_HEAD_DIM // 2
    sign = jnp.where((_iota2(ang.shape, 1) % DIL_HEAD_DIM) < half, -1.0, 1.0)
    cos_ref[...] = jnp.cos(ang)
    sin_ref[...] = jnp.sin(ang) * sign


def _rope_tables(positions):
    n = positions.size
    half = DIL_HEAD_DIM // 2
    inv = ROPE_THETA ** (-jnp.arange(half, dtype=F32) * 2.0 / DIL_HEAD_DIM)
    inv = jnp.tile(inv, LANES // half).reshape(1, LANES)
    t = 1024
    return pl.pallas_call(
        _rope_table_kernel,
        grid=(n // t,),
        in_specs=[pl.BlockSpec((t, 1), lambda i: (i, 0)),
                  pl.BlockSpec((1, LANES), lambda i: (0, 0))],
        out_specs=[pl.BlockSpec((t, LANES), lambda i: (i, 0)),
                   pl.BlockSpec((t, LANES), lambda i: (i, 0))],
        out_shape=[jax.ShapeDtypeStruct((n, LANES), F32)] * 2,
        compiler_params=_cparams(("parallel",)),
        name="rope_tables",
    )(positions.reshape(n, 1), inv)


def _dil_kernel(q0, q1, q2, k0, k1, k2, v0, v1, v2, cos_ref, sin_ref, o_ref,
                qs_ref, ks_ref, num_ref, m_ref, den_ref, *, seq):
    DH, BLK = DIL_HEAD_DIM, DIL_BLOCK
    half = DH // 2
    q_refs, k_refs, v_refs = (q0, q1, q2), (k0, k1, k2), (v0, v1, v2)
    cos = cos_ref[...]
    sin = sin_ref[...]
    first_half = (_iota2((seq, LANES), 1) % DH) < half

    def rope(x):
        swapped = jnp.where(first_half, pltpu.roll(x, LANES - half, 1), pltpu.roll(x, half, 1))
        return x * cos + swapped * sin

    for g in range(len(DIL_PAIRS)):
        qs_ref[g] = rope(q_refs[g][...]) * (DH ** -0.5)
        ks_ref[g] = rope(k_refs[g][...])

    ii = _iota2((BLK, BLK), 0)
    jj = _iota2((BLK, BLK), 1)
    i2 = _iota2((2 * BLK, BLK), 0) % BLK
    j2 = _iota2((2 * BLK, BLK), 1)
    mask_cur2 = j2 <= i2
    mask_prev2 = j2 >= i2
    assert LANES == 2 * DH and BLK == LANES
    head_lanes = [(jj // DH) == h for h in range(LANES // DH)]
    head_ones = [hl.astype(F32) for hl in head_lanes]

    for g, (window, dil) in enumerate(DIL_PAIRS):
        lsub = seq // dil
        nblk = lsub // BLK
        assert window // dil == BLK and lsub % BLK == 0
        v_ref = v_refs[g]

        def unit(u, g=g, dil=dil, v_ref=v_ref):
            r = u % dil
            n = u // dil
            rows = pl.ds(n * (BLK * dil) + r, BLK, stride=dil)
            qb = qs_ref[g, rows, :]
            kc = ks_ref[g, rows, :]
            vc = v_ref[rows, :]
            nblk = 2 if n > 0 else 1
            if nblk > 1:
                prow = pl.ds((n - 1) * (BLK * dil) + r, BLK, stride=dil)
                kp = ks_ref[g, prow, :]
                vp = v_ref[prow, :]
                has_prev = True
            nh = LANES // DH
            q2 = jnp.concatenate([jnp.where(head_lanes[h], qb, 0.0) for h in range(nh)], axis=0)
            s_c = lax.dot_general(q2, kc, (((1,), (1,)), ((), ())), preferred_element_type=F32)
            s_c = jnp.where(mask_cur2, s_c, NEG_INF)
            if nblk > 1:
                s_p = lax.dot_general(q2, kp, (((1,), (1,)), ((), ())), preferred_element_type=F32)
                s_p = jnp.where(mask_prev2, s_p, NEG_INF)
                mx = jnp.max(jnp.maximum(s_c, s_p), axis=-1, keepdims=True)
            else:
                mx = jnp.max(s_c, axis=-1, keepdims=True)
            p_c = jnp.exp(s_c - mx)
            if nblk > 1:
                p_p = jnp.exp(s_p - mx)
            num = jnp.zeros((BLK, LANES), F32)
            den = jnp.zeros((BLK, LANES), F32)
            for h in range(nh):
                hr = slice(h * BLK, (h + 1) * BLK)
                num = num + jnp.dot(p_c[hr], jnp.where(head_lanes[h], vc, 0.0), preferred_element_type=F32)
                den = den + jnp.dot(p_c[hr], head_ones[h], preferred_element_type=F32)
                if nblk > 1:
                    num = num + jnp.dot(p_p[hr], jnp.where(head_lanes[h], vp, 0.0), preferred_element_type=F32)
                    den = den + jnp.dot(p_p[hr], head_ones[h], preferred_element_type=F32)
            num_ref[g, rows, :] = num
            m_ref[g, rows, :] = jnp.where(head_lanes[0], mx[:BLK], mx[BLK:])
            den_ref[g, rows, :] = den

        for u in range(dil * nblk):
            unit(u)

    m_all = jnp.maximum(jnp.maximum(m_ref[0], m_ref[1]), m_ref[2])
    num = jnp.zeros((seq, LANES), F32)
    den = jnp.zeros((seq, LANES), F32)
    for g in range(len(DIL_PAIRS)):
        e = jnp.exp(m_ref[g] - m_all)
        num = num + e * num_ref[g]
        den = den + e * den_ref[g]
    o_ref[...] = (num / den).astype(o_ref.dtype)


def _dilated(yb, cos_t, sin_t, nbatch):
    n = yb.shape[0]
    seq = n // nbatch
    npair = DIL_HEADS_PER_GROUP * DIL_HEAD_DIM // LANES
    nblk_cols = DIL_W // LANES

    def spec(section, g):
        return pl.BlockSpec((seq, LANES), lambda b, p: (b, section * nblk_cols + g * npair + p))

    in_specs = ([spec(0, g) for g in range(3)] + [spec(1, g) for g in range(3)] + [spec(2, g) for g in range(3)]
                + [pl.BlockSpec((seq, LANES), lambda b, p: (b, 0))] * 2)
    return pl.pallas_call(
        functools.partial(_dil_kernel, seq=seq),
        grid=(nbatch, npair),
        in_specs=in_specs,
        out_specs=pl.BlockSpec((seq, LANES), lambda b, p: (b, p)),
        out_shape=jax.ShapeDtypeStruct((n, DIL_OUT), BF16),
        scratch_shapes=[pltpu.VMEM((3, seq, LANES), F32)] * 5,
        compiler_params=_cparams(("parallel", "parallel")),
        name="dilated_attn",
    )(*([yb] * 9), cos_t, sin_t)


def _merge_kernel(x_ref, g0, g1, g2, g3, ya, yb, yc, yd, pa, pb, pc, pd, o_ref):
    x = x_ref[...]
    acc = None
    for wg, y, p in ((g0, ya, pa), (g1, yb, pb), (g2, yc, pc), (g3, yd, pd)):
        gate = _sigmoid(jnp.dot(x, wg[...], preferred_element_type=F32))
        term = gate * jnp.dot(y[...], p[...], preferred_element_type=F32)
        acc = term if acc is None else acc + term
    o_ref[...] = acc.astype(o_ref.dtype)


def _merge(xb, wg, ys, ps, tm=1024, tn=512):
    n, d = xb.shape
    ncol = d // tn

    def gate_spec(br):
        return pl.BlockSpec((d, tn), lambda i, j: (0, br * ncol + j))

    in_specs = ([pl.BlockSpec((tm, d), lambda i, j: (i, 0))]
                + [gate_spec(br) for br in range(N_BRANCH)]
                + [pl.BlockSpec((tm, y.shape[1]), lambda i, j: (i, 0)) for y in ys]
                + [pl.BlockSpec((p.shape[0], tn), lambda i, j: (0, j)) for p in ps])
    return pl.pallas_call(
        _merge_kernel,
        grid=(n // tm, ncol),
        in_specs=in_specs,
        out_specs=pl.BlockSpec((tm, tn), lambda i, j: (i, j)),
        out_shape=jax.ShapeDtypeStruct((n, d), BF16),
        compiler_params=_cparams(("parallel", "arbitrary")),
        name="gated_merge",
    )(xb, wg, wg, wg, wg, *ys, *ps)


def _layernorm_rows(z, g, b):
    mu = jnp.mean(z, axis=-1, keepdims=True)
    var = jnp.mean(jnp.square(z - mu), axis=-1, keepdims=True)
    return (z - mu) * lax.rsqrt(var + LN_EPS) * g + b


def _pack_bf16_pairs(y):
    half = y.shape[1] // 2
    lo = lax.bitcast_convert_type(y[:, :half].astype(BF16).astype(F32), jnp.uint32)
    hi = lax.bitcast_convert_type(y[:, half:].astype(BF16).astype(F32), jnp.uint32)
    return (lo >> 16) | (hi & jnp.uint32(0xFFFF0000))


def _unpack_bf16_pairs(w):
    lo = lax.bitcast_convert_type(w << 16, F32).astype(BF16)
    hi = lax.bitcast_convert_type(w & jnp.uint32(0xFFFF0000), F32).astype(BF16)
    return lo, hi


def _store_token_major(ref, val):
    t, w = val.shape
    c = w // LANES
    for s in range(c):
        ref[pl.ds(s, t, stride=c), :] = val[:, s * LANES:(s + 1) * LANES]


def _load_token_major(ref, start, t, c):
    return [ref[pl.ds(start + s, t, stride=c), :] for s in range(c)]


def _outproj_ln_kernel(m_ref, w_ref, x_ref, g_ref, b_ref, o_ref, p_ref):
    h = jnp.dot(m_ref[...], w_ref[...], preferred_element_type=F32)
    y = _layernorm_rows(DN_ALPHA * x_ref[...] + h, g_ref[...], b_ref[...])
    o_ref[...] = y
    _store_token_major(p_ref, _pack_bf16_pairs(y))


def _outproj_ln(merged, w_out, x, g, b, tm=512):
    n, d = x.shape
    c = d // 2 // LANES
    return pl.pallas_call(
        _outproj_ln_kernel,
        grid=(n // tm,),
        in_specs=[pl.BlockSpec((tm, d), lambda i: (i, 0)),
                  pl.BlockSpec((d, d), lambda i: (0, 0)),
                  pl.BlockSpec((tm, d), lambda i: (i, 0)),
                  pl.BlockSpec((1, d), lambda i: (0, 0)),
                  pl.BlockSpec((1, d), lambda i: (0, 0))],
        out_specs=[pl.BlockSpec((tm, d), lambda i: (i, 0)),
                   pl.BlockSpec((tm * c, LANES), lambda i: (i, 0))],
        out_shape=[jax.ShapeDtypeStruct((n, d), F32), jax.ShapeDtypeStruct((n * c, LANES), jnp.uint32)],
        compiler_params=_cparams(("parallel",)),
        name="outproj_ln",
    )(merged, w_out, x, g.reshape(1, d), b.reshape(1, d))


def _router_kernel(x_ref, wr_ref, rb_ref, idx_ref, wts_ref, rank_ref, cnt_ref, carry_ref, *, tr):
    E, G = N_EXPERTS, N_GROUPS
    gsz = E // G

    @pl.when(pl.program_id(0) == 0)
    def _():
        carry_ref[...] = jnp.zeros_like(carry_ref)

    logits = jnp.dot(x_ref[...], wr_ref[...], precision=HI, preferred_element_type=F32)
    scores = _sigmoid(logits.T[:E, :])
    choice = scores + rb_ref[...]
    eidx = _iota2((E, tr), 0)

    c3 = choice.reshape(G, gsz, tr)
    e3 = eidx.reshape(G, gsz, tr)
    max1 = jnp.max(c3, axis=1, keepdims=True)
    first = jnp.min(jnp.where(c3 == max1, e3, E), axis=1, keepdims=True)
    max2 = jnp.max(jnp.where(e3 == first, NEG_INF, c3), axis=1, keepdims=True)
    gscore = (max1 + max2).reshape(G, tr)
    gidx = _iota2((G, tr), 0)
    gsel = jnp.zeros((G, tr), jnp.bool_)
    for _ in range(TOPK_GROUPS):
        gmax = jnp.max(gscore, axis=0, keepdims=True)
        pick = jnp.min(jnp.where(gscore == gmax, gidx, G), axis=0, keepdims=True)
        hit = gidx == pick
        gsel = jnp.logical_or(gsel, hit)
        gscore = jnp.where(hit, NEG_INF, gscore)
    esel = jnp.broadcast_to(gsel.reshape(G, 1, tr), (G, gsz, tr)).reshape(E, tr)

    cand = jnp.where(esel, choice, NEG_INF)
    sel = jnp.zeros((E, tr), F32)
    picks, weights, hits = [], [], []
    for k in range(TOP_K):
        vmax = jnp.max(cand, axis=0, keepdims=True)
        pick = jnp.min(jnp.where(cand == vmax, eidx, E), axis=0, keepdims=True)
        hit = eidx == pick
        picks.append(pick)
        hits.append(hit)
        weights.append(jnp.sum(jnp.where(hit, scores, 0.0), axis=0, keepdims=True))
        sel = jnp.where(hit, 1.0, sel)
        cand = jnp.where(hit, NEG_INF, cand)
    wsum = weights[0]
    for w in weights[1:]:
        wsum = wsum + w
    w_rows = jnp.concatenate([w / wsum * ROUTED_SCALE for w in weights], axis=0)

    before = (_iota2((tr, tr), 0) < _iota2((tr, tr), 1)).astype(BF16)
    rank = jnp.dot(sel.astype(BF16), before, preferred_element_type=F32) + carry_ref[:, 0:1]
    rank_rows = [jnp.sum(jnp.where(hits[k], rank, 0.0), axis=0, keepdims=True) for k in range(TOP_K)]
    carry_ref[...] = carry_ref[...] + jnp.sum(sel, axis=1, keepdims=True)

    idx_ref[...] = jnp.concatenate(picks, axis=0)
    rank_ref[...] = jnp.concatenate(rank_rows, axis=0).astype(jnp.int32)
    wts_ref[...] = jnp.concatenate([w_rows, jnp.zeros((LANES - TOP_K, tr), F32)], axis=0).T
    counts = _col_to_row(carry_ref[:, 0:1], E)
    cnt_ref[...] = jnp.broadcast_to(jnp.concatenate([counts, jnp.zeros((1, LANES - E), F32)], axis=1), (8, LANES))


def _router(x1, w_router, router_bias, tr=512):
    n, d = x1.shape
    wr = jnp.zeros((d, LANES), F32).at[:, :N_EXPERTS].set(w_router)
    slot_rows = pl.BlockSpec((TOP_K, tr), lambda i: (0, i))
    return pl.pallas_call(
        functools.partial(_router_kernel, tr=tr),
        grid=(n // tr,),
        in_specs=[pl.BlockSpec((tr, d), lambda i: (i, 0)),
                  pl.BlockSpec((d, LANES), lambda i: (0, 0)),
                  pl.BlockSpec((N_EXPERTS, 1), lambda i: (0, 0))],
        out_specs=[slot_rows, pl.BlockSpec((tr, LANES), lambda i: (i, 0)), slot_rows,
                   pl.BlockSpec((8, LANES), lambda i: (0, 0))],
        out_shape=[jax.ShapeDtypeStruct((TOP_K, n), jnp.int32), jax.ShapeDtypeStruct((n, LANES), F32),
                   jax.ShapeDtypeStruct((TOP_K, n), jnp.int32), jax.ShapeDtypeStruct((8, LANES), F32)],
        scratch_shapes=[pltpu.VMEM((N_EXPERTS, LANES), F32)],
        compiler_params=_cparams(("arbitrary",)),
        name="router",
    )(x1, wr, router_bias.reshape(N_EXPERTS, 1))


def _group_offsets(cnt, tm):
    padded = jnp.floor((cnt + (tm - 1)) * (1.0 / tm)) * tm
    upper = (_iota2((LANES, LANES), 0) < _iota2((LANES, LANES), 1)).astype(F32)
    offs = jnp.dot(padded, upper, precision=HI, preferred_element_type=F32)
    return padded, offs


def _pos_kernel(idx_ref, rank_ref, cnt_ref, pos_ref, *, tm):
    _, offs = _group_offsets(cnt_ref[...], tm)
    starts = _row_to_col(offs[0:1, :], LANES)[:N_EXPERTS, :]
    idx = idx_ref[...]
    eidx = _iota2((N_EXPERTS, idx.shape[1]), 0)
    rows = [jnp.sum(jnp.where(eidx == idx[k:k + 1, :], starts, 0.0), axis=0, keepdims=True)
            for k in range(TOP_K)]
    pos_ref[...] = jnp.concatenate(rows, axis=0).astype(jnp.int32) + rank_ref[...]


def _positions(idx_t, rank_t, cnt, tm, tp=2048):
    n = idx_t.shape[1]
    slot_rows = pl.BlockSpec((TOP_K, tp), lambda i: (0, i))
    return pl.pallas_call(
        functools.partial(_pos_kernel, tm=tm),
        grid=(n // tp,),
        in_specs=[slot_rows, slot_rows, pl.BlockSpec((8, LANES), lambda i: (0, 0))],
        out_specs=slot_rows,
        out_shape=jax.ShapeDtypeStruct((TOP_K, n), jnp.int32),
        compiler_params=_cparams(("parallel",)),
        name="dispatch_positions",
    )(idx_t, rank_t, cnt)


def _tile_map_kernel(cnt_ref, te_ref, *, tm, width):
    padded, offs = _group_offsets(cnt_ref[...], tm)
    ends = _row_to_col((offs + padded)[0:1, :], LANES)
    expert = _iota2((LANES, width), 0)
    start = (_iota2((LANES, width), 1) * tm).astype(F32)
    done = jnp.logical_and(ends <= start, expert < N_EXPERTS)
    te = jnp.sum(jnp.where(done, 1, 0), axis=0, keepdims=True)
    total = jnp.max(jnp.where(expert < N_EXPERTS, ends, 0.0), axis=0, keepdims=True)
    ntile = (total * (1.0 / tm)).astype(jnp.int32)
    vend = _row_to_col((offs + cnt_ref[...])[0:1, :], LANES)
    mine = jnp.sum(jnp.where(expert == te, vend, 0.0), axis=0, keepdims=True)
    valid = jnp.clip(mine - start[0:1, :], 0.0, float(tm)).astype(jnp.int32)
    row = _iota2((8, width), 0)
    te_ref[...] = jnp.where(row == 0, te, jnp.where(row == 1, ntile, jnp.where(row == 2, valid, 0)))


def _tile_map(cnt, tm, ntile_max):
    width = -(-ntile_max // LANES) * LANES
    out = pl.pallas_call(
        functools.partial(_tile_map_kernel, tm=tm, width=width),
        in_specs=[pl.BlockSpec((8, LANES), lambda: (0, 0))],
        out_specs=pl.BlockSpec((8, width), lambda: (0, 0)),
        out_shape=jax.ShapeDtypeStruct((8, width), jnp.int32),
        name="tile_map",
    )(cnt)
    return out[0, :ntile_max], out[1, :1], out[2, :ntile_max]


SC_SCATTER_ROWS = 64
SC_GATHER_ROWS = 32


def _sc_mesh():
    info = plsc.get_sparse_core_info()
    mesh = plsc.VectorSubcoreMesh(core_axis_name="c", subcore_axis_name="s")
    return mesh, info.num_cores, info.num_subcores


def _sc_scatter_rows(src3, idx3, nrows):
    n, c, _ = src3.shape
    _, nk, chunk = idx3.shape
    mesh, ncore, nsub = _sc_mesh()
    per_w = n // (ncore * nsub)
    assert chunk == SC_SCATTER_ROWS and per_w % chunk == 0

    @functools.partial(
        pl.kernel, mesh=mesh,
        out_type=jax.ShapeDtypeStruct((nrows, c, LANES), src3.dtype),
        scratch_types=[pltpu.VMEM((nk, chunk), jnp.int32),
                       pltpu.VMEM((chunk, c, LANES), src3.dtype),
                       pltpu.SemaphoreType.DMA],
    )
    def scatter_kernel(src_hbm, idx_hbm, out_hbm, idx_v, rows_v, sem):
        base = (lax.axis_index("s") * ncore + lax.axis_index("c")) * per_w

        @pl.loop(0, per_w // chunk)
        def _(j):
            off = pl.multiple_of(base + j * chunk, chunk)
            pltpu.sync_copy(idx_hbm.at[base // chunk + j], idx_v)
            pltpu.sync_copy(src_hbm.at[pl.ds(off, chunk)], rows_v)
            copies = [pltpu.async_copy(rows_v, out_hbm.at[idx_v.at[k]], sem) for k in range(nk)]
            for cp in copies:
                cp.wait()

    return scatter_kernel(src3, idx3)


def _sc_gather_rows(table3, idx2):
    _, c, _ = table3.shape
    nchunk, chunk = idx2.shape
    mesh, ncore, nsub = _sc_mesh()
    per_w = nchunk // (ncore * nsub)
    assert chunk == SC_GATHER_ROWS and per_w % 2 == 0

    @functools.partial(
        pl.kernel, mesh=mesh,
        out_type=jax.ShapeDtypeStruct((nchunk * chunk, c, LANES), table3.dtype),
        scratch_types=[pltpu.VMEM((per_w, chunk), jnp.int32),
                       pltpu.VMEM((chunk, c, LANES), table3.dtype),
                       pltpu.VMEM((chunk, c, LANES), table3.dtype),
                       pltpu.SemaphoreType.DMA, pltpu.SemaphoreType.DMA,
                       pltpu.SemaphoreType.DMA, pltpu.SemaphoreType.DMA],
    )
    def gather_kernel(table_hbm, idx_hbm, out_hbm, idx_v, rows0, rows1, g0, g1, w0, w1):
        first = (lax.axis_index("s") * ncore + lax.axis_index("c")) * per_w
        pltpu.sync_copy(idx_hbm.at[pl.ds(pl.multiple_of(first, per_w), per_w)], idx_v)

        @pl.loop(0, per_w, step=2)
        def _(j):
            ga = pltpu.async_copy(table_hbm.at[idx_v.at[j]], rows0, g0)
            gb = pltpu.async_copy(table_hbm.at[idx_v.at[j + 1]], rows1, g1)
            ga.wait()
            wa = pltpu.async_copy(rows0, out_hbm.at[pl.ds(pl.multiple_of((first + j) * chunk, chunk), chunk)], w0)
            gb.wait()
            wb = pltpu.async_copy(rows1, out_hbm.at[pl.ds(pl.multiple_of((first + j + 1) * chunk, chunk), chunk)], w1)
            wa.wait()
            wb.wait()

    return gather_kernel(table3, idx2)


FFN_LOOKAHEAD = 2
FFN_SLOTS = FFN_LOOKAHEAD + 1


def _ffn_kernel(te_ref, nt_ref, nv_ref, xs_ref, w1_ref, w3_ref, w2_ref, o_ref, w1b, w3b, w2b, ord_ref, *, tm):
    j = pl.program_id(0)
    i = j - FFN_LOOKAHEAD
    nt = nt_ref[0]

    @pl.when(j == 0)
    def _():
        for s in range(4):
            ord_ref[s] = 0

    def starts_group(t):
        tc = jnp.clip(t, 0, nt - 1)
        changed = te_ref[tc] != te_ref[jnp.maximum(tc - 1, 0)]
        return jnp.logical_or(t == 0, jnp.logical_and(jnp.logical_and(t > 0, t < nt), changed))

    for s, (w_ref, wb) in enumerate(((w1_ref, w1b), (w3_ref, w3b), (w2_ref, w2b))):
        @pl.when(starts_group(j - s))
        def _(s=s, w_ref=w_ref, wb=wb):
            wb[ord_ref[s] % FFN_SLOTS] = w_ref[0, 0].astype(BF16)
            ord_ref[s] = ord_ref[s] + 1

    @pl.when(jnp.logical_and(i >= 0, i < nt))
    def _():
        ic = jnp.maximum(i, 0)

        @pl.when(jnp.logical_and(i > 0, starts_group(i)))
        def _():
            ord_ref[3] = ord_ref[3] + 1

        slot = ord_ref[3] % FFN_SLOTS
        c = xs_ref.shape[0] // tm
        live = _iota2((tm, LANES), 0) < nv_ref[ic]
        parts = [_unpack_bf16_pairs(jnp.where(live, p, jnp.uint32(0)))
                 for p in _load_token_major(xs_ref, 0, tm, c)]
        lo = jnp.concatenate([p[0] for p in parts], axis=1)
        hi = jnp.concatenate([p[1] for p in parts], axis=1)
        half = lo.shape[1]
        h1 = (jnp.dot(lo, w1b[slot, :half, :], preferred_element_type=F32)
              + jnp.dot(hi, w1b[slot, half:, :], preferred_element_type=F32))
        h3 = (jnp.dot(lo, w3b[slot, :half, :], preferred_element_type=F32)
              + jnp.dot(hi, w3b[slot, half:, :], preferred_element_type=F32))
        h = (_silu(h1) * h3).astype(BF16)
        y = jnp.dot(h, w2b[slot], preferred_element_type=F32)
        _store_token_major(o_ref, _pack_bf16_pairs(y))

    @pl.when(i >= nt)
    def _():
        o_ref[...] = jnp.zeros_like(o_ref)


def _expert_ffn(te, nt, nv, xs, w1, w3, w2, layer, nrows, tm):
    d, f = w1.shape[2], w1.shape[3]
    c = xs.shape[0] // nrows
    ntile = nrows // tm

    def tile(j, te_r, nt_r, nv_r):
        return (jnp.clip(j - FFN_LOOKAHEAD, 0, nt_r[0] - 1), 0)

    def out_tile(j, te_r, nt_r, nv_r):
        return (jnp.maximum(j - FFN_LOOKAHEAD, 0), 0)

    def expert(delay):
        def index_map(j, te_r, nt_r, nv_r):
            return (layer, te_r[jnp.clip(j - delay, 0, nt_r[0] - 1)], 0, 0)
        return index_map

    assert FFN_LOOKAHEAD == 2
    grid_spec = pltpu.PrefetchScalarGridSpec(
        num_scalar_prefetch=3,
        grid=(ntile + FFN_LOOKAHEAD,),
        in_specs=[pl.BlockSpec((tm * c, LANES), tile),
                  pl.BlockSpec((1, 1, d, f), expert(0)),
                  pl.BlockSpec((1, 1, d, f), expert(1)),
                  pl.BlockSpec((1, 1, f, d), expert(2))],
        out_specs=pl.BlockSpec((tm * c, LANES), out_tile),
        scratch_shapes=[pltpu.VMEM((FFN_SLOTS, d, f), BF16), pltpu.VMEM((FFN_SLOTS, d, f), BF16),
                        pltpu.VMEM((FFN_SLOTS, f, d), BF16), pltpu.SMEM((4,), jnp.int32)],
    )
    return pl.pallas_call(
        functools.partial(_ffn_kernel, tm=tm),
        grid_spec=grid_spec,
        out_shape=jax.ShapeDtypeStruct((nrows * c, LANES), jnp.uint32),
        compiler_params=_cparams(("arbitrary",)),
        name="expert_ffn",
    )(te, nt, nv, xs, w1, w3, w2)


def _shared_expert_kernel(x_ref, ws1_ref, ws3_ref, ws2_ref, o_ref):
    xb = x_ref[...].astype(BF16)
    s1 = jnp.dot(xb, ws1_ref[...], preferred_element_type=F32)
    s3 = jnp.dot(xb, ws3_ref[...], preferred_element_type=F32)
    o_ref[...] = jnp.dot((_silu(s1) * s3).astype(BF16), ws2_ref[...], preferred_element_type=F32)


def _shared_expert(x1, ws1, ws3, ws2, tm=512):
    n, d = x1.shape
    sf = ws1.shape[1]
    return pl.pallas_call(
        _shared_expert_kernel,
        grid=(n // tm,),
        in_specs=[pl.BlockSpec((tm, d), lambda i: (i, 0)),
                  pl.BlockSpec((d, sf), lambda i: (0, 0)),
                  pl.BlockSpec((d, sf), lambda i: (0, 0)),
                  pl.BlockSpec((sf, d), lambda i: (0, 0))],
        out_specs=pl.BlockSpec((tm, d), lambda i: (i, 0)),
        out_shape=jax.ShapeDtypeStruct((n, d), F32),
        compiler_params=_cparams(("parallel",)),
        name="shared_expert",
    )(x1, ws1, ws3, ws2)


def _combine_kernel(*refs, tc):
    y_refs = refs[:TOP_K]
    wts_ref, x_ref, sh_ref, g_ref, b_ref, o_ref, ob_ref = refs[TOP_K:]
    c = y_refs[0].shape[0] // tc
    acc = sh_ref[...]
    wts = wts_ref[...]
    for k in range(TOP_K):
        parts = [_unpack_bf16_pairs(p) for p in _load_token_major(y_refs[k], 0, tc, c)]
        yk = jnp.concatenate([p[0] for p in parts] + [p[1] for p in parts], axis=1).astype(F32)
        acc = acc + wts[:, k:k + 1] * yk
    y = _layernorm_rows(DN_ALPHA * x_ref[...] + acc, g_ref[...], b_ref[...])
    o_ref[...] = y
    ob_ref[...] = y.astype(BF16)


def _combine(yg, wts, x1, shared, g, b, tc=256):
    n, d = x1.shape
    c = yg.shape[0] // (TOP_K * n)
    nblk = n // tc

    def slot_spec(k):
        return pl.BlockSpec((tc * c, LANES), lambda i: (k * nblk + i, 0))

    rows = pl.BlockSpec((tc, d), lambda i: (i, 0))
    return pl.pallas_call(
        functools.partial(_combine_kernel, tc=tc),
        grid=(nblk,),
        in_specs=[slot_spec(k) for k in range(TOP_K)] + [
            pl.BlockSpec((tc, LANES), lambda i: (i, 0)), rows, rows,
            pl.BlockSpec((1, d), lambda i: (0, 0)),
            pl.BlockSpec((1, d), lambda i: (0, 0))],
        out_specs=[rows, rows],
        out_shape=[jax.ShapeDtypeStruct((n, d), F32), jax.ShapeDtypeStruct((n, d), BF16)],
        compiler_params=_cparams(("parallel",)),
        name="moe_combine",
    )(*([yg] * TOP_K), wts, x1, shared, g.reshape(1, d), b.reshape(1, d))


def _cast_kernel(x_ref, o_ref):
    o_ref[...] = x_ref[...].astype(o_ref.dtype)


def _to_bf16(x, tm=1024):
    n, d = x.shape
    return pl.pallas_call(
        _cast_kernel,
        grid=(n // tm,),
        in_specs=[pl.BlockSpec((tm, d), lambda i: (i, 0))],
        out_specs=pl.BlockSpec((tm, d), lambda i: (i, 0)),
        out_shape=jax.ShapeDtypeStruct((n, d), BF16),
        compiler_params=_cparams(("parallel",)),
        name="cast_bf16",
    )(x)


FFN_TILE = 256


def _moe(x1, x1p, w_router, router_bias, w1, w3, w2, layer, ws1, ws3, ws2, g, b):
    n = x1.shape[0]
    tm = FFN_TILE
    nrows = n * TOP_K + N_EXPERTS * tm
    idx, wts, rank, cnt = _router(x1, w_router, router_bias)
    pos_t = _positions(idx, rank, cnt, tm)
    te, nt, nv = _tile_map(cnt, tm, nrows // tm)
    c = x1p.shape[0] // n
    pos_s = pos_t.reshape(TOP_K, n // SC_SCATTER_ROWS, SC_SCATTER_ROWS).transpose(1, 0, 2)
    xs = _sc_scatter_rows(x1p.reshape(n, c, LANES), pos_s, nrows)
    shared = _shared_expert(x1, ws1.astype(BF16), ws3.astype(BF16), ws2.astype(BF16))
    ys = _expert_ffn(te, nt, nv, xs.reshape(nrows * c, LANES), w1, w3, w2, layer, nrows, tm)
    yg = _sc_gather_rows(ys.reshape(nrows, c, LANES), pos_t.reshape(-1, SC_GATHER_ROWS))
    return _combine(yg.reshape(TOP_K * n * c, LANES), wts, x1, shared, g, b)


def kernel(x, positions, w_in, gla_wa2, gla_ba, gla_norm, mlstm_conv_w, mlstm_conv_b, mlstm_bi, mlstm_bf,
           mlstm_norm, sgu_ln_g, sgu_ln_b, sgu_ws, sgu_bs, w_pa, w_pb, w_pc, w_pd, w_out, ln1_g, ln1_b,
           w_router, router_bias, w1, w3, w2, ws1, ws3, ws2, ln2_g, ln2_b):
    nbatch, seq, d = x.shape
    n = nbatch * seq
    xf = x.reshape(n, d)
    xb = _to_bf16(xf)
    cos_t, sin_t = _rope_tables(positions)
    for l in range(DEPTH):
        wl = w_in[l]
        w_a = wl[:, _OFF_A:_OFF_A + _W_A].astype(BF16)
        w_b = wl[:, _OFF_B:_OFF_B + _W_B].astype(BF16)
        w_c = wl[:, _OFF_C:_OFF_C + _W_C].astype(BF16)
        w_d = wl[:, _OFF_D:_OFF_D + _W_D].astype(BF16)
        w_g = wl[:, _OFF_G:_OFF_G + _W_G].astype(BF16)
        w_s = jnp.concatenate(
            [wl[:, _OFF_LOW:_OFF_LOW + GLA_RANK], wl[:, _OFF_IF:_OFF_IF + 2 * MLSTM_HEADS],
             jnp.zeros((d, LANES - GLA_RANK - 2 * MLSTM_HEADS), F32)], axis=1).astype(BF16)
        y_a = _matmul(xb, w_a, 2048, 768)
        y_b = _matmul(xb, w_b, 2048, 768)
        y_c = _matmul(xb, w_c, 2048, 768)
        y_d = _matmul(xb, w_d, 2048, 1024)
        y_s = _matmul(xb, w_s, 2048, LANES)
        sm3 = y_s.reshape(nbatch, seq, LANES)
        o_a, o_d, o_c = _mixers_acd(
            y_a.reshape(nbatch, seq, _W_A), y_d.reshape(nbatch, seq, _W_D), sm3, y_c, gla_wa2[l], gla_ba[l],
            gla_norm[l], mlstm_conv_w[l], mlstm_conv_b[l], mlstm_bi[l], mlstm_bf[l], mlstm_norm[l],
            sgu_ln_g[l], sgu_ln_b[l], sgu_ws[l], sgu_bs[l])
        o_a = o_a.reshape(n, GLA_V)
        o_d = o_d.reshape(n, ML_W)
        o_b = _dilated(y_b, cos_t, sin_t, nbatch)
        merged = _merge(xb, w_g, (o_a, o_b, o_c, o_d),
                        (w_pa[l].astype(BF16), w_pb[l].astype(BF16), w_pc[l].astype(BF16), w_pd[l].astype(BF16)))
        x1, x1p = _outproj_ln(merged, w_out[l].astype(BF16), xf, ln1_g[l], ln1_b[l])
        xf, xb = _moe(x1, x1p, w_router[l], router_bias[l], w1, w3, w2, l, ws1[l], ws3[l], ws2[l],
                      ln2_g[l], ln2_b[l])
    return xf.reshape(nbatch, seq, d)
```

```python
import functools

import jax
import jax.numpy as jnp
from jax import lax
from jax.experimental import pallas as pl
from jax.experimental.pallas import tpu as pltpu
from jax.experimental.pallas import tpu_sc as plsc

D_MODEL = 2048
DEPTH = 2

GLA_HEADS = 4
GLA_DK = 64
GLA_DV = 128
GLA_RANK = 16
GLA_TAU = 16.0
GLA_CHUNK = 64
GLA_QK = GLA_HEADS * GLA_DK
GLA_V = GLA_HEADS * GLA_DV

DIL_PAIRS = ((128, 1), (512, 4), (2048, 16))
DIL_HEADS_PER_GROUP = 4
DIL_HEAD_DIM = 64
DIL_HEADS = len(DIL_PAIRS) * DIL_HEADS_PER_GROUP
DIL_W = DIL_HEADS * DIL_HEAD_DIM
DIL_OUT = DIL_HEADS_PER_GROUP * DIL_HEAD_DIM
DIL_BLOCK = 128
ROPE_THETA = 10000.0

SGU_CHUNK = 128
SGU_GROUPS = 6
SGU_GROUP_CH = 128
SGU_W = SGU_GROUPS * SGU_GROUP_CH

MLSTM_HEADS = 4
MLSTM_HEAD_DIM = 128
MLSTM_CHUNK = 64
MLSTM_CONV = 4
ML_W = MLSTM_HEADS * MLSTM_HEAD_DIM

N_EXPERTS = 64
TOP_K = 8
N_GROUPS = 8
TOPK_GROUPS = 4
EXPERT_FF = 512
SHARED_FF = 512
ROUTED_SCALE = 2.5

N_BRANCH = 4
DN_ALPHA = (2 * DEPTH) ** 0.25
LN_EPS = 1e-5

_OFF_A = 0
_W_A = 2 * GLA_QK + 2 * GLA_V
_OFF_LOW = _OFF_A + _W_A
_OFF_B = _OFF_LOW + GLA_RANK
_W_B = 3 * DIL_W
_OFF_C = _OFF_B + _W_B
_W_C = 2 * SGU_W
_OFF_D = _OFF_C + _W_C
_W_D = 4 * ML_W
_OFF_IF = _OFF_D + _W_D
_OFF_G = _OFF_IF + 2 * MLSTM_HEADS
_W_G = N_BRANCH * D_MODEL

LANES = 128
VMEM_LIMIT = 56 * 1024 * 1024

_SM_LOW = 0
_SM_I = GLA_RANK
_SM_F = GLA_RANK + MLSTM_HEADS

HI = lax.Precision.HIGHEST
F32 = jnp.float32
BF16 = jnp.bfloat16
NEG_INF = float("-inf")


def _cparams(sem):
    return pltpu.CompilerParams(dimension_semantics=sem, vmem_limit_bytes=VMEM_LIMIT)


def _log_sigmoid(x):
    return jnp.minimum(x, 0.0) - jnp.log1p(jnp.exp(-jnp.abs(x)))


def _sigmoid(x):
    return 1.0 / (1.0 + jnp.exp(-x))


def _silu(x):
    return x * _sigmoid(x)


def _iota2(shape, dim):
    return lax.broadcasted_iota(jnp.int32, shape, dim)


def _col_to_row(col, n):
    eye = _iota2((n, n), 0) == _iota2((n, n), 1)
    return jnp.sum(jnp.where(eye, col, 0.0), axis=0, keepdims=True)


def _row_to_col(row, n):
    eye = _iota2((n, n), 0) == _iota2((n, n), 1)
    return jnp.sum(jnp.where(eye, row, 0.0), axis=1, keepdims=True)


def _mm_kernel(x_ref, w_ref, o_ref):
    o_ref[...] = jnp.dot(x_ref[...], w_ref[...], preferred_element_type=F32).astype(o_ref.dtype)


def _matmul(x, w, tm, tn, out_dtype=F32):
    n, k = x.shape
    m = w.shape[1]
    return pl.pallas_call(
        _mm_kernel,
        grid=(n // tm, m // tn),
        in_specs=[pl.BlockSpec((tm, k), lambda i, j: (i, 0)),
                  pl.BlockSpec((k, tn), lambda i, j: (0, j))],
        out_specs=pl.BlockSpec((tm, tn), lambda i, j: (i, j)),
        out_shape=jax.ShapeDtypeStruct((n, m), out_dtype),
        compiler_params=_cparams(("parallel", "arbitrary")),
        name="in_proj",
    )(x, w)


MLSTM_HALO = 8


def _gla_chunk_stacked(y_ref, sm_ref, wa2_ref, ba_ref, g_ref, o_ref, state_ref, *, nb):
    L, H, DK, DV = GLA_CHUNK, GLA_HEADS, GLA_DK, GLA_DV
    R = nb * L
    ri = _iota2((R, R), 0)
    ci = _iota2((R, R), 1)
    causal = jnp.logical_and((ri // L) == (ci // L), ri >= ci)
    row_of = [slice(b * L, (b + 1) * L) for b in range(nb)]
    y = jnp.concatenate([y_ref[b] for b in range(nb)], axis=0)
    a_low = jnp.concatenate([sm_ref[b] for b in range(nb)], axis=0)[:, _SM_LOW:_SM_LOW + GLA_RANK]
    glog = jnp.dot(a_low, wa2_ref[...], preferred_element_type=F32) + ba_ref[...]
    g = _log_sigmoid(glog) * (1.0 / GLA_TAU)
    bc = jnp.dot(causal.astype(F32), g, precision=HI, preferred_element_type=F32)
    outs = []
    for h in range(H):
        q = y[:, h * DK:(h + 1) * DK] * (DK ** -0.5)
        k = y[:, GLA_QK + h * DK:GLA_QK + (h + 1) * DK]
        v = y[:, 2 * GLA_QK + h * DV:2 * GLA_QK + (h + 1) * DV]
        bh = bc[:, h * DK:(h + 1) * DK]
        qe = q * jnp.exp(bh)
        ke = k * jnp.exp(-bh)
        att = lax.dot_general(qe, ke, (((1,), (1,)), ((), ())), preferred_element_type=F32)
        att = jnp.where(causal, att, 0.0)
        sts = [state_ref[b * H + h] for b in range(nb)]
        o = jnp.dot(att, v, preferred_element_type=F32) + jnp.concatenate(
            [jnp.dot(qe[row_of[b]], sts[b], preferred_element_type=F32) for b in range(nb)], axis=0)
        b_lasts = [bh[(b + 1) * L - 1:(b + 1) * L, :] for b in range(nb)]
        b_last = jnp.concatenate([jnp.broadcast_to(bl, (L, DK)) for bl in b_lasts], axis=0)
        kd = k * jnp.exp(b_last - bh)
        for b in range(nb):
            decay = _row_to_col(jnp.exp(b_lasts[b]), DK)
            state_ref[b * H + h] = decay * sts[b] + lax.dot_general(
                kd[row_of[b]], v[row_of[b]], (((0,), (0,)), ((), ())), preferred_element_type=F32)
        o = o * lax.rsqrt(jnp.mean(o * o, axis=-1, keepdims=True) + LN_EPS)
        outs.append(o)
    o_all = jnp.concatenate(outs, axis=-1) * g_ref[...]
    r = y[:, 2 * GLA_QK + GLA_V:2 * GLA_QK + 2 * GLA_V]
    o_all = (o_all * _silu(r)).astype(o_ref.dtype)
    for b in range(nb):
        o_ref[b] = o_all[row_of[b]]


def _mlstm_chunk_stacked(y_ref, sm_ref, cw_ref, cb_ref, gb_ref, g_ref, o_ref, c_ref, n_ref, m_ref, tail_ref, *, nb):
    L, H, DH = MLSTM_CHUNK, MLSTM_HEADS, MLSTM_HEAD_DIM
    W2 = 2 * ML_W
    HALO = MLSTM_HALO
    R = nb * L
    ri = _iota2((R, R), 0)
    ci = _iota2((R, R), 1)
    same_batch = (ri // L) == (ci // L)
    causal = jnp.logical_and(same_batch, ri >= ci)
    ys, qks = [], []
    for b in range(nb):
        y = y_ref[b]
        qk_raw = y[:, :W2]
        ext = jnp.concatenate([tail_ref[b], qk_raw], axis=0)
        tail_ref[b] = qk_raw[L - HALO:, :]
        conv = cb_ref[...]
        for j in range(MLSTM_CONV):
            s0 = HALO - (MLSTM_CONV - 1) + j
            conv = conv + cw_ref[j:j + 1, :] * ext[s0:s0 + L, :]
        ys.append(y)
        qks.append(_silu(conv))
    y_all = jnp.concatenate(ys, axis=0)
    qk = jnp.concatenate(qks, axis=0)
    gates = jnp.concatenate([sm_ref[b] for b in range(nb)], axis=0) + gb_ref[...]
    bcum = jnp.dot(causal.astype(F32), _log_sigmoid(gates), precision=HI, preferred_element_type=F32)
    row_of = [slice(b * L, (b + 1) * L) for b in range(nb)]
    outs = []
    for h in range(H):
        q = qk[:, h * DH:(h + 1) * DH]
        k = qk[:, ML_W + h * DH:ML_W + (h + 1) * DH] * (DH ** -0.5)
        v = y_all[:, W2 + h * DH:W2 + (h + 1) * DH]
        b_col = bcum[:, _SM_F + h:_SM_F + h + 1]
        li_col = gates[:, _SM_I + h:_SM_I + h + 1]
        b_row = _col_to_row(b_col, R)
        li_row = _col_to_row(li_col, R)
        m_prevs = [m_ref[b * H + h][:, 0:1] for b in range(nb)]
        m_prev = jnp.concatenate([jnp.broadcast_to(mp, (L, 1)) for mp in m_prevs], axis=0)
        dmat = jnp.where(causal, b_col - b_row + li_row, NEG_INF)
        inter = b_col + m_prev
        m_t = jnp.maximum(inter, jnp.max(dmat, axis=-1, keepdims=True))
        w_in = jnp.exp(dmat - m_t)
        w_st = jnp.exp(inter - m_t)
        sc = lax.dot_general(q, k, (((1,), (1,)), ((), ())), preferred_element_type=F32) * w_in
        csts = [c_ref[b * H + h] for b in range(nb)]
        nsts = [n_ref[b * H + h] for b in range(nb)]
        qc = jnp.concatenate([jnp.dot(q[row_of[b]], csts[b], preferred_element_type=F32) for b in range(nb)], axis=0)
        n_rows = jnp.concatenate([jnp.broadcast_to(nsts[b], (L, DH)) for b in range(nb)], axis=0)
        num = jnp.dot(sc, v, preferred_element_type=F32) + w_st * qc
        den = jnp.sum(sc, axis=-1, keepdims=True) + w_st * jnp.sum(q * n_rows, axis=-1, keepdims=True)
        hh = num / jnp.maximum(jnp.abs(den), jnp.exp(-m_t))
        b_lasts = [b_col[(b + 1) * L - 1:(b + 1) * L, :] for b in range(nb)]
        b_last = jnp.concatenate([jnp.broadcast_to(bl, (L, 1)) for bl in b_lasts], axis=0)
        dec = b_last - b_col + li_col
        m_news = [jnp.maximum(b_lasts[b] + m_prevs[b], jnp.max(dec[row_of[b]], axis=0, keepdims=True))
                  for b in range(nb)]
        m_new = jnp.concatenate([jnp.broadcast_to(mn, (L, 1)) for mn in m_news], axis=0)
        wkk = jnp.exp(dec - m_new) * k
        for b in range(nb):
            keep = jnp.exp(b_lasts[b] + m_prevs[b] - m_news[b])
            c_ref[b * H + h] = keep * csts[b] + lax.dot_general(
                wkk[row_of[b]], v[row_of[b]], (((0,), (0,)), ((), ())), preferred_element_type=F32)
            n_ref[b * H + h] = keep * nsts[b] + jnp.sum(wkk[row_of[b]], axis=0, keepdims=True)
            m_ref[b * H + h] = jnp.broadcast_to(m_news[b], (1, LANES))
        o_pre = y_all[:, W2 + ML_W + h * DH:W2 + ML_W + (h + 1) * DH]
        hh = _sigmoid(o_pre) * hh
        hh = hh * lax.rsqrt(jnp.mean(hh * hh, axis=-1, keepdims=True) + LN_EPS)
        outs.append(hh)
    o_all = (jnp.concatenate(outs, axis=-1) * g_ref[...]).astype(o_ref.dtype)
    for b in range(nb):
        o_ref[b] = o_all[row_of[b]]


def _mixers_kernel(ya_ref, yd_ref, sm_ref, yc_ref, wa2_ref, ba_ref, ga_ref, cw_ref, cb_ref, gb_ref, gd_ref,
                   lg_ref, lb_ref, ws_ref, bst_ref, oa_ref, od_ref, oc_ref,
                   state_ref, c_ref, n_ref, m_ref, tail_ref, *, nb, sgu_chunks):
    @pl.when(pl.program_id(0) == 0)
    def _():
        for ref in (state_ref, c_ref, n_ref, m_ref, tail_ref):
            ref[...] = jnp.zeros_like(ref)

    _gla_chunk_stacked(ya_ref, sm_ref, wa2_ref, ba_ref, ga_ref, oa_ref, state_ref, nb=nb)
    _mlstm_chunk_stacked(yd_ref, sm_ref, cw_ref, cb_ref, gb_ref, gd_ref, od_ref, c_ref, n_ref, m_ref, tail_ref,
                         nb=nb)
    _sgu_kernel(yc_ref, lg_ref, lb_ref, ws_ref, bst_ref, oc_ref, nchunk=sgu_chunks)


def _mixers_acd(ya3, yd3, sm3, yc, wa2, ba, gla_norm, conv_w, conv_b, b_i, b_f, mlstm_norm,
                sgu_ln_g, sgu_ln_b, sgu_ws, sgu_bs):
    nb, s, _ = ya3.shape
    assert GLA_CHUNK == MLSTM_CHUNK
    L = GLA_CHUNK
    nstep = s // L
    n = yc.shape[0]
    sgu_rows = n // nstep
    assert sgu_rows % SGU_CHUNK == 0 and sgu_rows * nstep == n
    gate_bias = jnp.zeros((1, LANES), F32)
    gate_bias = gate_bias.at[0, _SM_I:_SM_I + MLSTM_HEADS].set(b_i).at[0, _SM_F:_SM_F + MLSTM_HEADS].set(b_f)
    bst = jnp.zeros((SGU_CHUNK, LANES), F32).at[:, :SGU_GROUPS].set(sgu_bs.T)

    def chunk(width):
        return pl.BlockSpec((nb, L, width), lambda i: (0, i, 0))

    def whole(*shape):
        return pl.BlockSpec(shape, lambda i: (0,) * len(shape))

    return pl.pallas_call(
        functools.partial(_mixers_kernel, nb=nb, sgu_chunks=sgu_rows // SGU_CHUNK),
        grid=(nstep,),
        in_specs=[chunk(_W_A), chunk(_W_D), chunk(LANES), pl.BlockSpec((sgu_rows, _W_C), lambda i: (i, 0)),
                  whole(GLA_RANK, GLA_QK), whole(1, GLA_QK), whole(1, GLA_V),
                  whole(MLSTM_CONV, 2 * ML_W), whole(1, 2 * ML_W), whole(1, LANES), whole(1, ML_W),
                  whole(1, SGU_W), whole(1, SGU_W), whole(SGU_GROUPS, SGU_CHUNK, SGU_CHUNK),
                  whole(SGU_CHUNK, LANES)],
        out_specs=[chunk(GLA_V), chunk(ML_W), pl.BlockSpec((sgu_rows, SGU_W), lambda i: (i, 0))],
        out_shape=[jax.ShapeDtypeStruct((nb, s, GLA_V), BF16), jax.ShapeDtypeStruct((nb, s, ML_W), BF16),
                   jax.ShapeDtypeStruct((n, SGU_W), BF16)],
        scratch_shapes=[pltpu.VMEM((nb * GLA_HEADS, GLA_DK, GLA_DV), F32),
                        pltpu.VMEM((nb * MLSTM_HEADS, MLSTM_HEAD_DIM, MLSTM_HEAD_DIM), F32),
                        pltpu.VMEM((nb * MLSTM_HEADS, 1, MLSTM_HEAD_DIM), F32),
                        pltpu.VMEM((nb * MLSTM_HEADS, 1, LANES), F32),
                        pltpu.VMEM((nb, MLSTM_HALO, 2 * ML_W), F32)],
        compiler_params=_cparams(("arbitrary",)),
        name="gla_mlstm_sgu",
    )(ya3, yd3, sm3, yc, wa2, ba.reshape(1, GLA_QK), gla_norm.reshape(1, GLA_V),
      conv_w, conv_b.reshape(1, 2 * ML_W), gate_bias, mlstm_norm.reshape(1, ML_W),
      sgu_ln_g.reshape(1, SGU_W), sgu_ln_b.reshape(1, SGU_W), sgu_ws, bst)


def _gelu(x):
    return 0.5 * x * (1.0 + lax.erf(x * (0.5 ** 0.5)))


def _sgu_kernel(y_ref, lg_ref, lb_ref, ws_ref, bst_ref, o_ref, *, nchunk):
    C, G, GC = SGU_CHUNK, SGU_GROUPS, SGU_GROUP_CH
    y = y_ref[...]
    zu = _gelu(y[:, :SGU_W])
    zv = _gelu(y[:, SGU_W:])
    mu = jnp.mean(zv, axis=-1, keepdims=True)
    var = jnp.mean(jnp.square(zv - mu), axis=-1, keepdims=True)
    vn = (zv - mu) * lax.rsqrt(var + LN_EPS) * lg_ref[...] + lb_ref[...]
    tril = _iota2((C, C), 0) >= _iota2((C, C), 1)
    for g in range(G):
        wc = jnp.where(tril, ws_ref[g], 0.0)
        bias = bst_ref[:, g:g + 1]
        for c in range(nchunk):
            rows = slice(c * C, (c + 1) * C)
            cols = slice(g * GC, (g + 1) * GC)
            s = jnp.dot(wc, vn[rows, cols], preferred_element_type=F32) + bias
            o_ref[rows, cols] = (zu[rows, cols] * s).astype(o_ref.dtype)


def _rope_table_kernel(pos_ref, inv_ref, cos_ref, sin_ref):
    ang = pos_ref[...].astype(F32) * inv_ref[...]
    half = DIL_HEAD_DIM // 2
    sign = jnp.where((_iota2(ang.shape, 1) % DIL_HEAD_DIM) < half, -1.0, 1.0)
    cos_ref[...] = jnp.cos(ang)
    sin_ref[...] = jnp.sin(ang) * sign


def _rope_tables(positions):
    n = positions.size
    half = DIL_HEAD_DIM // 2
    inv = ROPE_THETA ** (-jnp.arange(half, dtype=F32) * 2.0 / DIL_HEAD_DIM)
    inv = jnp.tile(inv, LANES // half).reshape(1, LANES)
    t = 1024
    return pl.pallas_call(
        _rope_table_kernel,
        grid=(n // t,),
        in_specs=[pl.BlockSpec((t, 1), lambda i: (i, 0)),
                  pl.BlockSpec((1, LANES), lambda i: (0, 0))],
        out_specs=[pl.BlockSpec((t, LANES), lambda i: (i, 0)),
                   pl.BlockSpec((t, LANES), lambda i: (i, 0))],
        out_shape=[jax.ShapeDtypeStruct((n, LANES), F32)] * 2,
        compiler_params=_cparams(("parallel",)),
        name="rope_tables",
    )(positions.reshape(n, 1), inv)


def _dil_kernel(q0, q1, q2, k0, k1, k2, v0, v1, v2, cos_ref, sin_ref, o_ref,
                qs_ref, ks_ref, num_ref, m_ref, den_ref, *, seq):
    DH, BLK = DIL_HEAD_DIM, DIL_BLOCK
    half = DH // 2
    q_refs, k_refs, v_refs = (q0, q1, q2), (k0, k1, k2), (v0, v1, v2)
    cos = cos_ref[...]
    sin = sin_ref[...]
    first_half = (_iota2((seq, LANES), 1) % DH) < half

    def rope(x):
        swapped = jnp.where(first_half, pltpu.roll(x, LANES - half, 1), pltpu.roll(x, half, 1))
        return x * cos + swapped * sin

    for g in range(len(DIL_PAIRS)):
        qs_ref[g] = rope(q_refs[g][...]) * (DH ** -0.5)
        ks_ref[g] = rope(k_refs[g][...])

    ii = _iota2((BLK, BLK), 0)
    jj = _iota2((BLK, BLK), 1)
    i2 = _iota2((2 * BLK, BLK), 0) % BLK
    j2 = _iota2((2 * BLK, BLK), 1)
    mask_cur2 = j2 <= i2
    mask_prev2 = j2 >= i2
    assert LANES == 2 * DH and BLK == LANES
    head_lanes = [(jj // DH) == h for h in range(LANES // DH)]
    head_ones = [hl.astype(F32) for hl in head_lanes]

    for g, (window, dil) in enumerate(DIL_PAIRS):
        lsub = seq // dil
        nblk = lsub // BLK
        assert window // dil == BLK and lsub % BLK == 0
        v_ref = v_refs[g]

        def unit(u, g=g, dil=dil, v_ref=v_ref):
            r = u % dil
            n = u // dil
            rows = pl.ds(n * (BLK * dil) + r, BLK, stride=dil)
            qb = qs_ref[g, rows, :]
            kc = ks_ref[g, rows, :]
            vc = v_ref[rows, :]
            nblk = 2 if n > 0 else 1
            if nblk > 1:
                prow = pl.ds((n - 1) * (BLK * dil) + r, BLK, stride=dil)
                kp = ks_ref[g, prow, :]
                vp = v_ref[prow, :]
                has_prev = True
            nh = LANES // DH
            q2 = jnp.concatenate([jnp.where(head_lanes[h], qb, 0.0) for h in range(nh)], axis=0)
            s_c = lax.dot_general(q2, kc, (((1,), (1,)), ((), ())), preferred_element_type=F32)
            s_c = jnp.where(mask_cur2, s_c, NEG_INF)
            if nblk > 1:
                s_p = lax.dot_general(q2, kp, (((1,), (1,)), ((), ())), preferred_element_type=F32)
                s_p = jnp.where(mask_prev2, s_p, NEG_INF)
                mx = jnp.max(jnp.maximum(s_c, s_p), axis=-1, keepdims=True)
            else:
                mx = jnp.max(s_c, axis=-1, keepdims=True)
            p_c = jnp.exp(s_c - mx)
            if nblk > 1:
                p_p = jnp.exp(s_p - mx)
            num = jnp.zeros((BLK, LANES), F32)
            den = jnp.zeros((BLK, LANES), F32)
            for h in range(nh):
                hr = slice(h * BLK, (h + 1) * BLK)
                num = num + jnp.dot(p_c[hr], jnp.where(head_lanes[h], vc, 0.0), preferred_element_type=F32)
                den = den + jnp.dot(p_c[hr], head_ones[h], preferred_element_type=F32)
                if nblk > 1:
                    num = num + jnp.dot(p_p[hr], jnp.where(head_lanes[h], vp, 0.0), preferred_element_type=F32)
                    den = den + jnp.dot(p_p[hr], head_ones[h], preferred_element_type=F32)
            num_ref[g, rows, :] = num
            m_ref[g, rows, :] = jnp.where(head_lanes[0], mx[:BLK], mx[BLK:])
            den_ref[g, rows, :] = den

        for u in range(dil * nblk):
            unit(u)

    m_all = jnp.maximum(jnp.maximum(m_ref[0], m_ref[1]), m_ref[2])
    num = jnp.zeros((seq, LANES), F32)
    den = jnp.zeros((seq, LANES), F32)
    for g in range(len(DIL_PAIRS)):
        e = jnp.exp(m_ref[g] - m_all)
        num = num + e * num_ref[g]
        den = den + e * den_ref[g]
    o_ref[...] = (num / den).astype(o_ref.dtype)


def _dilated(yb, cos_t, sin_t, nbatch):
    n = yb.shape[0]
    seq = n // nbatch
    npair = DIL_HEADS_PER_GROUP * DIL_HEAD_DIM // LANES
    nblk_cols = DIL_W // LANES

    def spec(section, g):
        return pl.BlockSpec((seq, LANES), lambda b, p: (b, section * nblk_cols + g * npair + p))

    in_specs = ([spec(0, g) for g in range(3)] + [spec(1, g) for g in range(3)] + [spec(2, g) for g in range(3)]
                + [pl.BlockSpec((seq, LANES), lambda b, p: (b, 0))] * 2)
    return pl.pallas_call(
        functools.partial(_dil_kernel, seq=seq),
        grid=(nbatch, npair),
        in_specs=in_specs,
        out_specs=pl.BlockSpec((seq, LANES), lambda b, p: (b, p)),
        out_shape=jax.ShapeDtypeStruct((n, DIL_OUT), BF16),
        scratch_shapes=[pltpu.VMEM((3, seq, LANES), F32)] * 5,
        compiler_params=_cparams(("parallel", "parallel")),
        name="dilated_attn",
    )(*([yb] * 9), cos_t, sin_t)


def _merge_kernel(x_ref, g0, g1, g2, g3, ya, yb, yc, yd, pa, pb, pc, pd, o_ref):
    x = x_ref[...]
    acc = None
    for wg, y, p in ((g0, ya, pa), (g1, yb, pb), (g2, yc, pc), (g3, yd, pd)):
        gate = _sigmoid(jnp.dot(x, wg[...], preferred_element_type=F32))
        term = gate * jnp.dot(y[...], p[...], preferred_element_type=F32)
        acc = term if acc is None else acc + term
    o_ref[...] = acc.astype(o_ref.dtype)


def _merge(xb, wg, ys, ps, tm=1024, tn=512):
    n, d = xb.shape
    ncol = d // tn

    def gate_spec(br):
        return pl.BlockSpec((d, tn), lambda i, j: (0, br * ncol + j))

    in_specs = ([pl.BlockSpec((tm, d), lambda i, j: (i, 0))]
                + [gate_spec(br) for br in range(N_BRANCH)]
                + [pl.BlockSpec((tm, y.shape[1]), lambda i, j: (i, 0)) for y in ys]
                + [pl.BlockSpec((p.shape[0], tn), lambda i, j: (0, j)) for p in ps])
    return pl.pallas_call(
        _merge_kernel,
        grid=(n // tm, ncol),
        in_specs=in_specs,
        out_specs=pl.BlockSpec((tm, tn), lambda i, j: (i, j)),
        out_shape=jax.ShapeDtypeStruct((n, d), BF16),
        compiler_params=_cparams(("parallel", "arbitrary")),
        name="gated_merge",
    )(xb, wg, wg, wg, wg, *ys, *ps)


def _layernorm_rows(z, g, b):
    mu = jnp.mean(z, axis=-1, keepdims=True)
    var = jnp.mean(jnp.square(z - mu), axis=-1, keepdims=True)
    return (z - mu) * lax.rsqrt(var + LN_EPS) * g + b


def _pack_bf16_pairs(y):
    half = y.shape[1] // 2
    lo = lax.bitcast_convert_type(y[:, :half].astype(BF16).astype(F32), jnp.uint32)
    hi = lax.bitcast_convert_type(y[:, half:].astype(BF16).astype(F32), jnp.uint32)
    return (lo >> 16) | (hi & jnp.uint32(0xFFFF0000))


def _unpack_bf16_pairs(w):
    lo = lax.bitcast_convert_type(w << 16, F32).astype(BF16)
    hi = lax.bitcast_convert_type(w & jnp.uint32(0xFFFF0000), F32).astype(BF16)
    return lo, hi


def _store_token_major(ref, val):
    t, w = val.shape
    c = w // LANES
    for s in range(c):
        ref[pl.ds(s, t, stride=c), :] = val[:, s * LANES:(s + 1) * LANES]


def _load_token_major(ref, start, t, c):
    return [ref[pl.ds(start + s, t, stride=c), :] for s in range(c)]


def _outproj_ln_kernel(m_ref, w_ref, x_ref, g_ref, b_ref, o_ref, p_ref):
    h = jnp.dot(m_ref[...], w_ref[...], preferred_element_type=F32)
    y = _layernorm_rows(DN_ALPHA * x_ref[...] + h, g_ref[...], b_ref[...])
    o_ref[...] = y
    _store_token_major(p_ref, _pack_bf16_pairs(y))


def _outproj_ln(merged, w_out, x, g, b, tm=512):
    n, d = x.shape
    c = d // 2 // LANES
    return pl.pallas_call(
        _outproj_ln_kernel,
        grid=(n // tm,),
        in_specs=[pl.BlockSpec((tm, d), lambda i: (i, 0)),
                  pl.BlockSpec((d, d), lambda i: (0, 0)),
                  pl.BlockSpec((tm, d), lambda i: (i, 0)),
                  pl.BlockSpec((1, d), lambda i: (0, 0)),
                  pl.BlockSpec((1, d), lambda i: (0, 0))],
        out_specs=[pl.BlockSpec((tm, d), lambda i: (i, 0)),
                   pl.BlockSpec((tm * c, LANES), lambda i: (i, 0))],
        out_shape=[jax.ShapeDtypeStruct((n, d), F32), jax.ShapeDtypeStruct((n * c, LANES), jnp.uint32)],
        compiler_params=_cparams(("parallel",)),
        name="outproj_ln",
    )(merged, w_out, x, g.reshape(1, d), b.reshape(1, d))


def _router_kernel(x_ref, wr_ref, rb_ref, idx_ref, wts_ref, rank_ref, cnt_ref, carry_ref, *, tr):
    E, G = N_EXPERTS, N_GROUPS
    gsz = E // G

    @pl.when(pl.program_id(0) == 0)
    def _():
        carry_ref[...] = jnp.zeros_like(carry_ref)

    logits = jnp.dot(x_ref[...], wr_ref[...], precision=HI, preferred_element_type=F32)
    scores = _sigmoid(logits.T[:E, :])
    choice = scores + rb_ref[...]
    eidx = _iota2((E, tr), 0)

    c3 = choice.reshape(G, gsz, tr)
    e3 = eidx.reshape(G, gsz, tr)
    max1 = jnp.max(c3, axis=1, keepdims=True)
    first = jnp.min(jnp.where(c3 == max1, e3, E), axis=1, keepdims=True)
    max2 = jnp.max(jnp.where(e3 == first, NEG_INF, c3), axis=1, keepdims=True)
    gscore = (max1 + max2).reshape(G, tr)
    gidx = _iota2((G, tr), 0)
    gsel = jnp.zeros((G, tr), jnp.bool_)
    for _ in range(TOPK_GROUPS):
        gmax = jnp.max(gscore, axis=0, keepdims=True)
        pick = jnp.min(jnp.where(gscore == gmax, gidx, G), axis=0, keepdims=True)
        hit = gidx == pick
        gsel = jnp.logical_or(gsel, hit)
        gscore = jnp.where(hit, NEG_INF, gscore)
    esel = jnp.broadcast_to(gsel.reshape(G, 1, tr), (G, gsz, tr)).reshape(E, tr)

    cand = jnp.where(esel, choice, NEG_INF)
    sel = jnp.zeros((E, tr), F32)
    picks, weights, hits = [], [], []
    for k in range(TOP_K):
        vmax = jnp.max(cand, axis=0, keepdims=True)
        pick = jnp.min(jnp.where(cand == vmax, eidx, E), axis=0, keepdims=True)
        hit = eidx == pick
        picks.append(pick)
        hits.append(hit)
        weights.append(jnp.sum(jnp.where(hit, scores, 0.0), axis=0, keepdims=True))
        sel = jnp.where(hit, 1.0, sel)
        cand = jnp.where(hit, NEG_INF, cand)
    wsum = weights[0]
    for w in weights[1:]:
        wsum = wsum + w
    w_rows = jnp.concatenate([w / wsum * ROUTED_SCALE for w in weights], axis=0)

    before = (_iota2((tr, tr), 0) < _iota2((tr, tr), 1)).astype(BF16)
    rank = jnp.dot(sel.astype(BF16), before, preferred_element_type=F32) + carry_ref[:, 0:1]
    rank_rows = [jnp.sum(jnp.where(hits[k], rank, 0.0), axis=0, keepdims=True) for k in range(TOP_K)]
    carry_ref[...] = carry_ref[...] + jnp.sum(sel, axis=1, keepdims=True)

    idx_ref[...] = jnp.concatenate(picks, axis=0)
    rank_ref[...] = jnp.concatenate(rank_rows, axis=0).astype(jnp.int32)
    wts_ref[...] = jnp.concatenate([w_rows, jnp.zeros((LANES - TOP_K, tr), F32)], axis=0).T
    counts = _col_to_row(carry_ref[:, 0:1], E)
    cnt_ref[...] = jnp.broadcast_to(jnp.concatenate([counts, jnp.zeros((1, LANES - E), F32)], axis=1), (8, LANES))


def _router(x1, w_router, router_bias, tr=512):
    n, d = x1.shape
    wr = jnp.zeros((d, LANES), F32).at[:, :N_EXPERTS].set(w_router)
    slot_rows = pl.BlockSpec((TOP_K, tr), lambda i: (0, i))
    return pl.pallas_call(
        functools.partial(_router_kernel, tr=tr),
        grid=(n // tr,),
        in_specs=[pl.BlockSpec((tr, d), lambda i: (i, 0)),
                  pl.BlockSpec((d, LANES), lambda i: (0, 0)),
                  pl.BlockSpec((N_EXPERTS, 1), lambda i: (0, 0))],
        out_specs=[slot_rows, pl.BlockSpec((tr, LANES), lambda i: (i, 0)), slot_rows,
                   pl.BlockSpec((8, LANES), lambda i: (0, 0))],
        out_shape=[jax.ShapeDtypeStruct((TOP_K, n), jnp.int32), jax.ShapeDtypeStruct((n, LANES), F32),
                   jax.ShapeDtypeStruct((TOP_K, n), jnp.int32), jax.ShapeDtypeStruct((8, LANES), F32)],
        scratch_shapes=[pltpu.VMEM((N_EXPERTS, LANES), F32)],
        compiler_params=_cparams(("arbitrary",)),
        name="router",
    )(x1, wr, router_bias.reshape(N_EXPERTS, 1))


def _group_offsets(cnt, tm):
    padded = jnp.floor((cnt + (tm - 1)) * (1.0 / tm)) * tm
    upper = (_iota2((LANES, LANES), 0) < _iota2((LANES, LANES), 1)).astype(F32)
    offs = jnp.dot(padded, upper, precision=HI, preferred_element_type=F32)
    return padded, offs


def _pos_kernel(idx_ref, rank_ref, cnt_ref, pos_ref, *, tm):
    _, offs = _group_offsets(cnt_ref[...], tm)
    starts = _row_to_col(offs[0:1, :], LANES)[:N_EXPERTS, :]
    idx = idx_ref[...]
    eidx = _iota2((N_EXPERTS, idx.shape[1]), 0)
    rows = [jnp.sum(jnp.where(eidx == idx[k:k + 1, :], starts, 0.0), axis=0, keepdims=True)
            for k in range(TOP_K)]
    pos_ref[...] = jnp.concatenate(rows, axis=0).astype(jnp.int32) + rank_ref[...]


def _positions(idx_t, rank_t, cnt, tm, tp=2048):
    n = idx_t.shape[1]
    slot_rows = pl.BlockSpec((TOP_K, tp), lambda i: (0, i))
    return pl.pallas_call(
        functools.partial(_pos_kernel, tm=tm),
        grid=(n // tp,),
        in_specs=[slot_rows, slot_rows, pl.BlockSpec((8, LANES), lambda i: (0, 0))],
        out_specs=slot_rows,
        out_shape=jax.ShapeDtypeStruct((TOP_K, n), jnp.int32),
        compiler_params=_cparams(("parallel",)),
        name="dispatch_positions",
    )(idx_t, rank_t, cnt)


def _tile_map_kernel(cnt_ref, te_ref, *, tm, width):
    padded, offs = _group_offsets(cnt_ref[...], tm)
    ends = _row_to_col((offs + padded)[0:1, :], LANES)
    expert = _iota2((LANES, width), 0)
    start = (_iota2((LANES, width), 1) * tm).astype(F32)
    done = jnp.logical_and(ends <= start, expert < N_EXPERTS)
    te = jnp.sum(jnp.where(done, 1, 0), axis=0, keepdims=True)
    total = jnp.max(jnp.where(expert < N_EXPERTS, ends, 0.0), axis=0, keepdims=True)
    ntile = (total * (1.0 / tm)).astype(jnp.int32)
    vend = _row_to_col((offs + cnt_ref[...])[0:1, :], LANES)
    mine = jnp.sum(jnp.where(expert == te, vend, 0.0), axis=0, keepdims=True)
    valid = jnp.clip(mine - start[0:1, :], 0.0, float(tm)).astype(jnp.int32)
    row = _iota2((8, width), 0)
    te_ref[...] = jnp.where(row == 0, te, jnp.where(row == 1, ntile, jnp.where(row == 2, valid, 0)))


def _tile_map(cnt, tm, ntile_max):
    width = -(-ntile_max // LANES) * LANES
    out = pl.pallas_call(
        functools.partial(_tile_map_kernel, tm=tm, width=width),
        in_specs=[pl.BlockSpec((8, LANES), lambda: (0, 0))],
        out_specs=pl.BlockSpec((8, width), lambda: (0, 0)),
        out_shape=jax.ShapeDtypeStruct((8, width), jnp.int32),
        name="tile_map",
    )(cnt)
    return out[0, :ntile_max], out[1, :1], out[2, :ntile_max]


SC_SCATTER_ROWS = 64
SC_GATHER_ROWS = 32


def _sc_mesh():
    info = plsc.get_sparse_core_info()
    mesh = plsc.VectorSubcoreMesh(core_axis_name="c", subcore_axis_name="s")
    return mesh, info.num_cores, info.num_subcores


def _sc_scatter_rows(src3, idx3, nrows):
    n, c, _ = src3.shape
    _, nk, chunk = idx3.shape
    mesh, ncore, nsub = _sc_mesh()
    per_w = n // (ncore * nsub)
    assert chunk == SC_SCATTER_ROWS and per_w % chunk == 0

    @functools.partial(
        pl.kernel, mesh=mesh,
        out_type=jax.ShapeDtypeStruct((nrows, c, LANES), src3.dtype),
        scratch_types=[pltpu.VMEM((nk, chunk), jnp.int32),
                       pltpu.VMEM((chunk, c, LANES), src3.dtype),
                       pltpu.SemaphoreType.DMA],
    )
    def scatter_kernel(src_hbm, idx_hbm, out_hbm, idx_v, rows_v, sem):
        base = (lax.axis_index("s") * ncore + lax.axis_index("c")) * per_w

        @pl.loop(0, per_w // chunk)
        def _(j):
            off = pl.multiple_of(base + j * chunk, chunk)
            pltpu.sync_copy(idx_hbm.at[base // chunk + j], idx_v)
            pltpu.sync_copy(src_hbm.at[pl.ds(off, chunk)], rows_v)
            copies = [pltpu.async_copy(rows_v, out_hbm.at[idx_v.at[k]], sem) for k in range(nk)]
            for cp in copies:
                cp.wait()

    return scatter_kernel(src3, idx3)


def _sc_gather_rows(table3, idx2):
    _, c, _ = table3.shape
    nchunk, chunk = idx2.shape
    mesh, ncore, nsub = _sc_mesh()
    per_w = nchunk // (ncore * nsub)
    assert chunk == SC_GATHER_ROWS and per_w % 2 == 0

    @functools.partial(
        pl.kernel, mesh=mesh,
        out_type=jax.ShapeDtypeStruct((nchunk * chunk, c, LANES), table3.dtype),
        scratch_types=[pltpu.VMEM((per_w, chunk), jnp.int32),
                       pltpu.VMEM((chunk, c, LANES), table3.dtype),
                       pltpu.VMEM((chunk, c, LANES), table3.dtype),
                       pltpu.SemaphoreType.DMA, pltpu.SemaphoreType.DMA,
                       pltpu.SemaphoreType.DMA, pltpu.SemaphoreType.DMA],
    )
    def gather_kernel(table_hbm, idx_hbm, out_hbm, idx_v, rows0, rows1, g0, g1, w0, w1):
        first = (lax.axis_index("s") * ncore + lax.axis_index("c")) * per_w
        pltpu.sync_copy(idx_hbm.at[pl.ds(pl.multiple_of(first, per_w), per_w)], idx_v)

        @pl.loop(0, per_w, step=2)
        def _(j):
            ga = pltpu.async_copy(table_hbm.at[idx_v.at[j]], rows0, g0)
            gb = pltpu.async_copy(table_hbm.at[idx_v.at[j + 1]], rows1, g1)
            ga.wait()
            wa = pltpu.async_copy(rows0, out_hbm.at[pl.ds(pl.multiple_of((first + j) * chunk, chunk), chunk)], w0)
            gb.wait()
            wb = pltpu.async_copy(rows1, out_hbm.at[pl.ds(pl.multiple_of((first + j + 1) * chunk, chunk), chunk)], w1)
            wa.wait()
            wb.wait()

    return gather_kernel(table3, idx2)


FFN_LOOKAHEAD = 2
FFN_SLOTS = FFN_LOOKAHEAD + 1


def _ffn_kernel(te_ref, nt_ref, nv_ref, xs_ref, w1_ref, w3_ref, w2_ref, o_ref, w1b, w3b, w2b, ord_ref, *, tm):
    j = pl.program_id(0)
    i = j - FFN_LOOKAHEAD
    nt = nt_ref[0]

    @pl.when(j == 0)
    def _():
        for s in range(4):
            ord_ref[s] = 0

    def starts_group(t):
        tc = jnp.clip(t, 0, nt - 1)
        changed = te_ref[tc] != te_ref[jnp.maximum(tc - 1, 0)]
        return jnp.logical_or(t == 0, jnp.logical_and(jnp.logical_and(t > 0, t < nt), changed))

    for s, (w_ref, wb) in enumerate(((w1_ref, w1b), (w3_ref, w3b), (w2_ref, w2b))):
        @pl.when(starts_group(j - s))
        def _(s=s, w_ref=w_ref, wb=wb):
            wb[ord_ref[s] % FFN_SLOTS] = w_ref[0, 0].astype(BF16)
            ord_ref[s] = ord_ref[s] + 1

    @pl.when(jnp.logical_and(i >= 0, i < nt))
    def _():
        ic = jnp.maximum(i, 0)

        @pl.when(jnp.logical_and(i > 0, starts_group(i)))
        def _():
            ord_ref[3] = ord_ref[3] + 1

        slot = ord_ref[3] % FFN_SLOTS
        c = xs_ref.shape[0] // tm
        live = _iota2((tm, LANES), 0) < nv_ref[ic]
        parts = [_unpack_bf16_pairs(jnp.where(live, p, jnp.uint32(0)))
                 for p in _load_token_major(xs_ref, 0, tm, c)]
        lo = jnp.concatenate([p[0] for p in parts], axis=1)
        hi = jnp.concatenate([p[1] for p in parts], axis=1)
        half = lo.shape[1]
        h1 = (jnp.dot(lo, w1b[slot, :half, :], preferred_element_type=F32)
              + jnp.dot(hi, w1b[slot, half:, :], preferred_element_type=F32))
        h3 = (jnp.dot(lo, w3b[slot, :half, :], preferred_element_type=F32)
              + jnp.dot(hi, w3b[slot, half:, :], preferred_element_type=F32))
        h = (_silu(h1) * h3).astype(BF16)
        y = jnp.dot(h, w2b[slot], preferred_element_type=F32)
        _store_token_major(o_ref, _pack_bf16_pairs(y))

    @pl.when(i >= nt)
    def _():
        o_ref[...] = jnp.zeros_like(o_ref)


def _expert_ffn(te, nt, nv, xs, w1, w3, w2, layer, nrows, tm):
    d, f = w1.shape[2], w1.shape[3]
    c = xs.shape[0] // nrows
    ntile = nrows // tm

    def tile(j, te_r, nt_r, nv_r):
        return (jnp.clip(j - FFN_LOOKAHEAD, 0, nt_r[0] - 1), 0)

    def out_tile(j, te_r, nt_r, nv_r):
        return (jnp.maximum(j - FFN_LOOKAHEAD, 0), 0)

    def expert(delay):
        def index_map(j, te_r, nt_r, nv_r):
            return (layer, te_r[jnp.clip(j - delay, 0, nt_r[0] - 1)], 0, 0)
        return index_map

    assert FFN_LOOKAHEAD == 2
    grid_spec = pltpu.PrefetchScalarGridSpec(
        num_scalar_prefetch=3,
        grid=(ntile + FFN_LOOKAHEAD,),
        in_specs=[pl.BlockSpec((tm * c, LANES), tile),
                  pl.BlockSpec((1, 1, d, f), expert(0)),
                  pl.BlockSpec((1, 1, d, f), expert(1)),
                  pl.BlockSpec((1, 1, f, d), expert(2))],
        out_specs=pl.BlockSpec((tm * c, LANES), out_tile),
        scratch_shapes=[pltpu.VMEM((FFN_SLOTS, d, f), BF16), pltpu.VMEM((FFN_SLOTS, d, f), BF16),
                        pltpu.VMEM((FFN_SLOTS, f, d), BF16), pltpu.SMEM((4,), jnp.int32)],
    )
    return pl.pallas_call(
        functools.partial(_ffn_kernel, tm=tm),
        grid_spec=grid_spec,
        out_shape=jax.ShapeDtypeStruct((nrows * c, LANES), jnp.uint32),
        compiler_params=_cparams(("arbitrary",)),
        name="expert_ffn",
    )(te, nt, nv, xs, w1, w3, w2)


def _shared_expert_kernel(x_ref, ws1_ref, ws3_ref, ws2_ref, o_ref):
    x = x_ref[...]
    xb = x.astype(BF16)
    s1 = jnp.dot(xb, ws1_ref[...], preferred_element_type=F32)
    s3 = jnp.dot(xb, ws3_ref[...], preferred_element_type=F32)
    o_ref[...] = DN_ALPHA * x + jnp.dot((_silu(s1) * s3).astype(BF16), ws2_ref[...], preferred_element_type=F32)


def _shared_expert(x1, ws1, ws3, ws2, tm=512):
    n, d = x1.shape
    sf = ws1.shape[1]
    return pl.pallas_call(
        _shared_expert_kernel,
        grid=(n // tm,),
        in_specs=[pl.BlockSpec((tm, d), lambda i: (i, 0)),
                  pl.BlockSpec((d, sf), lambda i: (0, 0)),
                  pl.BlockSpec((d, sf), lambda i: (0, 0)),
                  pl.BlockSpec((sf, d), lambda i: (0, 0))],
        out_specs=pl.BlockSpec((tm, d), lambda i: (i, 0)),
        out_shape=jax.ShapeDtypeStruct((n, d), F32),
        compiler_params=_cparams(("parallel",)),
        name="shared_expert",
    )(x1, ws1, ws3, ws2)


def _combine_kernel(*refs, tc):
    y_refs = refs[:TOP_K]
    wts_ref, base_ref, g_ref, b_ref, o_ref, ob_ref = refs[TOP_K:]
    c = y_refs[0].shape[0] // tc
    acc = base_ref[...]
    wts = wts_ref[...]
    for k in range(TOP_K):
        parts = [_unpack_bf16_pairs(p) for p in _load_token_major(y_refs[k], 0, tc, c)]
        yk = jnp.concatenate([p[0] for p in parts] + [p[1] for p in parts], axis=1).astype(F32)
        acc = acc + wts[:, k:k + 1] * yk
    y = _layernorm_rows(acc, g_ref[...], b_ref[...])
    o_ref[...] = y
    ob_ref[...] = y.astype(BF16)


def _combine(yg, wts, x1, shared, g, b, tc=256):
    n, d = x1.shape
    c = yg.shape[0] // (TOP_K * n)
    nblk = n // tc

    def slot_spec(k):
        return pl.BlockSpec((tc * c, LANES), lambda i: (k * nblk + i, 0))

    rows = pl.BlockSpec((tc, d), lambda i: (i, 0))
    return pl.pallas_call(
        functools.partial(_combine_kernel, tc=tc),
        grid=(nblk,),
        in_specs=[slot_spec(k) for k in range(TOP_K)] + [
            pl.BlockSpec((tc, LANES), lambda i: (i, 0)), rows,
            pl.BlockSpec((1, d), lambda i: (0, 0)),
            pl.BlockSpec((1, d), lambda i: (0, 0))],
        out_specs=[rows, rows],
        out_shape=[jax.ShapeDtypeStruct((n, d), F32), jax.ShapeDtypeStruct((n, d), BF16)],
        compiler_params=_cparams(("parallel",)),
        name="moe_combine",
    )(*([yg] * TOP_K), wts, shared, g.reshape(1, d), b.reshape(1, d))


def _cast_kernel(x_ref, o_ref):
    o_ref[...] = x_ref[...].astype(o_ref.dtype)


def _to_bf16(x, tm=1024):
    n, d = x.shape
    return pl.pallas_call(
        _cast_kernel,
        grid=(n // tm,),
        in_specs=[pl.BlockSpec((tm, d), lambda i: (i, 0))],
        out_specs=pl.BlockSpec((tm, d), lambda i: (i, 0)),
        out_shape=jax.ShapeDtypeStruct((n, d), BF16),
        compiler_params=_cparams(("parallel",)),
        name="cast_bf16",
    )(x)


FFN_TILE = 256


def _moe(x1, x1p, w_router, router_bias, w1, w3, w2, layer, ws1, ws3, ws2, g, b):
    n = x1.shape[0]
    tm = FFN_TILE
    nrows = n * TOP_K + N_EXPERTS * tm
    idx, wts, rank, cnt = _router(x1, w_router, router_bias)
    pos_t = _positions(idx, rank, cnt, tm)
    te, nt, nv = _tile_map(cnt, tm, nrows // tm)
    c = x1p.shape[0] // n
    pos_s = pos_t.reshape(TOP_K, n // SC_SCATTER_ROWS, SC_SCATTER_ROWS).transpose(1, 0, 2)
    xs = _sc_scatter_rows(x1p.reshape(n, c, LANES), pos_s, nrows)
    shared = _shared_expert(x1, ws1.astype(BF16), ws3.astype(BF16), ws2.astype(BF16))
    ys = _expert_ffn(te, nt, nv, xs.reshape(nrows * c, LANES), w1, w3, w2, layer, nrows, tm)
    yg = _sc_gather_rows(ys.reshape(nrows, c, LANES), pos_t.reshape(-1, SC_GATHER_ROWS))
    return _combine(yg.reshape(TOP_K * n * c, LANES), wts, x1, shared, g, b)


def kernel(x, positions, w_in, gla_wa2, gla_ba, gla_norm, mlstm_conv_w, mlstm_conv_b, mlstm_bi, mlstm_bf,
           mlstm_norm, sgu_ln_g, sgu_ln_b, sgu_ws, sgu_bs, w_pa, w_pb, w_pc, w_pd, w_out, ln1_g, ln1_b,
           w_router, router_bias, w1, w3, w2, ws1, ws3, ws2, ln2_g, ln2_b):
    nbatch, seq, d = x.shape
    n = nbatch * seq
    xf = x.reshape(n, d)
    xb = _to_bf16(xf)
    cos_t, sin_t = _rope_tables(positions)
    for l in range(DEPTH):
        wl = w_in[l]
        w_a = wl[:, _OFF_A:_OFF_A + _W_A].astype(BF16)
        w_b = wl[:, _OFF_B:_OFF_B + _W_B].astype(BF16)
        w_c = wl[:, _OFF_C:_OFF_C + _W_C].astype(BF16)
        w_d = wl[:, _OFF_D:_OFF_D + _W_D].astype(BF16)
        w_g = wl[:, _OFF_G:_OFF_G + _W_G].astype(BF16)
        w_s = jnp.concatenate(
            [wl[:, _OFF_LOW:_OFF_LOW + GLA_RANK], wl[:, _OFF_IF:_OFF_IF + 2 * MLSTM_HEADS],
             jnp.zeros((d, LANES - GLA_RANK - 2 * MLSTM_HEADS), F32)], axis=1).astype(BF16)
        y_a = _matmul(xb, w_a, 2048, 768)
        y_b = _matmul(xb, w_b, 2048, 768)
        y_c = _matmul(xb, w_c, 2048, 768)
        y_d = _matmul(xb, w_d, 2048, 1024)
        y_s = _matmul(xb, w_s, 2048, LANES)
        sm3 = y_s.reshape(nbatch, seq, LANES)
        o_a, o_d, o_c = _mixers_acd(
            y_a.reshape(nbatch, seq, _W_A), y_d.reshape(nbatch, seq, _W_D), sm3, y_c, gla_wa2[l], gla_ba[l],
            gla_norm[l], mlstm_conv_w[l], mlstm_conv_b[l], mlstm_bi[l], mlstm_bf[l], mlstm_norm[l],
            sgu_ln_g[l], sgu_ln_b[l], sgu_ws[l], sgu_bs[l])
        o_a = o_a.reshape(n, GLA_V)
        o_d = o_d.reshape(n, ML_W)
        o_b = _dilated(y_b, cos_t, sin_t, nbatch)
        merged = _merge(xb, w_g, (o_a, o_b, o_c, o_d),
                        (w_pa[l].astype(BF16), w_pb[l].astype(BF16), w_pc[l].astype(BF16), w_pd[l].astype(BF16)))
        x1, x1p = _outproj_ln(merged, w_out[l].astype(BF16), xf, ln1_g[l], ln1_b[l])
        xf, xb = _moe(x1, x1p, w_router[l], router_bias[l], w1, w3, w2, l, ws1[l], ws3[l], ws2[l],
                      ln2_g[l], ln2_b[l])
    return xf.reshape(nbatch, seq, d)
```
